```python
import jax, jax.numpy as jnp
from jax import lax
import numpy as np

D_MODEL = 2048
BATCH = 8
SEQ = 8192
DEPTH = 2

MIX_WIDTH = D_MODEL
DN_HEADS = 8
DN_HEAD_DIM = (MIX_WIDTH // 2) // DN_HEADS
DN_WIDTH = DN_HEADS * DN_HEAD_DIM
CONV_K = 4
CHUNK = 64
AT_HEAD_DIM = 64
AT_Q_HEADS = (MIX_WIDTH - DN_WIDTH) // AT_HEAD_DIM
AT_KV_HEADS = 2
AT_WIDTH = AT_Q_HEADS * AT_HEAD_DIM
AT_KV_WIDTH = AT_KV_HEADS * AT_HEAD_DIM
WINDOW = 128
ROPE_THETA = 10000.0
COL_SIZES = (3 * DN_WIDTH, DN_WIDTH, DN_HEADS, DN_HEADS, AT_WIDTH, AT_KV_WIDTH, AT_KV_WIDTH)
IN_COLS = 3 * DN_WIDTH + DN_WIDTH + 2 * DN_HEADS + AT_WIDTH + 2 * AT_KV_WIDTH
FFN_DIM = ((8 * D_MODEL + 3 * 256 - 1) // (3 * 256)) * 256
N_MOD = 6
EPS = 1e-6

kernel_name = "hybrid_deltanet_swa_sink_adaln_block"


def rms_norm(x, gain):
    xf = x.astype(jnp.float32)
    y = xf * lax.rsqrt(jnp.mean(xf * xf, axis=-1, keepdims=True) + EPS)
    return (y * gain.astype(jnp.float32)).astype(x.dtype)


def l2_norm(x):
    return x * lax.rsqrt(jnp.sum(x * x, axis=-1, keepdims=True) + EPS)


def rope(x, pos):
    d = x.shape[-1]
    half = d // 2
    inv_freq = ROPE_THETA ** (-jnp.arange(half, dtype=jnp.float32) * 2.0 / d)
    ang = pos.astype(jnp.float32)[:, None] * inv_freq[None, :]
    cos = jnp.cos(ang)[None, :, None, :]
    sin = jnp.sin(ang)[None, :, None, :]
    xf = x.astype(jnp.float32)
    x1, x2 = xf[..., :half], xf[..., half:]
    return jnp.concatenate([x1 * cos - x2 * sin, x2 * cos + x1 * sin], axis=-1).astype(x.dtype)


def to_chunks(t):
    b, tl, h = t.shape[:3]
    t = t.reshape(b, tl // CHUNK, CHUNK, h, *t.shape[3:])
    return jnp.moveaxis(t, 3, 1)


def chunk_gated_delta_rule(q, k, v, g, beta):
    b, tl, h, dv = v.shape
    qc, kc, vc = to_chunks(q), to_chunks(k), to_chunks(v)
    gc = jnp.cumsum(to_chunks(g), axis=-1)
    bc = to_chunks(beta)[..., None]
    causal = jnp.tril(jnp.ones((CHUNK, CHUNK), dtype=bool))
    strict = jnp.tril(jnp.ones((CHUNK, CHUNK), dtype=bool), -1)
    decay = jnp.exp(jnp.where(causal, gc[..., :, None] - gc[..., None, :], -jnp.inf))
    kb = kc * bc
    vb = vc * bc
    eye = jnp.eye(CHUNK, dtype=jnp.float32)
    lower = jnp.where(strict, jnp.einsum('bhncd,bhnsd->bhncs', kb, kc) * decay, 0.0) + eye
    t_inv = lax.linalg.triangular_solve(lower, jnp.broadcast_to(eye, lower.shape),
                                        left_side=True, lower=True, unit_diagonal=True)
    w = jnp.einsum('bhncs,bhnsd->bhncd', t_inv, kb * jnp.exp(gc)[..., None])
    u = jnp.einsum('bhncs,bhnsd->bhncd', t_inv, vb)
    intra = jnp.where(causal, jnp.einsum('bhncd,bhnsd->bhncs', qc, kc) * decay, 0.0)
    qg = qc * jnp.exp(gc)[..., None]
    kd = kc * jnp.exp(gc[..., -1:] - gc)[..., None]
    glast = jnp.exp(gc[..., -1])

    def step(state, inp):
        w_i, u_i, qg_i, intra_i, kd_i, gl_i = inp
        v_new = u_i - jnp.einsum('bhck,bhkv->bhcv', w_i, state)
        o_i = jnp.einsum('bhck,bhkv->bhcv', qg_i, state) + jnp.einsum('bhcs,bhsv->bhcv', intra_i, v_new)
        state = state * gl_i[..., None, None] + jnp.einsum('bhck,bhcv->bhkv', kd_i, v_new)
        return state, o_i

    seq_first = lambda t: jnp.moveaxis(t, 2, 0)
    state0 = jnp.zeros((b, h, q.shape[-1], dv), jnp.float32)
    _, o = lax.scan(step, state0, (seq_first(w), seq_first(u), seq_first(qg), seq_first(intra),
                                   seq_first(kd), jnp.moveaxis(glast, 2, 0)))
    o = jnp.moveaxis(jnp.moveaxis(o, 0, 2), 1, 3)
    return o.reshape(b, tl, h, dv)


def gated_deltanet(qkv, z, b_raw, a_raw, conv_w, a_log, dt_bias, norm_w):
    bsz, tl, _ = qkv.shape
    conv = lax.conv_general_dilated(qkv, conv_w[:, None, :].astype(qkv.dtype), window_strides=(1,),
                                    padding=[(CONV_K - 1, 0)], dimension_numbers=('NWC', 'WIO', 'NWC'),
                                    feature_group_count=3 * DN_WIDTH)
    conv = jax.nn.silu(conv.astype(jnp.float32))
    q, k, v = jnp.split(conv, 3, axis=-1)
    shp = (bsz, tl, DN_HEADS, DN_HEAD_DIM)
    q = l2_norm(q.reshape(shp)) * (DN_HEAD_DIM ** -0.5)
    k = l2_norm(k.reshape(shp))
    v = v.reshape(shp)
    beta = jax.nn.sigmoid(b_raw.astype(jnp.float32))
    g = -jnp.exp(a_log.astype(jnp.float32)) * jax.nn.softplus(a_raw.astype(jnp.float32) + dt_bias.astype(jnp.float32))
    o = chunk_gated_delta_rule(q, k, v, g, beta)
    o = o * lax.rsqrt(jnp.mean(o * o, axis=-1, keepdims=True) + EPS) * norm_w.astype(jnp.float32)
    o = o * jax.nn.silu(z.astype(jnp.float32).reshape(shp))
    return o.reshape(bsz, tl, DN_WIDTH).astype(qkv.dtype)


def sliding_window_attention_sinks(q, k, v, sinks):
    bsz, tl, _, dh = q.shape
    nb = tl // WINDOW
    grp = AT_Q_HEADS // AT_KV_HEADS
    qb = q.reshape(bsz, nb, WINDOW, AT_KV_HEADS, grp, dh)
    def band(t):
        tb = t.reshape(bsz, nb, WINDOW, AT_KV_HEADS, dh)
        prev = jnp.pad(tb, ((0, 0), (1, 0), (0, 0), (0, 0), (0, 0)))[:, :-1]
        return jnp.concatenate([prev, tb], axis=2)
    kw, vw = band(k), band(v)
    s = jnp.einsum('bnqhgd,bnkhd->bnhgqk', qb, kw).astype(jnp.float32) * (dh ** -0.5)
    r = jnp.arange(WINDOW)[:, None]
    j = jnp.arange(2 * WINDOW)[None, :]
    in_band = (j > r) & (j <= r + WINDOW)
    valid = (jnp.arange(nb)[:, None, None] > 0) | (j >= WINDOW)[None]
    mask = (in_band[None] & valid)[None, :, None, None]
    s = jnp.where(mask, s, -jnp.inf)
    sink = sinks.astype(jnp.float32).reshape(AT_KV_HEADS, grp)[None, None, :, :, None, None]
    m = jnp.maximum(jnp.max(s, axis=-1, keepdims=True), sink)
    p = jnp.exp(s - m)
    p = p / (jnp.sum(p, axis=-1, keepdims=True) + jnp.exp(sink - m))
    o = jnp.einsum('bnhgqk,bnkhd->bnqhgd', p.astype(v.dtype), vw)
    return o.reshape(bsz, tl, AT_WIDTH)


def _fwd_setup_inputs(seed: int = 0) -> dict:
    key = jax.random.key(seed)
    ks = jax.random.split(key, 20)
    f32 = jnp.float32
    nrm = lambda k, shp, s: jax.random.normal(k, shp, f32) * s
    return {
        "x": nrm(ks[0], (BATCH, SEQ, D_MODEL), 1.0),
        "c": nrm(ks[1], (BATCH, D_MODEL), 1.0),
        "ln_mix": 1.0 + nrm(ks[2], (DEPTH, D_MODEL), 0.02),
        "ln_ffn": 1.0 + nrm(ks[3], (DEPTH, D_MODEL), 0.02),
        "w_ada": nrm(ks[4], (DEPTH, D_MODEL, N_MOD * D_MODEL), D_MODEL ** -0.5),
        "b_ada": nrm(ks[5], (DEPTH, N_MOD * D_MODEL), 0.02),
        "w_in": nrm(ks[6], (DEPTH, D_MODEL, IN_COLS), D_MODEL ** -0.5),
        "dn_conv_w": nrm(ks[7], (DEPTH, CONV_K, 3 * DN_WIDTH), CONV_K ** -0.5),
        "dn_a_log": jnp.log(jax.random.uniform(ks[8], (DEPTH, DN_HEADS), f32, 1.0, 16.0)),
        "dn_dt_bias": jnp.log(jnp.expm1(jax.random.uniform(ks[9], (DEPTH, DN_HEADS), f32, 0.001, 0.1))),
        "dn_norm_w": 1.0 + nrm(ks[10], (DEPTH, DN_HEAD_DIM), 0.02),
        "attn_sinks": nrm(ks[11], (DEPTH, AT_Q_HEADS), 0.5),
        "w_out": nrm(ks[12], (DEPTH, MIX_WIDTH, D_MODEL), MIX_WIDTH ** -0.5),
        "w_gate_up": nrm(ks[13], (DEPTH, D_MODEL, 2 * FFN_DIM), D_MODEL ** -0.5),
        "w_down": nrm(ks[14], (DEPTH, FFN_DIM, D_MODEL), FFN_DIM ** -0.5),
        "ln_final": 1.0 + nrm(ks[15], (D_MODEL,), 0.02),
    }


def _fwd_reference(x, c, ln_mix, ln_ffn, w_ada, b_ada, w_in, dn_conv_w, dn_a_log, dn_dt_bias,
              dn_norm_w, attn_sinks, w_out, w_gate_up, w_down, ln_final):
    bsz, tl, _ = x.shape
    pos = jnp.arange(tl, dtype=jnp.int32)
    split_at = np.cumsum(COL_SIZES)[:-1].tolist()
    c_act = jax.nn.silu(c)
    for l in range(DEPTH):
        mod = c_act @ w_ada[l] + b_ada[l]
        sh_m, sc_m, gt_m, sh_f, sc_f, gt_f = [t[:, None, :] for t in jnp.split(mod, N_MOD, axis=-1)]
        h = rms_norm(x, ln_mix[l]) * (1.0 + sc_m) + sh_m
        proj = h @ w_in[l]
        dn_qkv, dn_z, dn_b, dn_a, at_q, at_k, at_v = jnp.split(proj, split_at, axis=-1)
        dn_out = gated_deltanet(dn_qkv, dn_z, dn_b, dn_a, dn_conv_w[l], dn_a_log[l],
                                dn_dt_bias[l], dn_norm_w[l])
        q = rope(at_q.reshape(bsz, tl, AT_Q_HEADS, AT_HEAD_DIM), pos)
        k = rope(at_k.reshape(bsz, tl, AT_KV_HEADS, AT_HEAD_DIM), pos)
        v = at_v.reshape(bsz, tl, AT_KV_HEADS, AT_HEAD_DIM)
        at_out = sliding_window_attention_sinks(q, k, v, attn_sinks[l])
        mix = jnp.concatenate([dn_out, at_out], axis=-1) @ w_out[l]
        x = x + gt_m * mix
        h = rms_norm(x, ln_ffn[l]) * (1.0 + sc_f) + sh_f
        gate, up = jnp.split(h @ w_gate_up[l], 2, axis=-1)
        x = x + gt_f * ((jax.nn.silu(gate) * up) @ w_down[l])
    return rms_norm(x, ln_final)


import jax as _jax
import jax.numpy as _jnp

TWIN_FORMAT = 'train_step'
FWD_PARAMS = ['x', 'c', 'ln_mix', 'ln_ffn', 'w_ada', 'b_ada', 'w_in', 'dn_conv_w', 'dn_a_log', 'dn_dt_bias', 'dn_norm_w', 'attn_sinks', 'w_out', 'w_gate_up', 'w_down', 'ln_final']
TWIN_WEIGHTS = ['ln_mix', 'ln_ffn', 'w_ada', 'b_ada', 'w_in', 'dn_conv_w', 'dn_a_log', 'dn_dt_bias', 'dn_norm_w', 'attn_sinks', 'w_out', 'w_gate_up', 'w_down', 'ln_final']
TWIN_DIFF_INPUT = 'x'
TWIN_INPUTS = ['x', 'c', 'ln_mix', 'ln_ffn', 'w_ada', 'b_ada', 'w_in', 'dn_conv_w', 'dn_a_log', 'dn_dt_bias', 'dn_norm_w', 'attn_sinks', 'w_out', 'w_gate_up', 'w_down', 'ln_final', 'loss_target', 'm_ln_mix', 'm_ln_ffn', 'm_w_ada', 'm_b_ada', 'm_w_in', 'm_dn_conv_w', 'm_dn_a_log', 'm_dn_dt_bias', 'm_dn_norm_w', 'm_attn_sinks', 'm_w_out', 'm_w_gate_up', 'm_w_down', 'm_ln_final', 'v_ln_mix', 'v_ln_ffn', 'v_w_ada', 'v_b_ada', 'v_w_in', 'v_dn_conv_w', 'v_dn_a_log', 'v_dn_dt_bias', 'v_dn_norm_w', 'v_attn_sinks', 'v_w_out', 'v_w_gate_up', 'v_w_down', 'v_ln_final']
TWIN_OUTPUTS = ['loss', 'grad_x', 'grad_ln_mix', 'grad_ln_ffn', 'grad_w_ada', 'grad_b_ada', 'grad_w_in', 'grad_dn_conv_w', 'grad_dn_a_log', 'grad_dn_dt_bias', 'grad_dn_norm_w', 'grad_attn_sinks', 'grad_w_out', 'grad_w_gate_up', 'grad_w_down', 'grad_ln_final', 'delta_ln_mix', 'delta_ln_ffn', 'delta_w_ada', 'delta_b_ada', 'delta_w_in', 'delta_dn_conv_w', 'delta_dn_a_log', 'delta_dn_dt_bias', 'delta_dn_norm_w', 'delta_attn_sinks', 'delta_w_out', 'delta_w_gate_up', 'delta_w_down', 'delta_ln_final', 'new_m_ln_mix', 'new_m_ln_ffn', 'new_m_w_ada', 'new_m_b_ada', 'new_m_w_in', 'new_m_dn_conv_w', 'new_m_dn_a_log', 'new_m_dn_dt_bias', 'new_m_dn_norm_w', 'new_m_attn_sinks', 'new_m_w_out', 'new_m_w_gate_up', 'new_m_w_down', 'new_m_ln_final', 'new_v_ln_mix', 'new_v_ln_ffn', 'new_v_w_ada', 'new_v_b_ada', 'new_v_w_in', 'new_v_dn_conv_w', 'new_v_dn_a_log', 'new_v_dn_dt_bias', 'new_v_dn_norm_w', 'new_v_attn_sinks', 'new_v_w_out', 'new_v_w_gate_up', 'new_v_w_down', 'new_v_ln_final']
TWIN_LEAF_KINDS = {'loss': 'loss', 'grad_x': 'grad_x', 'grad_ln_mix': 'grad_w', 'grad_ln_ffn': 'grad_w', 'grad_w_ada': 'grad_w', 'grad_b_ada': 'grad_w', 'grad_w_in': 'grad_w', 'grad_dn_conv_w': 'grad_w', 'grad_dn_a_log': 'grad_w', 'grad_dn_dt_bias': 'grad_w', 'grad_dn_norm_w': 'grad_w', 'grad_attn_sinks': 'grad_w', 'grad_w_out': 'grad_w', 'grad_w_gate_up': 'grad_w', 'grad_w_down': 'grad_w', 'grad_ln_final': 'grad_w', 'delta_ln_mix': 'delta_w', 'delta_ln_ffn': 'delta_w', 'delta_w_ada': 'delta_w', 'delta_b_ada': 'delta_w', 'delta_w_in': 'delta_w', 'delta_dn_conv_w': 'delta_w', 'delta_dn_a_log': 'delta_w', 'delta_dn_dt_bias': 'delta_w', 'delta_dn_norm_w': 'delta_w', 'delta_attn_sinks': 'delta_w', 'delta_w_out': 'delta_w', 'delta_w_gate_up': 'delta_w', 'delta_w_down': 'delta_w', 'delta_ln_final': 'delta_w', 'new_m_ln_mix': 'new_m', 'new_m_ln_ffn': 'new_m', 'new_m_w_ada': 'new_m', 'new_m_b_ada': 'new_m', 'new_m_w_in': 'new_m', 'new_m_dn_conv_w': 'new_m', 'new_m_dn_a_log': 'new_m', 'new_m_dn_dt_bias': 'new_m', 'new_m_dn_norm_w': 'new_m', 'new_m_attn_sinks': 'new_m', 'new_m_w_out': 'new_m', 'new_m_w_gate_up': 'new_m', 'new_m_w_down': 'new_m', 'new_m_ln_final': 'new_m', 'new_v_ln_mix': 'new_v', 'new_v_ln_ffn': 'new_v', 'new_v_w_ada': 'new_v', 'new_v_b_ada': 'new_v', 'new_v_w_in': 'new_v', 'new_v_dn_conv_w': 'new_v', 'new_v_dn_a_log': 'new_v', 'new_v_dn_dt_bias': 'new_v', 'new_v_dn_norm_w': 'new_v', 'new_v_attn_sinks': 'new_v', 'new_v_w_out': 'new_v', 'new_v_w_gate_up': 'new_v', 'new_v_w_down': 'new_v', 'new_v_ln_final': 'new_v'}


def _forward(args):
    return _fwd_reference(*[args[k] for k in FWD_PARAMS])


def _output_shape():
    def fwd():
        inp = _fwd_setup_inputs(0)
        return _fwd_reference(*[inp[k] for k in FWD_PARAMS])
    out = _jax.eval_shape(fwd)
    return out.shape, out.dtype

N_MICROBATCH = 1
ADAM_LR = 0.001
ADAM_B1 = 0.9
ADAM_B2 = 0.999
ADAM_EPS = 1e-08
ADAM_WD = 0.01
ADAM_STEP = 10
PER_EXAMPLE_BATCH_AXIS = {'x': 0, 'c': 0, 'loss_target': 0}
SHARED_INPUTS = []
_WEIGHT_DTYPES = {'ln_mix': _jnp.float32, 'ln_ffn': _jnp.float32, 'w_ada': _jnp.float32, 'b_ada': _jnp.float32, 'w_in': _jnp.float32, 'dn_conv_w': _jnp.float32, 'dn_a_log': _jnp.float32, 'dn_dt_bias': _jnp.float32, 'dn_norm_w': _jnp.float32, 'attn_sinks': _jnp.float32, 'w_out': _jnp.float32, 'w_gate_up': _jnp.float32, 'w_down': _jnp.float32, 'ln_final': _jnp.float32}
MOMENT_SCALE = {'ln_mix': 6.511899e-02, 'ln_ffn': 7.952758e-02, 'w_ada': 5.285931e-02, 'b_ada': 9.376182e-02, 'w_in': 4.932042e-02, 'dn_conv_w': 3.969091e-02, 'dn_a_log': 2.489767e-01, 'dn_dt_bias': 2.263914e-01, 'dn_norm_w': 1.720055e-01, 'attn_sinks': 2.370025e-02, 'w_out': 5.710178e-02, 'w_gate_up': 3.838298e-02, 'w_down': 6.308165e-02, 'ln_final': 3.239299e+01}


def _to_microbatches(a, axis):
    t = _jnp.moveaxis(a, axis, 0)
    t = t.reshape((N_MICROBATCH, t.shape[0] // N_MICROBATCH) + t.shape[1:])
    return _jnp.moveaxis(t, 1, axis + 1)


def setup_inputs(seed: int = 0) -> dict:
    inp = _fwd_setup_inputs(seed)
    key = _jax.random.fold_in(_jax.random.key(seed), 7919)
    shape, _ = _output_shape()
    out = dict(inp)
    out["loss_target"] = _jax.random.normal(_jax.random.fold_in(key, 0), shape, _jnp.float32)
    for i, name in enumerate(TWIN_WEIGHTS):
        w = inp[name].astype(_jnp.float32)
        if MOMENT_SCALE is None:
            s = _jnp.sqrt(_jnp.mean(_jnp.square(w)) + 1e-30)
        else:
            s = MOMENT_SCALE[name]
        km, kv = _jax.random.split(_jax.random.fold_in(key, i + 1))
        out[name] = w
        out["m_" + name] = s * _jax.random.normal(km, w.shape, _jnp.float32)
        out["v_" + name] = (s * s) * _jax.random.uniform(kv, w.shape, _jnp.float32, 0.5, 1.5)
    if N_MICROBATCH > 1:
        for name, axis in PER_EXAMPLE_BATCH_AXIS.items():
            out[name] = _to_microbatches(out[name], axis)
    return {'x': out['x'], 'c': out['c'], 'ln_mix': out['ln_mix'], 'ln_ffn': out['ln_ffn'], 'w_ada': out['w_ada'], 'b_ada': out['b_ada'], 'w_in': out['w_in'], 'dn_conv_w': out['dn_conv_w'], 'dn_a_log': out['dn_a_log'], 'dn_dt_bias': out['dn_dt_bias'], 'dn_norm_w': out['dn_norm_w'], 'attn_sinks': out['attn_sinks'], 'w_out': out['w_out'], 'w_gate_up': out['w_gate_up'], 'w_down': out['w_down'], 'ln_final': out['ln_final'], 'loss_target': out['loss_target'], 'm_ln_mix': out['m_ln_mix'], 'm_ln_ffn': out['m_ln_ffn'], 'm_w_ada': out['m_w_ada'], 'm_b_ada': out['m_b_ada'], 'm_w_in': out['m_w_in'], 'm_dn_conv_w': out['m_dn_conv_w'], 'm_dn_a_log': out['m_dn_a_log'], 'm_dn_dt_bias': out['m_dn_dt_bias'], 'm_dn_norm_w': out['m_dn_norm_w'], 'm_attn_sinks': out['m_attn_sinks'], 'm_w_out': out['m_w_out'], 'm_w_gate_up': out['m_w_gate_up'], 'm_w_down': out['m_w_down'], 'm_ln_final': out['m_ln_final'], 'v_ln_mix': out['v_ln_mix'], 'v_ln_ffn': out['v_ln_ffn'], 'v_w_ada': out['v_w_ada'], 'v_b_ada': out['v_b_ada'], 'v_w_in': out['v_w_in'], 'v_dn_conv_w': out['v_dn_conv_w'], 'v_dn_a_log': out['v_dn_a_log'], 'v_dn_dt_bias': out['v_dn_dt_bias'], 'v_dn_norm_w': out['v_dn_norm_w'], 'v_attn_sinks': out['v_attn_sinks'], 'v_w_out': out['v_w_out'], 'v_w_gate_up': out['v_w_gate_up'], 'v_w_down': out['v_w_down'], 'v_ln_final': out['v_ln_final']}


def _loss(weights, diff, rest, loss_target):
    with _jax.named_scope("forward"):
        args = {**rest, TWIN_DIFF_INPUT: diff, **{k: w.astype(_WEIGHT_DTYPES[k]) for k, w in weights.items()}}
        y = _forward(args)
    with _jax.named_scope("loss_head"):
        err = _jnp.square(y.astype(_jnp.float32) - loss_target)
        return 0.5 * _jnp.sum(_jnp.mean(err, axis=-1)) if err.ndim else 0.5 * err


def _adamw(w, g, m, v):
    m = ADAM_B1 * m + (1.0 - ADAM_B1) * g
    v = ADAM_B2 * v + (1.0 - ADAM_B2) * _jnp.square(g)
    m_hat = m / (1.0 - ADAM_B1 ** ADAM_STEP)
    v_hat = v / (1.0 - ADAM_B2 ** ADAM_STEP)
    delta = -ADAM_LR * (m_hat / (_jnp.sqrt(v_hat) + ADAM_EPS) + ADAM_WD * w)
    return delta, m, v


def reference(x, c, ln_mix, ln_ffn, w_ada, b_ada, w_in, dn_conv_w, dn_a_log, dn_dt_bias, dn_norm_w, attn_sinks, w_out, w_gate_up, w_down, ln_final, loss_target, m_ln_mix, m_ln_ffn, m_w_ada, m_b_ada, m_w_in, m_dn_conv_w, m_dn_a_log, m_dn_dt_bias, m_dn_norm_w, m_attn_sinks, m_w_out, m_w_gate_up, m_w_down, m_ln_final, v_ln_mix, v_ln_ffn, v_w_ada, v_b_ada, v_w_in, v_dn_conv_w, v_dn_a_log, v_dn_dt_bias, v_dn_norm_w, v_attn_sinks, v_w_out, v_w_gate_up, v_w_down, v_ln_final):
    given = dict(x=x, c=c, ln_mix=ln_mix, ln_ffn=ln_ffn, w_ada=w_ada, b_ada=b_ada, w_in=w_in, dn_conv_w=dn_conv_w, dn_a_log=dn_a_log, dn_dt_bias=dn_dt_bias, dn_norm_w=dn_norm_w, attn_sinks=attn_sinks, w_out=w_out, w_gate_up=w_gate_up, w_down=w_down, ln_final=ln_final, loss_target=loss_target, m_ln_mix=m_ln_mix, m_ln_ffn=m_ln_ffn, m_w_ada=m_w_ada, m_b_ada=m_b_ada, m_w_in=m_w_in, m_dn_conv_w=m_dn_conv_w, m_dn_a_log=m_dn_a_log, m_dn_dt_bias=m_dn_dt_bias, m_dn_norm_w=m_dn_norm_w, m_attn_sinks=m_attn_sinks, m_w_out=m_w_out, m_w_gate_up=m_w_gate_up, m_w_down=m_w_down, m_ln_final=m_ln_final, v_ln_mix=v_ln_mix, v_ln_ffn=v_ln_ffn, v_w_ada=v_w_ada, v_b_ada=v_b_ada, v_w_in=v_w_in, v_dn_conv_w=v_dn_conv_w, v_dn_a_log=v_dn_a_log, v_dn_dt_bias=v_dn_dt_bias, v_dn_norm_w=v_dn_norm_w, v_attn_sinks=v_attn_sinks, v_w_out=v_w_out, v_w_gate_up=v_w_gate_up, v_w_down=v_w_down, v_ln_final=v_ln_final)
    weights = {n: given[n] for n in TWIN_WEIGHTS}
    shared = {n: given[n] for n in SHARED_INPUTS}
    per_example = {n: given[n] for n in ['x', 'c']}
    grad_fn = _jax.value_and_grad(_loss, argnums=(0, 1))

    def one_microbatch(ex, loss_target):
        ex = dict(ex)
        diff = ex.pop(TWIN_DIFF_INPUT)
        return grad_fn(weights, diff, {**shared, **ex}, loss_target)

    if N_MICROBATCH == 1:
        loss, (grad_w, grad_x) = one_microbatch(per_example, given["loss_target"])
    else:
        def body(carry, xs):
            loss_sum, grad_sum = carry
            l_k, (gw_k, gx_k) = one_microbatch(xs[0], xs[1])
            with _jax.named_scope("update"):
                return (loss_sum + l_k, _jax.tree.map(_jnp.add, grad_sum, gw_k)), gx_k

        init = (_jnp.zeros((), _jnp.float32), _jax.tree.map(_jnp.zeros_like, weights))
        (loss, grad_w), grad_x = _jax.lax.scan(body, init, (per_example, given["loss_target"]))
    with _jax.named_scope("update"):
        delta_w, new_m, new_v = {}, {}, {}
        for n in TWIN_WEIGHTS:
            delta_w[n], new_m[n], new_v[n] = _adamw(weights[n], grad_w[n], given["m_" + n], given["v_" + n])
    return (loss, grad_x, *[grad_w[n] for n in TWIN_WEIGHTS], *[delta_w[n] for n in TWIN_WEIGHTS],
            *[new_m[n] for n in TWIN_WEIGHTS], *[new_v[n] for n in TWIN_WEIGHTS])
```

```python
import functools

import jax
import jax.numpy as jnp
from jax import lax
from jax.experimental import pallas as pl
from jax.experimental.pallas import tpu as pltpu

F32, BF16 = jnp.float32, jnp.bfloat16
HI = lax.Precision.HIGHEST
MESH = pl.DeviceIdType.MESH
ANY = pl.BlockSpec(memory_space=pl.ANY)

N_DEV = 8
D = 2048
DN_HEADS, DN_HD = 8, 128
DN_W = 1024
CONV_K = 4
CHUNK = 64
AT_HD, AT_QH, AT_KVH = 64, 16, 2
AT_W = 1024
WINDOW = 128
FFN = 5632
IN_COLS = 5392
IN_PAD = 5504
EPS = 1e-6
NEG = -1e30
LR, B1, B2, AEPS, WD, STEP = 0.001, 0.9, 0.999, 1e-08, 0.01, 10
VMEM_LIMIT = 56 * 1024 * 1024


def _pick(n, cands):
    for c in cands:
        if n % c == 0:
            return c
    return n


def _params(sem):
    return pltpu.CompilerParams(dimension_semantics=sem, vmem_limit_bytes=VMEM_LIMIT)


_DN = {"nn": (((1,), (0,)), ((), ())), "nt": (((1,), (1,)), ((), ())), "tn": (((0,), (0,)), ((), ()))}


def _mm(a, b, mode, out_dtype, name):
    if mode == "nn":
        (M, K), (_, N) = a.shape, b.shape
    elif mode == "nt":
        (M, K), (N, _) = a.shape, b.shape
    else:
        (K, M), (_, N) = a.shape, b.shape
    tm = _pick(M, (1024, 512, 256, 128, 64, 32, 16))
    tn = _pick(N, (1024, 512, 256, 128))
    tk = _pick(K, (512, 256, 128, 64, 32, 16))
    nk = K // tk
    dn = _DN[mode]

    def body(a_ref, b_ref, o_ref, acc):
        k = pl.program_id(2)

        @pl.when(k == 0)
        def _():
            acc[...] = jnp.zeros_like(acc)

        acc[...] += lax.dot_general(a_ref[...].astype(BF16), b_ref[...].astype(BF16), dn, preferred_element_type=F32)

        @pl.when(k == nk - 1)
        def _():
            o_ref[...] = acc[...].astype(o_ref.dtype)

    a_spec = pl.BlockSpec((tk, tm), lambda i, j, k: (k, i)) if mode == "tn" else pl.BlockSpec((tm, tk), lambda i, j, k: (i, k))
    b_spec = pl.BlockSpec((tn, tk), lambda i, j, k: (j, k)) if mode == "nt" else pl.BlockSpec((tk, tn), lambda i, j, k: (k, j))
    return pl.pallas_call(
        body, name=name, grid=(M // tm, N // tn, nk),
        in_specs=[a_spec, b_spec], out_specs=pl.BlockSpec((tm, tn), lambda i, j, k: (i, j)),
        out_shape=jax.ShapeDtypeStruct((M, N), out_dtype),
        scratch_shapes=[pltpu.VMEM((tm, tn), F32)],
        compiler_params=_params(("parallel", "parallel", "arbitrary")),
    )(a, b)


def _ew_spec(arr, rt, cb, tm, ncol):
    R, W = arr.shape
    return pl.BlockSpec((tm if rt else R, W // ncol if cb else W), lambda j, i: (i if rt else 0, j if cb else 0))


def _ew_fwd(f, ops, outs, *, tm, ncol=1, name):
    M = next(a.shape[0] for a, rt, _ in ops if rt)
    n_in = len(ops)

    def body(*refs):
        res = f(*[r[...].astype(F32) for r in refs[:n_in]])
        for o, r in zip(refs[n_in:], res):
            o[...] = r.astype(o.dtype)

    return pl.pallas_call(
        body, name=name, grid=(ncol, M // tm),
        in_specs=[_ew_spec(a, rt, cb, tm, ncol) for a, rt, cb in ops],
        out_specs=[pl.BlockSpec((tm, w // ncol if cb else w), lambda j, i, cb=cb: (i, j if cb else 0)) for w, _, cb in outs],
        out_shape=[jax.ShapeDtypeStruct((M, w), dt) for w, dt, _ in outs],
        compiler_params=_params(("parallel", "parallel")),
    )(*[a for a, _, _ in ops])


def _ew_bwd(f, ops, cts, diff, *, tm, ncol=1, name, gdt=None):
    M = next(a.shape[0] for a, rt, _ in ops if rt)
    n_in, n_ct = len(ops), len(cts)
    gdt = gdt or [F32] * len(diff)

    def body(*refs):
        j, i = pl.program_id(0), pl.program_id(1)
        vals = [r[...].astype(F32) for r in refs[:n_in]]

        def fd(*dv):
            full = list(vals)
            for idx, v in zip(diff, dv):
                full[idx] = v
            return tuple(f(*full))

        _, vjp = jax.vjp(fd, *[vals[idx] for idx in diff])
        gs = vjp(tuple(r[...].astype(F32) for r in refs[n_in:n_in + n_ct]))
        for idx, g, gref in zip(diff, gs, refs[n_in + n_ct:]):
            _, rt, cb = ops[idx]
            if rt:
                gref[...] = g.astype(gref.dtype)
            else:
                first = (i == 0) if cb else jnp.logical_and(i == 0, j == 0)

                @pl.when(first)
                def _(gref=gref):
                    gref[...] = jnp.zeros_like(gref)

                gref[...] += g

    return pl.pallas_call(
        body, name=name, grid=(ncol, M // tm),
        in_specs=[_ew_spec(a, rt, cb, tm, ncol) for a, rt, cb in ops]
        + [pl.BlockSpec((tm, a.shape[1] // ncol if cb else a.shape[1]), lambda j, i, cb=cb: (i, j if cb else 0)) for a, cb in cts],
        out_specs=[_ew_spec(*ops[idx], tm, ncol) for idx in diff],
        out_shape=[jax.ShapeDtypeStruct(ops[idx][0].shape, dt if ops[idx][1] else F32) for idx, dt in zip(diff, gdt)],
        compiler_params=_params(("arbitrary", "arbitrary")),
    )(*[a for a, _, _ in ops], *[a for a, _ in cts])


def _silu(x):
    return x * jax.nn.sigmoid(x)


def _f_normmod(x, ln, sc, bsc, sh, bsh):
    y = x * lax.rsqrt(jnp.mean(x * x, axis=-1, keepdims=True) + EPS) * ln
    return y * (1.0 + (sc + bsc)) + (sh + bsh), x


def _f_resgate(x, br, gt, bgt):
    return (x + (gt + bgt) * br,)


def _f_swiglu(gate, up):
    return (_silu(gate) * up,)


def _f_final(x, tgt, ln):
    y = x * lax.rsqrt(jnp.mean(x * x, axis=-1, keepdims=True) + EPS) * ln
    e = y - tgt
    return (jnp.broadcast_to(0.5 * jnp.mean(e * e, axis=-1, keepdims=True), (x.shape[0], 128)),)


def _f_bg(ba, alog, dt):
    col = lax.broadcasted_iota(jnp.int32, ba.shape, 1)
    z = ba + dt
    sp = jnp.maximum(z, 0.0) + jnp.log(1.0 + jnp.exp(-jnp.abs(z)))
    return (jnp.where(col < 8, jax.nn.sigmoid(ba), jnp.where(col < 16, -jnp.exp(alog) * sp, 0.0)),)


def _l2n(x):
    return x * lax.rsqrt(jnp.sum(x * x, axis=-1, keepdims=True) + EPS)


def _f_dnpre(cq, ck, cv):
    return _l2n(_silu(cq)) * (DN_HD ** -0.5), _l2n(_silu(ck)), _silu(cv)


def _f_dnpost(o, z, nw):
    return (o * lax.rsqrt(jnp.mean(o * o, axis=-1, keepdims=True) + EPS) * nw * _silu(z),)


def _conv_fwd(x, w, name):
    T, Cw = x.shape
    tm = 512

    def body(x_ref, h_ref, w_ref, o_ref):
        i = pl.program_id(1)
        cur, halo, wv = x_ref[...], h_ref[...], w_ref[...]
        halo = jnp.where(i > 0, halo, 0.0)
        row = lax.broadcasted_iota(jnp.int32, (8, 128), 0)
        acc = wv[3:4, :] * cur
        for s in (1, 2, 3):
            r = pltpu.roll(cur, s, 0)
            top = jnp.where(row < s, pltpu.roll(halo, s, 0), r[:8])
            acc += wv[3 - s:4 - s, :] * jnp.concatenate([top, r[8:]], axis=0)
        o_ref[...] = acc

    return pl.pallas_call(
        body, name=name, grid=(Cw // 128, T // tm),
        in_specs=[pl.BlockSpec((tm, 128), lambda j, i: (i, j)),
                  pl.BlockSpec((8, 128), lambda j, i: (jnp.maximum(i * (tm // 8) - 1, 0), j)),
                  pl.BlockSpec((4, 128), lambda j, i: (0, j))],
        out_specs=pl.BlockSpec((tm, 128), lambda j, i: (i, j)),
        out_shape=jax.ShapeDtypeStruct((T, Cw), F32),
        compiler_params=_params(("parallel", "parallel")),
    )(x, x, w)


def _conv_bwd(x, dy, w, name):
    T, Cw = x.shape
    tm = 512
    nt = T // tm

    def body(x_ref, h_ref, dy_ref, n_ref, w_ref, dx_ref, dw_ref):
        i = pl.program_id(1)
        cur, dcur, wv = x_ref[...], dy_ref[...], w_ref[...]
        halo = jnp.where(i > 0, h_ref[...], 0.0)
        nxt = jnp.where(i < nt - 1, n_ref[...], 0.0)
        row = lax.broadcasted_iota(jnp.int32, (8, 128), 0)

        @pl.when(i == 0)
        def _():
            dw_ref[...] = jnp.zeros_like(dw_ref)

        dx = wv[3:4, :] * dcur
        dw_ref[3:4, :] += jnp.sum(dcur * cur, axis=0, keepdims=True)
        for s in (1, 2, 3):
            r = pltpu.roll(cur, s, 0)
            top = jnp.where(row < s, pltpu.roll(halo, s, 0), r[:8])
            xs = jnp.concatenate([top, r[8:]], axis=0)
            dw_ref[3 - s:4 - s, :] += jnp.sum(dcur * xs, axis=0, keepdims=True)
            rf = pltpu.roll(dcur, tm - s, 0)
            bot = jnp.where(row >= 8 - s, pltpu.roll(nxt, 8 - s, 0), rf[tm - 8:])
            dx += wv[3 - s:4 - s, :] * jnp.concatenate([rf[:tm - 8], bot], axis=0)
        dx_ref[...] = dx

    return pl.pallas_call(
        body, name=name, grid=(Cw // 128, nt),
        in_specs=[pl.BlockSpec((tm, 128), lambda j, i: (i, j)),
                  pl.BlockSpec((8, 128), lambda j, i: (jnp.maximum(i * (tm // 8) - 1, 0), j)),
                  pl.BlockSpec((tm, 128), lambda j, i: (i, j)),
                  pl.BlockSpec((8, 128), lambda j, i: (jnp.minimum((i + 1) * (tm // 8), T // 8 - 1), j)),
                  pl.BlockSpec((4, 128), lambda j, i: (0, j))],
        out_specs=[pl.BlockSpec((tm, 128), lambda j, i: (i, j)), pl.BlockSpec((4, 128), lambda j, i: (0, j))],
        out_shape=[jax.ShapeDtypeStruct((T, Cw), F32), jax.ShapeDtypeStruct((4, Cw), F32)],
        compiler_params=_params(("arbitrary", "arbitrary")),
    )(x, x, dy, dy, w)


def _dot(a, b):
    return jnp.dot(a, b, precision=HI, preferred_element_type=F32)


def _dot_t(a, b):
    return lax.dot_general(a, b, (((1,), (1,)), ((), ())), precision=HI, preferred_element_type=F32)


def _chunk_f(q, k, v, bg, S, h):
    C = CHUNK
    lane = lax.broadcasted_iota(jnp.int32, (C, 128), 1)
    beta = jnp.sum(jnp.where(lane == h, bg, 0.0), axis=1, keepdims=True)
    g = jnp.sum(jnp.where(lane == h + 8, bg, 0.0), axis=1, keepdims=True)
    ri = lax.broadcasted_iota(jnp.int32, (C, C), 0)
    ci = lax.broadcasted_iota(jnp.int32, (C, C), 1)
    causal, strict, eye = ri >= ci, ri > ci, ri == ci
    g_row = jnp.sum(jnp.where(eye, g, 0.0), axis=0, keepdims=True)
    gc_col = jnp.sum(jnp.where(causal, g_row, 0.0), axis=1, keepdims=True)
    gc_row = jnp.sum(jnp.where(ri <= ci, g, 0.0), axis=0, keepdims=True)
    gc_last = jnp.sum(g, axis=0, keepdims=True)
    decay = jnp.exp(jnp.where(causal, gc_col - gc_row, NEG))
    kb, vb = k * beta, v * beta
    n1 = -jnp.where(strict, _dot_t(kb, k) * decay, 0.0)
    t_inv = jnp.where(eye, 1.0, 0.0) + n1
    p = n1
    for _ in range(5):
        p = _dot(p, p)
        t_inv = t_inv + _dot(t_inv, p)
    egc = jnp.exp(gc_col)
    w = _dot(t_inv, kb * egc)
    u = _dot(t_inv, vb)
    intra = jnp.where(causal, _dot_t(q, k) * decay, 0.0)
    kd = k * jnp.exp(gc_last - gc_col)
    v_new = u - _dot(w, S)
    o = _dot(q * egc, S) + _dot(intra, v_new)
    S_next = S * jnp.exp(gc_last) + lax.dot_general(kd, v_new, (((0,), (0,)), ((), ())), precision=HI, preferred_element_type=F32)
    return o, S_next


def _chunk_fwd(q, k, v, bg, name):
    T = q.shape[0]
    N = T // CHUNK

    def body(q_ref, k_ref, v_ref, bg_ref, o_ref, s_ref, S):
        n, h = pl.program_id(0), pl.program_id(1)

        @pl.when(n == 0)
        def _():
            S[h] = jnp.zeros((DN_HD, DN_HD), F32)

        s_in = S[h]
        s_ref[0, 0] = s_in
        o, s_next = _chunk_f(q_ref[...], k_ref[...], v_ref[...], bg_ref[...], s_in, h)
        o_ref[...] = o
        S[h] = s_next

    qs = pl.BlockSpec((CHUNK, DN_HD), lambda n, h: (n, h))
    return pl.pallas_call(
        body, name=name, grid=(N, DN_HEADS),
        in_specs=[qs, qs, qs, pl.BlockSpec((CHUNK, 128), lambda n, h: (n, 0))],
        out_specs=[qs, pl.BlockSpec((1, 1, DN_HD, DN_HD), lambda n, h: (n, h, 0, 0))],
        out_shape=[jax.ShapeDtypeStruct((T, DN_W), F32), jax.ShapeDtypeStruct((N, DN_HEADS, DN_HD, DN_HD), F32)],
        scratch_shapes=[pltpu.VMEM((DN_HEADS, DN_HD, DN_HD), F32)],
        compiler_params=_params(("arbitrary", "arbitrary")),
    )(q, k, v, bg)


def _chunk_bwd(q, k, v, bg, s_saved, do, name):
    T = q.shape[0]
    N = T // CHUNK

    def body(q_ref, k_ref, v_ref, bg_ref, s_ref, do_ref, dq_ref, dk_ref, dv_ref, dbg_ref, dS):
        n, h = pl.program_id(0), pl.program_id(1)

        @pl.when(n == 0)
        def _():
            dS[h] = jnp.zeros((DN_HD, DN_HD), F32)

        _, vjp = jax.vjp(functools.partial(_chunk_f, h=h), q_ref[...], k_ref[...], v_ref[...], bg_ref[...], s_ref[0, 0])
        dq, dk, dv, dbg, ds_in = vjp((do_ref[...], dS[h]))
        dq_ref[...] = dq
        dk_ref[...] = dk
        dv_ref[...] = dv

        @pl.when(h == 0)
        def _():
            dbg_ref[...] = jnp.zeros_like(dbg_ref)

        dbg_ref[...] += dbg
        dS[h] = ds_in

    qs = pl.BlockSpec((CHUNK, DN_HD), lambda n, h: (N - 1 - n, h))
    bs = pl.BlockSpec((CHUNK, 128), lambda n, h: (N - 1 - n, 0))
    return pl.pallas_call(
        body, name=name, grid=(N, DN_HEADS),
        in_specs=[qs, qs, qs, bs, pl.BlockSpec((1, 1, DN_HD, DN_HD), lambda n, h: (N - 1 - n, h, 0, 0)), qs],
        out_specs=[qs, qs, qs, bs],
        out_shape=[jax.ShapeDtypeStruct((T, DN_W), F32)] * 3 + [jax.ShapeDtypeStruct((T, 128), F32)],
        scratch_shapes=[pltpu.VMEM((DN_HEADS, DN_HD, DN_HD), F32)],
        compiler_params=_params(("arbitrary", "arbitrary")),
    )(q, k, v, bg, s_saved, do)


def _attn_f(q, kp, kc, vp, vc, sinkrow, cos_c, sin_c, cos_p, sin_p, rot, has_prev):
    def rope(x, c, s):
        return x * c + _dot(x, rot) * s

    qr, kcr, kpr = rope(q, cos_c, sin_c), rope(kc, cos_c, sin_c), rope(kp, cos_p, sin_p)
    r = lax.broadcasted_iota(jnp.int32, (WINDOW, WINDOW), 0)
    j = lax.broadcasted_iota(jnp.int32, (WINDOW, WINDOW), 1)
    sc = jnp.where(j <= r, _dot_t(qr, kcr) * (AT_HD ** -0.5), NEG)
    sp = jnp.where(jnp.logical_and(j > r, has_prev), _dot_t(qr, kpr) * (AT_HD ** -0.5), NEG)
    lane = lax.broadcasted_iota(jnp.int32, (1, 128), 1)
    sink = jnp.sum(jnp.where(lane == 0, sinkrow, 0.0), axis=1, keepdims=True)
    m = jnp.maximum(jnp.maximum(jnp.max(sc, axis=1, keepdims=True), jnp.max(sp, axis=1, keepdims=True)), sink)
    pc, pp = jnp.exp(sc - m), jnp.exp(sp - m)
    den = jnp.sum(pc, axis=1, keepdims=True) + jnp.sum(pp, axis=1, keepdims=True) + jnp.exp(sink - m)
    return (_dot(pc, vc) + _dot(pp, vp)) / den


def _attn_specs(nb):
    grp = AT_QH // AT_KVH
    qs = pl.BlockSpec((1, WINDOW, AT_HD), lambda h, n: (h, n, 0))
    kc = pl.BlockSpec((1, WINDOW, AT_HD), lambda h, n: (h // grp, n, 0))
    kp = pl.BlockSpec((1, WINDOW, AT_HD), lambda h, n: (h // grp, jnp.maximum(n - 1, 0), 0))
    tc = pl.BlockSpec((WINDOW, AT_HD), lambda h, n: (n, 0))
    tp = pl.BlockSpec((WINDOW, AT_HD), lambda h, n: (jnp.maximum(n - 1, 0), 0))
    sk = pl.BlockSpec((1, 1, 128), lambda h, n: (h, 0, 0))
    rt = pl.BlockSpec((AT_HD, AT_HD), lambda h, n: (0, 0))
    return qs, kc, kp, tc, tp, sk, rt


def _attn_fwd(q, k, v, sinks, cos, sin, rot, name):
    T = q.shape[1]
    nb = T // WINDOW
    qs, kc, kp, tc, tp, sk, rt = _attn_specs(nb)

    def body(q_ref, kp_ref, kc_ref, vp_ref, vc_ref, sk_ref, cc_ref, sc_ref, cp_ref, sp_ref, rot_ref, o_ref):
        n = pl.program_id(1)
        o_ref[0] = _attn_f(q_ref[0], kp_ref[0], kc_ref[0], vp_ref[0], vc_ref[0], sk_ref[0], cc_ref[...], sc_ref[...],
                           cp_ref[...], sp_ref[...], rot_ref[...], n > 0)

    return pl.pallas_call(
        body, name=name, grid=(AT_QH, nb),
        in_specs=[qs, kp, kc, kp, kc, sk, tc, tc, tp, tp, rt], out_specs=qs,
        out_shape=jax.ShapeDtypeStruct(q.shape, F32),
        compiler_params=_params(("parallel", "parallel")),
    )(q, k, k, v, v, sinks, cos, sin, cos, sin, rot)


def _attn_bwd(q, k, v, sinks, cos, sin, rot, do, name):
    T = q.shape[1]
    nb = T // WINDOW
    qs, kc, kp, tc, tp, sk, rt = _attn_specs(nb)

    def body(q_ref, kp_ref, kc_ref, vp_ref, vc_ref, sk_ref, cc_ref, sc_ref, cp_ref, sp_ref, rot_ref, do_ref,
             dq_ref, dkp_ref, dkc_ref, dvp_ref, dvc_ref, dsk_ref):
        n = pl.program_id(1)
        f = functools.partial(_attn_f, cos_c=cc_ref[...], sin_c=sc_ref[...], cos_p=cp_ref[...], sin_p=sp_ref[...],
                              rot=rot_ref[...], has_prev=n > 0)
        _, vjp = jax.vjp(f, q_ref[0], kp_ref[0], kc_ref[0], vp_ref[0], vc_ref[0], sk_ref[0])
        dq, dkp, dkc, dvp, dvc, dsk = vjp(do_ref[0])
        dq_ref[0], dkp_ref[0], dkc_ref[0], dvp_ref[0], dvc_ref[0] = dq, dkp, dkc, dvp, dvc

        @pl.when(n == 0)
        def _():
            dsk_ref[...] = jnp.zeros_like(dsk_ref)

        dsk_ref[0] += dsk

    return pl.pallas_call(
        body, name=name, grid=(AT_QH, nb),
        in_specs=[qs, kp, kc, kp, kc, sk, tc, tc, tp, tp, rt, qs], out_specs=[qs, qs, qs, qs, qs, sk],
        out_shape=[jax.ShapeDtypeStruct(q.shape, F32)] * 5 + [jax.ShapeDtypeStruct(sinks.shape, F32)],
        compiler_params=_params(("arbitrary", "arbitrary")),
    )(q, k, k, v, v, sinks, cos, sin, cos, sin, rot, do)


def _kv_combine(dc, dp, name):
    T = dc.shape[1]
    nb = T // WINDOW
    grp = AT_QH // AT_KVH

    def body(c_ref, p_ref, o_ref):
        n = pl.program_id(1)
        nxt = jnp.where(n < nb - 1, p_ref[...], 0.0)
        acc = c_ref[0] + nxt[0]
        for e in range(1, grp):
            acc += c_ref[e] + nxt[e]
        o_ref[0] = acc

    return pl.pallas_call(
        body, name=name, grid=(AT_KVH, nb),
        in_specs=[pl.BlockSpec((grp, WINDOW, AT_HD), lambda g, n: (g, n, 0)),
                  pl.BlockSpec((grp, WINDOW, AT_HD), lambda g, n: (g, jnp.minimum(n + 1, nb - 1), 0))],
        out_specs=pl.BlockSpec((1, WINDOW, AT_HD), lambda g, n: (g, n, 0)),
        out_shape=jax.ShapeDtypeStruct((AT_KVH, T, AT_HD), F32),
        compiler_params=_params(("parallel", "parallel")),
    )(dc, dp)


def _place():
    x, y, c = lax.axis_index("x"), lax.axis_index("y"), lax.axis_index("c")
    return x, y, c, 4 * x + 2 * y + c


def _peer(x, y, c, k):
    return (x ^ ((k >> 2) & 1), y ^ ((k >> 1) & 1), c ^ (k & 1))


def _exchange(src, gather, name):
    shape = (N_DEV,) + src.shape if gather else src.shape

    def body(s_ref, o_ref, send_sems, recv_sems, lsem):
        x, y, c, me = _place()
        mine = pltpu.make_async_copy(s_ref if gather else s_ref.at[me], o_ref.at[me], lsem)
        mine.start()
        cps = []
        for k in range(1, N_DEV):
            px, py, pc = _peer(x, y, c, k)
            cp = pltpu.make_async_remote_copy(
                src_ref=s_ref if gather else s_ref.at[4 * px + 2 * py + pc], dst_ref=o_ref.at[me],
                send_sem=send_sems.at[k - 1], recv_sem=recv_sems.at[k - 1], device_id=(px, py, pc), device_id_type=MESH)
            cp.start()
            cps.append(cp)
        for cp in cps:
            cp.wait()
        mine.wait()

    return pl.pallas_call(
        body, name=name, in_specs=[ANY], out_specs=ANY, out_shape=jax.ShapeDtypeStruct(shape, src.dtype),
        scratch_shapes=[pltpu.SemaphoreType.DMA((N_DEV - 1,)), pltpu.SemaphoreType.DMA((N_DEV - 1,)), pltpu.SemaphoreType.DMA],
    )(src)


def _adamw(w, m, v, parts, name):
    R, C = w.shape
    P = parts.shape[0]
    tr = _pick(R, (256, 128, 64, 32, 16, 8))
    c1, c2 = 1.0 - B1 ** STEP, 1.0 - B2 ** STEP

    def body(w_ref, m_ref, v_ref, p_ref, g_ref, d_ref, nm_ref, nv_ref):
        g = p_ref[0].astype(F32)
        for i in range(1, P):
            g = g + p_ref[i].astype(F32)
        wv = w_ref[...]
        nm = B1 * m_ref[...] + (1.0 - B1) * g
        nv = B2 * v_ref[...] + (1.0 - B2) * (g * g)
        g_ref[...] = g
        nm_ref[...] = nm
        nv_ref[...] = nv
        d_ref[...] = -LR * ((nm / c1) / (jnp.sqrt(nv / c2) + AEPS) + WD * wv)

    s2 = pl.BlockSpec((tr, C), lambda i: (i, 0))
    return pl.pallas_call(
        body, name=name, grid=(R // tr,),
        in_specs=[s2, s2, s2, pl.BlockSpec((P, tr, C), lambda i: (0, i, 0))], out_specs=[s2] * 4,
        out_shape=[jax.ShapeDtypeStruct((R, C), F32)] * 4,
        compiler_params=_params(("parallel",)),
    )(w, m, v, parts)


def _colsum(a, name):
    def body(a_ref, o_ref):
        o_ref[...] = jnp.broadcast_to(jnp.sum(a_ref[...], axis=0, keepdims=True), o_ref.shape)

    return pl.pallas_call(body, name=name, out_shape=jax.ShapeDtypeStruct((8, 128), F32))(a)


def _rows128(a):
    f = a.reshape(-1)
    return jnp.pad(f, (0, (-f.shape[0]) % 128)).reshape(-1, 128)


def _to_aligned(w):
    return jnp.concatenate([w[..., 0:4096], w[..., 4112:5392], w[..., 4096:4112],
                            jnp.zeros(w.shape[:-1] + (IN_PAD - IN_COLS,), w.dtype)], axis=-1)


def _from_aligned(w):
    return jnp.concatenate([w[..., 0:4096], w[..., 5376:5392], w[..., 4096:5376]], axis=-1)


def kernel(x, c, ln_mix, ln_ffn, w_ada, b_ada, w_in, dn_conv_w, dn_a_log, dn_dt_bias, dn_norm_w, attn_sinks, w_out, w_gate_up, w_down, ln_final, loss_target, m_ln_mix, m_ln_ffn, m_w_ada, m_b_ada, m_w_in, m_dn_conv_w, m_dn_a_log, m_dn_dt_bias, m_dn_norm_w, m_attn_sinks, m_w_out, m_w_gate_up, m_w_down, m_ln_final, v_ln_mix, v_ln_ffn, v_w_ada, v_b_ada, v_w_in, v_dn_conv_w, v_dn_a_log, v_dn_dt_bias, v_dn_norm_w, v_attn_sinks, v_w_out, v_w_gate_up, v_w_down, v_ln_final):
    T = x.shape[1]
    L = ln_mix.shape[0]
    me = 4 * lax.axis_index("x") + 2 * lax.axis_index("y") + lax.axis_index("c")
    xs = x[0]
    tgt = loss_target[0]

    g_in = _exchange(w_in.astype(BF16).reshape(L * D, -1), True, "ag_w_in").reshape(N_DEV, L, D, -1)
    g_out = _exchange(w_out.astype(BF16).reshape(-1, D), True, "ag_w_out").reshape(N_DEV, L, -1, D)
    g_gu = _exchange(w_gate_up.astype(BF16).reshape(L * D, -1), True, "ag_w_gu").reshape(N_DEV, L, D, -1)
    g_dn = _exchange(w_down.astype(BF16).reshape(-1, D), True, "ag_w_down").reshape(N_DEV, L, -1, D)
    g_cv = _exchange(dn_conv_w.reshape(L * CONV_K, -1), True, "ag_conv").reshape(N_DEV, L, CONV_K, -1)
    c_all = _exchange(jnp.pad(c, ((0, 7), (0, 0))), True, "ag_c")[:, 0, :]
    W_in = [_to_aligned(jnp.transpose(g_in[:, l], (1, 0, 2)).reshape(D, IN_COLS)) for l in range(L)]
    W_out = [g_out[:, l].reshape(D, D) for l in range(L)]
    W_gu = [jnp.transpose(g_gu[:, l], (1, 0, 2)).reshape(D, 2 * FFN) for l in range(L)]
    W_dn = [g_dn[:, l].reshape(FFN, D) for l in range(L)]
    W_cv = [jnp.transpose(g_cv[:, l], (1, 0, 2)).reshape(CONV_K, 3 * DN_W) for l in range(L)]

    c_act = _ew_fwd(lambda v: (_silu(v),), [(jnp.pad(c_all, ((0, 8), (0, 0))), True, False)], [(D, F32, False)], tm=16, name="c_act")[0]
    mods = []
    for l in range(L):
        ms = _mm(c_act, w_ada[l], "nn", F32, f"mod_mm{l}")
        ga = _exchange(ms, True, f"ag_mod{l}")
        mods.append(lax.dynamic_index_in_dim(ga, me, axis=1, keepdims=False).reshape(1, 6 * D))
    row = lambda a: a.reshape(1, -1)
    seg = lambda a, i: a[:, i * D:(i + 1) * D]

    half = AT_HD // 2
    inv_freq = 10000.0 ** (-jnp.arange(half, dtype=F32) * 2.0 / AT_HD)
    ang = jnp.arange(T, dtype=jnp.int32).astype(F32)[:, None] * inv_freq[None, :]
    cos = jnp.concatenate([jnp.cos(ang)] * 2, axis=-1)
    sin = jnp.concatenate([jnp.sin(ang)] * 2, axis=-1)
    ii = jnp.arange(AT_HD)
    rot = jnp.where(ii[:, None] == ii[None, :] + half, -1.0, 0.0) + jnp.where(ii[:, None] + half == ii[None, :], 1.0, 0.0)
    rot = rot.astype(F32)
    heads = lambda a, nh: jnp.transpose(a.reshape(T, nh, AT_HD), (1, 0, 2))
    unheads = lambda a: jnp.transpose(a, (1, 0, 2)).reshape(T, -1)
    pad16 = lambda a: jnp.pad(row(a), ((0, 0), (8, 128 - 16)))

    saved = []
    xc = xs
    for l in range(L):
        mod, bmod = mods[l], row(b_ada[l])
        s = {"x": xc}
        nm_ops = lambda xx, ln, a, b: [(xx, True, False), (row(ln), False, False), (seg(mod, a), False, False),
                                       (seg(bmod, a), False, False), (seg(mod, b), False, False), (seg(bmod, b), False, False)]
        h1 = _ew_fwd(lambda *a: _f_normmod(*a)[:1], nm_ops(xc, ln_mix[l], 1, 0), [(D, BF16, False)], tm=256, name=f"normmod1_{l}")[0]
        proj = _mm(h1, W_in[l], "nn", F32, f"mm_in{l}")
        qkv, z, aq = proj[:, :3072], proj[:, 3072:4096], proj[:, 4096:5120]
        ak, av, ba = proj[:, 5120:5248], proj[:, 5248:5376], proj[:, 5376:5504]
        conv = _conv_fwd(qkv, W_cv[l], f"conv{l}")
        cv3 = [(conv[:, i * DN_W:(i + 1) * DN_W], True, True) for i in range(3)]
        qn, kn, vn = _ew_fwd(_f_dnpre, cv3, [(DN_W, F32, True)] * 3, tm=512, ncol=DN_HEADS, name=f"dnpre{l}")
        bg_ops = [(ba, True, False), (pad16(dn_a_log[l]), False, False), (pad16(dn_dt_bias[l]), False, False)]
        bg = _ew_fwd(_f_bg, bg_ops, [(128, F32, False)], tm=1024, name=f"bg{l}")[0]
        o, s_saved = _chunk_fwd(qn, kn, vn, bg, f"chunk{l}")
        post_ops = [(o, True, True), (z, True, True), (row(dn_norm_w[l]), False, False)]
        dn_out = _ew_fwd(_f_dnpost, post_ops, [(DN_W, BF16, True)], tm=512, ncol=DN_HEADS, name=f"dnpost{l}")[0]
        qh, kh, vh = heads(aq, AT_QH), heads(ak, AT_KVH), heads(av, AT_KVH)
        sk = jnp.broadcast_to(attn_sinks[l][:, None, None], (AT_QH, 1, 128))
        at_out = unheads(_attn_fwd(qh, kh, vh, sk, cos, sin, rot, f"attn{l}")).astype(BF16)
        cat = jnp.concatenate([dn_out, at_out], axis=-1)
        mix = _mm(cat, W_out[l], "nn", F32, f"mm_out{l}")
        rg_ops = lambda xx, br, a: [(xx, True, False), (br, True, False), (seg(mod, a), False, False), (seg(bmod, a), False, False)]
        x1 = _ew_fwd(_f_resgate, rg_ops(xc, mix, 2), [(D, F32, False)], tm=256, name=f"resgate1_{l}")[0]
        h2 = _ew_fwd(lambda *a: _f_normmod(*a)[:1], nm_ops(x1, ln_ffn[l], 4, 3), [(D, BF16, False)], tm=256, name=f"normmod2_{l}")[0]
        gu = _mm(h2, W_gu[l], "nn", F32, f"mm_gu{l}")
        act = _ew_fwd(_f_swiglu, [(gu[:, :FFN], True, True), (gu[:, FFN:], True, True)], [(FFN, BF16, True)], tm=512, ncol=11, name=f"swiglu{l}")[0]
        down = _mm(act, W_dn[l], "nn", F32, f"mm_down{l}")
        x2 = _ew_fwd(_f_resgate, rg_ops(x1, down, 5), [(D, F32, False)], tm=256, name=f"resgate2_{l}")[0]
        s.update(h1=h1, qkv=qkv, z=z, ba=ba, conv=conv, qn=qn, kn=kn, vn=vn, bg=bg, o=o, s_saved=s_saved, qh=qh, kh=kh, vh=vh,
                 sk=sk, cat=cat, mix=mix, x1=x1, h2=h2, gu=gu, act=act, down=down, bg_ops=bg_ops, post_ops=post_ops, cv3=cv3,
                 nm1=nm_ops(xc, ln_mix[l], 1, 0), nm2=nm_ops(x1, ln_ffn[l], 4, 3), rg1=rg_ops(xc, mix, 2), rg2=rg_ops(x1, down, 5))
        saved.append(s)
        xc = x2

    fin_ops = [(xc, True, False), (tgt, True, False), (row(ln_final), False, False)]
    lrow = _ew_fwd(_f_final, fin_ops, [(128, F32, False)], tm=256, name="loss_rows")[0]
    loss = lax.psum(_colsum(lrow, "loss_sum")[0, 0], ("x", "y", "c"))
    dx, d_ln_final = _ew_bwd(_f_final, fin_ops, [(jnp.ones((T, 128), F32) / 128.0, False)], [0, 2], tm=256, name="loss_bwd")

    small = {k: [None] * L for k in ("ln_mix", "ln_ffn", "mod", "a_log", "dt", "norm_w", "sinks", "conv")}
    big = {k: [None] * L for k in ("w_in", "w_out", "w_gu", "w_dn")}
    for l in reversed(range(L)):
        s = saved[l]
        ddown, dgt_f = _ew_bwd(_f_resgate, s["rg2"], [(dx, False)], [1, 2], tm=256, name=f"resgate2_bwd{l}", gdt=[BF16, F32])
        big["w_dn"][l] = _mm(s["act"], ddown, "tn", BF16, f"wg_down{l}")
        dact = _mm(ddown, W_dn[l], "nt", F32, f"dg_down{l}")
        dgate, dup = _ew_bwd(_f_swiglu, [(s["gu"][:, :FFN], True, True), (s["gu"][:, FFN:], True, True)], [(dact, True)], [0, 1],
                             tm=512, ncol=11, name=f"swiglu_bwd{l}", gdt=[BF16, BF16])
        dgu = jnp.concatenate([dgate, dup], axis=-1)
        big["w_gu"][l] = _mm(s["h2"], dgu, "tn", BF16, f"wg_gu{l}")
        dh2 = _mm(dgu, W_gu[l], "nt", F32, f"dg_gu{l}")
        dx1, dln_f, dsc_f, dsh_f = _ew_bwd(_f_normmod, s["nm2"], [(dh2, False), (dx, False)], [0, 1, 2, 4], tm=256, name=f"normmod2_bwd{l}")
        dmix, dgt_m = _ew_bwd(_f_resgate, s["rg1"], [(dx1, False)], [1, 2], tm=256, name=f"resgate1_bwd{l}", gdt=[BF16, F32])
        big["w_out"][l] = _mm(s["cat"], dmix, "tn", BF16, f"wg_out{l}")
        dcat = _mm(dmix, W_out[l], "nt", F32, f"dg_out{l}")
        d_dn, d_at = dcat[:, :DN_W], dcat[:, DN_W:]
        dqh, dkp, dkc, dvp, dvc, dsk = _attn_bwd(s["qh"], s["kh"], s["vh"], s["sk"], cos, sin, rot, heads(d_at, AT_QH), f"attn_bwd{l}")
        dkh = _kv_combine(dkc, dkp, f"dk_comb{l}")
        dvh = _kv_combine(dvc, dvp, f"dv_comb{l}")
        do, dz, dnw = _ew_bwd(_f_dnpost, s["post_ops"], [(d_dn, True)], [0, 1, 2], tm=512, ncol=DN_HEADS, name=f"dnpost_bwd{l}")
        dqn, dkn, dvn, dbg = _chunk_bwd(s["qn"], s["kn"], s["vn"], s["bg"], s["s_saved"], do, f"chunk_bwd{l}")
        dcq, dck, dcv = _ew_bwd(_f_dnpre, s["cv3"], [(dqn, True), (dkn, True), (dvn, True)], [0, 1, 2], tm=512, ncol=DN_HEADS, name=f"dnpre_bwd{l}")
        dba, dalog, ddt = _ew_bwd(_f_bg, s["bg_ops"], [(dbg, False)], [0, 1, 2], tm=1024, name=f"bg_bwd{l}")
        dqkv, dcw = _conv_bwd(s["qkv"], jnp.concatenate([dcq, dck, dcv], axis=-1), W_cv[l], f"conv_bwd{l}")
        dproj = jnp.concatenate([dqkv, dz, unheads(dqh), unheads(dkh), unheads(dvh), dba], axis=-1).astype(BF16)
        big["w_in"][l] = _mm(s["h1"], dproj, "tn", BF16, f"wg_in{l}")
        dh1 = _mm(dproj, W_in[l], "nt", F32, f"dg_in{l}")
        dx, dln_m, dsc_m, dsh_m = _ew_bwd(_f_normmod, s["nm1"], [(dh1, False), (dx1, False)], [0, 1, 2, 4], tm=256, name=f"normmod1_bwd{l}")
        small["ln_mix"][l], small["ln_ffn"][l] = dln_m, dln_f
        small["mod"][l] = jnp.concatenate([dsh_m, dsc_m, dgt_m, dsh_f, dsc_f, dgt_f], axis=-1)
        small["a_log"][l], small["dt"][l] = dalog[:, 8:16], ddt[:, 8:16]
        small["norm_w"][l], small["sinks"][l], small["conv"][l] = dnw, dsk[:, 0, 0], dcw

    cat0 = lambda xs_: jnp.concatenate([_rows128(a) for a in xs_], axis=0)
    stk = lambda k: jnp.stack(small[k])
    pack = cat0([stk("ln_mix"), stk("ln_ffn"), stk("mod"), stk("a_log"), stk("dt"), stk("norm_w"), stk("sinks"), d_ln_final, stk("conv")])
    n_small = pack.shape[0] - L * CONV_K * 3 * DN_W // 128
    pack = jnp.pad(pack, ((0, (-pack.shape[0]) % 8), (0, 0)))
    gp = _exchange(pack, True, "ag_small")
    parts_small = gp[:, :n_small]
    dmod_all = gp[:, 2 * L * D // 128:2 * L * D // 128 + L * 6 * D // 128].reshape(N_DEV, L, 6 * D)
    conv_all = gp[:, n_small:n_small + L * CONV_K * 3 * DN_W // 128].reshape(N_DEV, L * CONV_K, 3 * DN_W)
    parts_conv = lax.dynamic_slice_in_dim(conv_all, me * (3 * DN_W // N_DEV), 3 * DN_W // N_DEV, axis=2)

    def shards(gs, cols):
        g = jnp.stack(gs)
        if cols:
            return jnp.transpose(g.reshape(L, g.shape[1], N_DEV, -1), (2, 0, 1, 3)).reshape(N_DEV, L * g.shape[1], -1)
        return jnp.transpose(g.reshape(L, N_DEV, -1, g.shape[2]), (1, 0, 2, 3)).reshape(N_DEV, -1, g.shape[2])

    p_in = _exchange(shards([_from_aligned(g) for g in big["w_in"]], True), False, "a2a_w_in")
    p_out = _exchange(shards(big["w_out"], False), False, "a2a_w_out")
    p_gu = _exchange(shards(big["w_gu"], True), False, "a2a_w_gu")
    p_dn = _exchange(shards(big["w_dn"], False), False, "a2a_w_down")
    dmod_mine = lax.dynamic_slice_in_dim(dmod_all, me * (6 * D // N_DEV), 6 * D // N_DEV, axis=2)
    g_ada = jnp.stack([_mm(c_act, jnp.pad(dmod_mine[:, l], ((0, 8), (0, 0))), "tn", F32, f"wg_ada{l}") for l in range(L)])

    def upd(w, m, v, parts, name):
        shp = w.shape
        r = lambda a: a.reshape(-1, shp[-1])
        return [o_.reshape(shp) for o_ in _adamw(r(w), r(m), r(v), parts.reshape(parts.shape[0], -1, shp[-1]), name)]

    res = {}
    res["w_ada"] = upd(w_ada, m_w_ada, v_w_ada, g_ada[None], "adamw_ada")
    res["w_in"] = upd(w_in, m_w_in, v_w_in, p_in, "adamw_in")
    res["dn_conv_w"] = upd(dn_conv_w, m_dn_conv_w, v_dn_conv_w, parts_conv, "adamw_conv")
    res["w_out"] = upd(w_out, m_w_out, v_w_out, p_out, "adamw_out")
    res["w_gate_up"] = upd(w_gate_up, m_w_gate_up, v_w_gate_up, p_gu, "adamw_gu")
    res["w_down"] = upd(w_down, m_w_down, v_w_down, p_dn, "adamw_down")
    names_s = ["ln_mix", "ln_ffn", "b_ada", "dn_a_log", "dn_dt_bias", "dn_norm_w", "attn_sinks", "ln_final"]
    ws = [ln_mix, ln_ffn, b_ada, dn_a_log, dn_dt_bias, dn_norm_w, attn_sinks, ln_final]
    ms = [m_ln_mix, m_ln_ffn, m_b_ada, m_dn_a_log, m_dn_dt_bias, m_dn_norm_w, m_attn_sinks, m_ln_final]
    vs = [v_ln_mix, v_ln_ffn, v_b_ada, v_dn_a_log, v_dn_dt_bias, v_dn_norm_w, v_attn_sinks, v_ln_final]
    padr = lambda a: jnp.pad(a, ((0, (-a.shape[0]) % 8), (0, 0)))
    vpad = jnp.pad(cat0(vs), ((0, (-n_small) % 8), (0, 0)), constant_values=1.0)
    outs_s = _adamw(padr(cat0(ws)), padr(cat0(ms)), vpad, jnp.pad(parts_small, ((0, 0), (0, (-n_small) % 8), (0, 0))), "adamw_small")
    off = 0
    for nme, wv in zip(names_s, ws):
        nrow = -(-wv.size // 128)
        res[nme] = [o_[off:off + nrow].reshape(-1)[:wv.size].reshape(wv.shape) for o_ in outs_s]
        off += nrow

    order = ["ln_mix", "ln_ffn", "w_ada", "b_ada", "w_in", "dn_conv_w", "dn_a_log", "dn_dt_bias", "dn_norm_w", "attn_sinks",
             "w_out", "w_gate_up", "w_down", "ln_final"]
    return (loss, dx[None], *[res[n][0] for n in order], *[res[n][1] for n in order], *[res[n][2] for n in order],
            *[res[n][3] for n in order])
```

```python
import functools

import jax
import jax.numpy as jnp
from jax import lax
from jax.experimental import pallas as pl
from jax.experimental.pallas import tpu as pltpu

F32, BF16 = jnp.float32, jnp.bfloat16
HI = lax.Precision.HIGHEST
MESH = pl.DeviceIdType.MESH
ANY = pl.BlockSpec(memory_space=pl.ANY)

N_DEV = 8
D = 2048
DN_HEADS, DN_HD = 8, 128
DN_W = 1024
CONV_K = 4
CHUNK = 64
AT_HD, AT_QH, AT_KVH = 64, 16, 2
AT_W = 1024
WINDOW = 128
FFN = 5632
IN_COLS = 5392
IN_PAD = 5632
EPS = 1e-6
NEG = -1e30
LR, B1, B2, AEPS, WD, STEP = 0.001, 0.9, 0.999, 1e-08, 0.01, 10
VMEM_LIMIT = 56 * 1024 * 1024


def _pick(n, cands):
    for c in cands:
        if n % c == 0:
            return c
    return n


def _params(sem):
    return pltpu.CompilerParams(dimension_semantics=sem, vmem_limit_bytes=VMEM_LIMIT)


_DN = {"nn": (((1,), (0,)), ((), ())), "nt": (((1,), (1,)), ((), ())), "tn": (((0,), (0,)), ((), ()))}


def _mm(a, b, mode, out_dtype, name):
    if mode == "nn":
        (M, K), (_, N) = a.shape, b.shape
    elif mode == "nt":
        (M, K), (N, _) = a.shape, b.shape
    else:
        (K, M), (_, N) = a.shape, b.shape
    tm = _pick(M, (1024, 512, 256, 128, 64, 32, 16))
    tn = _pick(N, (1024, 512, 256, 128))
    tk = _pick(K, (512, 256, 128, 64, 32, 16))
    nk = K // tk
    dn = _DN[mode]

    def body(a_ref, b_ref, o_ref, acc):
        k = pl.program_id(2)

        @pl.when(k == 0)
        def _():
            acc[...] = jnp.zeros_like(acc)

        acc[...] += lax.dot_general(a_ref[...].astype(BF16), b_ref[...].astype(BF16), dn, preferred_element_type=F32)

        @pl.when(k == nk - 1)
        def _():
            o_ref[...] = acc[...].astype(o_ref.dtype)

    a_spec = pl.BlockSpec((tk, tm), lambda i, j, k: (k, i)) if mode == "tn" else pl.BlockSpec((tm, tk), lambda i, j, k: (i, k))
    b_spec = pl.BlockSpec((tn, tk), lambda i, j, k: (j, k)) if mode == "nt" else pl.BlockSpec((tk, tn), lambda i, j, k: (k, j))
    return pl.pallas_call(
        body, name=name, grid=(M // tm, N // tn, nk),
        in_specs=[a_spec, b_spec], out_specs=pl.BlockSpec((tm, tn), lambda i, j, k: (i, j)),
        out_shape=jax.ShapeDtypeStruct((M, N), out_dtype),
        scratch_shapes=[pltpu.VMEM((tm, tn), F32)],
        compiler_params=_params(("parallel", "parallel", "arbitrary")),
    )(a, b)


def _ew_spec(arr, rt, cb, tm, ncol):
    R, W = arr.shape
    return pl.BlockSpec((tm if rt else R, W // ncol if cb else W), lambda j, i: (i if rt else 0, j if cb else 0))


def _ew_fwd(f, ops, outs, *, tm, ncol=1, name):
    M = next(a.shape[0] for a, rt, _ in ops if rt)
    n_in = len(ops)

    def body(*refs):
        res = f(*[r[...].astype(F32) for r in refs[:n_in]])
        for o, r in zip(refs[n_in:], res):
            o[...] = r.astype(o.dtype)

    return pl.pallas_call(
        body, name=name, grid=(ncol, M // tm),
        in_specs=[_ew_spec(a, rt, cb, tm, ncol) for a, rt, cb in ops],
        out_specs=[pl.BlockSpec((tm, w // ncol if cb else w), lambda j, i, cb=cb: (i, j if cb else 0)) for w, _, cb in outs],
        out_shape=[jax.ShapeDtypeStruct((M, w), dt) for w, dt, _ in outs],
        compiler_params=_params(("parallel", "parallel")),
    )(*[a for a, _, _ in ops])


def _ew_bwd(f, ops, cts, diff, *, tm, ncol=1, name, gdt=None):
    M = next(a.shape[0] for a, rt, _ in ops if rt)
    n_in, n_ct = len(ops), len(cts)
    gdt = gdt or [F32] * len(diff)

    def body(*refs):
        j, i = pl.program_id(0), pl.program_id(1)
        vals = [r[...].astype(F32) for r in refs[:n_in]]

        def fd(*dv):
            full = list(vals)
            for idx, v in zip(diff, dv):
                full[idx] = v
            return tuple(f(*full))

        _, vjp = jax.vjp(fd, *[vals[idx] for idx in diff])
        gs = vjp(tuple(r[...].astype(F32) for r in refs[n_in:n_in + n_ct]))
        for idx, g, gref in zip(diff, gs, refs[n_in + n_ct:]):
            _, rt, cb = ops[idx]
            if rt:
                gref[...] = g.astype(gref.dtype)
            else:
                first = (i == 0) if cb else jnp.logical_and(i == 0, j == 0)

                @pl.when(first)
                def _(gref=gref):
                    gref[...] = jnp.zeros_like(gref)

                gref[...] += g

    return pl.pallas_call(
        body, name=name, grid=(ncol, M // tm),
        in_specs=[_ew_spec(a, rt, cb, tm, ncol) for a, rt, cb in ops]
        + [pl.BlockSpec((tm, a.shape[1] // ncol if cb else a.shape[1]), lambda j, i, cb=cb: (i, j if cb else 0)) for a, cb in cts],
        out_specs=[_ew_spec(*ops[idx], tm, ncol) for idx in diff],
        out_shape=[jax.ShapeDtypeStruct(ops[idx][0].shape, dt if ops[idx][1] else F32) for idx, dt in zip(diff, gdt)],
        compiler_params=_params(("arbitrary", "arbitrary")),
    )(*[a for a, _, _ in ops], *[a for a, _ in cts])


def _silu(x):
    return x * jax.nn.sigmoid(x)


def _f_normmod(x, ln, sc, bsc, sh, bsh):
    y = x * lax.rsqrt(jnp.mean(x * x, axis=-1, keepdims=True) + EPS) * ln
    return y * (1.0 + (sc + bsc)) + (sh + bsh), x


def _f_resgate(x, br, gt, bgt):
    return (x + (gt + bgt) * br,)


def _f_swiglu(gate, up):
    return (_silu(gate) * up,)


def _f_final(x, tgt, ln):
    y = x * lax.rsqrt(jnp.mean(x * x, axis=-1, keepdims=True) + EPS) * ln
    e = y - tgt
    return (jnp.broadcast_to(0.5 * jnp.mean(e * e, axis=-1, keepdims=True), (x.shape[0], 128)),)


def _f_bg(ba, alog, dt):
    col = lax.broadcasted_iota(jnp.int32, ba.shape, 1)
    z = ba + dt
    sp = jnp.maximum(z, 0.0) + jnp.log(1.0 + jnp.exp(-jnp.abs(z)))
    return (jnp.where(col < 8, jax.nn.sigmoid(ba), jnp.where(col < 16, -jnp.exp(alog) * sp, 0.0)),)


def _l2n(x):
    return x * lax.rsqrt(jnp.sum(x * x, axis=-1, keepdims=True) + EPS)


def _f_dnpre(cq, ck, cv):
    return _l2n(_silu(cq)) * (DN_HD ** -0.5), _l2n(_silu(ck)), _silu(cv)


def _f_dnpost(o, z, nw):
    return (o * lax.rsqrt(jnp.mean(o * o, axis=-1, keepdims=True) + EPS) * nw * _silu(z),)


def _conv_fwd(x, w, name):
    T, Cw = x.shape
    tm = 512

    def body(x_ref, h_ref, w_ref, o_ref):
        i = pl.program_id(1)
        cur, halo, wv = x_ref[...], h_ref[...], w_ref[...]
        halo = jnp.where(i > 0, halo, 0.0)
        row = lax.broadcasted_iota(jnp.int32, (8, 128), 0)
        acc = wv[3:4, :] * cur
        for s in (1, 2, 3):
            r = pltpu.roll(cur, s, 0)
            top = jnp.where(row < s, pltpu.roll(halo, s, 0), r[:8])
            acc += wv[3 - s:4 - s, :] * jnp.concatenate([top, r[8:]], axis=0)
        o_ref[...] = acc

    return pl.pallas_call(
        body, name=name, grid=(Cw // 128, T // tm),
        in_specs=[pl.BlockSpec((tm, 128), lambda j, i: (i, j)),
                  pl.BlockSpec((8, 128), lambda j, i: (jnp.maximum(i * (tm // 8) - 1, 0), j)),
                  pl.BlockSpec((4, 128), lambda j, i: (0, j))],
        out_specs=pl.BlockSpec((tm, 128), lambda j, i: (i, j)),
        out_shape=jax.ShapeDtypeStruct((T, Cw), F32),
        compiler_params=_params(("parallel", "parallel")),
    )(x, x, w)


def _conv_bwd(x, dy, w, name):
    T, Cw = x.shape
    tm = 512
    nt = T // tm

    def body(x_ref, h_ref, dy_ref, n_ref, w_ref, dx_ref, dw_ref):
        i = pl.program_id(1)
        cur, dcur, wv = x_ref[...], dy_ref[...], w_ref[...]
        halo = jnp.where(i > 0, h_ref[...], 0.0)
        nxt = jnp.where(i < nt - 1, n_ref[...], 0.0)
        row = lax.broadcasted_iota(jnp.int32, (8, 128), 0)

        @pl.when(i == 0)
        def _():
            dw_ref[...] = jnp.zeros_like(dw_ref)

        dx = wv[3:4, :] * dcur
        dw_ref[3:4, :] += jnp.sum(dcur * cur, axis=0, keepdims=True)
        for s in (1, 2, 3):
            r = pltpu.roll(cur, s, 0)
            top = jnp.where(row < s, pltpu.roll(halo, s, 0), r[:8])
            xs = jnp.concatenate([top, r[8:]], axis=0)
            dw_ref[3 - s:4 - s, :] += jnp.sum(dcur * xs, axis=0, keepdims=True)
            rf = pltpu.roll(dcur, tm - s, 0)
            bot = jnp.where(row >= 8 - s, pltpu.roll(nxt, 8 - s, 0), rf[tm - 8:])
            dx += wv[3 - s:4 - s, :] * jnp.concatenate([rf[:tm - 8], bot], axis=0)
        dx_ref[...] = dx

    return pl.pallas_call(
        body, name=name, grid=(Cw // 128, nt),
        in_specs=[pl.BlockSpec((tm, 128), lambda j, i: (i, j)),
                  pl.BlockSpec((8, 128), lambda j, i: (jnp.maximum(i * (tm // 8) - 1, 0), j)),
                  pl.BlockSpec((tm, 128), lambda j, i: (i, j)),
                  pl.BlockSpec((8, 128), lambda j, i: (jnp.minimum((i + 1) * (tm // 8), T // 8 - 1), j)),
                  pl.BlockSpec((4, 128), lambda j, i: (0, j))],
        out_specs=[pl.BlockSpec((tm, 128), lambda j, i: (i, j)), pl.BlockSpec((4, 128), lambda j, i: (0, j))],
        out_shape=[jax.ShapeDtypeStruct((T, Cw), F32), jax.ShapeDtypeStruct((4, Cw), F32)],
        compiler_params=_params(("arbitrary", "arbitrary")),
    )(x, x, dy, dy, w)


def _dot(a, b):
    return jnp.dot(a, b, precision=HI, preferred_element_type=F32)


def _dot_t(a, b):
    return lax.dot_general(a, b, (((1,), (1,)), ((), ())), precision=HI, preferred_element_type=F32)


def _bdot(a, b):
    return jnp.dot(a.astype(BF16), b.astype(BF16), preferred_element_type=F32)


def _bdot_t(a, b):
    return lax.dot_general(a.astype(BF16), b.astype(BF16), (((1,), (1,)), ((), ())), preferred_element_type=F32)


def _chunk_f(q, k, v, bg, S, h):
    C = CHUNK
    lane = lax.broadcasted_iota(jnp.int32, (C, 128), 1)
    beta = jnp.sum(jnp.where(lane == h, bg, 0.0), axis=1, keepdims=True)
    g = jnp.sum(jnp.where(lane == h + 8, bg, 0.0), axis=1, keepdims=True)
    ri = lax.broadcasted_iota(jnp.int32, (C, C), 0)
    ci = lax.broadcasted_iota(jnp.int32, (C, C), 1)
    causal, strict, eye = ri >= ci, ri > ci, ri == ci
    g_row = jnp.sum(jnp.where(eye, g, 0.0), axis=0, keepdims=True)
    gc_col = jnp.sum(jnp.where(causal, g_row, 0.0), axis=1, keepdims=True)
    gc_row = jnp.sum(jnp.where(ri <= ci, g, 0.0), axis=0, keepdims=True)
    gc_last = jnp.sum(g, axis=0, keepdims=True)
    decay = jnp.exp(jnp.where(causal, gc_col - gc_row, NEG))
    kb, vb = k * beta, v * beta
    n1 = -jnp.where(strict, _bdot_t(kb, k) * decay, 0.0)
    t_inv = jnp.where(eye, 1.0, 0.0) + n1
    p = n1
    for _ in range(5):
        p = _dot(p, p)
        t_inv = t_inv + _dot(t_inv, p)
    egc = jnp.exp(gc_col)
    w = _bdot(t_inv, kb * egc)
    u = _bdot(t_inv, vb)
    intra = jnp.where(causal, _bdot_t(q, k) * decay, 0.0)
    kd = k * jnp.exp(gc_last - gc_col)
    v_new = u - _bdot(w, S)
    o = _bdot(q * egc, S) + _bdot(intra, v_new)
    S_next = S * jnp.exp(gc_last) + lax.dot_general(kd.astype(BF16), v_new.astype(BF16), (((0,), (0,)), ((), ())), preferred_element_type=F32)
    return o, S_next


def _chunk_fwd(q, k, v, bg, hb, name):
    T = q.shape[0]
    N = T // CHUNK

    def body(q_ref, k_ref, v_ref, bg_ref, o_ref, s_ref, S):
        n, j = pl.program_id(0), pl.program_id(1)
        bgv = bg_ref[...]
        for e in range(hb):
            h = j * hb + e
            cols = slice(e * DN_HD, (e + 1) * DN_HD)

            @pl.when(n == 0)
            def _(h=h):
                S[h] = jnp.zeros((DN_HD, DN_HD), F32)

            s_in = S[h]
            s_ref[0, e] = s_in
            o, s_next = _chunk_f(q_ref[:, cols], k_ref[:, cols], v_ref[:, cols], bgv, s_in, h)
            o_ref[:, cols] = o
            S[h] = s_next

    qs = pl.BlockSpec((CHUNK, hb * DN_HD), lambda n, j: (n, j))
    return pl.pallas_call(
        body, name=name, grid=(N, DN_HEADS // hb),
        in_specs=[qs, qs, qs, pl.BlockSpec((CHUNK, 128), lambda n, j: (n, 0))],
        out_specs=[qs, pl.BlockSpec((1, hb, DN_HD, DN_HD), lambda n, j: (n, j, 0, 0))],
        out_shape=[jax.ShapeDtypeStruct((T, DN_W), F32), jax.ShapeDtypeStruct((N, DN_HEADS, DN_HD, DN_HD), F32)],
        scratch_shapes=[pltpu.VMEM((DN_HEADS, DN_HD, DN_HD), F32)],
        compiler_params=_params(("arbitrary", "arbitrary")),
    )(q, k, v, bg)


def _chunk_bwd(q, k, v, bg, s_saved, do, hb, name):
    T = q.shape[0]
    N = T // CHUNK

    def body(q_ref, k_ref, v_ref, bg_ref, s_ref, do_ref, dq_ref, dk_ref, dv_ref, dbg_ref, dS):
        n, j = pl.program_id(0), pl.program_id(1)
        bgv = bg_ref[...]
        dbg_sum = None
        for e in range(hb):
            h = j * hb + e
            cols = slice(e * DN_HD, (e + 1) * DN_HD)

            @pl.when(n == 0)
            def _(h=h):
                dS[h] = jnp.zeros((DN_HD, DN_HD), F32)

            _, vjp = jax.vjp(functools.partial(_chunk_f, h=h), q_ref[:, cols], k_ref[:, cols], v_ref[:, cols], bgv, s_ref[0, e])
            dq, dk, dv, dbg, ds_in = vjp((do_ref[:, cols], dS[h]))
            dq_ref[:, cols] = dq
            dk_ref[:, cols] = dk
            dv_ref[:, cols] = dv
            dS[h] = ds_in
            dbg_sum = dbg if dbg_sum is None else dbg_sum + dbg

        @pl.when(j == 0)
        def _():
            dbg_ref[...] = jnp.zeros_like(dbg_ref)

        dbg_ref[...] += dbg_sum

    qs = pl.BlockSpec((CHUNK, hb * DN_HD), lambda n, j: (N - 1 - n, j))
    bs = pl.BlockSpec((CHUNK, 128), lambda n, j: (N - 1 - n, 0))
    return pl.pallas_call(
        body, name=name, grid=(N, DN_HEADS // hb),
        in_specs=[qs, qs, qs, bs, pl.BlockSpec((1, hb, DN_HD, DN_HD), lambda n, j: (N - 1 - n, j, 0, 0)), qs],
        out_specs=[qs, qs, qs, bs],
        out_shape=[jax.ShapeDtypeStruct((T, DN_W), F32)] * 3 + [jax.ShapeDtypeStruct((T, 128), F32)],
        scratch_shapes=[pltpu.VMEM((DN_HEADS, DN_HD, DN_HD), F32)],
        compiler_params=_params(("arbitrary", "arbitrary")),
    )(q, k, v, bg, s_saved, do)


GRP = AT_QH // AT_KVH


def _attn_f(q, kp, kc, vp, vc, sinks, cos_c, sin_c, cos_p, sin_p, rot, has_prev):
    def rope(x, c, s):
        return x * c + _dot(x, rot) * s

    kcr, kpr = rope(kc, cos_c, sin_c), rope(kp, cos_p, sin_p)
    r = lax.broadcasted_iota(jnp.int32, (WINDOW, WINDOW), 0)
    j = lax.broadcasted_iota(jnp.int32, (WINDOW, WINDOW), 1)
    in_c, in_p = j <= r, jnp.logical_and(j > r, has_prev)
    lane = lax.broadcasted_iota(jnp.int32, (1, 128), 1)
    outs = []
    for e in range(GRP):
        qr = rope(q[e], cos_c, sin_c)
        sc = jnp.where(in_c, _bdot_t(qr, kcr) * (AT_HD ** -0.5), NEG)
        sp = jnp.where(in_p, _bdot_t(qr, kpr) * (AT_HD ** -0.5), NEG)
        sink = jnp.sum(jnp.where(lane == 0, sinks[e], 0.0), axis=1, keepdims=True)
        m = jnp.maximum(jnp.maximum(jnp.max(sc, axis=1, keepdims=True), jnp.max(sp, axis=1, keepdims=True)), sink)
        pc, pp = jnp.exp(sc - m), jnp.exp(sp - m)
        den = jnp.sum(pc, axis=1, keepdims=True) + jnp.sum(pp, axis=1, keepdims=True) + jnp.exp(sink - m)
        outs.append((_bdot(pc, vc) + _bdot(pp, vp)) / den)
    return tuple(outs)


def _attn_specs(nb):
    qs = pl.BlockSpec((GRP, WINDOW, AT_HD), lambda g, n: (g, n, 0))
    kc = pl.BlockSpec((1, WINDOW, AT_HD), lambda g, n: (g, n, 0))
    kp = pl.BlockSpec((1, WINDOW, AT_HD), lambda g, n: (g, jnp.maximum(n - 1, 0), 0))
    tc = pl.BlockSpec((WINDOW, AT_HD), lambda g, n: (n, 0))
    tp = pl.BlockSpec((WINDOW, AT_HD), lambda g, n: (jnp.maximum(n - 1, 0), 0))
    sk = pl.BlockSpec((GRP, 1, 128), lambda g, n: (g, 0, 0))
    rt = pl.BlockSpec((AT_HD, AT_HD), lambda g, n: (0, 0))
    return qs, kc, kp, tc, tp, sk, rt


def _attn_fwd(q, k, v, sinks, cos, sin, rot, name):
    T = q.shape[1]
    nb = T // WINDOW
    qs, kc, kp, tc, tp, sk, rt = _attn_specs(nb)

    def body(q_ref, kp_ref, kc_ref, vp_ref, vc_ref, sk_ref, cc_ref, sc_ref, cp_ref, sp_ref, rot_ref, o_ref):
        n = pl.program_id(1)
        heads = range(GRP)
        outs = _attn_f(tuple(q_ref[e] for e in heads), kp_ref[0], kc_ref[0], vp_ref[0], vc_ref[0], tuple(sk_ref[e] for e in heads), cc_ref[...], sc_ref[...],
                       cp_ref[...], sp_ref[...], rot_ref[...], n > 0)
        for e in range(GRP):
            o_ref[e] = outs[e]

    return pl.pallas_call(
        body, name=name, grid=(AT_KVH, nb),
        in_specs=[qs, kp, kc, kp, kc, sk, tc, tc, tp, tp, rt], out_specs=qs,
        out_shape=jax.ShapeDtypeStruct(q.shape, F32),
        compiler_params=_params(("parallel", "parallel")),
    )(q, k, k, v, v, sinks, cos, sin, cos, sin, rot)


def _attn_bwd(q, k, v, sinks, cos, sin, rot, do, name):
    T = q.shape[1]
    nb = T // WINDOW
    qs, kc, kp, tc, tp, sk, rt = _attn_specs(nb)

    def body(q_ref, kp_ref, kc_ref, vp_ref, vc_ref, sk_ref, cc_ref, sc_ref, cp_ref, sp_ref, rot_ref, do_ref,
             dq_ref, dkp_ref, dkc_ref, dvp_ref, dvc_ref, dsk_ref):
        n = pl.program_id(1)
        f = functools.partial(_attn_f, cos_c=cc_ref[...], sin_c=sc_ref[...], cos_p=cp_ref[...], sin_p=sp_ref[...],
                              rot=rot_ref[...], has_prev=n > 0)
        heads = range(GRP)
        _, vjp = jax.vjp(f, tuple(q_ref[e] for e in heads), kp_ref[0], kc_ref[0], vp_ref[0], vc_ref[0], tuple(sk_ref[e] for e in heads))
        dq, dkp, dkc, dvp, dvc, dsk = vjp(tuple(do_ref[e] for e in heads))
        dkp_ref[0], dkc_ref[0], dvp_ref[0], dvc_ref[0] = dkp, dkc, dvp, dvc

        @pl.when(n == 0)
        def _():
            dsk_ref[...] = jnp.zeros_like(dsk_ref)

        for e in heads:
            dq_ref[e] = dq[e]
            dsk_ref[e] += dsk[e]

    return pl.pallas_call(
        body, name=name, grid=(AT_KVH, nb),
        in_specs=[qs, kp, kc, kp, kc, sk, tc, tc, tp, tp, rt, qs], out_specs=[qs, kc, kc, kc, kc, sk],
        out_shape=[jax.ShapeDtypeStruct(q.shape, F32)] + [jax.ShapeDtypeStruct(k.shape, F32)] * 4 + [jax.ShapeDtypeStruct(sinks.shape, F32)],
        compiler_params=_params(("arbitrary", "arbitrary")),
    )(q, k, k, v, v, sinks, cos, sin, cos, sin, rot, do)


def _kv_combine(dc, dp, name):
    T = dc.shape[1]
    nb = T // WINDOW

    def body(c_ref, p_ref, o_ref):
        n = pl.program_id(1)
        o_ref[...] = c_ref[...] + jnp.where(n < nb - 1, p_ref[...], 0.0)

    return pl.pallas_call(
        body, name=name, grid=(AT_KVH, nb),
        in_specs=[pl.BlockSpec((1, WINDOW, AT_HD), lambda g, n: (g, n, 0)),
                  pl.BlockSpec((1, WINDOW, AT_HD), lambda g, n: (g, jnp.minimum(n + 1, nb - 1), 0))],
        out_specs=pl.BlockSpec((1, WINDOW, AT_HD), lambda g, n: (g, n, 0)),
        out_shape=jax.ShapeDtypeStruct((AT_KVH, T, AT_HD), F32),
        compiler_params=_params(("parallel", "parallel")),
    )(dc, dp)


def _place():
    x, y, c = lax.axis_index("x"), lax.axis_index("y"), lax.axis_index("c")
    return x, y, c, 4 * x + 2 * y + c


def _peer(x, y, c, k):
    return (x ^ ((k >> 2) & 1), y ^ ((k >> 1) & 1), c ^ (k & 1))


def _exchange(src, gather, name):
    shape = (N_DEV,) + src.shape if gather else src.shape

    def body(s_ref, o_ref, send_sems, recv_sems, lsem):
        x, y, c, me = _place()
        mine = pltpu.make_async_copy(s_ref if gather else s_ref.at[me], o_ref.at[me], lsem)
        mine.start()
        cps = []
        for k in range(1, N_DEV):
            px, py, pc = _peer(x, y, c, k)
            cp = pltpu.make_async_remote_copy(
                src_ref=s_ref if gather else s_ref.at[4 * px + 2 * py + pc], dst_ref=o_ref.at[me],
                send_sem=send_sems.at[k - 1], recv_sem=recv_sems.at[k - 1], device_id=(px, py, pc), device_id_type=MESH)
            cp.start()
            cps.append(cp)
        for cp in cps:
            cp.wait()
        mine.wait()

    return pl.pallas_call(
        body, name=name, in_specs=[ANY], out_specs=ANY, out_shape=jax.ShapeDtypeStruct(shape, src.dtype),
        scratch_shapes=[pltpu.SemaphoreType.DMA((N_DEV - 1,)), pltpu.SemaphoreType.DMA((N_DEV - 1,)), pltpu.SemaphoreType.DMA],
    )(src)


def _adamw(w, m, v, parts, name):
    R, C = w.shape
    P = parts.shape[0]
    tr = _pick(R, (256, 128, 64, 32, 16, 8))
    c1, c2 = 1.0 - B1 ** STEP, 1.0 - B2 ** STEP

    def body(w_ref, m_ref, v_ref, p_ref, g_ref, d_ref, nm_ref, nv_ref):
        g = p_ref[0].astype(F32)
        for i in range(1, P):
            g = g + p_ref[i].astype(F32)
        wv = w_ref[...]
        nm = B1 * m_ref[...] + (1.0 - B1) * g
        nv = B2 * v_ref[...] + (1.0 - B2) * (g * g)
        g_ref[...] = g
        nm_ref[...] = nm
        nv_ref[...] = nv
        d_ref[...] = -LR * ((nm / c1) / (jnp.sqrt(nv / c2) + AEPS) + WD * wv)

    s2 = pl.BlockSpec((tr, C), lambda i: (i, 0))
    return pl.pallas_call(
        body, name=name, grid=(R // tr,),
        in_specs=[s2, s2, s2, pl.BlockSpec((P, tr, C), lambda i: (0, i, 0))], out_specs=[s2] * 4,
        out_shape=[jax.ShapeDtypeStruct((R, C), F32)] * 4,
        compiler_params=_params(("parallel",)),
    )(w, m, v, parts)


def _colsum(a, name):
    def body(a_ref, o_ref):
        o_ref[...] = jnp.broadcast_to(jnp.sum(a_ref[...], axis=0, keepdims=True), o_ref.shape)

    return pl.pallas_call(body, name=name, out_shape=jax.ShapeDtypeStruct((8, 128), F32))(a)


def _rows128(a):
    f = a.reshape(-1)
    return jnp.pad(f, (0, (-f.shape[0]) % 128)).reshape(-1, 128)


def _to_aligned(w):
    return jnp.concatenate([w[..., 0:4096], w[..., 4112:5392], w[..., 4096:4112],
                            jnp.zeros(w.shape[:-1] + (IN_PAD - IN_COLS,), w.dtype)], axis=-1)


def _from_aligned(w):
    return jnp.concatenate([w[..., 0:4096], w[..., 5376:5392], w[..., 4096:5376]], axis=-1)


def kernel(x, c, ln_mix, ln_ffn, w_ada, b_ada, w_in, dn_conv_w, dn_a_log, dn_dt_bias, dn_norm_w, attn_sinks, w_out, w_gate_up, w_down, ln_final, loss_target, m_ln_mix, m_ln_ffn, m_w_ada, m_b_ada, m_w_in, m_dn_conv_w, m_dn_a_log, m_dn_dt_bias, m_dn_norm_w, m_attn_sinks, m_w_out, m_w_gate_up, m_w_down, m_ln_final, v_ln_mix, v_ln_ffn, v_w_ada, v_b_ada, v_w_in, v_dn_conv_w, v_dn_a_log, v_dn_dt_bias, v_dn_norm_w, v_attn_sinks, v_w_out, v_w_gate_up, v_w_down, v_ln_final):
    T = x.shape[1]
    L = ln_mix.shape[0]
    me = 4 * lax.axis_index("x") + 2 * lax.axis_index("y") + lax.axis_index("c")
    xs = x[0]
    tgt = loss_target[0]

    g_in = _exchange(w_in.astype(BF16).reshape(L * D, -1), True, "ag_w_in").reshape(N_DEV, L, D, -1)
    g_out = _exchange(w_out.astype(BF16).reshape(-1, D), True, "ag_w_out").reshape(N_DEV, L, -1, D)
    g_gu = _exchange(w_gate_up.astype(BF16).reshape(L * D, -1), True, "ag_w_gu").reshape(N_DEV, L, D, -1)
    g_dn = _exchange(w_down.astype(BF16).reshape(-1, D), True, "ag_w_down").reshape(N_DEV, L, -1, D)
    g_cv = _exchange(dn_conv_w.reshape(L * CONV_K, -1), True, "ag_conv").reshape(N_DEV, L, CONV_K, -1)
    c_all = _exchange(jnp.pad(c, ((0, 7), (0, 0))), True, "ag_c")[:, 0, :]
    W_in = [_to_aligned(jnp.transpose(g_in[:, l], (1, 0, 2)).reshape(D, IN_COLS)) for l in range(L)]
    W_out = [g_out[:, l].reshape(D, D) for l in range(L)]
    W_gu = [jnp.transpose(g_gu[:, l], (1, 0, 2)).reshape(D, 2 * FFN) for l in range(L)]
    W_dn = [g_dn[:, l].reshape(FFN, D) for l in range(L)]
    W_cv = [jnp.transpose(g_cv[:, l], (1, 0, 2)).reshape(CONV_K, 3 * DN_W) for l in range(L)]

    c_act = _ew_fwd(lambda v: (_silu(v),), [(jnp.pad(c_all, ((0, 8), (0, 0))), True, False)], [(D, F32, False)], tm=16, name="c_act")[0]
    mods = []
    for l in range(L):
        ms = _mm(c_act, w_ada[l], "nn", F32, f"mod_mm{l}")
        ga = _exchange(ms, True, f"ag_mod{l}")
        mods.append(lax.dynamic_index_in_dim(ga, me, axis=1, keepdims=False).reshape(1, 6 * D))
    row = lambda a: a.reshape(1, -1)
    seg = lambda a, i: a[:, i * D:(i + 1) * D]

    half = AT_HD // 2
    inv_freq = 10000.0 ** (-jnp.arange(half, dtype=F32) * 2.0 / AT_HD)
    ang = jnp.arange(T, dtype=jnp.int32).astype(F32)[:, None] * inv_freq[None, :]
    cos = jnp.concatenate([jnp.cos(ang)] * 2, axis=-1)
    sin = jnp.concatenate([jnp.sin(ang)] * 2, axis=-1)
    ii = jnp.arange(AT_HD)
    rot = jnp.where(ii[:, None] == ii[None, :] + half, -1.0, 0.0) + jnp.where(ii[:, None] + half == ii[None, :], 1.0, 0.0)
    rot = rot.astype(F32)
    heads = lambda a, nh: jnp.transpose(a.reshape(T, nh, AT_HD), (1, 0, 2))
    unheads = lambda a: jnp.transpose(a, (1, 0, 2)).reshape(T, -1)
    pad16 = lambda a: jnp.pad(row(a), ((0, 0), (8, 128 - 16)))

    saved = []
    xc = xs
    for l in range(L):
        mod, bmod = mods[l], row(b_ada[l])
        s = {"x": xc}
        nm_ops = lambda xx, ln, a, b: [(xx, True, False), (row(ln), False, False), (seg(mod, a), False, False),
                                       (seg(bmod, a), False, False), (seg(mod, b), False, False), (seg(bmod, b), False, False)]
        h1 = _ew_fwd(lambda *a: _f_normmod(*a)[:1], nm_ops(xc, ln_mix[l], 1, 0), [(D, BF16, False)], tm=256, name=f"normmod1_{l}")[0]
        proj = _mm(h1, W_in[l], "nn", F32, f"mm_in{l}")
        qkv, z, aq = proj[:, :3072], proj[:, 3072:4096], proj[:, 4096:5120]
        ak, av, ba = proj[:, 5120:5248], proj[:, 5248:5376], proj[:, 5376:5504]
        conv = _conv_fwd(qkv, W_cv[l], f"conv{l}")
        cv3 = [(conv[:, i * DN_W:(i + 1) * DN_W], True, True) for i in range(3)]
        qn, kn, vn = _ew_fwd(_f_dnpre, cv3, [(DN_W, F32, True)] * 3, tm=512, ncol=DN_HEADS, name=f"dnpre{l}")
        bg_ops = [(ba, True, False), (pad16(dn_a_log[l]), False, False), (pad16(dn_dt_bias[l]), False, False)]
        bg = _ew_fwd(_f_bg, bg_ops, [(128, F32, False)], tm=1024, name=f"bg{l}")[0]
        o, s_saved = _chunk_fwd(qn, kn, vn, bg, 8, f"chunk{l}")
        post_ops = [(o, True, True), (z, True, True), (row(dn_norm_w[l]), False, False)]
        dn_out = _ew_fwd(_f_dnpost, post_ops, [(DN_W, BF16, True)], tm=512, ncol=DN_HEADS, name=f"dnpost{l}")[0]
        qh, kh, vh = heads(aq, AT_QH), heads(ak, AT_KVH), heads(av, AT_KVH)
        sk = jnp.broadcast_to(attn_sinks[l][:, None, None], (AT_QH, 1, 128))
        at_out = unheads(_attn_fwd(qh, kh, vh, sk, cos, sin, rot, f"attn{l}")).astype(BF16)
        cat = jnp.concatenate([dn_out, at_out], axis=-1)
        mix = _mm(cat, W_out[l], "nn", F32, f"mm_out{l}")
        rg_ops = lambda xx, br, a: [(xx, True, False), (br, True, False), (seg(mod, a), False, False), (seg(bmod, a), False, False)]
        x1 = _ew_fwd(_f_resgate, rg_ops(xc, mix, 2), [(D, F32, False)], tm=256, name=f"resgate1_{l}")[0]
        h2 = _ew_fwd(lambda *a: _f_normmod(*a)[:1], nm_ops(x1, ln_ffn[l], 4, 3), [(D, BF16, False)], tm=256, name=f"normmod2_{l}")[0]
        gu = _mm(h2, W_gu[l], "nn", F32, f"mm_gu{l}")
        act = _ew_fwd(_f_swiglu, [(gu[:, :FFN], True, True), (gu[:, FFN:], True, True)], [(FFN, BF16, True)], tm=512, ncol=11, name=f"swiglu{l}")[0]
        down = _mm(act, W_dn[l], "nn", F32, f"mm_down{l}")
        x2 = _ew_fwd(_f_resgate, rg_ops(x1, down, 5), [(D, F32, False)], tm=256, name=f"resgate2_{l}")[0]
        s.update(h1=h1, qkv=qkv, z=z, ba=ba, conv=conv, qn=qn, kn=kn, vn=vn, bg=bg, o=o, s_saved=s_saved, qh=qh, kh=kh, vh=vh,
                 sk=sk, cat=cat, mix=mix, x1=x1, h2=h2, gu=gu, act=act, down=down, bg_ops=bg_ops, post_ops=post_ops, cv3=cv3,
                 nm1=nm_ops(xc, ln_mix[l], 1, 0), nm2=nm_ops(x1, ln_ffn[l], 4, 3), rg1=rg_ops(xc, mix, 2), rg2=rg_ops(x1, down, 5))
        saved.append(s)
        xc = x2

    fin_ops = [(xc, True, False), (tgt, True, False), (row(ln_final), False, False)]
    lrow = _ew_fwd(_f_final, fin_ops, [(128, F32, False)], tm=256, name="loss_rows")[0]
    loss = lax.psum(_colsum(lrow, "loss_sum")[0, 0], ("x", "y", "c"))
    dx, d_ln_final = _ew_bwd(_f_final, fin_ops, [(jnp.ones((T, 128), F32) / 128.0, False)], [0, 2], tm=256, name="loss_bwd")

    small = {k: [None] * L for k in ("ln_mix", "ln_ffn", "mod", "a_log", "dt", "norm_w", "sinks", "conv")}
    big = {k: [None] * L for k in ("w_in", "w_out", "w_gu", "w_dn")}
    for l in reversed(range(L)):
        s = saved[l]
        ddown, dgt_f = _ew_bwd(_f_resgate, s["rg2"], [(dx, False)], [1, 2], tm=256, name=f"resgate2_bwd{l}", gdt=[BF16, F32])
        big["w_dn"][l] = _mm(s["act"], ddown, "tn", BF16, f"wg_down{l}")
        dact = _mm(ddown, W_dn[l], "nt", F32, f"dg_down{l}")
        dgate, dup = _ew_bwd(_f_swiglu, [(s["gu"][:, :FFN], True, True), (s["gu"][:, FFN:], True, True)], [(dact, True)], [0, 1],
                             tm=512, ncol=11, name=f"swiglu_bwd{l}", gdt=[BF16, BF16])
        dgu = jnp.concatenate([dgate, dup], axis=-1)
        big["w_gu"][l] = _mm(s["h2"], dgu, "tn", BF16, f"wg_gu{l}")
        dh2 = _mm(dgu, W_gu[l], "nt", F32, f"dg_gu{l}")
        dx1, dln_f, dsc_f, dsh_f = _ew_bwd(_f_normmod, s["nm2"], [(dh2, False), (dx, False)], [0, 1, 2, 4], tm=256, name=f"normmod2_bwd{l}")
        dmix, dgt_m = _ew_bwd(_f_resgate, s["rg1"], [(dx1, False)], [1, 2], tm=256, name=f"resgate1_bwd{l}", gdt=[BF16, F32])
        big["w_out"][l] = _mm(s["cat"], dmix, "tn", BF16, f"wg_out{l}")
        dcat = _mm(dmix, W_out[l], "nt", F32, f"dg_out{l}")
        d_dn, d_at = dcat[:, :DN_W], dcat[:, DN_W:]
        dqh, dkp, dkc, dvp, dvc, dsk = _attn_bwd(s["qh"], s["kh"], s["vh"], s["sk"], cos, sin, rot, heads(d_at, AT_QH), f"attn_bwd{l}")
        dkh = _kv_combine(dkc, dkp, f"dk_comb{l}")
        dvh = _kv_combine(dvc, dvp, f"dv_comb{l}")
        do, dz, dnw = _ew_bwd(_f_dnpost, s["post_ops"], [(d_dn, True)], [0, 1, 2], tm=512, ncol=DN_HEADS, name=f"dnpost_bwd{l}")
        dqn, dkn, dvn, dbg = _chunk_bwd(s["qn"], s["kn"], s["vn"], s["bg"], s["s_saved"], do, 4, f"chunk_bwd{l}")
        dcq, dck, dcv = _ew_bwd(_f_dnpre, s["cv3"], [(dqn, True), (dkn, True), (dvn, True)], [0, 1, 2], tm=512, ncol=DN_HEADS, name=f"dnpre_bwd{l}")
        dba, dalog, ddt = _ew_bwd(_f_bg, s["bg_ops"], [(dbg, False)], [0, 1, 2], tm=1024, name=f"bg_bwd{l}")
        dqkv, dcw = _conv_bwd(s["qkv"], jnp.concatenate([dcq, dck, dcv], axis=-1), W_cv[l], f"conv_bwd{l}")
        dproj = jnp.concatenate([dqkv, dz, unheads(dqh), unheads(dkh), unheads(dvh), dba, jnp.zeros((T, IN_PAD - 5504), F32)],
                                axis=-1).astype(BF16)
        big["w_in"][l] = _mm(s["h1"], dproj, "tn", BF16, f"wg_in{l}")
        dh1 = _mm(dproj, W_in[l], "nt", F32, f"dg_in{l}")
        dx, dln_m, dsc_m, dsh_m = _ew_bwd(_f_normmod, s["nm1"], [(dh1, False), (dx1, False)], [0, 1, 2, 4], tm=256, name=f"normmod1_bwd{l}")
        small["ln_mix"][l], small["ln_ffn"][l] = dln_m, dln_f
        small["mod"][l] = jnp.concatenate([dsh_m, dsc_m, dgt_m, dsh_f, dsc_f, dgt_f], axis=-1)
        small["a_log"][l], small["dt"][l] = dalog[:, 8:16], ddt[:, 8:16]
        small["norm_w"][l], small["sinks"][l], small["conv"][l] = dnw, dsk[:, 0, 0], dcw

    cat0 = lambda xs_: jnp.concatenate([_rows128(a) for a in xs_], axis=0)
    stk = lambda k: jnp.stack(small[k])
    pack = cat0([stk("ln_mix"), stk("ln_ffn"), stk("mod"), stk("a_log"), stk("dt"), stk("norm_w"), stk("sinks"), d_ln_final, stk("conv")])
    n_small = pack.shape[0] - L * CONV_K * 3 * DN_W // 128
    pack = jnp.pad(pack, ((0, (-pack.shape[0]) % 8), (0, 0)))
    gp = _exchange(pack, True, "ag_small")
    parts_small = gp[:, :n_small]
    dmod_all = gp[:, 2 * L * D // 128:2 * L * D // 128 + L * 6 * D // 128].reshape(N_DEV, L, 6 * D)
    conv_all = gp[:, n_small:n_small + L * CONV_K * 3 * DN_W // 128].reshape(N_DEV, L * CONV_K, 3 * DN_W)
    parts_conv = lax.dynamic_slice_in_dim(conv_all, me * (3 * DN_W // N_DEV), 3 * DN_W // N_DEV, axis=2)

    def shards(gs, cols):
        g = jnp.stack(gs)
        if cols:
            return jnp.transpose(g.reshape(L, g.shape[1], N_DEV, -1), (2, 0, 1, 3)).reshape(N_DEV, L * g.shape[1], -1)
        return jnp.transpose(g.reshape(L, N_DEV, -1, g.shape[2]), (1, 0, 2, 3)).reshape(N_DEV, -1, g.shape[2])

    p_in = _exchange(shards([_from_aligned(g) for g in big["w_in"]], True), False, "a2a_w_in")
    p_out = _exchange(shards(big["w_out"], False), False, "a2a_w_out")
    p_gu = _exchange(shards(big["w_gu"], True), False, "a2a_w_gu")
    p_dn = _exchange(shards(big["w_dn"], False), False, "a2a_w_down")
    dmod_mine = lax.dynamic_slice_in_dim(dmod_all, me * (6 * D // N_DEV), 6 * D // N_DEV, axis=2)
    g_ada = jnp.stack([_mm(c_act, jnp.pad(dmod_mine[:, l], ((0, 8), (0, 0))), "tn", F32, f"wg_ada{l}") for l in range(L)])

    def upd(w, m, v, parts, name):
        shp = w.shape
        r = lambda a: a.reshape(-1, shp[-1])
        return [o_.reshape(shp) for o_ in _adamw(r(w), r(m), r(v), parts.reshape(parts.shape[0], -1, shp[-1]), name)]

    res = {}
    res["w_ada"] = upd(w_ada, m_w_ada, v_w_ada, g_ada[None], "adamw_ada")
    res["w_in"] = upd(w_in, m_w_in, v_w_in, p_in, "adamw_in")
    res["dn_conv_w"] = upd(dn_conv_w, m_dn_conv_w, v_dn_conv_w, parts_conv, "adamw_conv")
    res["w_out"] = upd(w_out, m_w_out, v_w_out, p_out, "adamw_out")
    res["w_gate_up"] = upd(w_gate_up, m_w_gate_up, v_w_gate_up, p_gu, "adamw_gu")
    res["w_down"] = upd(w_down, m_w_down, v_w_down, p_dn, "adamw_down")
    names_s = ["ln_mix", "ln_ffn", "b_ada", "dn_a_log", "dn_dt_bias", "dn_norm_w", "attn_sinks", "ln_final"]
    ws = [ln_mix, ln_ffn, b_ada, dn_a_log, dn_dt_bias, dn_norm_w, attn_sinks, ln_final]
    ms = [m_ln_mix, m_ln_ffn, m_b_ada, m_dn_a_log, m_dn_dt_bias, m_dn_norm_w, m_attn_sinks, m_ln_final]
    vs = [v_ln_mix, v_ln_ffn, v_b_ada, v_dn_a_log, v_dn_dt_bias, v_dn_norm_w, v_attn_sinks, v_ln_final]
    padr = lambda a: jnp.pad(a, ((0, (-a.shape[0]) % 8), (0, 0)))
    vpad = jnp.pad(cat0(vs), ((0, (-n_small) % 8), (0, 0)), constant_values=1.0)
    outs_s = _adamw(padr(cat0(ws)), padr(cat0(ms)), vpad, jnp.pad(parts_small, ((0, 0), (0, (-n_small) % 8), (0, 0))), "adamw_small")
    off = 0
    for nme, wv in zip(names_s, ws):
        nrow = -(-wv.size // 128)
        res[nme] = [o_[off:off + nrow].reshape(-1)[:wv.size].reshape(wv.shape) for o_ in outs_s]
        off += nrow

    order = ["ln_mix", "ln_ffn", "w_ada", "b_ada", "w_in", "dn_conv_w", "dn_a_log", "dn_dt_bias", "dn_norm_w", "attn_sinks",
             "w_out", "w_gate_up", "w_down", "ln_final"]
    return (loss, dx[None], *[res[n][0] for n in order], *[res[n][1] for n in order], *[res[n][2] for n in order],
            *[res[n][3] for n in order])
```

```python
import functools

import jax
import jax.numpy as jnp
from jax import lax
from jax.experimental import pallas as pl
from jax.experimental.pallas import tpu as pltpu

F32, BF16 = jnp.float32, jnp.bfloat16
HI = lax.Precision.HIGHEST
MESH = pl.DeviceIdType.MESH
ANY = pl.BlockSpec(memory_space=pl.ANY)

N_DEV = 8
D = 2048
DN_HEADS, DN_HD = 8, 128
DN_W = 1024
CONV_K = 4
CHUNK = 64
AT_HD, AT_QH, AT_KVH = 64, 16, 2
AT_W = 1024
WINDOW = 128
FFN = 5632
IN_COLS = 5392
IN_PAD = 5632
EPS = 1e-6
NEG = -1e30
LR, B1, B2, AEPS, WD, STEP = 0.001, 0.9, 0.999, 1e-08, 0.01, 10
VMEM_LIMIT = 56 * 1024 * 1024


def _pick(n, cands):
    for c in cands:
        if n % c == 0:
            return c
    return n


def _params(sem):
    return pltpu.CompilerParams(dimension_semantics=sem, vmem_limit_bytes=VMEM_LIMIT)


_DN = {"nn": (((1,), (0,)), ((), ())), "nt": (((1,), (1,)), ((), ())), "tn": (((0,), (0,)), ((), ()))}


def _mm(a, b, mode, out_dtype, name):
    if mode == "nn":
        (M, K), (_, N) = a.shape, b.shape
    elif mode == "nt":
        (M, K), (N, _) = a.shape, b.shape
    else:
        (K, M), (_, N) = a.shape, b.shape
    tm = _pick(M, (1024, 512, 256, 128, 64, 32, 16))
    tn = _pick(N, (1024, 512, 256, 128))
    tk = K if K <= 2048 else _pick(K, (1024, 512, 256, 128))
    nk = K // tk
    dn = _DN[mode]

    def body(a_ref, b_ref, o_ref, *acc):
        part = lax.dot_general(a_ref[...].astype(BF16), b_ref[...].astype(BF16), dn, preferred_element_type=F32)
        if nk == 1:
            o_ref[...] = part.astype(o_ref.dtype)
            return
        k = pl.program_id(2)

        @pl.when(k == 0)
        def _():
            acc[0][...] = part

        @pl.when(k > 0)
        def _():
            acc[0][...] += part

        @pl.when(k == nk - 1)
        def _():
            o_ref[...] = acc[0][...].astype(o_ref.dtype)

    a_spec = pl.BlockSpec((tk, tm), lambda i, j, k: (k, i)) if mode == "tn" else pl.BlockSpec((tm, tk), lambda i, j, k: (i, k))
    b_spec = pl.BlockSpec((tn, tk), lambda i, j, k: (j, k)) if mode == "nt" else pl.BlockSpec((tk, tn), lambda i, j, k: (k, j))
    return pl.pallas_call(
        body, name=name, grid=(M // tm, N // tn, nk),
        in_specs=[a_spec, b_spec], out_specs=pl.BlockSpec((tm, tn), lambda i, j, k: (i, j)),
        out_shape=jax.ShapeDtypeStruct((M, N), out_dtype),
        scratch_shapes=[pltpu.VMEM((tm, tn), F32)] if nk > 1 else [],
        compiler_params=_params(("parallel", "parallel", "arbitrary")),
    )(a, b)


def _ew_spec(arr, rt, cb, tm, ncol):
    R, W = arr.shape
    return pl.BlockSpec((tm if rt else R, W // ncol if cb else W), lambda j, i: (i if rt else 0, j if cb else 0))


def _ew_fwd(f, ops, outs, *, tm, ncol=1, name):
    M = next(a.shape[0] for a, rt, _ in ops if rt)
    n_in = len(ops)

    def body(*refs):
        res = f(*[r[...].astype(F32) for r in refs[:n_in]])
        for o, r in zip(refs[n_in:], res):
            o[...] = r.astype(o.dtype)

    return pl.pallas_call(
        body, name=name, grid=(ncol, M // tm),
        in_specs=[_ew_spec(a, rt, cb, tm, ncol) for a, rt, cb in ops],
        out_specs=[pl.BlockSpec((tm, w // ncol if cb else w), lambda j, i, cb=cb: (i, j if cb else 0)) for w, _, cb in outs],
        out_shape=[jax.ShapeDtypeStruct((M, w), dt) for w, dt, _ in outs],
        compiler_params=_params(("parallel", "parallel")),
    )(*[a for a, _, _ in ops])


def _ew_bwd(f, ops, cts, diff, *, tm, ncol=1, name, gdt=None):
    M = next(a.shape[0] for a, rt, _ in ops if rt)
    n_in, n_ct = len(ops), len(cts)
    gdt = gdt or [F32] * len(diff)

    def body(*refs):
        j, i = pl.program_id(0), pl.program_id(1)
        vals = [r[...].astype(F32) for r in refs[:n_in]]

        def fd(*dv):
            full = list(vals)
            for idx, v in zip(diff, dv):
                full[idx] = v
            return tuple(f(*full))

        _, vjp = jax.vjp(fd, *[vals[idx] for idx in diff])
        gs = vjp(tuple(r[...].astype(F32) for r in refs[n_in:n_in + n_ct]))
        for idx, g, gref in zip(diff, gs, refs[n_in + n_ct:]):
            _, rt, cb = ops[idx]
            if rt:
                gref[...] = g.astype(gref.dtype)
            else:
                first = (i == 0) if cb else jnp.logical_and(i == 0, j == 0)

                @pl.when(first)
                def _(gref=gref):
                    gref[...] = jnp.zeros_like(gref)

                gref[...] += g

    return pl.pallas_call(
        body, name=name, grid=(ncol, M // tm),
        in_specs=[_ew_spec(a, rt, cb, tm, ncol) for a, rt, cb in ops]
        + [pl.BlockSpec((tm, a.shape[1] // ncol if cb else a.shape[1]), lambda j, i, cb=cb: (i, j if cb else 0)) for a, cb in cts],
        out_specs=[_ew_spec(*ops[idx], tm, ncol) for idx in diff],
        out_shape=[jax.ShapeDtypeStruct(ops[idx][0].shape, dt if ops[idx][1] else F32) for idx, dt in zip(diff, gdt)],
        compiler_params=_params(("arbitrary", "arbitrary")),
    )(*[a for a, _, _ in ops], *[a for a, _ in cts])


def _silu(x):
    return x * jax.nn.sigmoid(x)


def _f_normmod(x, ln, sc, bsc, sh, bsh):
    y = x * lax.rsqrt(jnp.mean(x * x, axis=-1, keepdims=True) + EPS) * ln
    return y * (1.0 + (sc + bsc)) + (sh + bsh), x


def _f_resgate(x, br, gt, bgt):
    return (x + (gt + bgt) * br,)


def _f_swiglu(gate, up):
    return (_silu(gate) * up,)


def _f_final(x, tgt, ln):
    y = x * lax.rsqrt(jnp.mean(x * x, axis=-1, keepdims=True) + EPS) * ln
    e = y - tgt
    return (jnp.broadcast_to(0.5 * jnp.mean(e * e, axis=-1, keepdims=True), (x.shape[0], 128)),)


def _f_bg(ba, alog, dt):
    col = lax.broadcasted_iota(jnp.int32, ba.shape, 1)
    z = ba + dt
    sp = jnp.maximum(z, 0.0) + jnp.log(1.0 + jnp.exp(-jnp.abs(z)))
    return (jnp.where(col < 8, jax.nn.sigmoid(ba), jnp.where(col < 16, -jnp.exp(alog) * sp, 0.0)),)


def _l2n(x):
    return x * lax.rsqrt(jnp.sum(x * x, axis=-1, keepdims=True) + EPS)


def _f_dnpre(cq, ck, cv):
    return _l2n(_silu(cq)) * (DN_HD ** -0.5), _l2n(_silu(ck)), _silu(cv)


def _f_dnpost(o, z, nw):
    return (o * lax.rsqrt(jnp.mean(o * o, axis=-1, keepdims=True) + EPS) * nw * _silu(z),)


def _conv_fwd(x, w, name):
    T, Cw = x.shape
    tm = 512

    def body(x_ref, h_ref, w_ref, o_ref):
        i = pl.program_id(1)
        cur, halo, wv = x_ref[...], h_ref[...], w_ref[...]
        halo = jnp.where(i > 0, halo, 0.0)
        row = lax.broadcasted_iota(jnp.int32, (8, 128), 0)
        acc = wv[3:4, :] * cur
        for s in (1, 2, 3):
            r = pltpu.roll(cur, s, 0)
            top = jnp.where(row < s, pltpu.roll(halo, s, 0), r[:8])
            acc += wv[3 - s:4 - s, :] * jnp.concatenate([top, r[8:]], axis=0)
        o_ref[...] = acc

    return pl.pallas_call(
        body, name=name, grid=(Cw // 128, T // tm),
        in_specs=[pl.BlockSpec((tm, 128), lambda j, i: (i, j)),
                  pl.BlockSpec((8, 128), lambda j, i: (jnp.maximum(i * (tm // 8) - 1, 0), j)),
                  pl.BlockSpec((4, 128), lambda j, i: (0, j))],
        out_specs=pl.BlockSpec((tm, 128), lambda j, i: (i, j)),
        out_shape=jax.ShapeDtypeStruct((T, Cw), F32),
        compiler_params=_params(("parallel", "parallel")),
    )(x, x, w)


def _conv_bwd(x, dy, w, name):
    T, Cw = x.shape
    tm = 512
    nt = T // tm

    def body(x_ref, h_ref, dy_ref, n_ref, w_ref, dx_ref, dw_ref):
        i = pl.program_id(1)
        cur, dcur, wv = x_ref[...], dy_ref[...], w_ref[...]
        halo = jnp.where(i > 0, h_ref[...], 0.0)
        nxt = jnp.where(i < nt - 1, n_ref[...], 0.0)
        row = lax.broadcasted_iota(jnp.int32, (8, 128), 0)

        @pl.when(i == 0)
        def _():
            dw_ref[...] = jnp.zeros_like(dw_ref)

        dx = wv[3:4, :] * dcur
        dw_ref[3:4, :] += jnp.sum(dcur * cur, axis=0, keepdims=True)
        for s in (1, 2, 3):
            r = pltpu.roll(cur, s, 0)
            top = jnp.where(row < s, pltpu.roll(halo, s, 0), r[:8])
            xs = jnp.concatenate([top, r[8:]], axis=0)
            dw_ref[3 - s:4 - s, :] += jnp.sum(dcur * xs, axis=0, keepdims=True)
            rf = pltpu.roll(dcur, tm - s, 0)
            bot = jnp.where(row >= 8 - s, pltpu.roll(nxt, 8 - s, 0), rf[tm - 8:])
            dx += wv[3 - s:4 - s, :] * jnp.concatenate([rf[:tm - 8], bot], axis=0)
        dx_ref[...] = dx

    return pl.pallas_call(
        body, name=name, grid=(Cw // 128, nt),
        in_specs=[pl.BlockSpec((tm, 128), lambda j, i: (i, j)),
                  pl.BlockSpec((8, 128), lambda j, i: (jnp.maximum(i * (tm // 8) - 1, 0), j)),
                  pl.BlockSpec((tm, 128), lambda j, i: (i, j)),
                  pl.BlockSpec((8, 128), lambda j, i: (jnp.minimum((i + 1) * (tm // 8), T // 8 - 1), j)),
                  pl.BlockSpec((4, 128), lambda j, i: (0, j))],
        out_specs=[pl.BlockSpec((tm, 128), lambda j, i: (i, j)), pl.BlockSpec((4, 128), lambda j, i: (0, j))],
        out_shape=[jax.ShapeDtypeStruct((T, Cw), F32), jax.ShapeDtypeStruct((4, Cw), F32)],
        compiler_params=_params(("arbitrary", "arbitrary")),
    )(x, x, dy, dy, w)


def _dot(a, b):
    return jnp.dot(a, b, precision=HI, preferred_element_type=F32)


def _dot_t(a, b):
    return lax.dot_general(a, b, (((1,), (1,)), ((), ())), precision=HI, preferred_element_type=F32)


def _bdot(a, b):
    return jnp.dot(a.astype(BF16), b.astype(BF16), preferred_element_type=F32)


def _bdot_t(a, b):
    return lax.dot_general(a.astype(BF16), b.astype(BF16), (((1,), (1,)), ((), ())), preferred_element_type=F32)


def _each(f, *lists):
    return [f(*a) for a in zip(*lists)]


def _chunk_f(qs, ks, vs, bg, Ss, hs):
    C = CHUNK
    lane = lax.broadcasted_iota(jnp.int32, (C, 128), 1)
    betas = [jnp.sum(jnp.where(lane == h, bg, 0.0), axis=1, keepdims=True) for h in hs]
    gs = [jnp.sum(jnp.where(lane == h + 8, bg, 0.0), axis=1, keepdims=True) for h in hs]
    ri = lax.broadcasted_iota(jnp.int32, (C, C), 0)
    ci = lax.broadcasted_iota(jnp.int32, (C, C), 1)
    causal, strict, eye = ri >= ci, ri > ci, ri == ci
    g_rows = _each(lambda g: jnp.sum(jnp.where(eye, g, 0.0), axis=0, keepdims=True), gs)
    gc_cols = _each(lambda gr: jnp.sum(jnp.where(causal, gr, 0.0), axis=1, keepdims=True), g_rows)
    gc_rows = _each(lambda g: jnp.sum(jnp.where(ri <= ci, g, 0.0), axis=0, keepdims=True), gs)
    gc_lasts = _each(lambda g: jnp.sum(g, axis=0, keepdims=True), gs)
    decays = _each(lambda c, r: jnp.exp(jnp.where(causal, c - r, NEG)), gc_cols, gc_rows)
    kbs = _each(jnp.multiply, ks, betas)
    vbs = _each(jnp.multiply, vs, betas)
    ps = _each(lambda kb, k, d: -jnp.where(strict, _bdot_t(kb, k) * d, 0.0), kbs, ks, decays)
    tis = [jnp.where(eye, 1.0, 0.0) + p for p in ps]
    for _ in range(5):
        ps = _each(lambda p: _dot(p, p), ps)
        tis = _each(lambda t, p: t + _dot(t, p), tis, ps)
    egcs = _each(jnp.exp, gc_cols)
    ws = _each(lambda t, kb, e: _bdot(t, kb * e), tis, kbs, egcs)
    us = _each(_bdot, tis, vbs)
    intras = _each(lambda q, k, d: jnp.where(causal, _bdot_t(q, k) * d, 0.0), qs, ks, decays)
    kds = _each(lambda k, gl, gc: k * jnp.exp(gl - gc), ks, gc_lasts, gc_cols)
    vns = _each(lambda u, w, S: u - _bdot(w, S), us, ws, Ss)
    os_ = _each(lambda q, e, S, i, vn: _bdot(q * e, S) + _bdot(i, vn), qs, egcs, Ss, intras, vns)
    Sn = _each(lambda S, gl, kd, vn: S * jnp.exp(gl) + lax.dot_general(
        kd.astype(BF16), vn.astype(BF16), (((0,), (0,)), ((), ())), preferred_element_type=F32), Ss, gc_lasts, kds, vns)
    return os_, Sn


def _chunk_fwd(q, k, v, bg, hb, name):
    T = q.shape[0]
    N = T // CHUNK

    def body(q_ref, k_ref, v_ref, bg_ref, o_ref, s_ref, S):
        n, j = pl.program_id(0), pl.program_id(1)
        mine = pl.ds(j * hb, hb)

        @pl.when(n == 0)
        def _():
            S[mine] = jnp.zeros((hb, DN_HD, DN_HD), F32)

        bgv = bg_ref[...]
        s_all = S[mine]
        s_ref[0] = s_all
        cols = [slice(e * DN_HD, (e + 1) * DN_HD) for e in range(hb)]
        os_, nxt = _chunk_f([q_ref[:, c] for c in cols], [k_ref[:, c] for c in cols], [v_ref[:, c] for c in cols], bgv,
                            [s_all[e] for e in range(hb)], [j * hb + e for e in range(hb)])
        for e in range(hb):
            o_ref[:, cols[e]] = os_[e]
            S[j * hb + e] = nxt[e]

    qs = pl.BlockSpec((CHUNK, hb * DN_HD), lambda n, j: (n, j))
    return pl.pallas_call(
        body, name=name, grid=(N, DN_HEADS // hb),
        in_specs=[qs, qs, qs, pl.BlockSpec((CHUNK, 128), lambda n, j: (n, 0))],
        out_specs=[qs, pl.BlockSpec((1, hb, DN_HD, DN_HD), lambda n, j: (n, j, 0, 0))],
        out_shape=[jax.ShapeDtypeStruct((T, DN_W), F32), jax.ShapeDtypeStruct((N, DN_HEADS, DN_HD, DN_HD), F32)],
        scratch_shapes=[pltpu.VMEM((DN_HEADS, DN_HD, DN_HD), F32)],
        compiler_params=_params(("arbitrary", "arbitrary")),
    )(q, k, v, bg)


def _chunk_bwd(q, k, v, bg, s_saved, do, hb, name):
    T = q.shape[0]
    N = T // CHUNK

    def body(q_ref, k_ref, v_ref, bg_ref, s_ref, do_ref, dq_ref, dk_ref, dv_ref, dbg_ref, dS):
        n, j = pl.program_id(0), pl.program_id(1)
        mine = pl.ds(j * hb, hb)

        @pl.when(n == 0)
        def _():
            dS[mine] = jnp.zeros((hb, DN_HD, DN_HD), F32)

        bgv = bg_ref[...]
        ds_all = dS[mine]
        cols = [slice(e * DN_HD, (e + 1) * DN_HD) for e in range(hb)]
        _, vjp = jax.vjp(functools.partial(_chunk_f, hs=[j * hb + e for e in range(hb)]),
                         [q_ref[:, c] for c in cols], [k_ref[:, c] for c in cols], [v_ref[:, c] for c in cols], bgv,
                         [s_ref[0, e] for e in range(hb)])
        dq, dk, dv, dbg_sum, nxt = vjp(([do_ref[:, c] for c in cols], [ds_all[e] for e in range(hb)]))
        for e in range(hb):
            dq_ref[:, cols[e]] = dq[e]
            dk_ref[:, cols[e]] = dk[e]
            dv_ref[:, cols[e]] = dv[e]
            dS[j * hb + e] = nxt[e]

        @pl.when(j == 0)
        def _():
            dbg_ref[...] = jnp.zeros_like(dbg_ref)

        dbg_ref[...] += dbg_sum

    qs = pl.BlockSpec((CHUNK, hb * DN_HD), lambda n, j: (N - 1 - n, j))
    bs = pl.BlockSpec((CHUNK, 128), lambda n, j: (N - 1 - n, 0))
    return pl.pallas_call(
        body, name=name, grid=(N, DN_HEADS // hb),
        in_specs=[qs, qs, qs, bs, pl.BlockSpec((1, hb, DN_HD, DN_HD), lambda n, j: (N - 1 - n, j, 0, 0)), qs],
        out_specs=[qs, qs, qs, bs],
        out_shape=[jax.ShapeDtypeStruct((T, DN_W), F32)] * 3 + [jax.ShapeDtypeStruct((T, 128), F32)],
        scratch_shapes=[pltpu.VMEM((DN_HEADS, DN_HD, DN_HD), F32)],
        compiler_params=_params(("arbitrary", "arbitrary")),
    )(q, k, v, bg, s_saved, do)


GRP = AT_QH // AT_KVH


def _attn_f(q, kp, kc, vp, vc, sinks, cos_c, sin_c, cos_p, sin_p, rot, has_prev):
    def rope(x, c, s):
        return x * c + _dot(x, rot) * s

    kcr, kpr = rope(kc, cos_c, sin_c), rope(kp, cos_p, sin_p)
    r = lax.broadcasted_iota(jnp.int32, (WINDOW, WINDOW), 0)
    j = lax.broadcasted_iota(jnp.int32, (WINDOW, WINDOW), 1)
    in_c, in_p = j <= r, jnp.logical_and(j > r, has_prev)
    lane = lax.broadcasted_iota(jnp.int32, (1, 128), 1)
    qrs = [rope(qe, cos_c, sin_c) for qe in q]
    scs = [jnp.where(in_c, _bdot_t(qr, kcr) * (AT_HD ** -0.5), NEG) for qr in qrs]
    sps = [jnp.where(in_p, _bdot_t(qr, kpr) * (AT_HD ** -0.5), NEG) for qr in qrs]
    snk = [jnp.sum(jnp.where(lane == 0, s, 0.0), axis=1, keepdims=True) for s in sinks]
    ms = _each(lambda sc, sp, s: jnp.maximum(jnp.maximum(jnp.max(sc, axis=1, keepdims=True), jnp.max(sp, axis=1, keepdims=True)), s),
               scs, sps, snk)
    pcs = _each(lambda sc, m: jnp.exp(sc - m), scs, ms)
    pps = _each(lambda sp, m: jnp.exp(sp - m), sps, ms)
    dens = _each(lambda pc, pp, s, m: jnp.sum(pc, axis=1, keepdims=True) + jnp.sum(pp, axis=1, keepdims=True) + jnp.exp(s - m),
                 pcs, pps, snk, ms)
    return tuple(_each(lambda pc, pp, den: (_bdot(pc, vc) + _bdot(pp, vp)) / den, pcs, pps, dens))


def _attn_specs(nb):
    qs = pl.BlockSpec((GRP, WINDOW, AT_HD), lambda g, n: (g, n, 0))
    kc = pl.BlockSpec((1, WINDOW, AT_HD), lambda g, n: (g, n, 0))
    kp = pl.BlockSpec((1, WINDOW, AT_HD), lambda g, n: (g, jnp.maximum(n - 1, 0), 0))
    tc = pl.BlockSpec((WINDOW, AT_HD), lambda g, n: (n, 0))
    tp = pl.BlockSpec((WINDOW, AT_HD), lambda g, n: (jnp.maximum(n - 1, 0), 0))
    sk = pl.BlockSpec((GRP, 1, 128), lambda g, n: (g, 0, 0))
    rt = pl.BlockSpec((AT_HD, AT_HD), lambda g, n: (0, 0))
    return qs, kc, kp, tc, tp, sk, rt


def _attn_fwd(q, k, v, sinks, cos, sin, rot, name):
    T = q.shape[1]
    nb = T // WINDOW
    qs, kc, kp, tc, tp, sk, rt = _attn_specs(nb)

    def body(q_ref, kp_ref, kc_ref, vp_ref, vc_ref, sk_ref, cc_ref, sc_ref, cp_ref, sp_ref, rot_ref, o_ref):
        n = pl.program_id(1)
        heads = range(GRP)
        outs = _attn_f(tuple(q_ref[e] for e in heads), kp_ref[0], kc_ref[0], vp_ref[0], vc_ref[0], tuple(sk_ref[e] for e in heads), cc_ref[...], sc_ref[...],
                       cp_ref[...], sp_ref[...], rot_ref[...], n > 0)
        for e in range(GRP):
            o_ref[e] = outs[e]

    return pl.pallas_call(
        body, name=name, grid=(AT_KVH, nb),
        in_specs=[qs, kp, kc, kp, kc, sk, tc, tc, tp, tp, rt], out_specs=qs,
        out_shape=jax.ShapeDtypeStruct(q.shape, F32),
        compiler_params=_params(("parallel", "parallel")),
    )(q, k, k, v, v, sinks, cos, sin, cos, sin, rot)


def _attn_bwd(q, k, v, sinks, cos, sin, rot, do, name):
    T = q.shape[1]
    nb = T // WINDOW
    qs, kc, kp, tc, tp, sk, rt = _attn_specs(nb)

    def body(q_ref, kp_ref, kc_ref, vp_ref, vc_ref, sk_ref, cc_ref, sc_ref, cp_ref, sp_ref, rot_ref, do_ref,
             dq_ref, dkp_ref, dkc_ref, dvp_ref, dvc_ref, dsk_ref):
        n = pl.program_id(1)
        f = functools.partial(_attn_f, cos_c=cc_ref[...], sin_c=sc_ref[...], cos_p=cp_ref[...], sin_p=sp_ref[...],
                              rot=rot_ref[...], has_prev=n > 0)
        heads = range(GRP)
        _, vjp = jax.vjp(f, tuple(q_ref[e] for e in heads), kp_ref[0], kc_ref[0], vp_ref[0], vc_ref[0], tuple(sk_ref[e] for e in heads))
        dq, dkp, dkc, dvp, dvc, dsk = vjp(tuple(do_ref[e] for e in heads))
        dkp_ref[0], dkc_ref[0], dvp_ref[0], dvc_ref[0] = dkp, dkc, dvp, dvc

        @pl.when(n == 0)
        def _():
            dsk_ref[...] = jnp.zeros_like(dsk_ref)

        for e in heads:
            dq_ref[e] = dq[e]
            dsk_ref[e] += dsk[e]

    return pl.pallas_call(
        body, name=name, grid=(AT_KVH, nb),
        in_specs=[qs, kp, kc, kp, kc, sk, tc, tc, tp, tp, rt, qs], out_specs=[qs, kc, kc, kc, kc, sk],
        out_shape=[jax.ShapeDtypeStruct(q.shape, F32)] + [jax.ShapeDtypeStruct(k.shape, F32)] * 4 + [jax.ShapeDtypeStruct(sinks.shape, F32)],
        compiler_params=_params(("arbitrary", "arbitrary")),
    )(q, k, k, v, v, sinks, cos, sin, cos, sin, rot, do)


def _kv_combine(dc, dp, name):
    T = dc.shape[1]
    nb = T // WINDOW

    def body(c_ref, p_ref, o_ref):
        n = pl.program_id(1)
        o_ref[...] = c_ref[...] + jnp.where(n < nb - 1, p_ref[...], 0.0)

    return pl.pallas_call(
        body, name=name, grid=(AT_KVH, nb),
        in_specs=[pl.BlockSpec((1, WINDOW, AT_HD), lambda g, n: (g, n, 0)),
                  pl.BlockSpec((1, WINDOW, AT_HD), lambda g, n: (g, jnp.minimum(n + 1, nb - 1), 0))],
        out_specs=pl.BlockSpec((1, WINDOW, AT_HD), lambda g, n: (g, n, 0)),
        out_shape=jax.ShapeDtypeStruct((AT_KVH, T, AT_HD), F32),
        compiler_params=_params(("parallel", "parallel")),
    )(dc, dp)


def _place():
    x, y, c = lax.axis_index("x"), lax.axis_index("y"), lax.axis_index("c")
    return x, y, c, 4 * x + 2 * y + c


def _gather(src, name):
    def body(s_ref, o_ref, send_sems, recv_sems, lsem):
        x, y, c, _ = _place()
        me, sib = (x, y, c), (x, y, 1 - c)
        chips = [(1 - x, y), (x, 1 - y), (1 - x, 1 - y)]

        def copy(k, block, to, src_ref=None):
            slab = o_ref.at[4 * block[0] + 2 * block[1] + block[2]]
            return pltpu.make_async_remote_copy(src_ref=slab if src_ref is None else src_ref, dst_ref=slab,
                                                send_sem=send_sems.at[k], recv_sem=recv_sems.at[k], device_id=to, device_id_type=MESH)

        mine = pltpu.make_async_copy(s_ref, o_ref.at[4 * x + 2 * y + c], lsem)
        mine.start()
        first = [copy(0, me, sib, s_ref)] + [copy(1 + j, me, (*chip, c), s_ref) for j, chip in enumerate(chips)]
        for cp in first:
            cp.start()
        passed = [copy(4 + j, (*chip, c), sib) for j, chip in enumerate(chips)]
        for j, chip in enumerate(chips):
            copy(1 + j, (*chip, c), me).wait_recv()
            passed[j].start()
        copy(0, sib, me).wait_recv()
        for j, chip in enumerate(chips):
            copy(4 + j, (*chip, 1 - c), me).wait_recv()
        for cp in first + passed:
            cp.wait_send()
        mine.wait()

    return pl.pallas_call(
        body, name=name, in_specs=[ANY], out_specs=ANY, out_shape=jax.ShapeDtypeStruct((N_DEV,) + src.shape, src.dtype),
        scratch_shapes=[pltpu.SemaphoreType.DMA((N_DEV - 1,)), pltpu.SemaphoreType.DMA((N_DEV - 1,)), pltpu.SemaphoreType.DMA],
    )(src)


def _pair_swap(src, name):
    def body(s_ref, o_ref, send_sem, recv_sem):
        x, y, c, _ = _place()
        cp = pltpu.make_async_remote_copy(src_ref=s_ref.at[1 - c], dst_ref=o_ref, send_sem=send_sem, recv_sem=recv_sem,
                                          device_id=(x, y, 1 - c), device_id_type=MESH)
        cp.start()
        cp.wait()

    return pl.pallas_call(
        body, name=name, in_specs=[ANY], out_specs=ANY, out_shape=jax.ShapeDtypeStruct(src.shape[1:], src.dtype),
        scratch_shapes=[pltpu.SemaphoreType.DMA, pltpu.SemaphoreType.DMA],
    )(src)


def _chip_a2a(src, name):
    def body(s_ref, o_ref, send_sems, recv_sems, lsem):
        x, y, c, _ = _place()
        chip = 2 * x + y
        mine = pltpu.make_async_copy(s_ref.at[chip], o_ref.at[chip], lsem)
        mine.start()
        cps = []
        for k in (1, 2, 3):
            px, py = x ^ (k >> 1), y ^ (k & 1)
            cps.append(pltpu.make_async_remote_copy(src_ref=s_ref.at[2 * px + py], dst_ref=o_ref.at[chip], send_sem=send_sems.at[k - 1],
                                                    recv_sem=recv_sems.at[k - 1], device_id=(px, py, c), device_id_type=MESH))
            cps[-1].start()
        for cp in cps:
            cp.wait()
        mine.wait()

    return pl.pallas_call(
        body, name=name, in_specs=[ANY], out_specs=ANY, out_shape=jax.ShapeDtypeStruct(src.shape, src.dtype),
        scratch_shapes=[pltpu.SemaphoreType.DMA((3,)), pltpu.SemaphoreType.DMA((3,)), pltpu.SemaphoreType.DMA],
    )(src)


def _add(a, b, name):
    R, C = a.shape
    tr = _pick(R, (512, 256, 128, 64, 32, 16))

    def body(a_ref, b_ref, o_ref):
        o_ref[...] = (a_ref[...].astype(F32) + b_ref[...].astype(F32)).astype(o_ref.dtype)

    s2 = pl.BlockSpec((tr, C), lambda i: (i, 0))
    return pl.pallas_call(body, name=name, grid=(R // tr,), in_specs=[s2, s2], out_specs=s2,
                          out_shape=jax.ShapeDtypeStruct((R, C), a.dtype), compiler_params=_params(("parallel",)))(a, b)


def _reduce_scatter(slabs, name):
    _, R, C = slabs.shape
    by_core = jnp.transpose(slabs.reshape(4, 2, R, C), (1, 0, 2, 3))
    c = lax.axis_index("c")
    theirs = _pair_swap(by_core, "swap_" + name)
    mine = lax.dynamic_index_in_dim(by_core, c, axis=0, keepdims=False)
    summed = _add(mine.reshape(4 * R, C), theirs.reshape(4 * R, C), "add_" + name).reshape(4, R, C)
    return _chip_a2a(summed, "a2a_" + name)


def _adamw(w, m, v, parts, name):
    R, C = w.shape
    P = parts.shape[0]
    tr = _pick(R, (256, 128, 64, 32, 16, 8))
    c1, c2 = 1.0 - B1 ** STEP, 1.0 - B2 ** STEP

    def body(w_ref, m_ref, v_ref, p_ref, g_ref, d_ref, nm_ref, nv_ref):
        g = p_ref[0].astype(F32)
        for i in range(1, P):
            g = g + p_ref[i].astype(F32)
        wv = w_ref[...]
        nm = B1 * m_ref[...] + (1.0 - B1) * g
        nv = B2 * v_ref[...] + (1.0 - B2) * (g * g)
        g_ref[...] = g
        nm_ref[...] = nm
        nv_ref[...] = nv
        d_ref[...] = -LR * ((nm / c1) / (jnp.sqrt(nv / c2) + AEPS) + WD * wv)

    s2 = pl.BlockSpec((tr, C), lambda i: (i, 0))
    return pl.pallas_call(
        body, name=name, grid=(R // tr,),
        in_specs=[s2, s2, s2, pl.BlockSpec((P, tr, C), lambda i: (0, i, 0))], out_specs=[s2] * 4,
        out_shape=[jax.ShapeDtypeStruct((R, C), F32)] * 4,
        compiler_params=_params(("parallel",)),
    )(w, m, v, parts)


def _colsum(a, name):
    def body(a_ref, o_ref):
        o_ref[...] = jnp.broadcast_to(jnp.sum(a_ref[...], axis=0, keepdims=True), o_ref.shape)

    return pl.pallas_call(body, name=name, out_shape=jax.ShapeDtypeStruct((8, 128), F32))(a)


def _rows128(a):
    f = a.reshape(-1)
    return jnp.pad(f, (0, (-f.shape[0]) % 128)).reshape(-1, 128)


def _to_aligned(w):
    return jnp.concatenate([w[..., 0:4096], w[..., 4112:5392], w[..., 4096:4112],
                            jnp.zeros(w.shape[:-1] + (IN_PAD - IN_COLS,), w.dtype)], axis=-1)


def _from_aligned(w):
    return jnp.concatenate([w[..., 0:4096], w[..., 5376:5392], w[..., 4096:5376]], axis=-1)


def kernel(x, c, ln_mix, ln_ffn, w_ada, b_ada, w_in, dn_conv_w, dn_a_log, dn_dt_bias, dn_norm_w, attn_sinks, w_out, w_gate_up, w_down, ln_final, loss_target, m_ln_mix, m_ln_ffn, m_w_ada, m_b_ada, m_w_in, m_dn_conv_w, m_dn_a_log, m_dn_dt_bias, m_dn_norm_w, m_attn_sinks, m_w_out, m_w_gate_up, m_w_down, m_ln_final, v_ln_mix, v_ln_ffn, v_w_ada, v_b_ada, v_w_in, v_dn_conv_w, v_dn_a_log, v_dn_dt_bias, v_dn_norm_w, v_attn_sinks, v_w_out, v_w_gate_up, v_w_down, v_ln_final):
    T = x.shape[1]
    L = ln_mix.shape[0]
    me = 4 * lax.axis_index("x") + 2 * lax.axis_index("y") + lax.axis_index("c")
    xs = x[0]
    tgt = loss_target[0]

    g_in = _gather(w_in.astype(BF16).reshape(L * D, -1), "ag_w_in").reshape(N_DEV, L, D, -1)
    g_out = _gather(w_out.astype(BF16).reshape(-1, D), "ag_w_out").reshape(N_DEV, L, -1, D)
    g_gu = _gather(w_gate_up.astype(BF16).reshape(L * D, -1), "ag_w_gu").reshape(N_DEV, L, D, -1)
    g_dn = _gather(w_down.astype(BF16).reshape(-1, D), "ag_w_down").reshape(N_DEV, L, -1, D)
    g_cv = _gather(dn_conv_w.reshape(L * CONV_K, -1), "ag_conv").reshape(N_DEV, L, CONV_K, -1)
    c_all = _gather(jnp.pad(c, ((0, 7), (0, 0))), "ag_c")[:, 0, :]
    W_in = [_to_aligned(jnp.transpose(g_in[:, l], (1, 0, 2)).reshape(D, IN_COLS)) for l in range(L)]
    W_out = [g_out[:, l].reshape(D, D) for l in range(L)]
    W_gu = [jnp.transpose(g_gu[:, l], (1, 0, 2)).reshape(D, 2 * FFN) for l in range(L)]
    W_dn = [g_dn[:, l].reshape(FFN, D) for l in range(L)]
    W_cv = [jnp.transpose(g_cv[:, l], (1, 0, 2)).reshape(CONV_K, 3 * DN_W) for l in range(L)]

    c_act = _ew_fwd(lambda v: (_silu(v),), [(jnp.pad(c_all, ((0, 8), (0, 0))), True, False)], [(D, F32, False)], tm=16, name="c_act")[0]
    mods = []
    for l in range(L):
        ms = _mm(c_act, w_ada[l], "nn", F32, f"mod_mm{l}")
        ga = _gather(ms, f"ag_mod{l}")
        mods.append(lax.dynamic_index_in_dim(ga, me, axis=1, keepdims=False).reshape(1, 6 * D))
    row = lambda a: a.reshape(1, -1)
    seg = lambda a, i: a[:, i * D:(i + 1) * D]

    half = AT_HD // 2
    inv_freq = 10000.0 ** (-jnp.arange(half, dtype=F32) * 2.0 / AT_HD)
    ang = jnp.arange(T, dtype=jnp.int32).astype(F32)[:, None] * inv_freq[None, :]
    cos = jnp.concatenate([jnp.cos(ang)] * 2, axis=-1)
    sin = jnp.concatenate([jnp.sin(ang)] * 2, axis=-1)
    ii = jnp.arange(AT_HD)
    rot = jnp.where(ii[:, None] == ii[None, :] + half, -1.0, 0.0) + jnp.where(ii[:, None] + half == ii[None, :], 1.0, 0.0)
    rot = rot.astype(F32)
    heads = lambda a, nh: jnp.transpose(a.reshape(T, nh, AT_HD), (1, 0, 2))
    unheads = lambda a: jnp.transpose(a, (1, 0, 2)).reshape(T, -1)
    pad16 = lambda a: jnp.pad(row(a), ((0, 0), (8, 128 - 16)))

    saved = []
    xc = xs
    for l in range(L):
        mod, bmod = mods[l], row(b_ada[l])
        s = {"x": xc}
        nm_ops = lambda xx, ln, a, b: [(xx, True, False), (row(ln), False, False), (seg(mod, a), False, False),
                                       (seg(bmod, a), False, False), (seg(mod, b), False, False), (seg(bmod, b), False, False)]
        h1 = _ew_fwd(lambda *a: _f_normmod(*a)[:1], nm_ops(xc, ln_mix[l], 1, 0), [(D, BF16, False)], tm=256, name=f"normmod1_{l}")[0]
        proj = _mm(h1, W_in[l], "nn", F32, f"mm_in{l}")
        qkv, z, aq = proj[:, :3072], proj[:, 3072:4096], proj[:, 4096:5120]
        ak, av, ba = proj[:, 5120:5248], proj[:, 5248:5376], proj[:, 5376:5504]
        conv = _conv_fwd(qkv, W_cv[l], f"conv{l}")
        cv3 = [(conv[:, i * DN_W:(i + 1) * DN_W], True, True) for i in range(3)]
        qn, kn, vn = _ew_fwd(_f_dnpre, cv3, [(DN_W, F32, True)] * 3, tm=512, ncol=DN_HEADS, name=f"dnpre{l}")
        bg_ops = [(ba, True, False), (pad16(dn_a_log[l]), False, False), (pad16(dn_dt_bias[l]), False, False)]
        bg = _ew_fwd(_f_bg, bg_ops, [(128, F32, False)], tm=1024, name=f"bg{l}")[0]
        o, s_saved = _chunk_fwd(qn, kn, vn, bg, 8, f"chunk{l}")
        post_ops = [(o, True, True), (z, True, True), (row(dn_norm_w[l]), False, False)]
        dn_out = _ew_fwd(_f_dnpost, post_ops, [(DN_W, BF16, True)], tm=512, ncol=DN_HEADS, name=f"dnpost{l}")[0]
        qh, kh, vh = heads(aq, AT_QH), heads(ak, AT_KVH), heads(av, AT_KVH)
        sk = jnp.broadcast_to(attn_sinks[l][:, None, None], (AT_QH, 1, 128))
        at_out = unheads(_attn_fwd(qh, kh, vh, sk, cos, sin, rot, f"attn{l}")).astype(BF16)
        cat = jnp.concatenate([dn_out, at_out], axis=-1)
        mix = _mm(cat, W_out[l], "nn", F32, f"mm_out{l}")
        rg_ops = lambda xx, br, a: [(xx, True, False), (br, True, False), (seg(mod, a), False, False), (seg(bmod, a), False, False)]
        x1 = _ew_fwd(_f_resgate, rg_ops(xc, mix, 2), [(D, F32, False)], tm=256, name=f"resgate1_{l}")[0]
        h2 = _ew_fwd(lambda *a: _f_normmod(*a)[:1], nm_ops(x1, ln_ffn[l], 4, 3), [(D, BF16, False)], tm=256, name=f"normmod2_{l}")[0]
        gu = _mm(h2, W_gu[l], "nn", F32, f"mm_gu{l}")
        act = _ew_fwd(_f_swiglu, [(gu[:, :FFN], True, True), (gu[:, FFN:], True, True)], [(FFN, BF16, True)], tm=512, ncol=11, name=f"swiglu{l}")[0]
        down = _mm(act, W_dn[l], "nn", F32, f"mm_down{l}")
        x2 = _ew_fwd(_f_resgate, rg_ops(x1, down, 5), [(D, F32, False)], tm=256, name=f"resgate2_{l}")[0]
        s.update(h1=h1, qkv=qkv, z=z, ba=ba, conv=conv, qn=qn, kn=kn, vn=vn, bg=bg, o=o, s_saved=s_saved, qh=qh, kh=kh, vh=vh,
                 sk=sk, cat=cat, mix=mix, x1=x1, h2=h2, gu=gu, act=act, down=down, bg_ops=bg_ops, post_ops=post_ops, cv3=cv3,
                 nm1=nm_ops(xc, ln_mix[l], 1, 0), nm2=nm_ops(x1, ln_ffn[l], 4, 3), rg1=rg_ops(xc, mix, 2), rg2=rg_ops(x1, down, 5))
        saved.append(s)
        xc = x2

    fin_ops = [(xc, True, False), (tgt, True, False), (row(ln_final), False, False)]
    lrow = _ew_fwd(_f_final, fin_ops, [(128, F32, False)], tm=256, name="loss_rows")[0]
    loss = lax.psum(_colsum(lrow, "loss_sum")[0, 0], ("x", "y", "c"))
    dx, d_ln_final = _ew_bwd(_f_final, fin_ops, [(jnp.ones((T, 128), F32) / 128.0, False)], [0, 2], tm=256, name="loss_bwd")

    small = {k: [None] * L for k in ("ln_mix", "ln_ffn", "mod", "a_log", "dt", "norm_w", "sinks", "conv")}
    big = {k: [None] * L for k in ("w_in", "w_out", "w_gu", "w_dn")}
    for l in reversed(range(L)):
        s = saved[l]
        ddown, dgt_f = _ew_bwd(_f_resgate, s["rg2"], [(dx, False)], [1, 2], tm=256, name=f"resgate2_bwd{l}", gdt=[BF16, F32])
        big["w_dn"][l] = _mm(s["act"], ddown, "tn", BF16, f"wg_down{l}")
        dact = _mm(ddown, W_dn[l], "nt", F32, f"dg_down{l}")
        dgate, dup = _ew_bwd(_f_swiglu, [(s["gu"][:, :FFN], True, True), (s["gu"][:, FFN:], True, True)], [(dact, True)], [0, 1],
                             tm=512, ncol=11, name=f"swiglu_bwd{l}", gdt=[BF16, BF16])
        dgu = jnp.concatenate([dgate, dup], axis=-1)
        big["w_gu"][l] = _mm(s["h2"], dgu, "tn", BF16, f"wg_gu{l}")
        dh2 = _mm(dgu, W_gu[l], "nt", F32, f"dg_gu{l}")
        dx1, dln_f, dsc_f, dsh_f = _ew_bwd(_f_normmod, s["nm2"], [(dh2, False), (dx, False)], [0, 1, 2, 4], tm=256, name=f"normmod2_bwd{l}")
        dmix, dgt_m = _ew_bwd(_f_resgate, s["rg1"], [(dx1, False)], [1, 2], tm=256, name=f"resgate1_bwd{l}", gdt=[BF16, F32])
        big["w_out"][l] = _mm(s["cat"], dmix, "tn", BF16, f"wg_out{l}")
        dcat = _mm(dmix, W_out[l], "nt", F32, f"dg_out{l}")
        d_dn, d_at = dcat[:, :DN_W], dcat[:, DN_W:]
        dqh, dkp, dkc, dvp, dvc, dsk = _attn_bwd(s["qh"], s["kh"], s["vh"], s["sk"], cos, sin, rot, heads(d_at, AT_QH), f"attn_bwd{l}")
        dkh = _kv_combine(dkc, dkp, f"dk_comb{l}")
        dvh = _kv_combine(dvc, dvp, f"dv_comb{l}")
        do, dz, dnw = _ew_bwd(_f_dnpost, s["post_ops"], [(d_dn, True)], [0, 1, 2], tm=512, ncol=DN_HEADS, name=f"dnpost_bwd{l}")
        dqn, dkn, dvn, dbg = _chunk_bwd(s["qn"], s["kn"], s["vn"], s["bg"], s["s_saved"], do, 4, f"chunk_bwd{l}")
        dcq, dck, dcv = _ew_bwd(_f_dnpre, s["cv3"], [(dqn, True), (dkn, True), (dvn, True)], [0, 1, 2], tm=512, ncol=DN_HEADS, name=f"dnpre_bwd{l}")
        dba, dalog, ddt = _ew_bwd(_f_bg, s["bg_ops"], [(dbg, False)], [0, 1, 2], tm=1024, name=f"bg_bwd{l}")
        dqkv, dcw = _conv_bwd(s["qkv"], jnp.concatenate([dcq, dck, dcv], axis=-1), W_cv[l], f"conv_bwd{l}")
        dproj = jnp.concatenate([dqkv, dz, unheads(dqh), unheads(dkh), unheads(dvh), dba, jnp.zeros((T, IN_PAD - 5504), F32)],
                                axis=-1).astype(BF16)
        big["w_in"][l] = _mm(s["h1"], dproj, "tn", BF16, f"wg_in{l}")
        dh1 = _mm(dproj, W_in[l], "nt", F32, f"dg_in{l}")
        dx, dln_m, dsc_m, dsh_m = _ew_bwd(_f_normmod, s["nm1"], [(dh1, False), (dx1, False)], [0, 1, 2, 4], tm=256, name=f"normmod1_bwd{l}")
        small["ln_mix"][l], small["ln_ffn"][l] = dln_m, dln_f
        small["mod"][l] = jnp.concatenate([dsh_m, dsc_m, dgt_m, dsh_f, dsc_f, dgt_f], axis=-1)
        small["a_log"][l], small["dt"][l] = dalog[:, 8:16], ddt[:, 8:16]
        small["norm_w"][l], small["sinks"][l], small["conv"][l] = dnw, dsk[:, 0, 0], dcw

    cat0 = lambda xs_: jnp.concatenate([_rows128(a) for a in xs_], axis=0)
    stk = lambda k: jnp.stack(small[k])
    pack = cat0([stk("ln_mix"), stk("ln_ffn"), stk("mod"), stk("a_log"), stk("dt"), stk("norm_w"), stk("sinks"), d_ln_final, stk("conv")])
    n_small = pack.shape[0] - L * CONV_K * 3 * DN_W // 128
    pack = jnp.pad(pack, ((0, (-pack.shape[0]) % 8), (0, 0)))
    gp = _gather(pack, "ag_small")
    parts_small = gp[:, :n_small]
    dmod_all = gp[:, 2 * L * D // 128:2 * L * D // 128 + L * 6 * D // 128].reshape(N_DEV, L, 6 * D)
    conv_all = gp[:, n_small:n_small + L * CONV_K * 3 * DN_W // 128].reshape(N_DEV, L * CONV_K, 3 * DN_W)
    parts_conv = lax.dynamic_slice_in_dim(conv_all, me * (3 * DN_W // N_DEV), 3 * DN_W // N_DEV, axis=2)

    def shards(gs, cols):
        g = jnp.stack(gs)
        if cols:
            return jnp.transpose(g.reshape(L, g.shape[1], N_DEV, -1), (2, 0, 1, 3)).reshape(N_DEV, L * g.shape[1], -1)
        return jnp.transpose(g.reshape(L, N_DEV, -1, g.shape[2]), (1, 0, 2, 3)).reshape(N_DEV, -1, g.shape[2])

    p_in = _reduce_scatter(shards([_from_aligned(g) for g in big["w_in"]], True), "w_in")
    p_out = _reduce_scatter(shards(big["w_out"], False), "w_out")
    p_gu = _reduce_scatter(shards(big["w_gu"], True), "w_gu")
    p_dn = _reduce_scatter(shards(big["w_dn"], False), "w_down")
    dmod_mine = lax.dynamic_slice_in_dim(dmod_all, me * (6 * D // N_DEV), 6 * D // N_DEV, axis=2)
    g_ada = jnp.stack([_mm(c_act, jnp.pad(dmod_mine[:, l], ((0, 8), (0, 0))), "tn", F32, f"wg_ada{l}") for l in range(L)])

    def upd(w, m, v, parts, name):
        shp = w.shape
        r = lambda a: a.reshape(-1, shp[-1])
        return [o_.reshape(shp) for o_ in _adamw(r(w), r(m), r(v), parts.reshape(parts.shape[0], -1, shp[-1]), name)]

    res = {}
    res["w_ada"] = upd(w_ada, m_w_ada, v_w_ada, g_ada[None], "adamw_ada")
    res["w_in"] = upd(w_in, m_w_in, v_w_in, p_in, "adamw_in")
    res["dn_conv_w"] = upd(dn_conv_w, m_dn_conv_w, v_dn_conv_w, parts_conv, "adamw_conv")
    res["w_out"] = upd(w_out, m_w_out, v_w_out, p_out, "adamw_out")
    res["w_gate_up"] = upd(w_gate_up, m_w_gate_up, v_w_gate_up, p_gu, "adamw_gu")
    res["w_down"] = upd(w_down, m_w_down, v_w_down, p_dn, "adamw_down")
    names_s = ["ln_mix", "ln_ffn", "b_ada", "dn_a_log", "dn_dt_bias", "dn_norm_w", "attn_sinks", "ln_final"]
    ws = [ln_mix, ln_ffn, b_ada, dn_a_log, dn_dt_bias, dn_norm_w, attn_sinks, ln_final]
    ms = [m_ln_mix, m_ln_ffn, m_b_ada, m_dn_a_log, m_dn_dt_bias, m_dn_norm_w, m_attn_sinks, m_ln_final]
    vs = [v_ln_mix, v_ln_ffn, v_b_ada, v_dn_a_log, v_dn_dt_bias, v_dn_norm_w, v_attn_sinks, v_ln_final]
    padr = lambda a: jnp.pad(a, ((0, (-a.shape[0]) % 8), (0, 0)))
    vpad = jnp.pad(cat0(vs), ((0, (-n_small) % 8), (0, 0)), constant_values=1.0)
    outs_s = _adamw(padr(cat0(ws)), padr(cat0(ms)), vpad, jnp.pad(parts_small, ((0, 0), (0, (-n_small) % 8), (0, 0))), "adamw_small")
    off = 0
    for nme, wv in zip(names_s, ws):
        nrow = -(-wv.size // 128)
        res[nme] = [o_[off:off + nrow].reshape(-1)[:wv.size].reshape(wv.shape) for o_ in outs_s]
        off += nrow

    order = ["ln_mix", "ln_ffn", "w_ada", "b_ada", "w_in", "dn_conv_w", "dn_a_log", "dn_dt_bias", "dn_norm_w", "attn_sinks",
             "w_out", "w_gate_up", "w_down", "ln_final"]
    return (loss, dx[None], *[res[n][0] for n in order], *[res[n][1] for n in order], *[res[n][2] for n in order],
            *[res[n][3] for n in order])
```

```python
import functools

import jax
import jax.numpy as jnp
from jax import lax
from jax.experimental import pallas as pl
from jax.experimental.pallas import tpu as pltpu

F32, BF16 = jnp.float32, jnp.bfloat16
HI = lax.Precision.HIGHEST
MESH = pl.DeviceIdType.MESH
ANY = pl.BlockSpec(memory_space=pl.ANY)

N_DEV = 8
D = 2048
DN_HEADS, DN_HD = 8, 128
DN_W = 1024
CONV_K = 4
CHUNK = 64
AT_HD, AT_QH, AT_KVH = 64, 16, 2
AT_W = 1024
WINDOW = 128
FFN = 5632
IN_COLS = 5392
IN_PAD = 5632
EPS = 1e-6
NEG = -1e30
LR, B1, B2, AEPS, WD, STEP = 0.001, 0.9, 0.999, 1e-08, 0.01, 10
VMEM_LIMIT = 56 * 1024 * 1024


def _pick(n, cands):
    for c in cands:
        if n % c == 0:
            return c
    return n


def _params(sem):
    return pltpu.CompilerParams(dimension_semantics=sem, vmem_limit_bytes=VMEM_LIMIT)


_DN = {"nn": (((1,), (0,)), ((), ())), "nt": (((1,), (1,)), ((), ())), "tn": (((0,), (0,)), ((), ()))}


def _mm(a, b, mode, out_dtype, name, acc_in=None):
    if mode == "nn":
        (M, K), (_, N) = a.shape, b.shape
    elif mode == "nt":
        (M, K), (N, _) = a.shape, b.shape
    else:
        (K, M), (_, N) = a.shape, b.shape
    tm = _pick(M, (1024, 512, 256, 128, 64, 32, 16))
    tn = _pick(N, (1024, 512, 256, 128))
    tk = K if K <= 2048 else _pick(K, (2816, 2048, 1024, 512, 256, 128))
    nk = K // tk
    dn = _DN[mode]
    n_in = 2 if acc_in is None else 3

    def body(*refs):
        a_ref, b_ref, o_ref = refs[0], refs[1], refs[n_in]
        part = lax.dot_general(a_ref[...].astype(BF16), b_ref[...].astype(BF16), dn, preferred_element_type=F32)
        if nk == 1:
            if acc_in is not None:
                part = part + refs[2][...]
            o_ref[...] = part.astype(o_ref.dtype)
            return
        acc = refs[n_in + 1]
        k = pl.program_id(2)

        @pl.when(k == 0)
        def _():
            acc[...] = part if acc_in is None else part + refs[2][...]

        @pl.when(k > 0)
        def _():
            acc[...] += part

        @pl.when(k == nk - 1)
        def _():
            o_ref[...] = acc[...].astype(o_ref.dtype)

    a_spec = pl.BlockSpec((tk, tm), lambda i, j, k: (k, i)) if mode == "tn" else pl.BlockSpec((tm, tk), lambda i, j, k: (i, k))
    b_spec = pl.BlockSpec((tn, tk), lambda i, j, k: (j, k)) if mode == "nt" else pl.BlockSpec((tk, tn), lambda i, j, k: (k, j))
    o_spec = pl.BlockSpec((tm, tn), lambda i, j, k: (i, j))
    return pl.pallas_call(
        body, name=name, grid=(M // tm, N // tn, nk),
        in_specs=[a_spec, b_spec] + ([] if acc_in is None else [o_spec]), out_specs=o_spec,
        out_shape=jax.ShapeDtypeStruct((M, N), out_dtype),
        scratch_shapes=[pltpu.VMEM((tm, tn), F32)] if nk > 1 else [],
        compiler_params=_params(("parallel", "parallel", "arbitrary")),
    )(*((a, b) if acc_in is None else (a, b, acc_in)))


def _op(op):
    arr, rt, cb = op[:3]
    start, width = op[3] if len(op) > 3 else (0, arr.shape[1])
    return arr, rt, cb, start, width


def _ew_spec(op, tm, ncol):
    arr, rt, cb, start, width = _op(op)
    bw = width // ncol if cb else width
    first = start // bw
    return pl.BlockSpec((tm if rt else arr.shape[0], bw), lambda j, i: (i if rt else 0, first + (j if cb else 0)))


def _ew_fwd(f, ops, outs, *, tm, ncol=1, name, with_j=False):
    M = next(op[0].shape[0] for op in ops if op[1])
    n_in = len(ops)

    def body(*refs):
        res = f(*[r[...].astype(F32) for r in refs[:n_in]], *([pl.program_id(0)] if with_j else []))
        for o, r in zip(refs[n_in:], res):
            o[...] = r.astype(o.dtype)

    return pl.pallas_call(
        body, name=name, grid=(ncol, M // tm),
        in_specs=[_ew_spec(op, tm, ncol) for op in ops],
        out_specs=[pl.BlockSpec((tm, w // ncol if cb else w), lambda j, i, cb=cb: (i, j if cb else 0)) for w, _, cb in outs],
        out_shape=[jax.ShapeDtypeStruct((M, w), dt) for w, dt, _ in outs],
        compiler_params=_params(("parallel", "parallel")),
    )(*[op[0] for op in ops])


def _ew_bwd(f, ops, cts, diff, *, tm, ncol=1, name, gdt=None, with_j=False):
    M = next(op[0].shape[0] for op in ops if op[1])
    n_in, n_ct = len(ops), len(cts)
    gdt = gdt or [F32] * len(diff)

    def body(*refs):
        j, i = pl.program_id(0), pl.program_id(1)
        vals = [r[...].astype(F32) for r in refs[:n_in]]

        def fd(*dv):
            full = list(vals)
            for idx, v in zip(diff, dv):
                full[idx] = v
            return tuple(f(*full, *([j] if with_j else [])))

        _, vjp = jax.vjp(fd, *[vals[idx] for idx in diff])
        gs = vjp(tuple(r[...].astype(F32) for r in refs[n_in:n_in + n_ct]))
        for idx, g, gref in zip(diff, gs, refs[n_in + n_ct:]):
            _, rt, cb = ops[idx][:3]
            if rt:
                gref[...] = g.astype(gref.dtype)
            else:
                first = (i == 0) if cb else jnp.logical_and(i == 0, j == 0)

                @pl.when(first)
                def _(gref=gref):
                    gref[...] = jnp.zeros_like(gref)

                gref[...] += g

    def g_spec(op):
        arr, rt, cb, start, width = _op(op)
        return _ew_spec((jax.ShapeDtypeStruct((arr.shape[0], width), F32), rt, cb), tm, ncol) if rt else _ew_spec(op, tm, ncol)

    def g_shape(op, dt):
        arr, rt, cb, start, width = _op(op)
        return jax.ShapeDtypeStruct((M, width), dt) if rt else jax.ShapeDtypeStruct(arr.shape, F32)

    return pl.pallas_call(
        body, name=name, grid=(ncol, M // tm),
        in_specs=[_ew_spec(op, tm, ncol) for op in ops]
        + [pl.BlockSpec((tm, a.shape[1] // ncol if cb else a.shape[1]), lambda j, i, cb=cb: (i, j if cb else 0)) for a, cb in cts],
        out_specs=[g_spec(ops[idx]) for idx in diff],
        out_shape=[g_shape(ops[idx], dt) for idx, dt in zip(diff, gdt)],
        compiler_params=_params(("arbitrary", "arbitrary")),
    )(*[op[0] for op in ops], *[a for a, _ in cts])


def _silu(x):
    return x * jax.nn.sigmoid(x)


def _f_normmod(x, ln, sc, bsc, sh, bsh):
    y = x * lax.rsqrt(jnp.mean(x * x, axis=-1, keepdims=True) + EPS) * ln
    return y * (1.0 + (sc + bsc)) + (sh + bsh), x


def _f_resgate(x, br, gt, bgt):
    return (x + (gt + bgt) * br,)


def _f_swiglu(gate, up):
    return (_silu(gate) * up,)


def _f_final(x, tgt, ln):
    y = x * lax.rsqrt(jnp.mean(x * x, axis=-1, keepdims=True) + EPS) * ln
    e = y - tgt
    return (jnp.broadcast_to(0.5 * jnp.mean(e * e, axis=-1, keepdims=True), (x.shape[0], 128)),)


def _f_bg(ba, alog, dt):
    col = lax.broadcasted_iota(jnp.int32, ba.shape, 1)
    z = ba + dt
    sp = jnp.maximum(z, 0.0) + jnp.log(1.0 + jnp.exp(-jnp.abs(z)))
    return (jnp.where(col < 8, jax.nn.sigmoid(ba), jnp.where(col < 16, -jnp.exp(alog) * sp, 0.0)),)


def _l2n(x):
    return x * lax.rsqrt(jnp.sum(x * x, axis=-1, keepdims=True) + EPS)


def _f_dnpre(c, j):
    a = _silu(c)
    return (jnp.where(j < 2 * DN_HEADS, _l2n(a) * jnp.where(j < DN_HEADS, DN_HD ** -0.5, 1.0), a),)


def _f_dnpost(o, z, nw):
    return (o * lax.rsqrt(jnp.mean(o * o, axis=-1, keepdims=True) + EPS) * nw * _silu(z),)


CONV_BW = 512


def _conv_fwd(x, w, Cw, name):
    T = x.shape[0]
    tm, bw = 512, CONV_BW

    def body(x_ref, h_ref, w_ref, o_ref):
        i = pl.program_id(1)
        cur, halo, wv = x_ref[...], h_ref[...], w_ref[...]
        halo = jnp.where(i > 0, halo, 0.0)
        row = lax.broadcasted_iota(jnp.int32, (8, bw), 0)
        acc = wv[3:4, :] * cur
        for s in (1, 2, 3):
            r = pltpu.roll(cur, s, 0)
            top = jnp.where(row < s, pltpu.roll(halo, s, 0), r[:8])
            acc += wv[3 - s:4 - s, :] * jnp.concatenate([top, r[8:]], axis=0)
        o_ref[...] = acc

    return pl.pallas_call(
        body, name=name, grid=(Cw // bw, T // tm),
        in_specs=[pl.BlockSpec((tm, bw), lambda j, i: (i, j)),
                  pl.BlockSpec((8, bw), lambda j, i: (jnp.maximum(i * (tm // 8) - 1, 0), j)),
                  pl.BlockSpec((4, bw), lambda j, i: (0, j))],
        out_specs=pl.BlockSpec((tm, bw), lambda j, i: (i, j)),
        out_shape=jax.ShapeDtypeStruct((T, Cw), F32),
        compiler_params=_params(("parallel", "parallel")),
    )(x, x, w)


def _conv_bwd(x, dy, w, name):
    T, Cw = dy.shape
    tm, bw = 512, CONV_BW
    nt = T // tm

    def body(x_ref, h_ref, dy_ref, n_ref, w_ref, dx_ref, dw_ref):
        i = pl.program_id(1)
        cur, dcur, wv = x_ref[...], dy_ref[...], w_ref[...]
        halo = jnp.where(i > 0, h_ref[...], 0.0)
        nxt = jnp.where(i < nt - 1, n_ref[...], 0.0)
        row = lax.broadcasted_iota(jnp.int32, (8, bw), 0)

        @pl.when(i == 0)
        def _():
            dw_ref[...] = jnp.zeros_like(dw_ref)

        dx = wv[3:4, :] * dcur
        dw_ref[3:4, :] += jnp.sum(dcur * cur, axis=0, keepdims=True)
        for s in (1, 2, 3):
            r = pltpu.roll(cur, s, 0)
            top = jnp.where(row < s, pltpu.roll(halo, s, 0), r[:8])
            xs = jnp.concatenate([top, r[8:]], axis=0)
            dw_ref[3 - s:4 - s, :] += jnp.sum(dcur * xs, axis=0, keepdims=True)
            rf = pltpu.roll(dcur, tm - s, 0)
            bot = jnp.where(row >= 8 - s, pltpu.roll(nxt, 8 - s, 0), rf[tm - 8:])
            dx += wv[3 - s:4 - s, :] * jnp.concatenate([rf[:tm - 8], bot], axis=0)
        dx_ref[...] = dx

    return pl.pallas_call(
        body, name=name, grid=(Cw // bw, nt),
        in_specs=[pl.BlockSpec((tm, bw), lambda j, i: (i, j)),
                  pl.BlockSpec((8, bw), lambda j, i: (jnp.maximum(i * (tm // 8) - 1, 0), j)),
                  pl.BlockSpec((tm, bw), lambda j, i: (i, j)),
                  pl.BlockSpec((8, bw), lambda j, i: (jnp.minimum((i + 1) * (tm // 8), T // 8 - 1), j)),
                  pl.BlockSpec((4, bw), lambda j, i: (0, j))],
        out_specs=[pl.BlockSpec((tm, bw), lambda j, i: (i, j)), pl.BlockSpec((4, bw), lambda j, i: (0, j))],
        out_shape=[jax.ShapeDtypeStruct((T, Cw), F32), jax.ShapeDtypeStruct((4, Cw), F32)],
        compiler_params=_params(("arbitrary", "arbitrary")),
    )(x, x, dy, dy, w)


def _dot(a, b):
    return jnp.dot(a, b, precision=HI, preferred_element_type=F32)


def _dot_t(a, b):
    return lax.dot_general(a, b, (((1,), (1,)), ((), ())), precision=HI, preferred_element_type=F32)


def _bdot(a, b):
    return jnp.dot(a.astype(BF16), b.astype(BF16), preferred_element_type=F32)


def _bdot_t(a, b):
    return lax.dot_general(a.astype(BF16), b.astype(BF16), (((1,), (1,)), ((), ())), preferred_element_type=F32)


def _each(f, *lists):
    return [f(*a) for a in zip(*lists)]


@jax.custom_vjp
def _unit_lower_inverses(ps):
    eye = lax.broadcasted_iota(jnp.int32, ps[0].shape, 0) == lax.broadcasted_iota(jnp.int32, ps[0].shape, 1)
    tis = [jnp.where(eye, 1.0, 0.0) + p for p in ps]
    for _ in range(5):
        ps = _each(lambda p: _dot(p, p), ps)
        tis = _each(lambda t, p: t + _dot(t, p), tis, ps)
    return tis


def _uli_fwd(ps):
    tis = _unit_lower_inverses(ps)
    return tis, tis


def _uli_bwd(tis, cts):
    tt = lambda a, b: lax.dot_general(a, b, (((0,), (0,)), ((), ())), precision=HI, preferred_element_type=F32)
    half = _each(tt, tis, cts)
    return (_each(_dot_t, half, tis),)


_unit_lower_inverses.defvjp(_uli_fwd, _uli_bwd)


def _chunk_f(qs, ks, vs, bg, Ss, hs):
    C = CHUNK
    lane = lax.broadcasted_iota(jnp.int32, (C, 128), 1)
    betas = [jnp.sum(jnp.where(lane == h, bg, 0.0), axis=1, keepdims=True) for h in hs]
    gs = [jnp.sum(jnp.where(lane == h + 8, bg, 0.0), axis=1, keepdims=True) for h in hs]
    ri = lax.broadcasted_iota(jnp.int32, (C, C), 0)
    ci = lax.broadcasted_iota(jnp.int32, (C, C), 1)
    causal, strict, eye = ri >= ci, ri > ci, ri == ci
    g_rows = _each(lambda g: jnp.sum(jnp.where(eye, g, 0.0), axis=0, keepdims=True), gs)
    gc_cols = _each(lambda gr: jnp.sum(jnp.where(causal, gr, 0.0), axis=1, keepdims=True), g_rows)
    gc_rows = _each(lambda g: jnp.sum(jnp.where(ri <= ci, g, 0.0), axis=0, keepdims=True), gs)
    gc_lasts = _each(lambda g: jnp.sum(g, axis=0, keepdims=True), gs)
    decays = _each(lambda c, r: jnp.exp(jnp.where(causal, c - r, NEG)), gc_cols, gc_rows)
    kbs = _each(jnp.multiply, ks, betas)
    vbs = _each(jnp.multiply, vs, betas)
    ps = _each(lambda kb, k, d: -jnp.where(strict, _bdot_t(kb, k) * d, 0.0), kbs, ks, decays)
    tis = _unit_lower_inverses(ps)
    egcs = _each(jnp.exp, gc_cols)
    ws = _each(lambda t, kb, e: _bdot(t, kb * e), tis, kbs, egcs)
    us = _each(_bdot, tis, vbs)
    intras = _each(lambda q, k, d: jnp.where(causal, _bdot_t(q, k) * d, 0.0), qs, ks, decays)
    kds = _each(lambda k, gl, gc: k * jnp.exp(gl - gc), ks, gc_lasts, gc_cols)
    vns = _each(lambda u, w, S: u - _bdot(w, S), us, ws, Ss)
    os_ = _each(lambda q, e, S, i, vn: _bdot(q * e, S) + _bdot(i, vn), qs, egcs, Ss, intras, vns)
    Sn = _each(lambda S, gl, kd, vn: S * jnp.exp(gl) + lax.dot_general(
        kd.astype(BF16), vn.astype(BF16), (((0,), (0,)), ((), ())), preferred_element_type=F32), Ss, gc_lasts, kds, vns)
    return os_, Sn


def _head_cols(ref, part):
    return [ref[:, part * DN_W + e * DN_HD:part * DN_W + (e + 1) * DN_HD] for e in range(DN_HEADS)]


def _chunk_fwd(qkv, bg, name):
    T = qkv.shape[0]
    N = T // CHUNK
    H = DN_HEADS

    def body(x_ref, bg_ref, o_ref, s_ref, S):
        n = pl.program_id(0)

        @pl.when(n == 0)
        def _():
            S[...] = jnp.zeros_like(S)

        s_all = S[...]
        s_ref[0] = s_all
        os_, nxt = _chunk_f(_head_cols(x_ref, 0), _head_cols(x_ref, 1), _head_cols(x_ref, 2), bg_ref[...],
                            [s_all[e] for e in range(H)], list(range(H)))
        for e in range(H):
            o_ref[:, e * DN_HD:(e + 1) * DN_HD] = os_[e]
            S[e] = nxt[e]

    return pl.pallas_call(
        body, name=name, grid=(N,),
        in_specs=[pl.BlockSpec((CHUNK, 3 * DN_W), lambda n: (n, 0)), pl.BlockSpec((CHUNK, 128), lambda n: (n, 0))],
        out_specs=[pl.BlockSpec((CHUNK, DN_W), lambda n: (n, 0)), pl.BlockSpec((1, H, DN_HD, DN_HD), lambda n: (n, 0, 0, 0))],
        out_shape=[jax.ShapeDtypeStruct((T, DN_W), F32), jax.ShapeDtypeStruct((N, H, DN_HD, DN_HD), F32)],
        scratch_shapes=[pltpu.VMEM((H, DN_HD, DN_HD), F32)],
        compiler_params=_params(("arbitrary",)),
    )(qkv, bg)


def _chunk_bwd(qkv, bg, s_saved, do, name):
    T = qkv.shape[0]
    N = T // CHUNK
    H = DN_HEADS

    def body(x_ref, bg_ref, s_ref, do_ref, dx_ref, dbg_ref, dS):
        n = pl.program_id(0)

        @pl.when(n == 0)
        def _():
            dS[...] = jnp.zeros_like(dS)

        ds_all = dS[...]
        _, vjp = jax.vjp(functools.partial(_chunk_f, hs=list(range(H))), _head_cols(x_ref, 0), _head_cols(x_ref, 1),
                         _head_cols(x_ref, 2), bg_ref[...], [s_ref[0, e] for e in range(H)])
        dq, dk, dv, dbg, nxt = vjp(([do_ref[:, e * DN_HD:(e + 1) * DN_HD] for e in range(H)], [ds_all[e] for e in range(H)]))
        for part, g in enumerate((dq, dk, dv)):
            for e in range(H):
                dx_ref[:, part * DN_W + e * DN_HD:part * DN_W + (e + 1) * DN_HD] = g[e]
        for e in range(H):
            dS[e] = nxt[e]
        dbg_ref[...] = dbg

    rev = lambda n: (N - 1 - n, 0)
    return pl.pallas_call(
        body, name=name, grid=(N,),
        in_specs=[pl.BlockSpec((CHUNK, 3 * DN_W), rev), pl.BlockSpec((CHUNK, 128), rev),
                  pl.BlockSpec((1, H, DN_HD, DN_HD), lambda n: (N - 1 - n, 0, 0, 0)), pl.BlockSpec((CHUNK, DN_W), rev)],
        out_specs=[pl.BlockSpec((CHUNK, 3 * DN_W), rev), pl.BlockSpec((CHUNK, 128), rev)],
        out_shape=[jax.ShapeDtypeStruct((T, 3 * DN_W), F32), jax.ShapeDtypeStruct((T, 128), F32)],
        scratch_shapes=[pltpu.VMEM((H, DN_HD, DN_HD), F32)],
        compiler_params=_params(("arbitrary",)),
    )(qkv, bg, s_saved, do)


GRP = AT_QH // AT_KVH


def _attn_f(q, kp, kc, vp, vc, sinks, cos_c, sin_c, cos_p, sin_p, rot, has_prev):
    def rope(x, c, s):
        return x * c + _dot(x, rot) * s

    kcr, kpr = rope(kc, cos_c, sin_c), rope(kp, cos_p, sin_p)
    r = lax.broadcasted_iota(jnp.int32, (WINDOW, WINDOW), 0)
    j = lax.broadcasted_iota(jnp.int32, (WINDOW, WINDOW), 1)
    in_c, in_p = j <= r, jnp.logical_and(j > r, has_prev)
    lane = lax.broadcasted_iota(jnp.int32, (1, 128), 1)
    qrs = [rope(qe, cos_c, sin_c) for qe in q]
    scs = [jnp.where(in_c, _bdot_t(qr, kcr) * (AT_HD ** -0.5), NEG) for qr in qrs]
    sps = [jnp.where(in_p, _bdot_t(qr, kpr) * (AT_HD ** -0.5), NEG) for qr in qrs]
    snk = [jnp.sum(jnp.where(lane == 0, s, 0.0), axis=1, keepdims=True) for s in sinks]
    ms = _each(lambda sc, sp, s: jnp.maximum(jnp.maximum(jnp.max(sc, axis=1, keepdims=True), jnp.max(sp, axis=1, keepdims=True)), s),
               scs, sps, snk)
    pcs = _each(lambda sc, m: jnp.exp(sc - m), scs, ms)
    pps = _each(lambda sp, m: jnp.exp(sp - m), sps, ms)
    dens = _each(lambda pc, pp, s, m: jnp.sum(pc, axis=1, keepdims=True) + jnp.sum(pp, axis=1, keepdims=True) + jnp.exp(s - m),
                 pcs, pps, snk, ms)
    return tuple(_each(lambda pc, pp, den: (_bdot(pc, vc) + _bdot(pp, vp)) / den, pcs, pps, dens))


def _attn_specs(nb):
    qs = pl.BlockSpec((GRP, WINDOW, AT_HD), lambda g, n: (g, n, 0))
    kc = pl.BlockSpec((1, WINDOW, AT_HD), lambda g, n: (g, n, 0))
    kp = pl.BlockSpec((1, WINDOW, AT_HD), lambda g, n: (g, jnp.maximum(n - 1, 0), 0))
    tc = pl.BlockSpec((WINDOW, AT_HD), lambda g, n: (n, 0))
    tp = pl.BlockSpec((WINDOW, AT_HD), lambda g, n: (jnp.maximum(n - 1, 0), 0))
    sk = pl.BlockSpec((GRP, 1, 128), lambda g, n: (g, 0, 0))
    rt = pl.BlockSpec((AT_HD, AT_HD), lambda g, n: (0, 0))
    return qs, kc, kp, tc, tp, sk, rt


def _attn_fwd(q, k, v, sinks, cos, sin, rot, name):
    T = q.shape[1]
    nb = T // WINDOW
    qs, kc, kp, tc, tp, sk, rt = _attn_specs(nb)

    def body(q_ref, kp_ref, kc_ref, vp_ref, vc_ref, sk_ref, cc_ref, sc_ref, cp_ref, sp_ref, rot_ref, o_ref):
        n = pl.program_id(1)
        heads = range(GRP)
        outs = _attn_f(tuple(q_ref[e] for e in heads), kp_ref[0], kc_ref[0], vp_ref[0], vc_ref[0], tuple(sk_ref[e] for e in heads), cc_ref[...], sc_ref[...],
                       cp_ref[...], sp_ref[...], rot_ref[...], n > 0)
        for e in range(GRP):
            o_ref[e] = outs[e]

    return pl.pallas_call(
        body, name=name, grid=(AT_KVH, nb),
        in_specs=[qs, kp, kc, kp, kc, sk, tc, tc, tp, tp, rt], out_specs=qs,
        out_shape=jax.ShapeDtypeStruct(q.shape, F32),
        compiler_params=_params(("parallel", "parallel")),
    )(q, k, k, v, v, sinks, cos, sin, cos, sin, rot)


def _attn_bwd(q, k, v, sinks, cos, sin, rot, do, name):
    T = q.shape[1]
    nb = T // WINDOW
    qs, kc, kp, tc, tp, sk, rt = _attn_specs(nb)

    def body(q_ref, kp_ref, kc_ref, vp_ref, vc_ref, sk_ref, cc_ref, sc_ref, cp_ref, sp_ref, rot_ref, do_ref,
             dq_ref, dkp_ref, dkc_ref, dvp_ref, dvc_ref, dsk_ref):
        n = pl.program_id(1)
        f = functools.partial(_attn_f, cos_c=cc_ref[...], sin_c=sc_ref[...], cos_p=cp_ref[...], sin_p=sp_ref[...],
                              rot=rot_ref[...], has_prev=n > 0)
        heads = range(GRP)
        _, vjp = jax.vjp(f, tuple(q_ref[e] for e in heads), kp_ref[0], kc_ref[0], vp_ref[0], vc_ref[0], tuple(sk_ref[e] for e in heads))
        dq, dkp, dkc, dvp, dvc, dsk = vjp(tuple(do_ref[e] for e in heads))
        dkp_ref[0], dkc_ref[0], dvp_ref[0], dvc_ref[0] = dkp, dkc, dvp, dvc

        @pl.when(n == 0)
        def _():
            dsk_ref[...] = jnp.zeros_like(dsk_ref)

        for e in heads:
            dq_ref[e] = dq[e]
            dsk_ref[e] += dsk[e]

    return pl.pallas_call(
        body, name=name, grid=(AT_KVH, nb),
        in_specs=[qs, kp, kc, kp, kc, sk, tc, tc, tp, tp, rt, qs], out_specs=[qs, kc, kc, kc, kc, sk],
        out_shape=[jax.ShapeDtypeStruct(q.shape, F32)] + [jax.ShapeDtypeStruct(k.shape, F32)] * 4 + [jax.ShapeDtypeStruct(sinks.shape, F32)],
        compiler_params=_params(("arbitrary", "arbitrary")),
    )(q, k, k, v, v, sinks, cos, sin, cos, sin, rot, do)


def _kv_combine(dc, dp, name):
    T = dc.shape[1]
    nb = T // WINDOW

    def body(c_ref, p_ref, o_ref):
        n = pl.program_id(1)
        o_ref[...] = c_ref[...] + jnp.where(n < nb - 1, p_ref[...], 0.0)

    return pl.pallas_call(
        body, name=name, grid=(AT_KVH, nb),
        in_specs=[pl.BlockSpec((1, WINDOW, AT_HD), lambda g, n: (g, n, 0)),
                  pl.BlockSpec((1, WINDOW, AT_HD), lambda g, n: (g, jnp.minimum(n + 1, nb - 1), 0))],
        out_specs=pl.BlockSpec((1, WINDOW, AT_HD), lambda g, n: (g, n, 0)),
        out_shape=jax.ShapeDtypeStruct((AT_KVH, T, AT_HD), F32),
        compiler_params=_params(("parallel", "parallel")),
    )(dc, dp)


def _place():
    x, y, c = lax.axis_index("x"), lax.axis_index("y"), lax.axis_index("c")
    return x, y, c, 4 * x + 2 * y + c


def _gather(src, name):
    def body(s_ref, o_ref, send_sems, recv_sems, lsem):
        x, y, c, _ = _place()
        me, sib = (x, y, c), (x, y, 1 - c)
        chips = [(1 - x, y), (x, 1 - y), (1 - x, 1 - y)]

        def copy(k, block, to, src_ref=None):
            slab = o_ref.at[4 * block[0] + 2 * block[1] + block[2]]
            return pltpu.make_async_remote_copy(src_ref=slab if src_ref is None else src_ref, dst_ref=slab,
                                                send_sem=send_sems.at[k], recv_sem=recv_sems.at[k], device_id=to, device_id_type=MESH)

        mine = pltpu.make_async_copy(s_ref, o_ref.at[4 * x + 2 * y + c], lsem)
        mine.start()
        first = [copy(0, me, sib, s_ref)] + [copy(1 + j, me, (*chip, c), s_ref) for j, chip in enumerate(chips)]
        for cp in first:
            cp.start()
        passed = [copy(4 + j, (*chip, c), sib) for j, chip in enumerate(chips)]
        for j, chip in enumerate(chips):
            copy(1 + j, (*chip, c), me).wait_recv()
            passed[j].start()
        copy(0, sib, me).wait_recv()
        for j, chip in enumerate(chips):
            copy(4 + j, (*chip, 1 - c), me).wait_recv()
        for cp in first + passed:
            cp.wait_send()
        mine.wait()

    return pl.pallas_call(
        body, name=name, in_specs=[ANY], out_specs=ANY, out_shape=jax.ShapeDtypeStruct((N_DEV,) + src.shape, src.dtype),
        scratch_shapes=[pltpu.SemaphoreType.DMA((N_DEV - 1,)), pltpu.SemaphoreType.DMA((N_DEV - 1,)), pltpu.SemaphoreType.DMA],
    )(src)


def _pair_swap(src, name):
    def body(s_ref, o_ref, send_sem, recv_sem):
        x, y, c, _ = _place()
        cp = pltpu.make_async_remote_copy(src_ref=s_ref.at[1 - c], dst_ref=o_ref, send_sem=send_sem, recv_sem=recv_sem,
                                          device_id=(x, y, 1 - c), device_id_type=MESH)
        cp.start()
        cp.wait()

    return pl.pallas_call(
        body, name=name, in_specs=[ANY], out_specs=ANY, out_shape=jax.ShapeDtypeStruct(src.shape[1:], src.dtype),
        scratch_shapes=[pltpu.SemaphoreType.DMA, pltpu.SemaphoreType.DMA],
    )(src)


def _chip_a2a(src, name):
    def body(s_ref, o_ref, send_sems, recv_sems, lsem):
        x, y, c, _ = _place()
        chip = 2 * x + y
        mine = pltpu.make_async_copy(s_ref.at[chip], o_ref.at[chip], lsem)
        mine.start()
        cps = []
        for k in (1, 2, 3):
            px, py = x ^ (k >> 1), y ^ (k & 1)
            cps.append(pltpu.make_async_remote_copy(src_ref=s_ref.at[2 * px + py], dst_ref=o_ref.at[chip], send_sem=send_sems.at[k - 1],
                                                    recv_sem=recv_sems.at[k - 1], device_id=(px, py, c), device_id_type=MESH))
            cps[-1].start()
        for cp in cps:
            cp.wait()
        mine.wait()

    return pl.pallas_call(
        body, name=name, in_specs=[ANY], out_specs=ANY, out_shape=jax.ShapeDtypeStruct(src.shape, src.dtype),
        scratch_shapes=[pltpu.SemaphoreType.DMA((3,)), pltpu.SemaphoreType.DMA((3,)), pltpu.SemaphoreType.DMA],
    )(src)


def _add(a, b, name):
    R, C = a.shape
    tr = _pick(R, (512, 256, 128, 64, 32, 16))

    def body(a_ref, b_ref, o_ref):
        o_ref[...] = (a_ref[...].astype(F32) + b_ref[...].astype(F32)).astype(o_ref.dtype)

    s2 = pl.BlockSpec((tr, C), lambda i: (i, 0))
    return pl.pallas_call(body, name=name, grid=(R // tr,), in_specs=[s2, s2], out_specs=s2,
                          out_shape=jax.ShapeDtypeStruct((R, C), a.dtype), compiler_params=_params(("parallel",)))(a, b)


def _reduce_scatter(slabs, name):
    _, R, C = slabs.shape
    by_core = jnp.transpose(slabs.reshape(4, 2, R, C), (1, 0, 2, 3))
    c = lax.axis_index("c")
    theirs = _pair_swap(by_core, "swap_" + name)
    mine = lax.dynamic_index_in_dim(by_core, c, axis=0, keepdims=False)
    summed = _add(mine.reshape(4 * R, C), theirs.reshape(4 * R, C), "add_" + name).reshape(4, R, C)
    return _chip_a2a(summed, "a2a_" + name)


def _adamw(w, m, v, parts, name):
    R, C = w.shape
    P = parts.shape[0]
    tr = _pick(R, (256, 128, 64, 32, 16, 8))
    c1, c2 = 1.0 - B1 ** STEP, 1.0 - B2 ** STEP

    def body(w_ref, m_ref, v_ref, p_ref, g_ref, d_ref, nm_ref, nv_ref):
        g = p_ref[0].astype(F32)
        for i in range(1, P):
            g = g + p_ref[i].astype(F32)
        wv = w_ref[...]
        nm = B1 * m_ref[...] + (1.0 - B1) * g
        nv = B2 * v_ref[...] + (1.0 - B2) * (g * g)
        g_ref[...] = g
        nm_ref[...] = nm
        nv_ref[...] = nv
        d_ref[...] = -LR * ((nm / c1) / (jnp.sqrt(nv / c2) + AEPS) + WD * wv)

    s2 = pl.BlockSpec((tr, C), lambda i: (i, 0))
    return pl.pallas_call(
        body, name=name, grid=(R // tr,),
        in_specs=[s2, s2, s2, pl.BlockSpec((P, tr, C), lambda i: (0, i, 0))], out_specs=[s2] * 4,
        out_shape=[jax.ShapeDtypeStruct((R, C), F32)] * 4,
        compiler_params=_params(("parallel",)),
    )(w, m, v, parts)


def _colsum(a, name):
    def body(a_ref, o_ref):
        o_ref[...] = jnp.broadcast_to(jnp.sum(a_ref[...], axis=0, keepdims=True), o_ref.shape)

    return pl.pallas_call(body, name=name, out_shape=jax.ShapeDtypeStruct((8, 128), F32))(a)


def _rows128(a):
    f = a.reshape(-1)
    return jnp.pad(f, (0, (-f.shape[0]) % 128)).reshape(-1, 128)


def _to_aligned(w):
    return jnp.concatenate([w[..., 0:4096], w[..., 4112:5392], w[..., 4096:4112],
                            jnp.zeros(w.shape[:-1] + (IN_PAD - IN_COLS,), w.dtype)], axis=-1)


def _from_aligned(w):
    return jnp.concatenate([w[..., 0:4096], w[..., 5376:5392], w[..., 4096:5376]], axis=-1)


def kernel(x, c, ln_mix, ln_ffn, w_ada, b_ada, w_in, dn_conv_w, dn_a_log, dn_dt_bias, dn_norm_w, attn_sinks, w_out, w_gate_up, w_down, ln_final, loss_target, m_ln_mix, m_ln_ffn, m_w_ada, m_b_ada, m_w_in, m_dn_conv_w, m_dn_a_log, m_dn_dt_bias, m_dn_norm_w, m_attn_sinks, m_w_out, m_w_gate_up, m_w_down, m_ln_final, v_ln_mix, v_ln_ffn, v_w_ada, v_b_ada, v_w_in, v_dn_conv_w, v_dn_a_log, v_dn_dt_bias, v_dn_norm_w, v_attn_sinks, v_w_out, v_w_gate_up, v_w_down, v_ln_final):
    T = x.shape[1]
    L = ln_mix.shape[0]
    me = 4 * lax.axis_index("x") + 2 * lax.axis_index("y") + lax.axis_index("c")
    xs = x[0]
    tgt = loss_target[0]

    g_in = _gather(w_in.astype(BF16).reshape(L * D, -1), "ag_w_in").reshape(N_DEV, L, D, -1)
    g_out = _gather(w_out.astype(BF16).reshape(-1, D), "ag_w_out").reshape(N_DEV, L, -1, D)
    g_gu = _gather(w_gate_up.astype(BF16).reshape(L * D, -1), "ag_w_gu").reshape(N_DEV, L, D, -1)
    g_dn = _gather(w_down.astype(BF16).reshape(-1, D), "ag_w_down").reshape(N_DEV, L, -1, D)
    g_cv = _gather(dn_conv_w.reshape(L * CONV_K, -1), "ag_conv").reshape(N_DEV, L, CONV_K, -1)
    c_all = _gather(jnp.pad(c, ((0, 7), (0, 0))), "ag_c")[:, 0, :]
    W_in = [_to_aligned(jnp.transpose(g_in[:, l], (1, 0, 2)).reshape(D, IN_COLS)) for l in range(L)]
    W_out = [g_out[:, l].reshape(D, D) for l in range(L)]
    W_gate = [jnp.transpose(g_gu[:4, l], (1, 0, 2)).reshape(D, FFN) for l in range(L)]
    W_up = [jnp.transpose(g_gu[4:, l], (1, 0, 2)).reshape(D, FFN) for l in range(L)]
    W_dn = [g_dn[:, l].reshape(FFN, D) for l in range(L)]
    W_cv = [jnp.transpose(g_cv[:, l], (1, 0, 2)).reshape(CONV_K, 3 * DN_W) for l in range(L)]

    c_act = _ew_fwd(lambda v: (_silu(v),), [(jnp.pad(c_all, ((0, 8), (0, 0))), True, False)], [(D, F32, False)], tm=16, name="c_act")[0]
    mods = []
    for l in range(L):
        ms = _mm(c_act, w_ada[l], "nn", F32, f"mod_mm{l}")
        ga = _gather(ms, f"ag_mod{l}")
        mods.append(lax.dynamic_index_in_dim(ga, me, axis=1, keepdims=False).reshape(1, 6 * D))
    row = lambda a: a.reshape(1, -1)
    seg = lambda a, i: a[:, i * D:(i + 1) * D]

    half = AT_HD // 2
    inv_freq = 10000.0 ** (-jnp.arange(half, dtype=F32) * 2.0 / AT_HD)
    ang = jnp.arange(T, dtype=jnp.int32).astype(F32)[:, None] * inv_freq[None, :]
    cos = jnp.concatenate([jnp.cos(ang)] * 2, axis=-1)
    sin = jnp.concatenate([jnp.sin(ang)] * 2, axis=-1)
    ii = jnp.arange(AT_HD)
    rot = jnp.where(ii[:, None] == ii[None, :] + half, -1.0, 0.0) + jnp.where(ii[:, None] + half == ii[None, :], 1.0, 0.0)
    rot = rot.astype(F32)
    heads = lambda a, nh: jnp.transpose(a.reshape(T, nh, AT_HD), (1, 0, 2))
    unheads = lambda a: jnp.transpose(a, (1, 0, 2)).reshape(T, -1)
    pad16 = lambda a: jnp.pad(row(a), ((0, 0), (8, 128 - 16)))

    saved = []
    xc = xs
    for l in range(L):
        mod, bmod = mods[l], row(b_ada[l])
        s = {"x": xc}
        nm_ops = lambda xx, ln, a, b: [(xx, True, False), (row(ln), False, False), (seg(mod, a), False, False),
                                       (seg(bmod, a), False, False), (seg(mod, b), False, False), (seg(bmod, b), False, False)]
        h1 = _ew_fwd(lambda *a: _f_normmod(*a)[:1], nm_ops(xc, ln_mix[l], 1, 0), [(D, BF16, False)], tm=256, name=f"normmod1_{l}")[0]
        proj = _mm(h1, W_in[l], "nn", F32, f"mm_in{l}")
        aq, ak, av = proj[:, 4096:5120], proj[:, 5120:5248], proj[:, 5248:5376]
        conv = _conv_fwd(proj, W_cv[l], 3 * DN_W, f"conv{l}")
        pre_ops = [(conv, True, True)]
        qkvn = _ew_fwd(_f_dnpre, pre_ops, [(3 * DN_W, F32, True)], tm=512, ncol=3 * DN_HEADS, name=f"dnpre{l}", with_j=True)[0]
        bg_ops = [(proj, True, False, (5376, 128)), (pad16(dn_a_log[l]), False, False), (pad16(dn_dt_bias[l]), False, False)]
        bg = _ew_fwd(_f_bg, bg_ops, [(128, F32, False)], tm=1024, name=f"bg{l}")[0]
        o, s_saved = _chunk_fwd(qkvn, bg, f"chunk{l}")
        post_ops = [(o, True, True), (proj, True, True, (3072, DN_W)), (row(dn_norm_w[l]), False, False)]
        dn_out = _ew_fwd(_f_dnpost, post_ops, [(DN_W, BF16, True)], tm=512, ncol=DN_HEADS, name=f"dnpost{l}")[0]
        qh, kh, vh = heads(aq, AT_QH), heads(ak, AT_KVH), heads(av, AT_KVH)
        sk = jnp.broadcast_to(attn_sinks[l][:, None, None], (AT_QH, 1, 128))
        at_out = unheads(_attn_fwd(qh, kh, vh, sk, cos, sin, rot, f"attn{l}")).astype(BF16)
        mix = _mm(at_out, W_out[l][DN_W:], "nn", F32, f"mm_out_at{l}", acc_in=_mm(dn_out, W_out[l][:DN_W], "nn", F32, f"mm_out_dn{l}"))
        rg_ops = lambda xx, br, a: [(xx, True, False), (br, True, False), (seg(mod, a), False, False), (seg(bmod, a), False, False)]
        x1 = _ew_fwd(_f_resgate, rg_ops(xc, mix, 2), [(D, F32, False)], tm=256, name=f"resgate1_{l}")[0]
        h2 = _ew_fwd(lambda *a: _f_normmod(*a)[:1], nm_ops(x1, ln_ffn[l], 4, 3), [(D, BF16, False)], tm=256, name=f"normmod2_{l}")[0]
        gate = _mm(h2, W_gate[l], "nn", BF16, f"mm_gate{l}")
        up = _mm(h2, W_up[l], "nn", BF16, f"mm_up{l}")
        sw_ops = [(gate, True, True), (up, True, True)]
        act = _ew_fwd(_f_swiglu, sw_ops, [(FFN, BF16, True)], tm=512, ncol=11, name=f"swiglu{l}")[0]
        down = _mm(act, W_dn[l], "nn", F32, f"mm_down{l}")
        x2 = _ew_fwd(_f_resgate, rg_ops(x1, down, 5), [(D, F32, False)], tm=256, name=f"resgate2_{l}")[0]
        s.update(h1=h1, proj=proj, qkvn=qkvn, bg=bg, s_saved=s_saved, qh=qh, kh=kh, vh=vh, sk=sk, dn_out=dn_out, at_out=at_out,
                 h2=h2, act=act, bg_ops=bg_ops, post_ops=post_ops, pre_ops=pre_ops, sw_ops=sw_ops,
                 nm1=nm_ops(xc, ln_mix[l], 1, 0), nm2=nm_ops(x1, ln_ffn[l], 4, 3), rg1=rg_ops(xc, mix, 2), rg2=rg_ops(x1, down, 5))
        saved.append(s)
        xc = x2

    fin_ops = [(xc, True, False), (tgt, True, False), (row(ln_final), False, False)]
    lrow = _ew_fwd(_f_final, fin_ops, [(128, F32, False)], tm=256, name="loss_rows")[0]
    loss = lax.psum(_colsum(lrow, "loss_sum")[0, 0], ("x", "y", "c"))
    dx, d_ln_final = _ew_bwd(_f_final, fin_ops, [(jnp.ones((T, 128), F32) / 128.0, False)], [0, 2], tm=256, name="loss_bwd")

    small = {k: [None] * L for k in ("ln_mix", "ln_ffn", "mod", "a_log", "dt", "norm_w", "sinks", "conv")}
    big = {k: [None] * L for k in ("w_in", "w_out", "w_gate", "w_up", "w_dn")}
    for l in reversed(range(L)):
        s = saved[l]
        ddown, dgt_f = _ew_bwd(_f_resgate, s["rg2"], [(dx, False)], [1, 2], tm=256, name=f"resgate2_bwd{l}", gdt=[BF16, F32])
        big["w_dn"][l] = _mm(s["act"], ddown, "tn", BF16, f"wg_down{l}")
        dact = _mm(ddown, W_dn[l], "nt", F32, f"dg_down{l}")
        dgate, dup = _ew_bwd(_f_swiglu, s["sw_ops"], [(dact, True)], [0, 1], tm=512, ncol=11, name=f"swiglu_bwd{l}", gdt=[BF16, BF16])
        big["w_gate"][l] = _mm(s["h2"], dgate, "tn", BF16, f"wg_gate{l}")
        big["w_up"][l] = _mm(s["h2"], dup, "tn", BF16, f"wg_up{l}")
        dh2 = _mm(dup, W_up[l], "nt", F32, f"dg_up{l}", acc_in=_mm(dgate, W_gate[l], "nt", F32, f"dg_gate{l}"))
        dx1, dln_f, dsc_f, dsh_f = _ew_bwd(_f_normmod, s["nm2"], [(dh2, False), (dx, False)], [0, 1, 2, 4], tm=256, name=f"normmod2_bwd{l}")
        dmix, dgt_m = _ew_bwd(_f_resgate, s["rg1"], [(dx1, False)], [1, 2], tm=256, name=f"resgate1_bwd{l}", gdt=[BF16, F32])
        big["w_out"][l] = jnp.concatenate([_mm(s["dn_out"], dmix, "tn", BF16, f"wg_out_dn{l}"),
                                           _mm(s["at_out"], dmix, "tn", BF16, f"wg_out_at{l}")], axis=0)
        d_dn = _mm(dmix, W_out[l][:DN_W], "nt", F32, f"dg_out_dn{l}")
        d_at = _mm(dmix, W_out[l][DN_W:], "nt", F32, f"dg_out_at{l}")
        dqh, dkp, dkc, dvp, dvc, dsk = _attn_bwd(s["qh"], s["kh"], s["vh"], s["sk"], cos, sin, rot, heads(d_at, AT_QH), f"attn_bwd{l}")
        dkh = _kv_combine(dkc, dkp, f"dk_comb{l}")
        dvh = _kv_combine(dvc, dvp, f"dv_comb{l}")
        do, dz, dnw = _ew_bwd(_f_dnpost, s["post_ops"], [(d_dn, True)], [0, 1, 2], tm=512, ncol=DN_HEADS, name=f"dnpost_bwd{l}")
        dqkvn, dbg = _chunk_bwd(s["qkvn"], s["bg"], s["s_saved"], do, f"chunk_bwd{l}")
        dconv = _ew_bwd(_f_dnpre, s["pre_ops"], [(dqkvn, True)], [0], tm=512, ncol=3 * DN_HEADS, name=f"dnpre_bwd{l}", with_j=True)[0]
        dba, dalog, ddt = _ew_bwd(_f_bg, s["bg_ops"], [(dbg, False)], [0, 1, 2], tm=1024, name=f"bg_bwd{l}")
        dqkv, dcw = _conv_bwd(s["proj"], dconv, W_cv[l], f"conv_bwd{l}")
        dproj = jnp.concatenate([dqkv, dz, unheads(dqh), unheads(dkh), unheads(dvh), dba, jnp.zeros((T, IN_PAD - 5504), F32)],
                                axis=-1).astype(BF16)
        big["w_in"][l] = _mm(s["h1"], dproj, "tn", BF16, f"wg_in{l}")
        dh1 = _mm(dproj, W_in[l], "nt", F32, f"dg_in{l}")
        dx, dln_m, dsc_m, dsh_m = _ew_bwd(_f_normmod, s["nm1"], [(dh1, False), (dx1, False)], [0, 1, 2, 4], tm=256, name=f"normmod1_bwd{l}")
        small["ln_mix"][l], small["ln_ffn"][l] = dln_m, dln_f
        small["mod"][l] = jnp.concatenate([dsh_m, dsc_m, dgt_m, dsh_f, dsc_f, dgt_f], axis=-1)
        small["a_log"][l], small["dt"][l] = dalog[:, 8:16], ddt[:, 8:16]
        small["norm_w"][l], small["sinks"][l], small["conv"][l] = dnw, dsk[:, 0, 0], dcw

    cat0 = lambda xs_: jnp.concatenate([_rows128(a) for a in xs_], axis=0)
    stk = lambda k: jnp.stack(small[k])
    pack = cat0([stk("ln_mix"), stk("ln_ffn"), stk("mod"), stk("a_log"), stk("dt"), stk("norm_w"), stk("sinks"), d_ln_final, stk("conv")])
    n_small = pack.shape[0] - L * CONV_K * 3 * DN_W // 128
    pack = jnp.pad(pack, ((0, (-pack.shape[0]) % 8), (0, 0)))
    gp = _gather(pack, "ag_small")
    parts_small = gp[:, :n_small]
    dmod_all = gp[:, 2 * L * D // 128:2 * L * D // 128 + L * 6 * D // 128].reshape(N_DEV, L, 6 * D)
    conv_all = gp[:, n_small:n_small + L * CONV_K * 3 * DN_W // 128].reshape(N_DEV, L * CONV_K, 3 * DN_W)
    parts_conv = lax.dynamic_slice_in_dim(conv_all, me * (3 * DN_W // N_DEV), 3 * DN_W // N_DEV, axis=2)

    def shards(gs, cols, n=N_DEV):
        g = jnp.stack(gs)
        if cols:
            return jnp.transpose(g.reshape(L, g.shape[1], n, -1), (2, 0, 1, 3)).reshape(n, L * g.shape[1], -1)
        return jnp.transpose(g.reshape(L, n, -1, g.shape[2]), (1, 0, 2, 3)).reshape(n, -1, g.shape[2])

    p_in = _reduce_scatter(shards([_from_aligned(g) for g in big["w_in"]], True), "w_in")
    p_out = _reduce_scatter(shards(big["w_out"], False), "w_out")
    p_gu = _reduce_scatter(jnp.concatenate([shards(big["w_gate"], True, 4), shards(big["w_up"], True, 4)], axis=0), "w_gu")
    p_dn = _reduce_scatter(shards(big["w_dn"], False), "w_down")
    dmod_mine = lax.dynamic_slice_in_dim(dmod_all, me * (6 * D // N_DEV), 6 * D // N_DEV, axis=2)
    g_ada = jnp.stack([_mm(c_act, jnp.pad(dmod_mine[:, l], ((0, 8), (0, 0))), "tn", F32, f"wg_ada{l}") for l in range(L)])

    def upd(w, m, v, parts, name):
        shp = w.shape
        r = lambda a: a.reshape(-1, shp[-1])
        return [o_.reshape(shp) for o_ in _adamw(r(w), r(m), r(v), parts.reshape(parts.shape[0], -1, shp[-1]), name)]

    res = {}
    res["w_ada"] = upd(w_ada, m_w_ada, v_w_ada, g_ada[None], "adamw_ada")
    res["w_in"] = upd(w_in, m_w_in, v_w_in, p_in, "adamw_in")
    res["dn_conv_w"] = upd(dn_conv_w, m_dn_conv_w, v_dn_conv_w, parts_conv, "adamw_conv")
    res["w_out"] = upd(w_out, m_w_out, v_w_out, p_out, "adamw_out")
    res["w_gate_up"] = upd(w_gate_up, m_w_gate_up, v_w_gate_up, p_gu, "adamw_gu")
    res["w_down"] = upd(w_down, m_w_down, v_w_down, p_dn, "adamw_down")
    names_s = ["ln_mix", "ln_ffn", "b_ada", "dn_a_log", "dn_dt_bias", "dn_norm_w", "attn_sinks", "ln_final"]
    ws = [ln_mix, ln_ffn, b_ada, dn_a_log, dn_dt_bias, dn_norm_w, attn_sinks, ln_final]
    ms = [m_ln_mix, m_ln_ffn, m_b_ada, m_dn_a_log, m_dn_dt_bias, m_dn_norm_w, m_attn_sinks, m_ln_final]
    vs = [v_ln_mix, v_ln_ffn, v_b_ada, v_dn_a_log, v_dn_dt_bias, v_dn_norm_w, v_attn_sinks, v_ln_final]
    padr = lambda a: jnp.pad(a, ((0, (-a.shape[0]) % 8), (0, 0)))
    vpad = jnp.pad(cat0(vs), ((0, (-n_small) % 8), (0, 0)), constant_values=1.0)
    outs_s = _adamw(padr(cat0(ws)), padr(cat0(ms)), vpad, jnp.pad(parts_small, ((0, 0), (0, (-n_small) % 8), (0, 0))), "adamw_small")
    off = 0
    for nme, wv in zip(names_s, ws):
        nrow = -(-wv.size // 128)
        res[nme] = [o_[off:off + nrow].reshape(-1)[:wv.size].reshape(wv.shape) for o_ in outs_s]
        off += nrow

    order = ["ln_mix", "ln_ffn", "w_ada", "b_ada", "w_in", "dn_conv_w", "dn_a_log", "dn_dt_bias", "dn_norm_w", "attn_sinks",
             "w_out", "w_gate_up", "w_down", "ln_final"]
    return (loss, dx[None], *[res[n][0] for n in order], *[res[n][1] for n in order], *[res[n][2] for n in order],
            *[res[n][3] for n in order])
```

```python
import functools

import jax
import jax.numpy as jnp
from jax import lax
from jax.experimental import pallas as pl
from jax.experimental.pallas import tpu as pltpu

F32, BF16 = jnp.float32, jnp.bfloat16
HI = lax.Precision.HIGH
MESH = pl.DeviceIdType.MESH
ANY = pl.BlockSpec(memory_space=pl.ANY)

N_DEV = 8
D = 2048
DN_HEADS, DN_HD = 8, 128
DN_W = 1024
CONV_K = 4
CHUNK = 64
AT_HD, AT_QH, AT_KVH = 64, 16, 2
AT_W = 1024
WINDOW = 128
FFN = 5632
IN_COLS = 5392
IN_PAD = 5632
EPS = 1e-6
NEG = -1e30
LR, B1, B2, AEPS, WD, STEP = 0.001, 0.9, 0.999, 1e-08, 0.01, 10
VMEM_LIMIT = 56 * 1024 * 1024


def _pick(n, cands):
    for c in cands:
        if n % c == 0:
            return c
    return n


def _params(sem):
    return pltpu.CompilerParams(dimension_semantics=sem, vmem_limit_bytes=VMEM_LIMIT)


_DN = {"nn": (((1,), (0,)), ((), ())), "nt": (((1,), (1,)), ((), ())), "tn": (((0,), (0,)), ((), ()))}


def _mm(a, b, mode, out_dtype, name, acc_in=None):
    if mode == "nn":
        (M, K), (_, N) = a.shape, b.shape
    elif mode == "nt":
        (M, K), (N, _) = a.shape, b.shape
    else:
        (K, M), (_, N) = a.shape, b.shape
    tm = _pick(M, (1024, 512, 256, 128, 64, 32, 16))
    tn = _pick(N, (1024, 512, 256, 128))
    tk = K if K <= 2048 else _pick(K, (2816, 2048, 1024, 512, 256, 128))
    nk = K // tk
    dn = _DN[mode]
    n_in = 2 if acc_in is None else 3

    def body(*refs):
        a_ref, b_ref, o_ref = refs[0], refs[1], refs[n_in]
        part = lax.dot_general(a_ref[...].astype(BF16), b_ref[...].astype(BF16), dn, preferred_element_type=F32)
        if nk == 1:
            if acc_in is not None:
                part = part + refs[2][...]
            o_ref[...] = part.astype(o_ref.dtype)
            return
        acc = refs[n_in + 1]
        k = pl.program_id(2)

        @pl.when(k == 0)
        def _():
            acc[...] = part if acc_in is None else part + refs[2][...]

        @pl.when(k > 0)
        def _():
            acc[...] += part

        @pl.when(k == nk - 1)
        def _():
            o_ref[...] = acc[...].astype(o_ref.dtype)

    a_spec = pl.BlockSpec((tk, tm), lambda i, j, k: (k, i)) if mode == "tn" else pl.BlockSpec((tm, tk), lambda i, j, k: (i, k))
    b_spec = pl.BlockSpec((tn, tk), lambda i, j, k: (j, k)) if mode == "nt" else pl.BlockSpec((tk, tn), lambda i, j, k: (k, j))
    o_spec = pl.BlockSpec((tm, tn), lambda i, j, k: (i, j))
    return pl.pallas_call(
        body, name=name, grid=(M // tm, N // tn, nk),
        in_specs=[a_spec, b_spec] + ([] if acc_in is None else [o_spec]), out_specs=o_spec,
        out_shape=jax.ShapeDtypeStruct((M, N), out_dtype),
        scratch_shapes=[pltpu.VMEM((tm, tn), F32)] if nk > 1 else [],
        compiler_params=_params(("parallel", "parallel", "arbitrary")),
    )(*((a, b) if acc_in is None else (a, b, acc_in)))


def _op(op):
    arr, rt, cb = op[:3]
    start, width = op[3] if len(op) > 3 else (0, arr.shape[1])
    return arr, rt, cb, start, width


def _ew_spec(op, tm, ncol):
    arr, rt, cb, start, width = _op(op)
    bw = width // ncol if cb else width
    first = start // bw
    return pl.BlockSpec((tm if rt else arr.shape[0], bw), lambda j, i: (i if rt else 0, first + (j if cb else 0)))


def _ew_fwd(f, ops, outs, *, tm, ncol=1, name, with_j=False):
    M = next(op[0].shape[0] for op in ops if op[1])
    tm = min(tm, M)
    n_in = len(ops)

    def body(*refs):
        res = f(*[r[...].astype(F32) for r in refs[:n_in]], *([pl.program_id(0)] if with_j else []))
        for o, r in zip(refs[n_in:], res):
            o[...] = r.astype(o.dtype)

    return pl.pallas_call(
        body, name=name, grid=(ncol, M // tm),
        in_specs=[_ew_spec(op, tm, ncol) for op in ops],
        out_specs=[pl.BlockSpec((tm, w // ncol if cb else w), lambda j, i, cb=cb: (i, j if cb else 0)) for w, _, cb in outs],
        out_shape=[jax.ShapeDtypeStruct((M, w), dt) for w, dt, _ in outs],
        compiler_params=_params(("parallel", "parallel")),
    )(*[op[0] for op in ops])


def _ew_bwd(f, ops, cts, diff, *, tm, ncol=1, name, gdt=None, with_j=False):
    M = next(op[0].shape[0] for op in ops if op[1])
    tm = min(tm, M)
    n_in, n_ct = len(ops), len(cts)
    gdt = gdt or [F32] * len(diff)

    def body(*refs):
        j, i = pl.program_id(0), pl.program_id(1)
        vals = [r[...].astype(F32) for r in refs[:n_in]]

        def fd(*dv):
            full = list(vals)
            for idx, v in zip(diff, dv):
                full[idx] = v
            return tuple(f(*full, *([j] if with_j else [])))

        _, vjp = jax.vjp(fd, *[vals[idx] for idx in diff])
        gs = vjp(tuple(r[...].astype(F32) for r in refs[n_in:n_in + n_ct]))
        for idx, g, gref in zip(diff, gs, refs[n_in + n_ct:]):
            _, rt, cb = ops[idx][:3]
            if rt:
                gref[...] = g.astype(gref.dtype)
            else:
                first = (i == 0) if cb else jnp.logical_and(i == 0, j == 0)

                @pl.when(first)
                def _(gref=gref):
                    gref[...] = jnp.zeros_like(gref)

                gref[...] += g

    def g_spec(op):
        arr, rt, cb, start, width = _op(op)
        return _ew_spec((jax.ShapeDtypeStruct((arr.shape[0], width), F32), rt, cb), tm, ncol) if rt else _ew_spec(op, tm, ncol)

    def g_shape(op, dt):
        arr, rt, cb, start, width = _op(op)
        return jax.ShapeDtypeStruct((M, width), dt) if rt else jax.ShapeDtypeStruct(arr.shape, F32)

    return pl.pallas_call(
        body, name=name, grid=(ncol, M // tm),
        in_specs=[_ew_spec(op, tm, ncol) for op in ops]
        + [pl.BlockSpec((tm, a.shape[1] // ncol if cb else a.shape[1]), lambda j, i, cb=cb: (i, j if cb else 0)) for a, cb in cts],
        out_specs=[g_spec(ops[idx]) for idx in diff],
        out_shape=[g_shape(ops[idx], dt) for idx, dt in zip(diff, gdt)],
        compiler_params=_params(("arbitrary", "arbitrary")),
    )(*[op[0] for op in ops], *[a for a, _ in cts])


def _silu(x):
    return x * jax.nn.sigmoid(x)


def _f_normmod(x, ln, sc, bsc, sh, bsh):
    y = x * lax.rsqrt(jnp.mean(x * x, axis=-1, keepdims=True) + EPS) * ln
    return y * (1.0 + (sc + bsc)) + (sh + bsh), x


def _f_resgate(x, br, gt, bgt):
    return (x + (gt + bgt) * br,)


def _f_swiglu(gate, up):
    return (_silu(gate) * up,)


def _f_final(x, tgt, ln):
    y = x * lax.rsqrt(jnp.mean(x * x, axis=-1, keepdims=True) + EPS) * ln
    e = y - tgt
    return (jnp.broadcast_to(0.5 * jnp.mean(e * e, axis=-1, keepdims=True), (x.shape[0], 128)),)


def _f_bg(ba, alog, dt):
    col = lax.broadcasted_iota(jnp.int32, ba.shape, 1)
    z = ba + dt
    sp = jnp.maximum(z, 0.0) + jnp.log(1.0 + jnp.exp(-jnp.abs(z)))
    return (jnp.where(col < 8, jax.nn.sigmoid(ba), jnp.where(col < 16, -jnp.exp(alog) * sp, 0.0)),)


def _l2n(x):
    return x * lax.rsqrt(jnp.sum(x * x, axis=-1, keepdims=True) + EPS)


def _f_dnpre(c, j):
    a = _silu(c)
    return (jnp.where(j < 2 * DN_HEADS, _l2n(a) * jnp.where(j < DN_HEADS, DN_HD ** -0.5, 1.0), a),)


def _f_dnpost(o, z, nw):
    return (o * lax.rsqrt(jnp.mean(o * o, axis=-1, keepdims=True) + EPS) * nw * _silu(z),)


CONV_BW = 512


def _conv_fwd(x, w, Cw, name):
    T = x.shape[0]
    tm, bw = 512, CONV_BW

    def body(x_ref, h_ref, w_ref, o_ref):
        i = pl.program_id(1)
        cur, halo, wv = x_ref[...], h_ref[...], w_ref[...]
        halo = jnp.where(i > 0, halo, 0.0)
        row = lax.broadcasted_iota(jnp.int32, (8, bw), 0)
        acc = wv[3:4, :] * cur
        for s in (1, 2, 3):
            r = pltpu.roll(cur, s, 0)
            top = jnp.where(row < s, pltpu.roll(halo, s, 0), r[:8])
            acc += wv[3 - s:4 - s, :] * jnp.concatenate([top, r[8:]], axis=0)
        o_ref[...] = acc

    return pl.pallas_call(
        body, name=name, grid=(Cw // bw, T // tm),
        in_specs=[pl.BlockSpec((tm, bw), lambda j, i: (i, j)),
                  pl.BlockSpec((8, bw), lambda j, i: (jnp.maximum(i * (tm // 8) - 1, 0), j)),
                  pl.BlockSpec((4, bw), lambda j, i: (0, j))],
        out_specs=pl.BlockSpec((tm, bw), lambda j, i: (i, j)),
        out_shape=jax.ShapeDtypeStruct((T, Cw), F32),
        compiler_params=_params(("parallel", "parallel")),
    )(x, x, w)


def _conv_bwd(x, dy, w, name):
    T, Cw = dy.shape
    tm, bw = 512, CONV_BW
    nt = T // tm

    def body(x_ref, h_ref, dy_ref, n_ref, w_ref, dx_ref, dw_ref):
        i = pl.program_id(1)
        cur, dcur, wv = x_ref[...], dy_ref[...], w_ref[...]
        halo = jnp.where(i > 0, h_ref[...], 0.0)
        nxt = jnp.where(i < nt - 1, n_ref[...], 0.0)
        row = lax.broadcasted_iota(jnp.int32, (8, bw), 0)

        @pl.when(i == 0)
        def _():
            dw_ref[...] = jnp.zeros_like(dw_ref)

        dx = wv[3:4, :] * dcur
        dw_ref[3:4, :] += jnp.sum(dcur * cur, axis=0, keepdims=True)
        for s in (1, 2, 3):
            r = pltpu.roll(cur, s, 0)
            top = jnp.where(row < s, pltpu.roll(halo, s, 0), r[:8])
            xs = jnp.concatenate([top, r[8:]], axis=0)
            dw_ref[3 - s:4 - s, :] += jnp.sum(dcur * xs, axis=0, keepdims=True)
            rf = pltpu.roll(dcur, tm - s, 0)
            bot = jnp.where(row >= 8 - s, pltpu.roll(nxt, 8 - s, 0), rf[tm - 8:])
            dx += wv[3 - s:4 - s, :] * jnp.concatenate([rf[:tm - 8], bot], axis=0)
        dx_ref[...] = dx

    return pl.pallas_call(
        body, name=name, grid=(Cw // bw, nt),
        in_specs=[pl.BlockSpec((tm, bw), lambda j, i: (i, j)),
                  pl.BlockSpec((8, bw), lambda j, i: (jnp.maximum(i * (tm // 8) - 1, 0), j)),
                  pl.BlockSpec((tm, bw), lambda j, i: (i, j)),
                  pl.BlockSpec((8, bw), lambda j, i: (jnp.minimum((i + 1) * (tm // 8), T // 8 - 1), j)),
                  pl.BlockSpec((4, bw), lambda j, i: (0, j))],
        out_specs=[pl.BlockSpec((tm, bw), lambda j, i: (i, j)), pl.BlockSpec((4, bw), lambda j, i: (0, j))],
        out_shape=[jax.ShapeDtypeStruct((T, Cw), F32), jax.ShapeDtypeStruct((4, Cw), F32)],
        compiler_params=_params(("arbitrary", "arbitrary")),
    )(x, x, dy, dy, w)


def _dot(a, b):
    return jnp.dot(a, b, precision=HI, preferred_element_type=F32)


def _dot_t(a, b):
    return lax.dot_general(a, b, (((1,), (1,)), ((), ())), precision=HI, preferred_element_type=F32)


def _bdot(a, b):
    return jnp.dot(a.astype(BF16), b.astype(BF16), preferred_element_type=F32)


def _bdot_t(a, b):
    return lax.dot_general(a.astype(BF16), b.astype(BF16), (((1,), (1,)), ((), ())), preferred_element_type=F32)


def _each(f, *lists):
    return [f(*a) for a in zip(*lists)]


@jax.custom_vjp
def _unit_lower_inverses(ps):
    C = ps[0].shape[0]
    dist = jnp.bitwise_xor(lax.broadcasted_iota(jnp.int32, (C, C), 0), lax.broadcasted_iota(jnp.int32, (C, C), 1))
    ns = [jnp.where(dist < 8, p, 0.0) for p in ps]
    tis = [jnp.where(dist == 0, 1.0, 0.0) + n for n in ns]
    for _ in range(2):
        ns = _each(lambda n: _dot(n, n), ns)
        tis = _each(lambda t, n: t + _dot(t, n), tis, ns)
    b = 8
    while b < C:
        mids = [jnp.where(jnp.logical_and(dist >= b, dist < 2 * b), p, 0.0) for p in ps]
        halves = _each(_dot, mids, tis)
        tis = _each(lambda t, h: t + _dot(t, h), tis, halves)
        b *= 2
    return tis


def _uli_fwd(ps):
    tis = _unit_lower_inverses(ps)
    return tis, tis


def _uli_bwd(tis, cts):
    tt = lambda a, b: lax.dot_general(a, b, (((0,), (0,)), ((), ())), precision=HI, preferred_element_type=F32)
    half = _each(tt, tis, cts)
    return (_each(_dot_t, half, tis),)


_unit_lower_inverses.defvjp(_uli_fwd, _uli_bwd)


def _chunk_f(qs, ks, vs, bg, Ss, hs):
    C = CHUNK
    lane = lax.broadcasted_iota(jnp.int32, (C, 128), 1)
    betas = [jnp.sum(jnp.where(lane == h, bg, 0.0), axis=1, keepdims=True) for h in hs]
    gs = [jnp.sum(jnp.where(lane == h + 8, bg, 0.0), axis=1, keepdims=True) for h in hs]
    ri = lax.broadcasted_iota(jnp.int32, (C, C), 0)
    ci = lax.broadcasted_iota(jnp.int32, (C, C), 1)
    causal, strict, eye = ri >= ci, ri > ci, ri == ci
    g_rows = _each(lambda g: jnp.sum(jnp.where(eye, g, 0.0), axis=0, keepdims=True), gs)
    gc_cols = _each(lambda gr: jnp.sum(jnp.where(causal, gr, 0.0), axis=1, keepdims=True), g_rows)
    gc_rows = _each(lambda g: jnp.sum(jnp.where(ri <= ci, g, 0.0), axis=0, keepdims=True), gs)
    gc_lasts = _each(lambda g: jnp.sum(g, axis=0, keepdims=True), gs)
    decays = _each(lambda c, r: jnp.exp(jnp.where(causal, c - r, NEG)), gc_cols, gc_rows)
    kbs = _each(jnp.multiply, ks, betas)
    vbs = _each(jnp.multiply, vs, betas)
    ps = _each(lambda kb, k, d: -jnp.where(strict, _bdot_t(kb, k) * d, 0.0), kbs, ks, decays)
    tis = _unit_lower_inverses(ps)
    egcs = _each(jnp.exp, gc_cols)
    ws = _each(lambda t, kb, e: _bdot(t, kb * e), tis, kbs, egcs)
    us = _each(_bdot, tis, vbs)
    intras = _each(lambda q, k, d: jnp.where(causal, _bdot_t(q, k) * d, 0.0), qs, ks, decays)
    kds = _each(lambda k, gl, gc: k * jnp.exp(gl - gc), ks, gc_lasts, gc_cols)
    vns = _each(lambda u, w, S: u - _bdot(w, S), us, ws, Ss)
    os_ = _each(lambda q, e, S, i, vn: _bdot(q * e, S) + _bdot(i, vn), qs, egcs, Ss, intras, vns)
    Sn = _each(lambda S, gl, kd, vn: S * jnp.exp(gl) + lax.dot_general(
        kd.astype(BF16), vn.astype(BF16), (((0,), (0,)), ((), ())), preferred_element_type=F32), Ss, gc_lasts, kds, vns)
    return os_, Sn


def _head_cols(ref, part):
    return [ref[:, part * DN_W + e * DN_HD:part * DN_W + (e + 1) * DN_HD] for e in range(DN_HEADS)]


def _chunk_fwd(qkv, bg, name):
    T = qkv.shape[0]
    N = T // CHUNK
    H = DN_HEADS

    def body(x_ref, bg_ref, o_ref, s_ref, S):
        n = pl.program_id(0)

        @pl.when(n == 0)
        def _():
            S[...] = jnp.zeros_like(S)

        s_all = S[...]
        s_ref[0] = s_all
        os_, nxt = _chunk_f(_head_cols(x_ref, 0), _head_cols(x_ref, 1), _head_cols(x_ref, 2), bg_ref[...],
                            [s_all[e] for e in range(H)], list(range(H)))
        for e in range(H):
            o_ref[:, e * DN_HD:(e + 1) * DN_HD] = os_[e]
            S[e] = nxt[e]

    return pl.pallas_call(
        body, name=name, grid=(N,),
        in_specs=[pl.BlockSpec((CHUNK, 3 * DN_W), lambda n: (n, 0)), pl.BlockSpec((CHUNK, 128), lambda n: (n, 0))],
        out_specs=[pl.BlockSpec((CHUNK, DN_W), lambda n: (n, 0)), pl.BlockSpec((1, H, DN_HD, DN_HD), lambda n: (n, 0, 0, 0))],
        out_shape=[jax.ShapeDtypeStruct((T, DN_W), F32), jax.ShapeDtypeStruct((N, H, DN_HD, DN_HD), F32)],
        scratch_shapes=[pltpu.VMEM((H, DN_HD, DN_HD), F32)],
        compiler_params=_params(("arbitrary",)),
    )(qkv, bg)


def _chunk_bwd(qkv, bg, s_saved, do, name):
    T = qkv.shape[0]
    N = T // CHUNK
    H = DN_HEADS

    def body(x_ref, bg_ref, s_ref, do_ref, dx_ref, dbg_ref, dS):
        n = pl.program_id(0)

        @pl.when(n == 0)
        def _():
            dS[...] = jnp.zeros_like(dS)

        ds_all = dS[...]
        _, vjp = jax.vjp(functools.partial(_chunk_f, hs=list(range(H))), _head_cols(x_ref, 0), _head_cols(x_ref, 1),
                         _head_cols(x_ref, 2), bg_ref[...], [s_ref[0, e] for e in range(H)])
        dq, dk, dv, dbg, nxt = vjp(([do_ref[:, e * DN_HD:(e + 1) * DN_HD] for e in range(H)], [ds_all[e] for e in range(H)]))
        for part, g in enumerate((dq, dk, dv)):
            for e in range(H):
                dx_ref[:, part * DN_W + e * DN_HD:part * DN_W + (e + 1) * DN_HD] = g[e]
        for e in range(H):
            dS[e] = nxt[e]
        dbg_ref[...] = dbg

    rev = lambda n: (N - 1 - n, 0)
    return pl.pallas_call(
        body, name=name, grid=(N,),
        in_specs=[pl.BlockSpec((CHUNK, 3 * DN_W), rev), pl.BlockSpec((CHUNK, 128), rev),
                  pl.BlockSpec((1, H, DN_HD, DN_HD), lambda n: (N - 1 - n, 0, 0, 0)), pl.BlockSpec((CHUNK, DN_W), rev)],
        out_specs=[pl.BlockSpec((CHUNK, 3 * DN_W), rev), pl.BlockSpec((CHUNK, 128), rev)],
        out_shape=[jax.ShapeDtypeStruct((T, 3 * DN_W), F32), jax.ShapeDtypeStruct((T, 128), F32)],
        scratch_shapes=[pltpu.VMEM((H, DN_HD, DN_HD), F32)],
        compiler_params=_params(("arbitrary",)),
    )(qkv, bg, s_saved, do)


GRP = AT_QH // AT_KVH


def _attn_f(q, kp, kc, vp, vc, sinks, cos_c, sin_c, cos_p, sin_p, rot, has_prev):
    def rope(x, c, s):
        return x * c + _dot(x, rot) * s

    kcr, kpr = rope(kc, cos_c, sin_c), rope(kp, cos_p, sin_p)
    r = lax.broadcasted_iota(jnp.int32, (WINDOW, WINDOW), 0)
    j = lax.broadcasted_iota(jnp.int32, (WINDOW, WINDOW), 1)
    in_c, in_p = j <= r, jnp.logical_and(j > r, has_prev)
    lane = lax.broadcasted_iota(jnp.int32, (1, 128), 1)
    qrs = [rope(qe, cos_c, sin_c) for qe in q]
    scs = [jnp.where(in_c, _bdot_t(qr, kcr) * (AT_HD ** -0.5), NEG) for qr in qrs]
    sps = [jnp.where(in_p, _bdot_t(qr, kpr) * (AT_HD ** -0.5), NEG) for qr in qrs]
    snk = [jnp.sum(jnp.where(lane == 0, s, 0.0), axis=1, keepdims=True) for s in sinks]
    ms = _each(lambda sc, sp, s: jnp.maximum(jnp.maximum(jnp.max(sc, axis=1, keepdims=True), jnp.max(sp, axis=1, keepdims=True)), s),
               scs, sps, snk)
    pcs = _each(lambda sc, m: jnp.exp(sc - m), scs, ms)
    pps = _each(lambda sp, m: jnp.exp(sp - m), sps, ms)
    dens = _each(lambda pc, pp, s, m: jnp.sum(pc, axis=1, keepdims=True) + jnp.sum(pp, axis=1, keepdims=True) + jnp.exp(s - m),
                 pcs, pps, snk, ms)
    return tuple(_each(lambda pc, pp, den: (_bdot(pc, vc) + _bdot(pp, vp)) / den, pcs, pps, dens))


def _attn_specs(nb):
    qs = pl.BlockSpec((GRP, WINDOW, AT_HD), lambda g, n: (g, n, 0))
    kc = pl.BlockSpec((1, WINDOW, AT_HD), lambda g, n: (g, n, 0))
    kp = pl.BlockSpec((1, WINDOW, AT_HD), lambda g, n: (g, jnp.maximum(n - 1, 0), 0))
    tc = pl.BlockSpec((WINDOW, AT_HD), lambda g, n: (n, 0))
    tp = pl.BlockSpec((WINDOW, AT_HD), lambda g, n: (jnp.maximum(n - 1, 0), 0))
    sk = pl.BlockSpec((GRP, 1, 128), lambda g, n: (g, 0, 0))
    rt = pl.BlockSpec((AT_HD, AT_HD), lambda g, n: (0, 0))
    return qs, kc, kp, tc, tp, sk, rt


def _attn_fwd(q, k, v, sinks, cos, sin, rot, name):
    T = q.shape[1]
    nb = T // WINDOW
    qs, kc, kp, tc, tp, sk, rt = _attn_specs(nb)

    def body(q_ref, kp_ref, kc_ref, vp_ref, vc_ref, sk_ref, cc_ref, sc_ref, cp_ref, sp_ref, rot_ref, o_ref):
        n = pl.program_id(1)
        heads = range(GRP)
        outs = _attn_f(tuple(q_ref[e] for e in heads), kp_ref[0], kc_ref[0], vp_ref[0], vc_ref[0], tuple(sk_ref[e] for e in heads), cc_ref[...], sc_ref[...],
                       cp_ref[...], sp_ref[...], rot_ref[...], n > 0)
        for e in range(GRP):
            o_ref[e] = outs[e]

    return pl.pallas_call(
        body, name=name, grid=(AT_KVH, nb),
        in_specs=[qs, kp, kc, kp, kc, sk, tc, tc, tp, tp, rt], out_specs=qs,
        out_shape=jax.ShapeDtypeStruct(q.shape, F32),
        compiler_params=_params(("parallel", "parallel")),
    )(q, k, k, v, v, sinks, cos, sin, cos, sin, rot)


def _attn_bwd(q, k, v, sinks, cos, sin, rot, do, name):
    T = q.shape[1]
    nb = T // WINDOW
    qs, kc, kp, tc, tp, sk, rt = _attn_specs(nb)

    def body(q_ref, kp_ref, kc_ref, vp_ref, vc_ref, sk_ref, cc_ref, sc_ref, cp_ref, sp_ref, rot_ref, do_ref,
             dq_ref, dkp_ref, dkc_ref, dvp_ref, dvc_ref, dsk_ref):
        n = pl.program_id(1)
        f = functools.partial(_attn_f, cos_c=cc_ref[...], sin_c=sc_ref[...], cos_p=cp_ref[...], sin_p=sp_ref[...],
                              rot=rot_ref[...], has_prev=n > 0)
        heads = range(GRP)
        _, vjp = jax.vjp(f, tuple(q_ref[e] for e in heads), kp_ref[0], kc_ref[0], vp_ref[0], vc_ref[0], tuple(sk_ref[e] for e in heads))
        dq, dkp, dkc, dvp, dvc, dsk = vjp(tuple(do_ref[e] for e in heads))
        dkp_ref[0], dkc_ref[0], dvp_ref[0], dvc_ref[0] = dkp, dkc, dvp, dvc

        @pl.when(n == 0)
        def _():
            dsk_ref[...] = jnp.zeros_like(dsk_ref)

        for e in heads:
            dq_ref[e] = dq[e]
            dsk_ref[e] += dsk[e]

    return pl.pallas_call(
        body, name=name, grid=(AT_KVH, nb),
        in_specs=[qs, kp, kc, kp, kc, sk, tc, tc, tp, tp, rt, qs], out_specs=[qs, kc, kc, kc, kc, sk],
        out_shape=[jax.ShapeDtypeStruct(q.shape, F32)] + [jax.ShapeDtypeStruct(k.shape, F32)] * 4 + [jax.ShapeDtypeStruct(sinks.shape, F32)],
        compiler_params=_params(("arbitrary", "arbitrary")),
    )(q, k, k, v, v, sinks, cos, sin, cos, sin, rot, do)


def _kv_combine(dc, dp, name):
    T = dc.shape[1]
    R = 8 * WINDOW
    ns = T // R

    def body(c_ref, p_ref, q_ref, o_ref):
        n = pl.program_id(1)
        tail = jnp.where(n < ns - 1, q_ref[0], 0.0)
        o_ref[0] = c_ref[0] + jnp.concatenate([p_ref[0, WINDOW:, :], tail], axis=0)

    return pl.pallas_call(
        body, name=name, grid=(AT_KVH, ns),
        in_specs=[pl.BlockSpec((1, R, AT_HD), lambda g, n: (g, n, 0)), pl.BlockSpec((1, R, AT_HD), lambda g, n: (g, n, 0)),
                  pl.BlockSpec((1, WINDOW, AT_HD), lambda g, n: (g, jnp.minimum((n + 1) * 8, T // WINDOW - 1), 0))],
        out_specs=pl.BlockSpec((1, R, AT_HD), lambda g, n: (g, n, 0)),
        out_shape=jax.ShapeDtypeStruct((AT_KVH, T, AT_HD), F32),
        compiler_params=_params(("parallel", "parallel")),
    )(dc, dp, dp)


def _place():
    x, y, c = lax.axis_index("x"), lax.axis_index("y"), lax.axis_index("c")
    return x, y, c, 4 * x + 2 * y + c


def _gather(src, name):
    def body(s_ref, o_ref, send_sems, recv_sems, lsem):
        x, y, c, _ = _place()
        me, sib = (x, y, c), (x, y, 1 - c)
        chips = [(1 - x, y), (x, 1 - y), (1 - x, 1 - y)]

        def copy(k, block, to, src_ref=None):
            slab = o_ref.at[4 * block[0] + 2 * block[1] + block[2]]
            return pltpu.make_async_remote_copy(src_ref=slab if src_ref is None else src_ref, dst_ref=slab,
                                                send_sem=send_sems.at[k], recv_sem=recv_sems.at[k], device_id=to, device_id_type=MESH)

        mine = pltpu.make_async_copy(s_ref, o_ref.at[4 * x + 2 * y + c], lsem)
        mine.start()
        first = [copy(0, me, sib, s_ref)] + [copy(1 + j, me, (*chip, c), s_ref) for j, chip in enumerate(chips)]
        for cp in first:
            cp.start()
        passed = [copy(4 + j, (*chip, c), sib) for j, chip in enumerate(chips)]
        for j, chip in enumerate(chips):
            copy(1 + j, (*chip, c), me).wait_recv()
            passed[j].start()
        copy(0, sib, me).wait_recv()
        for j, chip in enumerate(chips):
            copy(4 + j, (*chip, 1 - c), me).wait_recv()
        for cp in first + passed:
            cp.wait_send()
        mine.wait()

    return pl.pallas_call(
        body, name=name, in_specs=[ANY], out_specs=ANY, out_shape=jax.ShapeDtypeStruct((N_DEV,) + src.shape, src.dtype),
        scratch_shapes=[pltpu.SemaphoreType.DMA((N_DEV - 1,)), pltpu.SemaphoreType.DMA((N_DEV - 1,)), pltpu.SemaphoreType.DMA],
    )(src)


def _pair_swap(src, name):
    def body(s_ref, o_ref, send_sem, recv_sem):
        x, y, c, _ = _place()
        cp = pltpu.make_async_remote_copy(src_ref=s_ref.at[1 - c], dst_ref=o_ref, send_sem=send_sem, recv_sem=recv_sem,
                                          device_id=(x, y, 1 - c), device_id_type=MESH)
        cp.start()
        cp.wait()

    return pl.pallas_call(
        body, name=name, in_specs=[ANY], out_specs=ANY, out_shape=jax.ShapeDtypeStruct(src.shape[1:], src.dtype),
        scratch_shapes=[pltpu.SemaphoreType.DMA, pltpu.SemaphoreType.DMA],
    )(src)


def _chip_a2a(src, name):
    def body(s_ref, o_ref, send_sems, recv_sems, lsem):
        x, y, c, _ = _place()
        chip = 2 * x + y
        mine = pltpu.make_async_copy(s_ref.at[chip], o_ref.at[chip], lsem)
        mine.start()
        cps = []
        for k in (1, 2, 3):
            px, py = x ^ (k >> 1), y ^ (k & 1)
            cps.append(pltpu.make_async_remote_copy(src_ref=s_ref.at[2 * px + py], dst_ref=o_ref.at[chip], send_sem=send_sems.at[k - 1],
                                                    recv_sem=recv_sems.at[k - 1], device_id=(px, py, c), device_id_type=MESH))
            cps[-1].start()
        for cp in cps:
            cp.wait()
        mine.wait()

    return pl.pallas_call(
        body, name=name, in_specs=[ANY], out_specs=ANY, out_shape=jax.ShapeDtypeStruct(src.shape, src.dtype),
        scratch_shapes=[pltpu.SemaphoreType.DMA((3,)), pltpu.SemaphoreType.DMA((3,)), pltpu.SemaphoreType.DMA],
    )(src)


def _add(a, b, name):
    R, C = a.shape
    tr = _pick(R, (512, 256, 128, 64, 32, 16))

    def body(a_ref, b_ref, o_ref):
        o_ref[...] = (a_ref[...].astype(F32) + b_ref[...].astype(F32)).astype(o_ref.dtype)

    s2 = pl.BlockSpec((tr, C), lambda i: (i, 0))
    return pl.pallas_call(body, name=name, grid=(R // tr,), in_specs=[s2, s2], out_specs=s2,
                          out_shape=jax.ShapeDtypeStruct((R, C), a.dtype), compiler_params=_params(("parallel",)))(a, b)


def _reduce_scatter(slabs, name):
    _, R, C = slabs.shape
    by_core = jnp.transpose(slabs.reshape(4, 2, R, C), (1, 0, 2, 3))
    c = lax.axis_index("c")
    theirs = _pair_swap(by_core, "swap_" + name)
    mine = lax.dynamic_index_in_dim(by_core, c, axis=0, keepdims=False)
    summed = _add(mine.reshape(4 * R, C), theirs.reshape(4 * R, C), "add_" + name).reshape(4, R, C)
    return _chip_a2a(summed, "a2a_" + name)


def _adamw(w, m, v, parts, name):
    R, C = w.shape
    P = parts.shape[0]
    tr = _pick(R, (256, 128, 64, 32, 16, 8))
    c1, c2 = 1.0 - B1 ** STEP, 1.0 - B2 ** STEP

    def body(w_ref, m_ref, v_ref, p_ref, g_ref, d_ref, nm_ref, nv_ref):
        g = p_ref[0].astype(F32)
        for i in range(1, P):
            g = g + p_ref[i].astype(F32)
        wv = w_ref[...]
        nm = B1 * m_ref[...] + (1.0 - B1) * g
        nv = B2 * v_ref[...] + (1.0 - B2) * (g * g)
        g_ref[...] = g
        nm_ref[...] = nm
        nv_ref[...] = nv
        d_ref[...] = -LR * ((nm / c1) / (jnp.sqrt(nv / c2) + AEPS) + WD * wv)

    s2 = pl.BlockSpec((tr, C), lambda i: (i, 0))
    return pl.pallas_call(
        body, name=name, grid=(R // tr,),
        in_specs=[s2, s2, s2, pl.BlockSpec((P, tr, C), lambda i: (0, i, 0))], out_specs=[s2] * 4,
        out_shape=[jax.ShapeDtypeStruct((R, C), F32)] * 4,
        compiler_params=_params(("parallel",)),
    )(w, m, v, parts)


def _colsum(a, name):
    def body(a_ref, o_ref):
        o_ref[...] = jnp.broadcast_to(jnp.sum(a_ref[...], axis=0, keepdims=True), o_ref.shape)

    return pl.pallas_call(body, name=name, out_shape=jax.ShapeDtypeStruct((8, 128), F32))(a)


def _rows128(a):
    f = a.reshape(-1)
    return jnp.pad(f, (0, (-f.shape[0]) % 128)).reshape(-1, 128)


def _to_aligned(w):
    return jnp.concatenate([w[..., 0:4096], w[..., 4112:5392], w[..., 4096:4112],
                            jnp.zeros(w.shape[:-1] + (IN_PAD - IN_COLS,), w.dtype)], axis=-1)


def _from_aligned(w):
    return jnp.concatenate([w[..., 0:4096], w[..., 5376:5392], w[..., 4096:5376]], axis=-1)


def kernel(x, c, ln_mix, ln_ffn, w_ada, b_ada, w_in, dn_conv_w, dn_a_log, dn_dt_bias, dn_norm_w, attn_sinks, w_out, w_gate_up, w_down, ln_final, loss_target, m_ln_mix, m_ln_ffn, m_w_ada, m_b_ada, m_w_in, m_dn_conv_w, m_dn_a_log, m_dn_dt_bias, m_dn_norm_w, m_attn_sinks, m_w_out, m_w_gate_up, m_w_down, m_ln_final, v_ln_mix, v_ln_ffn, v_w_ada, v_b_ada, v_w_in, v_dn_conv_w, v_dn_a_log, v_dn_dt_bias, v_dn_norm_w, v_attn_sinks, v_w_out, v_w_gate_up, v_w_down, v_ln_final):
    T = x.shape[1]
    L = ln_mix.shape[0]
    me = 4 * lax.axis_index("x") + 2 * lax.axis_index("y") + lax.axis_index("c")
    xs = x[0]
    tgt = loss_target[0]

    g_in = _gather(w_in.astype(BF16).reshape(L * D, -1), "ag_w_in").reshape(N_DEV, L, D, -1)
    g_out = _gather(w_out.astype(BF16).reshape(-1, D), "ag_w_out").reshape(N_DEV, L, -1, D)
    g_gu = _gather(w_gate_up.astype(BF16).reshape(L * D, -1), "ag_w_gu").reshape(N_DEV, L, D, -1)
    g_dn = _gather(w_down.astype(BF16).reshape(-1, D), "ag_w_down").reshape(N_DEV, L, -1, D)
    g_cv = _gather(dn_conv_w.reshape(L * CONV_K, -1), "ag_conv").reshape(N_DEV, L, CONV_K, -1)
    c_all = _gather(jnp.pad(c, ((0, 7), (0, 0))), "ag_c")[:, 0, :]
    W_in = [_to_aligned(jnp.transpose(g_in[:, l], (1, 0, 2)).reshape(D, IN_COLS)) for l in range(L)]
    W_out = [g_out[:, l].reshape(D, D) for l in range(L)]
    W_gate = [jnp.transpose(g_gu[:4, l], (1, 0, 2)).reshape(D, FFN) for l in range(L)]
    W_up = [jnp.transpose(g_gu[4:, l], (1, 0, 2)).reshape(D, FFN) for l in range(L)]
    W_dn = [g_dn[:, l].reshape(FFN, D) for l in range(L)]
    W_cv = [jnp.transpose(g_cv[:, l], (1, 0, 2)).reshape(CONV_K, 3 * DN_W) for l in range(L)]

    c_act = _ew_fwd(lambda v: (_silu(v),), [(jnp.pad(c_all, ((0, 8), (0, 0))), True, False)], [(D, F32, False)], tm=16, name="c_act")[0]
    mods = []
    for l in range(L):
        ms = _mm(c_act, w_ada[l], "nn", F32, f"mod_mm{l}")
        ga = _gather(ms, f"ag_mod{l}")
        mods.append(lax.dynamic_index_in_dim(ga, me, axis=1, keepdims=False).reshape(1, 6 * D))
    row = lambda a: a.reshape(1, -1)
    seg = lambda a, i: a[:, i * D:(i + 1) * D]

    half = AT_HD // 2
    inv_freq = 10000.0 ** (-jnp.arange(half, dtype=F32) * 2.0 / AT_HD)
    ang = jnp.arange(T, dtype=jnp.int32).astype(F32)[:, None] * inv_freq[None, :]
    cos = jnp.concatenate([jnp.cos(ang)] * 2, axis=-1)
    sin = jnp.concatenate([jnp.sin(ang)] * 2, axis=-1)
    ii = jnp.arange(AT_HD)
    rot = jnp.where(ii[:, None] == ii[None, :] + half, -1.0, 0.0) + jnp.where(ii[:, None] + half == ii[None, :], 1.0, 0.0)
    rot = rot.astype(F32)
    heads = lambda a, nh: jnp.transpose(a.reshape(T, nh, AT_HD), (1, 0, 2))
    unheads = lambda a: jnp.transpose(a, (1, 0, 2)).reshape(T, -1)
    pad16 = lambda a: jnp.pad(row(a), ((0, 0), (8, 128 - 16)))

    saved = []
    xc = xs
    for l in range(L):
        mod, bmod = mods[l], row(b_ada[l])
        s = {"x": xc}
        nm_ops = lambda xx, ln, a, b: [(xx, True, False), (row(ln), False, False), (seg(mod, a), False, False),
                                       (seg(bmod, a), False, False), (seg(mod, b), False, False), (seg(bmod, b), False, False)]
        h1 = _ew_fwd(lambda *a: _f_normmod(*a)[:1], nm_ops(xc, ln_mix[l], 1, 0), [(D, BF16, False)], tm=256, name=f"normmod1_{l}")[0]
        proj = _mm(h1, W_in[l], "nn", F32, f"mm_in{l}")
        aq, ak, av = proj[:, 4096:5120], proj[:, 5120:5248], proj[:, 5248:5376]
        conv = _conv_fwd(proj, W_cv[l], 3 * DN_W, f"conv{l}")
        pre_ops = [(conv, True, True)]
        qkvn = _ew_fwd(_f_dnpre, pre_ops, [(3 * DN_W, F32, True)], tm=2048, ncol=3 * DN_HEADS, name=f"dnpre{l}", with_j=True)[0]
        bg_ops = [(proj, True, False, (5376, 128)), (pad16(dn_a_log[l]), False, False), (pad16(dn_dt_bias[l]), False, False)]
        bg = _ew_fwd(_f_bg, bg_ops, [(128, F32, False)], tm=1024, name=f"bg{l}")[0]
        o, s_saved = _chunk_fwd(qkvn, bg, f"chunk{l}")
        post_ops = [(o, True, True), (proj, True, True, (3072, DN_W)), (row(dn_norm_w[l]), False, False)]
        dn_out = _ew_fwd(_f_dnpost, post_ops, [(DN_W, BF16, True)], tm=2048, ncol=DN_HEADS, name=f"dnpost{l}")[0]
        qh, kh, vh = heads(aq, AT_QH), heads(ak, AT_KVH), heads(av, AT_KVH)
        sk = jnp.broadcast_to(attn_sinks[l][:, None, None], (AT_QH, 1, 128))
        at_out = unheads(_attn_fwd(qh, kh, vh, sk, cos, sin, rot, f"attn{l}")).astype(BF16)
        mix = _mm(at_out, W_out[l][DN_W:], "nn", F32, f"mm_out_at{l}", acc_in=_mm(dn_out, W_out[l][:DN_W], "nn", F32, f"mm_out_dn{l}"))
        rg_ops = lambda xx, br, a: [(xx, True, False), (br, True, False), (seg(mod, a), False, False), (seg(bmod, a), False, False)]
        x1 = _ew_fwd(_f_resgate, rg_ops(xc, mix, 2), [(D, F32, False)], tm=256, name=f"resgate1_{l}")[0]
        h2 = _ew_fwd(lambda *a: _f_normmod(*a)[:1], nm_ops(x1, ln_ffn[l], 4, 3), [(D, BF16, False)], tm=256, name=f"normmod2_{l}")[0]
        gate = _mm(h2, W_gate[l], "nn", BF16, f"mm_gate{l}")
        up = _mm(h2, W_up[l], "nn", BF16, f"mm_up{l}")
        sw_ops = [(gate, True, True), (up, True, True)]
        act = _ew_fwd(_f_swiglu, sw_ops, [(FFN, BF16, True)], tm=2048, ncol=11, name=f"swiglu{l}")[0]
        down = _mm(act, W_dn[l], "nn", F32, f"mm_down{l}")
        x2 = _ew_fwd(_f_resgate, rg_ops(x1, down, 5), [(D, F32, False)], tm=256, name=f"resgate2_{l}")[0]
        s.update(h1=h1, proj=proj, qkvn=qkvn, bg=bg, s_saved=s_saved, qh=qh, kh=kh, vh=vh, sk=sk, dn_out=dn_out, at_out=at_out,
                 h2=h2, act=act, bg_ops=bg_ops, post_ops=post_ops, pre_ops=pre_ops, sw_ops=sw_ops,
                 nm1=nm_ops(xc, ln_mix[l], 1, 0), nm2=nm_ops(x1, ln_ffn[l], 4, 3), rg1=rg_ops(xc, mix, 2), rg2=rg_ops(x1, down, 5))
        saved.append(s)
        xc = x2

    fin_ops = [(xc, True, False), (tgt, True, False), (row(ln_final), False, False)]
    lrow = _ew_fwd(_f_final, fin_ops, [(128, F32, False)], tm=256, name="loss_rows")[0]
    loss = lax.psum(_colsum(lrow, "loss_sum")[0, 0], ("x", "y", "c"))
    dx, d_ln_final = _ew_bwd(_f_final, fin_ops, [(jnp.ones((T, 128), F32) / 128.0, False)], [0, 2], tm=256, name="loss_bwd")

    small = {k: [None] * L for k in ("ln_mix", "ln_ffn", "mod", "a_log", "dt", "norm_w", "sinks", "conv")}
    big = {k: [None] * L for k in ("w_in", "w_out", "w_gate", "w_up", "w_dn")}
    for l in reversed(range(L)):
        s = saved[l]
        ddown, dgt_f = _ew_bwd(_f_resgate, s["rg2"], [(dx, False)], [1, 2], tm=256, name=f"resgate2_bwd{l}", gdt=[BF16, F32])
        big["w_dn"][l] = _mm(s["act"], ddown, "tn", BF16, f"wg_down{l}")
        dact = _mm(ddown, W_dn[l], "nt", F32, f"dg_down{l}")
        dgate, dup = _ew_bwd(_f_swiglu, s["sw_ops"], [(dact, True)], [0, 1], tm=2048, ncol=11, name=f"swiglu_bwd{l}", gdt=[BF16, BF16])
        big["w_gate"][l] = _mm(s["h2"], dgate, "tn", BF16, f"wg_gate{l}")
        big["w_up"][l] = _mm(s["h2"], dup, "tn", BF16, f"wg_up{l}")
        dh2 = _mm(dup, W_up[l], "nt", F32, f"dg_up{l}", acc_in=_mm(dgate, W_gate[l], "nt", F32, f"dg_gate{l}"))
        dx1, dln_f, dsc_f, dsh_f = _ew_bwd(_f_normmod, s["nm2"], [(dh2, False), (dx, False)], [0, 1, 2, 4], tm=256, name=f"normmod2_bwd{l}")
        dmix, dgt_m = _ew_bwd(_f_resgate, s["rg1"], [(dx1, False)], [1, 2], tm=256, name=f"resgate1_bwd{l}", gdt=[BF16, F32])
        big["w_out"][l] = jnp.concatenate([_mm(s["dn_out"], dmix, "tn", BF16, f"wg_out_dn{l}"),
                                           _mm(s["at_out"], dmix, "tn", BF16, f"wg_out_at{l}")], axis=0)
        d_dn = _mm(dmix, W_out[l][:DN_W], "nt", F32, f"dg_out_dn{l}")
        d_at = _mm(dmix, W_out[l][DN_W:], "nt", F32, f"dg_out_at{l}")
        dqh, dkp, dkc, dvp, dvc, dsk = _attn_bwd(s["qh"], s["kh"], s["vh"], s["sk"], cos, sin, rot, heads(d_at, AT_QH), f"attn_bwd{l}")
        dkh = _kv_combine(dkc, dkp, f"dk_comb{l}")
        dvh = _kv_combine(dvc, dvp, f"dv_comb{l}")
        do, dz, dnw = _ew_bwd(_f_dnpost, s["post_ops"], [(d_dn, True)], [0, 1, 2], tm=2048, ncol=DN_HEADS, name=f"dnpost_bwd{l}")
        dqkvn, dbg = _chunk_bwd(s["qkvn"], s["bg"], s["s_saved"], do, f"chunk_bwd{l}")
        dconv = _ew_bwd(_f_dnpre, s["pre_ops"], [(dqkvn, True)], [0], tm=2048, ncol=3 * DN_HEADS, name=f"dnpre_bwd{l}", with_j=True)[0]
        dba, dalog, ddt = _ew_bwd(_f_bg, s["bg_ops"], [(dbg, False)], [0, 1, 2], tm=1024, name=f"bg_bwd{l}")
        dqkv, dcw = _conv_bwd(s["proj"], dconv, W_cv[l], f"conv_bwd{l}")
        dproj = jnp.concatenate([dqkv, dz, unheads(dqh), unheads(dkh), unheads(dvh), dba, jnp.zeros((T, IN_PAD - 5504), F32)],
                                axis=-1).astype(BF16)
        big["w_in"][l] = _mm(s["h1"], dproj, "tn", BF16, f"wg_in{l}")
        dh1 = _mm(dproj, W_in[l], "nt", F32, f"dg_in{l}")
        dx, dln_m, dsc_m, dsh_m = _ew_bwd(_f_normmod, s["nm1"], [(dh1, False), (dx1, False)], [0, 1, 2, 4], tm=256, name=f"normmod1_bwd{l}")
        small["ln_mix"][l], small["ln_ffn"][l] = dln_m, dln_f
        small["mod"][l] = jnp.concatenate([dsh_m, dsc_m, dgt_m, dsh_f, dsc_f, dgt_f], axis=-1)
        small["a_log"][l], small["dt"][l] = dalog[:, 8:16], ddt[:, 8:16]
        small["norm_w"][l], small["sinks"][l], small["conv"][l] = dnw, dsk[:, 0, 0], dcw

    cat0 = lambda xs_: jnp.concatenate([_rows128(a) for a in xs_], axis=0)
    stk = lambda k: jnp.stack(small[k])
    pack = cat0([stk("ln_mix"), stk("ln_ffn"), stk("mod"), stk("a_log"), stk("dt"), stk("norm_w"), stk("sinks"), d_ln_final, stk("conv")])
    n_small = pack.shape[0] - L * CONV_K * 3 * DN_W // 128
    pack = jnp.pad(pack, ((0, (-pack.shape[0]) % 8), (0, 0)))
    gp = _gather(pack, "ag_small")
    parts_small = gp[:, :n_small]
    dmod_all = gp[:, 2 * L * D // 128:2 * L * D // 128 + L * 6 * D // 128].reshape(N_DEV, L, 6 * D)
    conv_all = gp[:, n_small:n_small + L * CONV_K * 3 * DN_W // 128].reshape(N_DEV, L * CONV_K, 3 * DN_W)
    parts_conv = lax.dynamic_slice_in_dim(conv_all, me * (3 * DN_W // N_DEV), 3 * DN_W // N_DEV, axis=2)

    def shards(gs, cols, n=N_DEV):
        g = jnp.stack(gs)
        if cols:
            return jnp.transpose(g.reshape(L, g.shape[1], n, -1), (2, 0, 1, 3)).reshape(n, L * g.shape[1], -1)
        return jnp.transpose(g.reshape(L, n, -1, g.shape[2]), (1, 0, 2, 3)).reshape(n, -1, g.shape[2])

    p_in = _reduce_scatter(shards([_from_aligned(g) for g in big["w_in"]], True), "w_in")
    p_out = _reduce_scatter(shards(big["w_out"], False), "w_out")
    p_gu = _reduce_scatter(jnp.concatenate([shards(big["w_gate"], True, 4), shards(big["w_up"], True, 4)], axis=0), "w_gu")
    p_dn = _reduce_scatter(shards(big["w_dn"], False), "w_down")
    dmod_mine = lax.dynamic_slice_in_dim(dmod_all, me * (6 * D // N_DEV), 6 * D // N_DEV, axis=2)
    g_ada = jnp.stack([_mm(c_act, jnp.pad(dmod_mine[:, l], ((0, 8), (0, 0))), "tn", F32, f"wg_ada{l}") for l in range(L)])

    def upd(w, m, v, parts, name):
        shp = w.shape
        r = lambda a: a.reshape(-1, shp[-1])
        return [o_.reshape(shp) for o_ in _adamw(r(w), r(m), r(v), parts.reshape(parts.shape[0], -1, shp[-1]), name)]

    res = {}
    res["w_ada"] = upd(w_ada, m_w_ada, v_w_ada, g_ada[None], "adamw_ada")
    res["w_in"] = upd(w_in, m_w_in, v_w_in, p_in, "adamw_in")
    res["dn_conv_w"] = upd(dn_conv_w, m_dn_conv_w, v_dn_conv_w, parts_conv, "adamw_conv")
    res["w_out"] = upd(w_out, m_w_out, v_w_out, p_out, "adamw_out")
    res["w_gate_up"] = upd(w_gate_up, m_w_gate_up, v_w_gate_up, p_gu, "adamw_gu")
    res["w_down"] = upd(w_down, m_w_down, v_w_down, p_dn, "adamw_down")
    names_s = ["ln_mix", "ln_ffn", "b_ada", "dn_a_log", "dn_dt_bias", "dn_norm_w", "attn_sinks", "ln_final"]
    ws = [ln_mix, ln_ffn, b_ada, dn_a_log, dn_dt_bias, dn_norm_w, attn_sinks, ln_final]
    ms = [m_ln_mix, m_ln_ffn, m_b_ada, m_dn_a_log, m_dn_dt_bias, m_dn_norm_w, m_attn_sinks, m_ln_final]
    vs = [v_ln_mix, v_ln_ffn, v_b_ada, v_dn_a_log, v_dn_dt_bias, v_dn_norm_w, v_attn_sinks, v_ln_final]
    padr = lambda a: jnp.pad(a, ((0, (-a.shape[0]) % 8), (0, 0)))
    vpad = jnp.pad(cat0(vs), ((0, (-n_small) % 8), (0, 0)), constant_values=1.0)
    outs_s = _adamw(padr(cat0(ws)), padr(cat0(ms)), vpad, jnp.pad(parts_small, ((0, 0), (0, (-n_small) % 8), (0, 0))), "adamw_small")
    off = 0
    for nme, wv in zip(names_s, ws):
        nrow = -(-wv.size // 128)
        res[nme] = [o_[off:off + nrow].reshape(-1)[:wv.size].reshape(wv.shape) for o_ in outs_s]
        off += nrow

    order = ["ln_mix", "ln_ffn", "w_ada", "b_ada", "w_in", "dn_conv_w", "dn_a_log", "dn_dt_bias", "dn_norm_w", "attn_sinks",
             "w_out", "w_gate_up", "w_down", "ln_final"]
    return (loss, dx[None], *[res[n][0] for n in order], *[res[n][1] for n in order], *[res[n][2] for n in order],
            *[res[n][3] for n in order])
```

```python
import functools

import jax
import jax.numpy as jnp
from jax import lax
from jax.experimental import pallas as pl
from jax.experimental.pallas import tpu as pltpu

F32, BF16 = jnp.float32, jnp.bfloat16
HI = lax.Precision.HIGH
MESH = pl.DeviceIdType.MESH
ANY = pl.BlockSpec(memory_space=pl.ANY)

N_DEV = 8
D = 2048
DN_HEADS, DN_HD = 8, 128
DN_W = 1024
CONV_K = 4
CHUNK = 64
AT_HD, AT_QH, AT_KVH = 64, 16, 2
AT_W = 1024
WINDOW = 128
FFN = 5632
IN_COLS = 5392
IN_PAD = 5632
EPS = 1e-6
NEG = -1e30
LR, B1, B2, AEPS, WD, STEP = 0.001, 0.9, 0.999, 1e-08, 0.01, 10
VMEM_LIMIT = 56 * 1024 * 1024


def _pick(n, cands):
    for c in cands:
        if n % c == 0:
            return c
    return n


def _params(sem):
    return pltpu.CompilerParams(dimension_semantics=sem, vmem_limit_bytes=VMEM_LIMIT)


_DN = {"nn": (((1,), (0,)), ((), ())), "nt": (((1,), (1,)), ((), ())), "tn": (((0,), (0,)), ((), ()))}


def _mm(a, b, mode, out_dtype, name, acc_in=None):
    if mode == "nn":
        (M, K), (_, N) = a.shape, b.shape
    elif mode == "nt":
        (M, K), (N, _) = a.shape, b.shape
    else:
        (K, M), (_, N) = a.shape, b.shape
    tm = _pick(M, (1024, 512, 256, 128, 64, 32, 16))
    tn = _pick(N, (1024, 512, 256, 128))
    tk = K if K <= 2048 else _pick(K, (2816, 2048, 1024, 512, 256, 128))
    nk = K // tk
    dn = _DN[mode]
    n_in = 2 if acc_in is None else 3

    def body(*refs):
        a_ref, b_ref, o_ref = refs[0], refs[1], refs[n_in]
        part = lax.dot_general(a_ref[...].astype(BF16), b_ref[...].astype(BF16), dn, preferred_element_type=F32)
        if nk == 1:
            if acc_in is not None:
                part = part + refs[2][...]
            o_ref[...] = part.astype(o_ref.dtype)
            return
        acc = refs[n_in + 1]
        k = pl.program_id(2)

        @pl.when(k == 0)
        def _():
            acc[...] = part if acc_in is None else part + refs[2][...]

        @pl.when(k > 0)
        def _():
            acc[...] += part

        @pl.when(k == nk - 1)
        def _():
            o_ref[...] = acc[...].astype(o_ref.dtype)

    a_spec = pl.BlockSpec((tk, tm), lambda i, j, k: (k, i)) if mode == "tn" else pl.BlockSpec((tm, tk), lambda i, j, k: (i, k))
    b_spec = pl.BlockSpec((tn, tk), lambda i, j, k: (j, k)) if mode == "nt" else pl.BlockSpec((tk, tn), lambda i, j, k: (k, j))
    o_spec = pl.BlockSpec((tm, tn), lambda i, j, k: (i, j))
    return pl.pallas_call(
        body, name=name, grid=(M // tm, N // tn, nk),
        in_specs=[a_spec, b_spec] + ([] if acc_in is None else [o_spec]), out_specs=o_spec,
        out_shape=jax.ShapeDtypeStruct((M, N), out_dtype),
        scratch_shapes=[pltpu.VMEM((tm, tn), F32)] if nk > 1 else [],
        compiler_params=_params(("parallel", "parallel", "arbitrary")),
    )(*((a, b) if acc_in is None else (a, b, acc_in)))


def _op(op):
    arr, rt, cb = op[:3]
    start, width = op[3] if len(op) > 3 else (0, arr.shape[1])
    return arr, rt, cb, start, width


def _ew_spec(op, tm, ncol):
    arr, rt, cb, start, width = _op(op)
    bw = width // ncol if cb else width
    first = start // bw
    return pl.BlockSpec((tm if rt else arr.shape[0], bw), lambda j, i: (i if rt else 0, first + (j if cb else 0)))


def _ew_fwd(f, ops, outs, *, tm, ncol=1, name, with_j=False):
    M = next(op[0].shape[0] for op in ops if op[1])
    tm = min(tm, M)
    n_in = len(ops)

    def body(*refs):
        res = f(*[r[...].astype(F32) for r in refs[:n_in]], *([pl.program_id(0)] if with_j else []))
        for o, r in zip(refs[n_in:], res):
            o[...] = r.astype(o.dtype)

    return pl.pallas_call(
        body, name=name, grid=(ncol, M // tm),
        in_specs=[_ew_spec(op, tm, ncol) for op in ops],
        out_specs=[pl.BlockSpec((tm, w // ncol if cb else w), lambda j, i, cb=cb: (i, j if cb else 0)) for w, _, cb in outs],
        out_shape=[jax.ShapeDtypeStruct((M, w), dt) for w, dt, _ in outs],
        compiler_params=_params(("parallel", "parallel")),
    )(*[op[0] for op in ops])


def _ew_bwd(f, ops, cts, diff, *, tm, ncol=1, name, gdt=None, with_j=False):
    M = next(op[0].shape[0] for op in ops if op[1])
    tm = min(tm, M)
    n_in, n_ct = len(ops), len(cts)
    gdt = gdt or [F32] * len(diff)

    def body(*refs):
        j, i = pl.program_id(0), pl.program_id(1)
        vals = [r[...].astype(F32) for r in refs[:n_in]]

        def fd(*dv):
            full = list(vals)
            for idx, v in zip(diff, dv):
                full[idx] = v
            return tuple(f(*full, *([j] if with_j else [])))

        _, vjp = jax.vjp(fd, *[vals[idx] for idx in diff])
        gs = vjp(tuple(r[...].astype(F32) for r in refs[n_in:n_in + n_ct]))
        for idx, g, gref in zip(diff, gs, refs[n_in + n_ct:]):
            _, rt, cb = ops[idx][:3]
            if rt:
                gref[...] = g.astype(gref.dtype)
            else:
                first = (i == 0) if cb else jnp.logical_and(i == 0, j == 0)

                @pl.when(first)
                def _(gref=gref):
                    gref[...] = jnp.zeros_like(gref)

                gref[...] += g

    def g_spec(op):
        arr, rt, cb, start, width = _op(op)
        return _ew_spec((jax.ShapeDtypeStruct((arr.shape[0], width), F32), rt, cb), tm, ncol) if rt else _ew_spec(op, tm, ncol)

    def g_shape(op, dt):
        arr, rt, cb, start, width = _op(op)
        return jax.ShapeDtypeStruct((M, width), dt) if rt else jax.ShapeDtypeStruct(arr.shape, F32)

    return pl.pallas_call(
        body, name=name, grid=(ncol, M // tm),
        in_specs=[_ew_spec(op, tm, ncol) for op in ops]
        + [pl.BlockSpec((tm, a.shape[1] // ncol if cb else a.shape[1]), lambda j, i, cb=cb: (i, j if cb else 0)) for a, cb in cts],
        out_specs=[g_spec(ops[idx]) for idx in diff],
        out_shape=[g_shape(ops[idx], dt) for idx, dt in zip(diff, gdt)],
        compiler_params=_params(("arbitrary", "arbitrary")),
    )(*[op[0] for op in ops], *[a for a, _ in cts])


def _silu(x):
    return x * jax.nn.sigmoid(x)


def _f_normmod(x, ln, sc, bsc, sh, bsh):
    y = x * lax.rsqrt(jnp.mean(x * x, axis=-1, keepdims=True) + EPS) * ln
    return y * (1.0 + (sc + bsc)) + (sh + bsh), x


def _f_resgate(x, br, gt, bgt):
    return (x + (gt + bgt) * br,)


def _f_swiglu(gate, up):
    return (_silu(gate) * up,)


def _f_final(x, tgt, ln):
    y = x * lax.rsqrt(jnp.mean(x * x, axis=-1, keepdims=True) + EPS) * ln
    e = y - tgt
    return (jnp.broadcast_to(0.5 * jnp.mean(e * e, axis=-1, keepdims=True), (x.shape[0], 128)),)


def _f_bg(ba, alog, dt):
    col = lax.broadcasted_iota(jnp.int32, ba.shape, 1)
    z = ba + dt
    sp = jnp.maximum(z, 0.0) + jnp.log(1.0 + jnp.exp(-jnp.abs(z)))
    return (jnp.where(col < 8, jax.nn.sigmoid(ba), jnp.where(col < 16, -jnp.exp(alog) * sp, 0.0)),)


def _l2n(x):
    return x * lax.rsqrt(jnp.sum(x * x, axis=-1, keepdims=True) + EPS)


def _f_dnpre(c, j):
    a = _silu(c)
    return (jnp.where(j < 2 * DN_HEADS, _l2n(a) * jnp.where(j < DN_HEADS, DN_HD ** -0.5, 1.0), a),)


def _f_dnpost(o, z, nw):
    return (o * lax.rsqrt(jnp.mean(o * o, axis=-1, keepdims=True) + EPS) * nw * _silu(z),)


CONV_BW = 512


def _conv_fwd(x, w, Cw, name):
    T = x.shape[0]
    tm, bw = 512, CONV_BW

    def body(x_ref, h_ref, w_ref, o_ref):
        i = pl.program_id(1)
        cur, halo, wv = x_ref[...], h_ref[...], w_ref[...]
        halo = jnp.where(i > 0, halo, 0.0)
        row = lax.broadcasted_iota(jnp.int32, (8, bw), 0)
        acc = wv[3:4, :] * cur
        for s in (1, 2, 3):
            r = pltpu.roll(cur, s, 0)
            top = jnp.where(row < s, pltpu.roll(halo, s, 0), r[:8])
            acc += wv[3 - s:4 - s, :] * jnp.concatenate([top, r[8:]], axis=0)
        o_ref[...] = acc

    return pl.pallas_call(
        body, name=name, grid=(Cw // bw, T // tm),
        in_specs=[pl.BlockSpec((tm, bw), lambda j, i: (i, j)),
                  pl.BlockSpec((8, bw), lambda j, i: (jnp.maximum(i * (tm // 8) - 1, 0), j)),
                  pl.BlockSpec((4, bw), lambda j, i: (0, j))],
        out_specs=pl.BlockSpec((tm, bw), lambda j, i: (i, j)),
        out_shape=jax.ShapeDtypeStruct((T, Cw), F32),
        compiler_params=_params(("parallel", "parallel")),
    )(x, x, w)


def _conv_bwd(x, dy, w, name):
    T, Cw = dy.shape
    tm, bw = 512, CONV_BW
    nt = T // tm

    def body(x_ref, h_ref, dy_ref, n_ref, w_ref, dx_ref, dw_ref):
        i = pl.program_id(1)
        cur, dcur, wv = x_ref[...], dy_ref[...], w_ref[...]
        halo = jnp.where(i > 0, h_ref[...], 0.0)
        nxt = jnp.where(i < nt - 1, n_ref[...], 0.0)
        row = lax.broadcasted_iota(jnp.int32, (8, bw), 0)

        @pl.when(i == 0)
        def _():
            dw_ref[...] = jnp.zeros_like(dw_ref)

        dx = wv[3:4, :] * dcur
        dw_ref[3:4, :] += jnp.sum(dcur * cur, axis=0, keepdims=True)
        for s in (1, 2, 3):
            r = pltpu.roll(cur, s, 0)
            top = jnp.where(row < s, pltpu.roll(halo, s, 0), r[:8])
            xs = jnp.concatenate([top, r[8:]], axis=0)
            dw_ref[3 - s:4 - s, :] += jnp.sum(dcur * xs, axis=0, keepdims=True)
            rf = pltpu.roll(dcur, tm - s, 0)
            bot = jnp.where(row >= 8 - s, pltpu.roll(nxt, 8 - s, 0), rf[tm - 8:])
            dx += wv[3 - s:4 - s, :] * jnp.concatenate([rf[:tm - 8], bot], axis=0)
        dx_ref[...] = dx

    return pl.pallas_call(
        body, name=name, grid=(Cw // bw, nt),
        in_specs=[pl.BlockSpec((tm, bw), lambda j, i: (i, j)),
                  pl.BlockSpec((8, bw), lambda j, i: (jnp.maximum(i * (tm // 8) - 1, 0), j)),
                  pl.BlockSpec((tm, bw), lambda j, i: (i, j)),
                  pl.BlockSpec((8, bw), lambda j, i: (jnp.minimum((i + 1) * (tm // 8), T // 8 - 1), j)),
                  pl.BlockSpec((4, bw), lambda j, i: (0, j))],
        out_specs=[pl.BlockSpec((tm, bw), lambda j, i: (i, j)), pl.BlockSpec((4, bw), lambda j, i: (0, j))],
        out_shape=[jax.ShapeDtypeStruct((T, Cw), F32), jax.ShapeDtypeStruct((4, Cw), F32)],
        compiler_params=_params(("arbitrary", "arbitrary")),
    )(x, x, dy, dy, w)


def _dot(a, b):
    return jnp.dot(a, b, precision=HI, preferred_element_type=F32)


def _dot_t(a, b):
    return lax.dot_general(a, b, (((1,), (1,)), ((), ())), precision=HI, preferred_element_type=F32)


def _bdot(a, b):
    return jnp.dot(a.astype(BF16), b.astype(BF16), preferred_element_type=F32)


def _bdot_t(a, b):
    return lax.dot_general(a.astype(BF16), b.astype(BF16), (((1,), (1,)), ((), ())), preferred_element_type=F32)


def _each(f, *lists):
    return [f(*a) for a in zip(*lists)]


@jax.custom_vjp
def _unit_lower_inverses(ps):
    C = ps[0].shape[0]
    dist = jnp.bitwise_xor(lax.broadcasted_iota(jnp.int32, (C, C), 0), lax.broadcasted_iota(jnp.int32, (C, C), 1))
    ns = [jnp.where(dist < 8, p, 0.0) for p in ps]
    tis = [jnp.where(dist == 0, 1.0, 0.0) + n for n in ns]
    for _ in range(2):
        ns = _each(lambda n: _dot(n, n), ns)
        tis = _each(lambda t, n: t + _dot(t, n), tis, ns)
    b = 8
    while b < C:
        mids = [jnp.where(jnp.logical_and(dist >= b, dist < 2 * b), p, 0.0) for p in ps]
        halves = _each(_dot, mids, tis)
        tis = _each(lambda t, h: t + _dot(t, h), tis, halves)
        b *= 2
    return tis


def _uli_fwd(ps):
    tis = _unit_lower_inverses(ps)
    return tis, tis


def _uli_bwd(tis, cts):
    tt = lambda a, b: lax.dot_general(a, b, (((0,), (0,)), ((), ())), precision=HI, preferred_element_type=F32)
    half = _each(tt, tis, cts)
    return (_each(_dot_t, half, tis),)


_unit_lower_inverses.defvjp(_uli_fwd, _uli_bwd)


def _chunk_f(qs, ks, vs, bg, Ss, hs):
    C = CHUNK
    lane = lax.broadcasted_iota(jnp.int32, (C, 128), 1)
    betas = [jnp.sum(jnp.where(lane == h, bg, 0.0), axis=1, keepdims=True) for h in hs]
    gs = [jnp.sum(jnp.where(lane == h + 8, bg, 0.0), axis=1, keepdims=True) for h in hs]
    ri = lax.broadcasted_iota(jnp.int32, (C, C), 0)
    ci = lax.broadcasted_iota(jnp.int32, (C, C), 1)
    causal, strict, eye = ri >= ci, ri > ci, ri == ci
    g_rows = _each(lambda g: jnp.sum(jnp.where(eye, g, 0.0), axis=0, keepdims=True), gs)
    gc_cols = _each(lambda gr: jnp.sum(jnp.where(causal, gr, 0.0), axis=1, keepdims=True), g_rows)
    gc_rows = _each(lambda g: jnp.sum(jnp.where(ri <= ci, g, 0.0), axis=0, keepdims=True), gs)
    gc_lasts = _each(lambda g: jnp.sum(g, axis=0, keepdims=True), gs)
    decays = _each(lambda c, r: jnp.exp(jnp.where(causal, c - r, NEG)), gc_cols, gc_rows)
    kbs = _each(jnp.multiply, ks, betas)
    vbs = _each(jnp.multiply, vs, betas)
    ps = _each(lambda kb, k, d: -jnp.where(strict, _bdot_t(kb, k) * d, 0.0), kbs, ks, decays)
    tis = _unit_lower_inverses(ps)
    egcs = _each(jnp.exp, gc_cols)
    ws = _each(lambda t, kb, e: _bdot(t, kb * e), tis, kbs, egcs)
    us = _each(_bdot, tis, vbs)
    intras = _each(lambda q, k, d: jnp.where(causal, _bdot_t(q, k) * d, 0.0), qs, ks, decays)
    kds = _each(lambda k, gl, gc: k * jnp.exp(gl - gc), ks, gc_lasts, gc_cols)
    vns = _each(lambda u, w, S: u - _bdot(w, S), us, ws, Ss)
    os_ = _each(lambda q, e, S, i, vn: _bdot(q * e, S) + _bdot(i, vn), qs, egcs, Ss, intras, vns)
    Sn = _each(lambda S, gl, kd, vn: S * jnp.exp(gl) + lax.dot_general(
        kd.astype(BF16), vn.astype(BF16), (((0,), (0,)), ((), ())), preferred_element_type=F32), Ss, gc_lasts, kds, vns)
    return os_, Sn


def _head_cols(ref, part):
    return [ref[:, part * DN_W + e * DN_HD:part * DN_W + (e + 1) * DN_HD] for e in range(DN_HEADS)]


def _chunk_fwd(qkv, bg, name, gather_src=None):
    T = qkv.shape[0]
    N = T // CHUNK
    H = DN_HEADS
    carry = gather_src is not None

    def body(*refs):
        x_ref, bg_ref = refs[:2]
        o_ref, s_ref = refs[2 + carry:4 + carry]
        S = refs[4 + 2 * carry]
        n = pl.program_id(0)
        if carry:
            start, forward, finish = _gather_plan(refs[2], refs[5], *refs[7:])
            pl.when(n == 0)(start)
            pl.when(n == N // 2)(forward)

        @pl.when(n == 0)
        def _():
            S[...] = jnp.zeros_like(S)

        s_all = S[...]
        s_ref[0] = s_all
        os_, nxt = _chunk_f(_head_cols(x_ref, 0), _head_cols(x_ref, 1), _head_cols(x_ref, 2), bg_ref[...],
                            [s_all[e] for e in range(H)], list(range(H)))
        for e in range(H):
            o_ref[:, e * DN_HD:(e + 1) * DN_HD] = os_[e]
            S[e] = nxt[e]
        if carry:
            pl.when(n == N - 1)(finish)

    gshape = [jax.ShapeDtypeStruct((N_DEV,) + gather_src.shape, gather_src.dtype)] if carry else []
    return pl.pallas_call(
        body, name=name, grid=(N,),
        in_specs=[pl.BlockSpec((CHUNK, 3 * DN_W), lambda n: (n, 0)), pl.BlockSpec((CHUNK, 128), lambda n: (n, 0))] + [ANY] * carry,
        out_specs=[pl.BlockSpec((CHUNK, DN_W), lambda n: (n, 0)), pl.BlockSpec((1, H, DN_HD, DN_HD), lambda n: (n, 0, 0, 0))] + [ANY] * carry,
        out_shape=[jax.ShapeDtypeStruct((T, DN_W), F32), jax.ShapeDtypeStruct((N, H, DN_HD, DN_HD), F32)] + gshape,
        scratch_shapes=[pltpu.VMEM((H, DN_HD, DN_HD), F32)] + (GATHER_SEMS if carry else []),
        compiler_params=_params(("arbitrary",)),
    )(qkv, bg, *([gather_src] if carry else []))


def _chunk_bwd(qkv, bg, s_saved, do, name, a2a_srcs=()):
    T = qkv.shape[0]
    N = T // CHUNK
    H = DN_HEADS
    na = len(a2a_srcs)

    def body(*refs):
        x_ref, bg_ref, s_ref, do_ref = refs[:4]
        dx_ref, dbg_ref = refs[4 + na:6 + na]
        dS = refs[6 + 2 * na]
        n = pl.program_id(0)
        if na:
            start, finish = _a2a_plan(refs[4:4 + na], refs[6 + na:6 + 2 * na], *refs[7 + 2 * na:])
            pl.when(n == 0)(start)

        @pl.when(n == 0)
        def _():
            dS[...] = jnp.zeros_like(dS)

        ds_all = dS[...]
        _, vjp = jax.vjp(functools.partial(_chunk_f, hs=list(range(H))), _head_cols(x_ref, 0), _head_cols(x_ref, 1),
                         _head_cols(x_ref, 2), bg_ref[...], [s_ref[0, e] for e in range(H)])
        dq, dk, dv, dbg, nxt = vjp(([do_ref[:, e * DN_HD:(e + 1) * DN_HD] for e in range(H)], [ds_all[e] for e in range(H)]))
        for part, g in enumerate((dq, dk, dv)):
            for e in range(H):
                dx_ref[:, part * DN_W + e * DN_HD:part * DN_W + (e + 1) * DN_HD] = g[e]
        for e in range(H):
            dS[e] = nxt[e]
        dbg_ref[...] = dbg
        if na:
            pl.when(n == N - 1)(finish)

    rev = lambda n: (N - 1 - n, 0)
    return pl.pallas_call(
        body, name=name, grid=(N,),
        in_specs=[pl.BlockSpec((CHUNK, 3 * DN_W), rev), pl.BlockSpec((CHUNK, 128), rev),
                  pl.BlockSpec((1, H, DN_HD, DN_HD), lambda n: (N - 1 - n, 0, 0, 0)), pl.BlockSpec((CHUNK, DN_W), rev)] + [ANY] * na,
        out_specs=[pl.BlockSpec((CHUNK, 3 * DN_W), rev), pl.BlockSpec((CHUNK, 128), rev)] + [ANY] * na,
        out_shape=[jax.ShapeDtypeStruct((T, 3 * DN_W), F32), jax.ShapeDtypeStruct((T, 128), F32)]
        + [jax.ShapeDtypeStruct(a.shape, a.dtype) for a in a2a_srcs],
        scratch_shapes=[pltpu.VMEM((H, DN_HD, DN_HD), F32)] + (_a2a_sems(na) if na else []),
        compiler_params=_params(("arbitrary",)),
    )(qkv, bg, s_saved, do, *a2a_srcs)


GRP = AT_QH // AT_KVH


def _attn_f(q, kp, kc, vp, vc, sinks, cos_c, sin_c, cos_p, sin_p, rot, has_prev):
    def rope(x, c, s):
        return x * c + _dot(x, rot) * s

    kcr, kpr = rope(kc, cos_c, sin_c), rope(kp, cos_p, sin_p)
    r = lax.broadcasted_iota(jnp.int32, (WINDOW, WINDOW), 0)
    j = lax.broadcasted_iota(jnp.int32, (WINDOW, WINDOW), 1)
    in_c, in_p = j <= r, jnp.logical_and(j > r, has_prev)
    lane = lax.broadcasted_iota(jnp.int32, (1, 128), 1)
    qrs = [rope(qe, cos_c, sin_c) for qe in q]
    scs = [jnp.where(in_c, _bdot_t(qr, kcr) * (AT_HD ** -0.5), NEG) for qr in qrs]
    sps = [jnp.where(in_p, _bdot_t(qr, kpr) * (AT_HD ** -0.5), NEG) for qr in qrs]
    snk = [jnp.sum(jnp.where(lane == 0, s, 0.0), axis=1, keepdims=True) for s in sinks]
    ms = _each(lambda sc, sp, s: jnp.maximum(jnp.maximum(jnp.max(sc, axis=1, keepdims=True), jnp.max(sp, axis=1, keepdims=True)), s),
               scs, sps, snk)
    pcs = _each(lambda sc, m: jnp.exp(sc - m), scs, ms)
    pps = _each(lambda sp, m: jnp.exp(sp - m), sps, ms)
    dens = _each(lambda pc, pp, s, m: jnp.sum(pc, axis=1, keepdims=True) + jnp.sum(pp, axis=1, keepdims=True) + jnp.exp(s - m),
                 pcs, pps, snk, ms)
    return tuple(_each(lambda pc, pp, den: (_bdot(pc, vc) + _bdot(pp, vp)) / den, pcs, pps, dens))


def _attn_specs(nb):
    qs = pl.BlockSpec((GRP, WINDOW, AT_HD), lambda g, n: (g, n, 0))
    kc = pl.BlockSpec((1, WINDOW, AT_HD), lambda g, n: (g, n, 0))
    kp = pl.BlockSpec((1, WINDOW, AT_HD), lambda g, n: (g, jnp.maximum(n - 1, 0), 0))
    tc = pl.BlockSpec((WINDOW, AT_HD), lambda g, n: (n, 0))
    tp = pl.BlockSpec((WINDOW, AT_HD), lambda g, n: (jnp.maximum(n - 1, 0), 0))
    sk = pl.BlockSpec((GRP, 1, 128), lambda g, n: (g, 0, 0))
    rt = pl.BlockSpec((AT_HD, AT_HD), lambda g, n: (0, 0))
    return qs, kc, kp, tc, tp, sk, rt


def _attn_fwd(q, k, v, sinks, cos, sin, rot, name, gather_src=None):
    T = q.shape[1]
    nb = T // WINDOW
    qs, kc, kp, tc, tp, sk, rt = _attn_specs(nb)
    carry = gather_src is not None

    def body(*refs):
        q_ref, kp_ref, kc_ref, vp_ref, vc_ref, sk_ref, cc_ref, sc_ref, cp_ref, sp_ref, rot_ref = refs[:11]
        o_ref = refs[11 + carry]
        n = pl.program_id(1)
        if carry:
            step = pl.program_id(0) * nb + n
            start, forward, finish = _gather_plan(refs[11], refs[13], *refs[14:])
            pl.when(step == 0)(start)
            pl.when(step == nb)(forward)
        heads = range(GRP)
        outs = _attn_f(tuple(q_ref[e] for e in heads), kp_ref[0], kc_ref[0], vp_ref[0], vc_ref[0], tuple(sk_ref[e] for e in heads), cc_ref[...], sc_ref[...],
                       cp_ref[...], sp_ref[...], rot_ref[...], n > 0)
        for e in range(GRP):
            o_ref[e] = outs[e]
        if carry:
            pl.when(step == AT_KVH * nb - 1)(finish)

    gshape = [jax.ShapeDtypeStruct((N_DEV,) + gather_src.shape, gather_src.dtype)] if carry else []
    return pl.pallas_call(
        body, name=name, grid=(AT_KVH, nb),
        in_specs=[qs, kp, kc, kp, kc, sk, tc, tc, tp, tp, rt] + [ANY] * carry, out_specs=[qs] + [ANY] * carry,
        out_shape=[jax.ShapeDtypeStruct(q.shape, F32)] + gshape,
        scratch_shapes=GATHER_SEMS if carry else [],
        compiler_params=_params(("arbitrary", "arbitrary") if carry else ("parallel", "parallel")),
    )(q, k, k, v, v, sinks, cos, sin, cos, sin, rot, *([gather_src] if carry else []))


def _attn_bwd(q, k, v, sinks, cos, sin, rot, do, name):
    T = q.shape[1]
    nb = T // WINDOW
    qs, kc, kp, tc, tp, sk, rt = _attn_specs(nb)

    def body(q_ref, kp_ref, kc_ref, vp_ref, vc_ref, sk_ref, cc_ref, sc_ref, cp_ref, sp_ref, rot_ref, do_ref,
             dq_ref, dkp_ref, dkc_ref, dvp_ref, dvc_ref, dsk_ref):
        n = pl.program_id(1)
        f = functools.partial(_attn_f, cos_c=cc_ref[...], sin_c=sc_ref[...], cos_p=cp_ref[...], sin_p=sp_ref[...],
                              rot=rot_ref[...], has_prev=n > 0)
        heads = range(GRP)
        _, vjp = jax.vjp(f, tuple(q_ref[e] for e in heads), kp_ref[0], kc_ref[0], vp_ref[0], vc_ref[0], tuple(sk_ref[e] for e in heads))
        dq, dkp, dkc, dvp, dvc, dsk = vjp(tuple(do_ref[e] for e in heads))
        dkp_ref[0], dkc_ref[0], dvp_ref[0], dvc_ref[0] = dkp, dkc, dvp, dvc

        @pl.when(n == 0)
        def _():
            dsk_ref[...] = jnp.zeros_like(dsk_ref)

        for e in heads:
            dq_ref[e] = dq[e]
            dsk_ref[e] += dsk[e]

    return pl.pallas_call(
        body, name=name, grid=(AT_KVH, nb),
        in_specs=[qs, kp, kc, kp, kc, sk, tc, tc, tp, tp, rt, qs], out_specs=[qs, kc, kc, kc, kc, sk],
        out_shape=[jax.ShapeDtypeStruct(q.shape, F32)] + [jax.ShapeDtypeStruct(k.shape, F32)] * 4 + [jax.ShapeDtypeStruct(sinks.shape, F32)],
        compiler_params=_params(("arbitrary", "arbitrary")),
    )(q, k, k, v, v, sinks, cos, sin, cos, sin, rot, do)


def _kv_combine(dc, dp, name):
    T = dc.shape[1]
    R = 8 * WINDOW
    ns = T // R

    def body(c_ref, p_ref, q_ref, o_ref):
        n = pl.program_id(1)
        tail = jnp.where(n < ns - 1, q_ref[0], 0.0)
        o_ref[0] = c_ref[0] + jnp.concatenate([p_ref[0, WINDOW:, :], tail], axis=0)

    return pl.pallas_call(
        body, name=name, grid=(AT_KVH, ns),
        in_specs=[pl.BlockSpec((1, R, AT_HD), lambda g, n: (g, n, 0)), pl.BlockSpec((1, R, AT_HD), lambda g, n: (g, n, 0)),
                  pl.BlockSpec((1, WINDOW, AT_HD), lambda g, n: (g, jnp.minimum((n + 1) * 8, T // WINDOW - 1), 0))],
        out_specs=pl.BlockSpec((1, R, AT_HD), lambda g, n: (g, n, 0)),
        out_shape=jax.ShapeDtypeStruct((AT_KVH, T, AT_HD), F32),
        compiler_params=_params(("parallel", "parallel")),
    )(dc, dp, dp)


def _place():
    x, y, c = lax.axis_index("x"), lax.axis_index("y"), lax.axis_index("c")
    return x, y, c, 4 * x + 2 * y + c


def _gather_plan(s_ref, o_ref, send_sems, recv_sems, lsem):
    x, y, c, _ = _place()
    me, sib = (x, y, c), (x, y, 1 - c)
    chips = [(1 - x, y), (x, 1 - y), (1 - x, 1 - y)]

    def copy(k, block, to, src_ref=None):
        slab = o_ref.at[4 * block[0] + 2 * block[1] + block[2]]
        return pltpu.make_async_remote_copy(src_ref=slab if src_ref is None else src_ref, dst_ref=slab,
                                            send_sem=send_sems.at[k], recv_sem=recv_sems.at[k], device_id=to, device_id_type=MESH)

    mine = pltpu.make_async_copy(s_ref, o_ref.at[4 * x + 2 * y + c], lsem)
    first = [copy(0, me, sib, s_ref)] + [copy(1 + j, me, (*chip, c), s_ref) for j, chip in enumerate(chips)]
    passed = [copy(4 + j, (*chip, c), sib) for j, chip in enumerate(chips)]

    def start():
        mine.start()
        for cp in first:
            cp.start()

    def forward():
        for j, chip in enumerate(chips):
            copy(1 + j, (*chip, c), me).wait_recv()
            passed[j].start()

    def finish():
        copy(0, sib, me).wait_recv()
        for j, chip in enumerate(chips):
            copy(4 + j, (*chip, 1 - c), me).wait_recv()
        for cp in first + passed:
            cp.wait_send()
        mine.wait()

    return start, forward, finish


GATHER_SEMS = [pltpu.SemaphoreType.DMA((N_DEV - 1,)), pltpu.SemaphoreType.DMA((N_DEV - 1,)), pltpu.SemaphoreType.DMA]


def _gather(src, name):
    def body(s_ref, o_ref, send_sems, recv_sems, lsem):
        for phase in _gather_plan(s_ref, o_ref, send_sems, recv_sems, lsem):
            phase()

    return pl.pallas_call(
        body, name=name, in_specs=[ANY], out_specs=ANY, out_shape=jax.ShapeDtypeStruct((N_DEV,) + src.shape, src.dtype),
        scratch_shapes=GATHER_SEMS,
    )(src)


def _pair_swap(src, name):
    def body(s_ref, o_ref, send_sem, recv_sem):
        x, y, c, _ = _place()
        cp = pltpu.make_async_remote_copy(src_ref=s_ref.at[1 - c], dst_ref=o_ref, send_sem=send_sem, recv_sem=recv_sem,
                                          device_id=(x, y, 1 - c), device_id_type=MESH)
        cp.start()
        cp.wait()

    return pl.pallas_call(
        body, name=name, in_specs=[ANY], out_specs=ANY, out_shape=jax.ShapeDtypeStruct(src.shape[1:], src.dtype),
        scratch_shapes=[pltpu.SemaphoreType.DMA, pltpu.SemaphoreType.DMA],
    )(src)


def _a2a_plan(s_refs, o_refs, send_sems, recv_sems, lsems):
    x, y, c, _ = _place()
    chip = 2 * x + y
    local, remote = [], []
    for a, (s_ref, o_ref) in enumerate(zip(s_refs, o_refs)):
        local.append(pltpu.make_async_copy(s_ref.at[chip], o_ref.at[chip], lsems.at[a]))
        for k in (1, 2, 3):
            px, py = x ^ (k >> 1), y ^ (k & 1)
            remote.append(pltpu.make_async_remote_copy(
                src_ref=s_ref.at[2 * px + py], dst_ref=o_ref.at[chip], send_sem=send_sems.at[3 * a + k - 1],
                recv_sem=recv_sems.at[3 * a + k - 1], device_id=(px, py, c), device_id_type=MESH))

    def start():
        for cp in local + remote:
            cp.start()

    def finish():
        for cp in remote + local:
            cp.wait()

    return start, finish


def _a2a_sems(n):
    return [pltpu.SemaphoreType.DMA((3 * n,)), pltpu.SemaphoreType.DMA((3 * n,)), pltpu.SemaphoreType.DMA((n,))]


def _chip_a2a(srcs, name):
    n = len(srcs)

    def body(*refs):
        for phase in _a2a_plan(refs[:n], refs[n:2 * n], *refs[2 * n:]):
            phase()

    return pl.pallas_call(
        body, name=name, in_specs=[ANY] * n, out_specs=[ANY] * n, out_shape=[jax.ShapeDtypeStruct(a.shape, a.dtype) for a in srcs],
        scratch_shapes=_a2a_sems(n),
    )(*srcs)


def _add(a, b, name):
    R, C = a.shape
    tr = _pick(R, (512, 256, 128, 64, 32, 16))

    def body(a_ref, b_ref, o_ref):
        o_ref[...] = (a_ref[...].astype(F32) + b_ref[...].astype(F32)).astype(o_ref.dtype)

    s2 = pl.BlockSpec((tr, C), lambda i: (i, 0))
    return pl.pallas_call(body, name=name, grid=(R // tr,), in_specs=[s2, s2], out_specs=s2,
                          out_shape=jax.ShapeDtypeStruct((R, C), a.dtype), compiler_params=_params(("parallel",)))(a, b)


def _chip_sums(slabs, name):
    _, R, C = slabs.shape
    by_core = jnp.transpose(slabs.reshape(4, 2, R, C), (1, 0, 2, 3))
    theirs = _pair_swap(by_core, "swap_" + name)
    mine = lax.dynamic_index_in_dim(by_core, lax.axis_index("c"), axis=0, keepdims=False)
    return _add(mine.reshape(4 * R, C), theirs.reshape(4 * R, C), "add_" + name).reshape(4, R, C)


def _adamw(w, m, v, parts, name):
    R, C = w.shape
    P = parts.shape[0]
    tr = _pick(R, (256, 128, 64, 32, 16, 8))
    c1, c2 = 1.0 - B1 ** STEP, 1.0 - B2 ** STEP

    def body(w_ref, m_ref, v_ref, p_ref, g_ref, d_ref, nm_ref, nv_ref):
        g = p_ref[0].astype(F32)
        for i in range(1, P):
            g = g + p_ref[i].astype(F32)
        wv = w_ref[...]
        nm = B1 * m_ref[...] + (1.0 - B1) * g
        nv = B2 * v_ref[...] + (1.0 - B2) * (g * g)
        g_ref[...] = g
        nm_ref[...] = nm
        nv_ref[...] = nv
        d_ref[...] = -LR * ((nm / c1) / (jnp.sqrt(nv / c2) + AEPS) + WD * wv)

    s2 = pl.BlockSpec((tr, C), lambda i: (i, 0))
    return pl.pallas_call(
        body, name=name, grid=(R // tr,),
        in_specs=[s2, s2, s2, pl.BlockSpec((P, tr, C), lambda i: (0, i, 0))], out_specs=[s2] * 4,
        out_shape=[jax.ShapeDtypeStruct((R, C), F32)] * 4,
        compiler_params=_params(("parallel",)),
    )(w, m, v, parts)


def _colsum(a, name):
    def body(a_ref, o_ref):
        o_ref[...] = jnp.broadcast_to(jnp.sum(a_ref[...], axis=0, keepdims=True), o_ref.shape)

    return pl.pallas_call(body, name=name, out_shape=jax.ShapeDtypeStruct((8, 128), F32))(a)


def _rows128(a):
    f = a.reshape(-1)
    return jnp.pad(f, (0, (-f.shape[0]) % 128)).reshape(-1, 128)


def _to_aligned(w):
    return jnp.concatenate([w[..., 0:4096], w[..., 4112:5392], w[..., 4096:4112],
                            jnp.zeros(w.shape[:-1] + (IN_PAD - IN_COLS,), w.dtype)], axis=-1)


def _from_aligned(w):
    return jnp.concatenate([w[..., 0:4096], w[..., 5376:5392], w[..., 4096:5376]], axis=-1)


def kernel(x, c, ln_mix, ln_ffn, w_ada, b_ada, w_in, dn_conv_w, dn_a_log, dn_dt_bias, dn_norm_w, attn_sinks, w_out, w_gate_up, w_down, ln_final, loss_target, m_ln_mix, m_ln_ffn, m_w_ada, m_b_ada, m_w_in, m_dn_conv_w, m_dn_a_log, m_dn_dt_bias, m_dn_norm_w, m_attn_sinks, m_w_out, m_w_gate_up, m_w_down, m_ln_final, v_ln_mix, v_ln_ffn, v_w_ada, v_b_ada, v_w_in, v_dn_conv_w, v_dn_a_log, v_dn_dt_bias, v_dn_norm_w, v_attn_sinks, v_w_out, v_w_gate_up, v_w_down, v_ln_final):
    T = x.shape[1]
    L = ln_mix.shape[0]
    me = 4 * lax.axis_index("x") + 2 * lax.axis_index("y") + lax.axis_index("c")
    xs = x[0]
    tgt = loss_target[0]

    g_in = _gather(w_in.astype(BF16).reshape(L * D, -1), "ag_w_in").reshape(N_DEV, L, D, -1)
    g_out = _gather(w_out.astype(BF16).reshape(-1, D), "ag_w_out").reshape(N_DEV, L, -1, D)
    g_cv = _gather(dn_conv_w.reshape(L * CONV_K, -1), "ag_conv").reshape(N_DEV, L, CONV_K, -1)
    c_all = _gather(jnp.pad(c, ((0, 7), (0, 0))), "ag_c")[:, 0, :]
    W_in = [_to_aligned(jnp.transpose(g_in[:, l], (1, 0, 2)).reshape(D, IN_COLS)) for l in range(L)]
    W_out = [g_out[:, l].reshape(D, D) for l in range(L)]
    W_gate, W_up, W_dn = [None] * L, [None] * L, [None] * L
    W_cv = [jnp.transpose(g_cv[:, l], (1, 0, 2)).reshape(CONV_K, 3 * DN_W) for l in range(L)]

    c_act = _ew_fwd(lambda v: (_silu(v),), [(jnp.pad(c_all, ((0, 8), (0, 0))), True, False)], [(D, F32, False)], tm=16, name="c_act")[0]
    mods = []
    for l in range(L):
        ms = _mm(c_act, w_ada[l], "nn", F32, f"mod_mm{l}")
        ga = _gather(ms, f"ag_mod{l}")
        mods.append(lax.dynamic_index_in_dim(ga, me, axis=1, keepdims=False).reshape(1, 6 * D))
    row = lambda a: a.reshape(1, -1)
    seg = lambda a, i: a[:, i * D:(i + 1) * D]

    half = AT_HD // 2
    inv_freq = 10000.0 ** (-jnp.arange(half, dtype=F32) * 2.0 / AT_HD)
    ang = jnp.arange(T, dtype=jnp.int32).astype(F32)[:, None] * inv_freq[None, :]
    cos = jnp.concatenate([jnp.cos(ang)] * 2, axis=-1)
    sin = jnp.concatenate([jnp.sin(ang)] * 2, axis=-1)
    ii = jnp.arange(AT_HD)
    rot = jnp.where(ii[:, None] == ii[None, :] + half, -1.0, 0.0) + jnp.where(ii[:, None] + half == ii[None, :], 1.0, 0.0)
    rot = rot.astype(F32)
    heads = lambda a, nh: jnp.transpose(a.reshape(T, nh, AT_HD), (1, 0, 2))
    unheads = lambda a: jnp.transpose(a, (1, 0, 2)).reshape(T, -1)
    pad16 = lambda a: jnp.pad(row(a), ((0, 0), (8, 128 - 16)))

    saved = []
    xc = xs
    for l in range(L):
        mod, bmod = mods[l], row(b_ada[l])
        s = {"x": xc}
        nm_ops = lambda xx, ln, a, b: [(xx, True, False), (row(ln), False, False), (seg(mod, a), False, False),
                                       (seg(bmod, a), False, False), (seg(mod, b), False, False), (seg(bmod, b), False, False)]
        h1 = _ew_fwd(lambda *a: _f_normmod(*a)[:1], nm_ops(xc, ln_mix[l], 1, 0), [(D, BF16, False)], tm=256, name=f"normmod1_{l}")[0]
        proj = _mm(h1, W_in[l], "nn", F32, f"mm_in{l}")
        aq, ak, av = proj[:, 4096:5120], proj[:, 5120:5248], proj[:, 5248:5376]
        conv = _conv_fwd(proj, W_cv[l], 3 * DN_W, f"conv{l}")
        pre_ops = [(conv, True, True)]
        qkvn = _ew_fwd(_f_dnpre, pre_ops, [(3 * DN_W, F32, True)], tm=2048, ncol=3 * DN_HEADS, name=f"dnpre{l}", with_j=True)[0]
        bg_ops = [(proj, True, False, (5376, 128)), (pad16(dn_a_log[l]), False, False), (pad16(dn_dt_bias[l]), False, False)]
        bg = _ew_fwd(_f_bg, bg_ops, [(128, F32, False)], tm=1024, name=f"bg{l}")[0]
        o, s_saved, g_gu = _chunk_fwd(qkvn, bg, f"chunk{l}", gather_src=w_gate_up[l].astype(BF16))
        W_gate[l] = jnp.transpose(g_gu[:4], (1, 0, 2)).reshape(D, FFN)
        W_up[l] = jnp.transpose(g_gu[4:], (1, 0, 2)).reshape(D, FFN)
        post_ops = [(o, True, True), (proj, True, True, (3072, DN_W)), (row(dn_norm_w[l]), False, False)]
        dn_out = _ew_fwd(_f_dnpost, post_ops, [(DN_W, BF16, True)], tm=2048, ncol=DN_HEADS, name=f"dnpost{l}")[0]
        qh, kh, vh = heads(aq, AT_QH), heads(ak, AT_KVH), heads(av, AT_KVH)
        sk = jnp.broadcast_to(attn_sinks[l][:, None, None], (AT_QH, 1, 128))
        at_o, g_dn = _attn_fwd(qh, kh, vh, sk, cos, sin, rot, f"attn{l}", gather_src=w_down[l].astype(BF16))
        W_dn[l] = g_dn.reshape(FFN, D)
        at_out = unheads(at_o).astype(BF16)
        mix = _mm(at_out, W_out[l][DN_W:], "nn", F32, f"mm_out_at{l}", acc_in=_mm(dn_out, W_out[l][:DN_W], "nn", F32, f"mm_out_dn{l}"))
        rg_ops = lambda xx, br, a: [(xx, True, False), (br, True, False), (seg(mod, a), False, False), (seg(bmod, a), False, False)]
        x1 = _ew_fwd(_f_resgate, rg_ops(xc, mix, 2), [(D, F32, False)], tm=256, name=f"resgate1_{l}")[0]
        h2 = _ew_fwd(lambda *a: _f_normmod(*a)[:1], nm_ops(x1, ln_ffn[l], 4, 3), [(D, BF16, False)], tm=256, name=f"normmod2_{l}")[0]
        gate = _mm(h2, W_gate[l], "nn", BF16, f"mm_gate{l}")
        up = _mm(h2, W_up[l], "nn", BF16, f"mm_up{l}")
        sw_ops = [(gate, True, True), (up, True, True)]
        act = _ew_fwd(_f_swiglu, sw_ops, [(FFN, BF16, True)], tm=2048, ncol=11, name=f"swiglu{l}")[0]
        down = _mm(act, W_dn[l], "nn", F32, f"mm_down{l}")
        x2 = _ew_fwd(_f_resgate, rg_ops(x1, down, 5), [(D, F32, False)], tm=256, name=f"resgate2_{l}")[0]
        s.update(h1=h1, proj=proj, qkvn=qkvn, bg=bg, s_saved=s_saved, qh=qh, kh=kh, vh=vh, sk=sk, dn_out=dn_out, at_out=at_out,
                 h2=h2, act=act, bg_ops=bg_ops, post_ops=post_ops, pre_ops=pre_ops, sw_ops=sw_ops,
                 nm1=nm_ops(xc, ln_mix[l], 1, 0), nm2=nm_ops(x1, ln_ffn[l], 4, 3), rg1=rg_ops(xc, mix, 2), rg2=rg_ops(x1, down, 5))
        saved.append(s)
        xc = x2

    fin_ops = [(xc, True, False), (tgt, True, False), (row(ln_final), False, False)]
    lrow = _ew_fwd(_f_final, fin_ops, [(128, F32, False)], tm=256, name="loss_rows")[0]
    loss = lax.psum(_colsum(lrow, "loss_sum")[0, 0], ("x", "y", "c"))
    dx, d_ln_final = _ew_bwd(_f_final, fin_ops, [(jnp.ones((T, 128), F32) / 128.0, False)], [0, 2], tm=256, name="loss_bwd")

    small = {k: [None] * L for k in ("ln_mix", "ln_ffn", "mod", "a_log", "dt", "norm_w", "sinks", "conv")}
    big = {}
    recv = [None] * L
    pending = ()

    def shards(g, cols, n=N_DEV):
        return jnp.transpose(g.reshape(g.shape[0], n, -1), (1, 0, 2)) if cols else g.reshape(n, -1, g.shape[1])

    for l in reversed(range(L)):
        s = saved[l]
        ddown, dgt_f = _ew_bwd(_f_resgate, s["rg2"], [(dx, False)], [1, 2], tm=256, name=f"resgate2_bwd{l}", gdt=[BF16, F32])
        big["w_dn"] = _mm(s["act"], ddown, "tn", BF16, f"wg_down{l}")
        dact = _mm(ddown, W_dn[l], "nt", F32, f"dg_down{l}")
        dgate, dup = _ew_bwd(_f_swiglu, s["sw_ops"], [(dact, True)], [0, 1], tm=2048, ncol=11, name=f"swiglu_bwd{l}", gdt=[BF16, BF16])
        big["w_gate"] = _mm(s["h2"], dgate, "tn", BF16, f"wg_gate{l}")
        big["w_up"] = _mm(s["h2"], dup, "tn", BF16, f"wg_up{l}")
        dh2 = _mm(dup, W_up[l], "nt", F32, f"dg_up{l}", acc_in=_mm(dgate, W_gate[l], "nt", F32, f"dg_gate{l}"))
        dx1, dln_f, dsc_f, dsh_f = _ew_bwd(_f_normmod, s["nm2"], [(dh2, False), (dx, False)], [0, 1, 2, 4], tm=256, name=f"normmod2_bwd{l}")
        dmix, dgt_m = _ew_bwd(_f_resgate, s["rg1"], [(dx1, False)], [1, 2], tm=256, name=f"resgate1_bwd{l}", gdt=[BF16, F32])
        big["w_out"] = jnp.concatenate([_mm(s["dn_out"], dmix, "tn", BF16, f"wg_out_dn{l}"),
                                        _mm(s["at_out"], dmix, "tn", BF16, f"wg_out_at{l}")], axis=0)
        d_dn = _mm(dmix, W_out[l][:DN_W], "nt", F32, f"dg_out_dn{l}")
        d_at = _mm(dmix, W_out[l][DN_W:], "nt", F32, f"dg_out_at{l}")
        dqh, dkp, dkc, dvp, dvc, dsk = _attn_bwd(s["qh"], s["kh"], s["vh"], s["sk"], cos, sin, rot, heads(d_at, AT_QH), f"attn_bwd{l}")
        dkh = _kv_combine(dkc, dkp, f"dk_comb{l}")
        dvh = _kv_combine(dvc, dvp, f"dv_comb{l}")
        do, dz, dnw = _ew_bwd(_f_dnpost, s["post_ops"], [(d_dn, True)], [0, 1, 2], tm=2048, ncol=DN_HEADS, name=f"dnpost_bwd{l}")
        dqkvn, dbg, *got = _chunk_bwd(s["qkvn"], s["bg"], s["s_saved"], do, f"chunk_bwd{l}", a2a_srcs=pending)
        if pending:
            recv[l + 1] = got
        dconv = _ew_bwd(_f_dnpre, s["pre_ops"], [(dqkvn, True)], [0], tm=2048, ncol=3 * DN_HEADS, name=f"dnpre_bwd{l}", with_j=True)[0]
        dba, dalog, ddt = _ew_bwd(_f_bg, s["bg_ops"], [(dbg, False)], [0, 1, 2], tm=1024, name=f"bg_bwd{l}")
        dqkv, dcw = _conv_bwd(s["proj"], dconv, W_cv[l], f"conv_bwd{l}")
        dproj = jnp.concatenate([dqkv, dz, unheads(dqh), unheads(dkh), unheads(dvh), dba, jnp.zeros((T, IN_PAD - 5504), F32)],
                                axis=-1).astype(BF16)
        big["w_in"] = _mm(s["h1"], dproj, "tn", BF16, f"wg_in{l}")
        dh1 = _mm(dproj, W_in[l], "nt", F32, f"dg_in{l}")
        dx, dln_m, dsc_m, dsh_m = _ew_bwd(_f_normmod, s["nm1"], [(dh1, False), (dx1, False)], [0, 1, 2, 4], tm=256, name=f"normmod1_bwd{l}")
        small["ln_mix"][l], small["ln_ffn"][l] = dln_m, dln_f
        small["mod"][l] = jnp.concatenate([dsh_m, dsc_m, dgt_m, dsh_f, dsc_f, dgt_f], axis=-1)
        small["a_log"][l], small["dt"][l] = dalog[:, 8:16], ddt[:, 8:16]
        small["norm_w"][l], small["sinks"][l], small["conv"][l] = dnw, dsk[:, 0, 0], dcw
        pending = (_chip_sums(shards(_from_aligned(big["w_in"]), True), f"w_in{l}"),
                   _chip_sums(shards(big["w_out"], False), f"w_out{l}"),
                   _chip_sums(jnp.concatenate([shards(big["w_gate"], True, 4), shards(big["w_up"], True, 4)], axis=0), f"w_gu{l}"),
                   _chip_sums(shards(big["w_dn"], False), f"w_down{l}"))
    recv[0] = _chip_a2a(pending, "a2a_layer0")
    p_in, p_out, p_gu, p_dn = [jnp.concatenate([recv[l][i] for l in range(L)], axis=1) for i in range(4)]

    cat0 = lambda xs_: jnp.concatenate([_rows128(a) for a in xs_], axis=0)
    stk = lambda k: jnp.stack(small[k])
    pack = cat0([stk("ln_mix"), stk("ln_ffn"), stk("mod"), stk("a_log"), stk("dt"), stk("norm_w"), stk("sinks"), d_ln_final, stk("conv")])
    n_small = pack.shape[0] - L * CONV_K * 3 * DN_W // 128
    pack = jnp.pad(pack, ((0, (-pack.shape[0]) % 8), (0, 0)))
    gp = _gather(pack, "ag_small")
    parts_small = gp[:, :n_small]
    dmod_all = gp[:, 2 * L * D // 128:2 * L * D // 128 + L * 6 * D // 128].reshape(N_DEV, L, 6 * D)
    conv_all = gp[:, n_small:n_small + L * CONV_K * 3 * DN_W // 128].reshape(N_DEV, L * CONV_K, 3 * DN_W)
    parts_conv = lax.dynamic_slice_in_dim(conv_all, me * (3 * DN_W // N_DEV), 3 * DN_W // N_DEV, axis=2)

    dmod_mine = lax.dynamic_slice_in_dim(dmod_all, me * (6 * D // N_DEV), 6 * D // N_DEV, axis=2)
    g_ada = jnp.stack([_mm(c_act, jnp.pad(dmod_mine[:, l], ((0, 8), (0, 0))), "tn", F32, f"wg_ada{l}") for l in range(L)])

    def upd(w, m, v, parts, name):
        shp = w.shape
        r = lambda a: a.reshape(-1, shp[-1])
        return [o_.reshape(shp) for o_ in _adamw(r(w), r(m), r(v), parts.reshape(parts.shape[0], -1, shp[-1]), name)]

    res = {}
    res["w_ada"] = upd(w_ada, m_w_ada, v_w_ada, g_ada[None], "adamw_ada")
    res["w_in"] = upd(w_in, m_w_in, v_w_in, p_in, "adamw_in")
    res["dn_conv_w"] = upd(dn_conv_w, m_dn_conv_w, v_dn_conv_w, parts_conv, "adamw_conv")
    res["w_out"] = upd(w_out, m_w_out, v_w_out, p_out, "adamw_out")
    res["w_gate_up"] = upd(w_gate_up, m_w_gate_up, v_w_gate_up, p_gu, "adamw_gu")
    res["w_down"] = upd(w_down, m_w_down, v_w_down, p_dn, "adamw_down")
    names_s = ["ln_mix", "ln_ffn", "b_ada", "dn_a_log", "dn_dt_bias", "dn_norm_w", "attn_sinks", "ln_final"]
    ws = [ln_mix, ln_ffn, b_ada, dn_a_log, dn_dt_bias, dn_norm_w, attn_sinks, ln_final]
    ms = [m_ln_mix, m_ln_ffn, m_b_ada, m_dn_a_log, m_dn_dt_bias, m_dn_norm_w, m_attn_sinks, m_ln_final]
    vs = [v_ln_mix, v_ln_ffn, v_b_ada, v_dn_a_log, v_dn_dt_bias, v_dn_norm_w, v_attn_sinks, v_ln_final]
    padr = lambda a: jnp.pad(a, ((0, (-a.shape[0]) % 8), (0, 0)))
    vpad = jnp.pad(cat0(vs), ((0, (-n_small) % 8), (0, 0)), constant_values=1.0)
    outs_s = _adamw(padr(cat0(ws)), padr(cat0(ms)), vpad, jnp.pad(parts_small, ((0, 0), (0, (-n_small) % 8), (0, 0))), "adamw_small")
    off = 0
    for nme, wv in zip(names_s, ws):
        nrow = -(-wv.size // 128)
        res[nme] = [o_[off:off + nrow].reshape(-1)[:wv.size].reshape(wv.shape) for o_ in outs_s]
        off += nrow

    order = ["ln_mix", "ln_ffn", "w_ada", "b_ada", "w_in", "dn_conv_w", "dn_a_log", "dn_dt_bias", "dn_norm_w", "attn_sinks",
             "w_out", "w_gate_up", "w_down", "ln_final"]
    return (loss, dx[None], *[res[n][0] for n in order], *[res[n][1] for n in order], *[res[n][2] for n in order],
            *[res[n][3] for n in order])
```

```python
import functools

import jax
import jax.numpy as jnp
from jax import lax
from jax.experimental import pallas as pl
from jax.experimental.pallas import tpu as pltpu

F32, BF16 = jnp.float32, jnp.bfloat16
HI = lax.Precision.HIGH
MESH = pl.DeviceIdType.MESH
ANY = pl.BlockSpec(memory_space=pl.ANY)

N_DEV = 8
D = 2048
DN_HEADS, DN_HD = 8, 128
DN_W = 1024
CONV_K = 4
CHUNK = 64
AT_HD, AT_QH, AT_KVH = 64, 16, 2
AT_W = 1024
WINDOW = 128
FFN = 5632
IN_COLS = 5392
IN_PAD = 5632
EPS = 1e-6
NEG = -1e30
LR, B1, B2, AEPS, WD, STEP = 0.001, 0.9, 0.999, 1e-08, 0.01, 10
VMEM_LIMIT = 56 * 1024 * 1024


def _pick(n, cands):
    for c in cands:
        if n % c == 0:
            return c
    return n


def _params(sem):
    return pltpu.CompilerParams(dimension_semantics=sem, vmem_limit_bytes=VMEM_LIMIT)


_DN = {"nn": (((1,), (0,)), ((), ())), "nt": (((1,), (1,)), ((), ())), "tn": (((0,), (0,)), ((), ()))}


def _mm(a, b, mode, out_dtype, name, acc_in=None, gather_srcs=()):
    if mode == "nn":
        (M, K), (_, N) = a.shape, b.shape
    elif mode == "nt":
        (M, K), (N, _) = a.shape, b.shape
    else:
        (K, M), (_, N) = a.shape, b.shape
    tm = _pick(M, (1024, 512, 256, 128, 64, 32, 16))
    tn = _pick(N, (1024, 512, 256, 128))
    tk = K if K <= 2048 else _pick(K, (2816, 2048, 1024, 512, 256, 128))
    nk = K // tk
    dn = _DN[mode]
    n_mm = 2 if acc_in is None else 3
    ng = len(gather_srcs)
    n_in = n_mm + ng
    steps = (M // tm) * (N // tn) * nk

    def body(*refs):
        a_ref, b_ref, o_ref = refs[0], refs[1], refs[n_in]
        scratch = refs[n_in + 1 + ng:]
        k = pl.program_id(2)
        step = (pl.program_id(0) * (N // tn) + pl.program_id(1)) * nk + k
        plans = [_gather_plan(refs[n_mm + g], refs[n_in + 1 + g], *scratch[(nk > 1) + 3 * g:(nk > 1) + 3 * g + 3]) for g in range(ng)]
        for start, _, _ in plans:
            pl.when(step == 0)(start)
        for _, forward, _ in plans:
            pl.when(step == steps // 2)(forward)
        part = lax.dot_general(a_ref[...].astype(BF16), b_ref[...].astype(BF16), dn, preferred_element_type=F32)
        if nk == 1:
            if acc_in is not None:
                part = part + refs[2][...]
            o_ref[...] = part.astype(o_ref.dtype)
        else:
            acc = scratch[0]

            @pl.when(k == 0)
            def _():
                acc[...] = part if acc_in is None else part + refs[2][...]

            @pl.when(k > 0)
            def _():
                acc[...] += part

            @pl.when(k == nk - 1)
            def _():
                o_ref[...] = acc[...].astype(o_ref.dtype)
        for _, _, finish in plans:
            pl.when(step == steps - 1)(finish)

    a_spec = pl.BlockSpec((tk, tm), lambda i, j, k: (k, i)) if mode == "tn" else pl.BlockSpec((tm, tk), lambda i, j, k: (i, k))
    b_spec = pl.BlockSpec((tn, tk), lambda i, j, k: (j, k)) if mode == "nt" else pl.BlockSpec((tk, tn), lambda i, j, k: (k, j))
    o_spec = pl.BlockSpec((tm, tn), lambda i, j, k: (i, j))
    res = pl.pallas_call(
        body, name=name, grid=(M // tm, N // tn, nk),
        in_specs=[a_spec, b_spec] + ([] if acc_in is None else [o_spec]) + [ANY] * ng, out_specs=[o_spec] + [ANY] * ng,
        out_shape=[jax.ShapeDtypeStruct((M, N), out_dtype)] + [jax.ShapeDtypeStruct((N_DEV,) + g.shape, g.dtype) for g in gather_srcs],
        scratch_shapes=([pltpu.VMEM((tm, tn), F32)] if nk > 1 else []) + GATHER_SEMS * ng,
        compiler_params=_params(("arbitrary",) * 3 if ng else ("parallel", "parallel", "arbitrary")),
    )(*((a, b) if acc_in is None else (a, b, acc_in)), *gather_srcs)
    return res if ng else res[0]


def _op(op):
    arr, rt, cb = op[:3]
    start, width = op[3] if len(op) > 3 else (0, arr.shape[1])
    return arr, rt, cb, start, width


def _ew_spec(op, tm, ncol):
    arr, rt, cb, start, width = _op(op)
    bw = width // ncol if cb else width
    first = start // bw
    return pl.BlockSpec((tm if rt else arr.shape[0], bw), lambda j, i: (i if rt else 0, first + (j if cb else 0)))


def _ew_fwd(f, ops, outs, *, tm, ncol=1, name, with_j=False):
    M = next(op[0].shape[0] for op in ops if op[1])
    tm = min(tm, M)
    n_in = len(ops)

    def body(*refs):
        res = f(*[r[...].astype(F32) for r in refs[:n_in]], *([pl.program_id(0)] if with_j else []))
        for o, r in zip(refs[n_in:], res):
            o[...] = r.astype(o.dtype)

    return pl.pallas_call(
        body, name=name, grid=(ncol, M // tm),
        in_specs=[_ew_spec(op, tm, ncol) for op in ops],
        out_specs=[pl.BlockSpec((tm, w // ncol if cb else w), lambda j, i, cb=cb: (i, j if cb else 0)) for w, _, cb in outs],
        out_shape=[jax.ShapeDtypeStruct((M, w), dt) for w, dt, _ in outs],
        compiler_params=_params(("parallel", "parallel")),
    )(*[op[0] for op in ops])


def _ew_bwd(f, ops, cts, diff, *, tm, ncol=1, name, gdt=None, with_j=False):
    M = next(op[0].shape[0] for op in ops if op[1])
    tm = min(tm, M)
    n_in, n_ct = len(ops), len(cts)
    gdt = gdt or [F32] * len(diff)

    def body(*refs):
        j, i = pl.program_id(0), pl.program_id(1)
        vals = [r[...].astype(F32) for r in refs[:n_in]]

        def fd(*dv):
            full = list(vals)
            for idx, v in zip(diff, dv):
                full[idx] = v
            return tuple(f(*full, *([j] if with_j else [])))

        _, vjp = jax.vjp(fd, *[vals[idx] for idx in diff])
        gs = vjp(tuple(r[...].astype(F32) for r in refs[n_in:n_in + n_ct]))
        for idx, g, gref in zip(diff, gs, refs[n_in + n_ct:]):
            _, rt, cb = ops[idx][:3]
            if rt:
                gref[...] = g.astype(gref.dtype)
            else:
                first = (i == 0) if cb else jnp.logical_and(i == 0, j == 0)

                @pl.when(first)
                def _(gref=gref):
                    gref[...] = jnp.zeros_like(gref)

                gref[...] += g

    def g_spec(op):
        arr, rt, cb, start, width = _op(op)
        return _ew_spec((jax.ShapeDtypeStruct((arr.shape[0], width), F32), rt, cb), tm, ncol) if rt else _ew_spec(op, tm, ncol)

    def g_shape(op, dt):
        arr, rt, cb, start, width = _op(op)
        return jax.ShapeDtypeStruct((M, width), dt) if rt else jax.ShapeDtypeStruct(arr.shape, F32)

    return pl.pallas_call(
        body, name=name, grid=(ncol, M // tm),
        in_specs=[_ew_spec(op, tm, ncol) for op in ops]
        + [pl.BlockSpec((tm, a.shape[1] // ncol if cb else a.shape[1]), lambda j, i, cb=cb: (i, j if cb else 0)) for a, cb in cts],
        out_specs=[g_spec(ops[idx]) for idx in diff],
        out_shape=[g_shape(ops[idx], dt) for idx, dt in zip(diff, gdt)],
        compiler_params=_params(("arbitrary", "arbitrary")),
    )(*[op[0] for op in ops], *[a for a, _ in cts])


def _silu(x):
    return x * jax.nn.sigmoid(x)


def _f_normmod(x, ln, sc, bsc, sh, bsh):
    y = x * lax.rsqrt(jnp.mean(x * x, axis=-1, keepdims=True) + EPS) * ln
    return y * (1.0 + (sc + bsc)) + (sh + bsh), x


def _f_resgate(x, br, gt, bgt):
    return (x + (gt + bgt) * br,)


def _f_swiglu(gate, up):
    return (_silu(gate) * up,)


def _f_final(x, tgt, ln):
    y = x * lax.rsqrt(jnp.mean(x * x, axis=-1, keepdims=True) + EPS) * ln
    e = y - tgt
    return (jnp.broadcast_to(0.5 * jnp.mean(e * e, axis=-1, keepdims=True), (x.shape[0], 128)),)


def _f_bg(ba, alog, dt):
    col = lax.broadcasted_iota(jnp.int32, ba.shape, 1)
    z = ba + dt
    sp = jnp.maximum(z, 0.0) + jnp.log(1.0 + jnp.exp(-jnp.abs(z)))
    return (jnp.where(col < 8, jax.nn.sigmoid(ba), jnp.where(col < 16, -jnp.exp(alog) * sp, 0.0)),)


def _l2n(x):
    return x * lax.rsqrt(jnp.sum(x * x, axis=-1, keepdims=True) + EPS)


def _f_dnpre(c, j):
    a = _silu(c)
    return (jnp.where(j < 2 * DN_HEADS, _l2n(a) * jnp.where(j < DN_HEADS, DN_HD ** -0.5, 1.0), a),)


def _f_dnpost(o, z, nw):
    return (o * lax.rsqrt(jnp.mean(o * o, axis=-1, keepdims=True) + EPS) * nw * _silu(z),)


CONV_BW = 512


def _conv_fwd(x, w, Cw, name):
    T = x.shape[0]
    tm, bw = 512, CONV_BW

    def body(x_ref, h_ref, w_ref, o_ref):
        i = pl.program_id(1)
        cur, halo, wv = x_ref[...], h_ref[...], w_ref[...]
        halo = jnp.where(i > 0, halo, 0.0)
        row = lax.broadcasted_iota(jnp.int32, (8, bw), 0)
        acc = wv[3:4, :] * cur
        for s in (1, 2, 3):
            r = pltpu.roll(cur, s, 0)
            top = jnp.where(row < s, pltpu.roll(halo, s, 0), r[:8])
            acc += wv[3 - s:4 - s, :] * jnp.concatenate([top, r[8:]], axis=0)
        o_ref[...] = acc

    return pl.pallas_call(
        body, name=name, grid=(Cw // bw, T // tm),
        in_specs=[pl.BlockSpec((tm, bw), lambda j, i: (i, j)),
                  pl.BlockSpec((8, bw), lambda j, i: (jnp.maximum(i * (tm // 8) - 1, 0), j)),
                  pl.BlockSpec((4, bw), lambda j, i: (0, j))],
        out_specs=pl.BlockSpec((tm, bw), lambda j, i: (i, j)),
        out_shape=jax.ShapeDtypeStruct((T, Cw), F32),
        compiler_params=_params(("parallel", "parallel")),
    )(x, x, w)


def _conv_bwd(x, dy, w, name):
    T, Cw = dy.shape
    tm, bw = 512, CONV_BW
    nt = T // tm

    def body(x_ref, h_ref, dy_ref, n_ref, w_ref, dx_ref, dw_ref):
        i = pl.program_id(1)
        cur, dcur, wv = x_ref[...], dy_ref[...], w_ref[...]
        halo = jnp.where(i > 0, h_ref[...], 0.0)
        nxt = jnp.where(i < nt - 1, n_ref[...], 0.0)
        row = lax.broadcasted_iota(jnp.int32, (8, bw), 0)

        @pl.when(i == 0)
        def _():
            dw_ref[...] = jnp.zeros_like(dw_ref)

        dx = wv[3:4, :] * dcur
        dw_ref[3:4, :] += jnp.sum(dcur * cur, axis=0, keepdims=True)
        for s in (1, 2, 3):
            r = pltpu.roll(cur, s, 0)
            top = jnp.where(row < s, pltpu.roll(halo, s, 0), r[:8])
            xs = jnp.concatenate([top, r[8:]], axis=0)
            dw_ref[3 - s:4 - s, :] += jnp.sum(dcur * xs, axis=0, keepdims=True)
            rf = pltpu.roll(dcur, tm - s, 0)
            bot = jnp.where(row >= 8 - s, pltpu.roll(nxt, 8 - s, 0), rf[tm - 8:])
            dx += wv[3 - s:4 - s, :] * jnp.concatenate([rf[:tm - 8], bot], axis=0)
        dx_ref[...] = dx

    return pl.pallas_call(
        body, name=name, grid=(Cw // bw, nt),
        in_specs=[pl.BlockSpec((tm, bw), lambda j, i: (i, j)),
                  pl.BlockSpec((8, bw), lambda j, i: (jnp.maximum(i * (tm // 8) - 1, 0), j)),
                  pl.BlockSpec((tm, bw), lambda j, i: (i, j)),
                  pl.BlockSpec((8, bw), lambda j, i: (jnp.minimum((i + 1) * (tm // 8), T // 8 - 1), j)),
                  pl.BlockSpec((4, bw), lambda j, i: (0, j))],
        out_specs=[pl.BlockSpec((tm, bw), lambda j, i: (i, j)), pl.BlockSpec((4, bw), lambda j, i: (0, j))],
        out_shape=[jax.ShapeDtypeStruct((T, Cw), F32), jax.ShapeDtypeStruct((4, Cw), F32)],
        compiler_params=_params(("arbitrary", "arbitrary")),
    )(x, x, dy, dy, w)


def _dot(a, b):
    return jnp.dot(a, b, precision=HI, preferred_element_type=F32)


def _dot_t(a, b):
    return lax.dot_general(a, b, (((1,), (1,)), ((), ())), precision=HI, preferred_element_type=F32)


def _bdot(a, b):
    return jnp.dot(a.astype(BF16), b.astype(BF16), preferred_element_type=F32)


def _bdot_t(a, b):
    return lax.dot_general(a.astype(BF16), b.astype(BF16), (((1,), (1,)), ((), ())), preferred_element_type=F32)


def _each(f, *lists):
    return [f(*a) for a in zip(*lists)]


@jax.custom_vjp
def _unit_lower_inverses(ps):
    C = ps[0].shape[0]
    dist = jnp.bitwise_xor(lax.broadcasted_iota(jnp.int32, (C, C), 0), lax.broadcasted_iota(jnp.int32, (C, C), 1))
    ns = [jnp.where(dist < 8, p, 0.0) for p in ps]
    tis = [jnp.where(dist == 0, 1.0, 0.0) + n for n in ns]
    for _ in range(2):
        ns = _each(lambda n: _dot(n, n), ns)
        tis = _each(lambda t, n: t + _dot(t, n), tis, ns)
    b = 8
    while b < C:
        mids = [jnp.where(jnp.logical_and(dist >= b, dist < 2 * b), p, 0.0) for p in ps]
        halves = _each(_dot, mids, tis)
        tis = _each(lambda t, h: t + _dot(t, h), tis, halves)
        b *= 2
    return tis


def _uli_fwd(ps):
    tis = _unit_lower_inverses(ps)
    return tis, tis


def _uli_bwd(tis, cts):
    tt = lambda a, b: lax.dot_general(a, b, (((0,), (0,)), ((), ())), precision=HI, preferred_element_type=F32)
    half = _each(tt, tis, cts)
    return (_each(_dot_t, half, tis),)


_unit_lower_inverses.defvjp(_uli_fwd, _uli_bwd)


def _chunk_f(qs, ks, vs, bg, Ss, hs):
    C = CHUNK
    lane = lax.broadcasted_iota(jnp.int32, (C, 128), 1)
    betas = [jnp.sum(jnp.where(lane == h, bg, 0.0), axis=1, keepdims=True) for h in hs]
    gs = [jnp.sum(jnp.where(lane == h + 8, bg, 0.0), axis=1, keepdims=True) for h in hs]
    ri = lax.broadcasted_iota(jnp.int32, (C, C), 0)
    ci = lax.broadcasted_iota(jnp.int32, (C, C), 1)
    causal, strict, eye = ri >= ci, ri > ci, ri == ci
    g_rows = _each(lambda g: jnp.sum(jnp.where(eye, g, 0.0), axis=0, keepdims=True), gs)
    gc_cols = _each(lambda gr: jnp.sum(jnp.where(causal, gr, 0.0), axis=1, keepdims=True), g_rows)
    gc_rows = _each(lambda g: jnp.sum(jnp.where(ri <= ci, g, 0.0), axis=0, keepdims=True), gs)
    gc_lasts = _each(lambda g: jnp.sum(g, axis=0, keepdims=True), gs)
    decays = _each(lambda c, r: jnp.exp(jnp.where(causal, c - r, NEG)), gc_cols, gc_rows)
    kbs = _each(jnp.multiply, ks, betas)
    vbs = _each(jnp.multiply, vs, betas)
    ps = _each(lambda kb, k, d: -jnp.where(strict, _bdot_t(kb, k) * d, 0.0), kbs, ks, decays)
    tis = _unit_lower_inverses(ps)
    egcs = _each(jnp.exp, gc_cols)
    ws = _each(lambda t, kb, e: _bdot(t, kb * e), tis, kbs, egcs)
    us = _each(_bdot, tis, vbs)
    intras = _each(lambda q, k, d: jnp.where(causal, _bdot_t(q, k) * d, 0.0), qs, ks, decays)
    kds = _each(lambda k, gl, gc: k * jnp.exp(gl - gc), ks, gc_lasts, gc_cols)
    vns = _each(lambda u, w, S: u - _bdot(w, S), us, ws, Ss)
    os_ = _each(lambda q, e, S, i, vn: _bdot(q * e, S) + _bdot(i, vn), qs, egcs, Ss, intras, vns)
    Sn = _each(lambda S, gl, kd, vn: S * jnp.exp(gl) + lax.dot_general(
        kd.astype(BF16), vn.astype(BF16), (((0,), (0,)), ((), ())), preferred_element_type=F32), Ss, gc_lasts, kds, vns)
    return os_, Sn


def _head_cols(ref, part):
    return [ref[:, part * DN_W + e * DN_HD:part * DN_W + (e + 1) * DN_HD] for e in range(DN_HEADS)]


def _chunk_fwd(qkv, bg, name, gather_src=None):
    T = qkv.shape[0]
    N = T // CHUNK
    H = DN_HEADS
    carry = gather_src is not None

    def body(*refs):
        x_ref, bg_ref = refs[:2]
        o_ref, s_ref = refs[2 + carry:4 + carry]
        S = refs[4 + 2 * carry]
        n = pl.program_id(0)
        if carry:
            start, forward, finish = _gather_plan(refs[2], refs[5], *refs[7:])
            pl.when(n == 0)(start)
            pl.when(n == N // 2)(forward)

        @pl.when(n == 0)
        def _():
            S[...] = jnp.zeros_like(S)

        s_all = S[...]
        s_ref[0] = s_all
        os_, nxt = _chunk_f(_head_cols(x_ref, 0), _head_cols(x_ref, 1), _head_cols(x_ref, 2), bg_ref[...],
                            [s_all[e] for e in range(H)], list(range(H)))
        for e in range(H):
            o_ref[:, e * DN_HD:(e + 1) * DN_HD] = os_[e]
            S[e] = nxt[e]
        if carry:
            pl.when(n == N - 1)(finish)

    gshape = [jax.ShapeDtypeStruct((N_DEV,) + gather_src.shape, gather_src.dtype)] if carry else []
    return pl.pallas_call(
        body, name=name, grid=(N,),
        in_specs=[pl.BlockSpec((CHUNK, 3 * DN_W), lambda n: (n, 0)), pl.BlockSpec((CHUNK, 128), lambda n: (n, 0))] + [ANY] * carry,
        out_specs=[pl.BlockSpec((CHUNK, DN_W), lambda n: (n, 0)), pl.BlockSpec((1, H, DN_HD, DN_HD), lambda n: (n, 0, 0, 0))] + [ANY] * carry,
        out_shape=[jax.ShapeDtypeStruct((T, DN_W), F32), jax.ShapeDtypeStruct((N, H, DN_HD, DN_HD), F32)] + gshape,
        scratch_shapes=[pltpu.VMEM((H, DN_HD, DN_HD), F32)] + (GATHER_SEMS if carry else []),
        compiler_params=_params(("arbitrary",)),
    )(qkv, bg, *([gather_src] if carry else []))


def _chunk_bwd(qkv, bg, s_saved, do, name, a2a_srcs=()):
    T = qkv.shape[0]
    N = T // CHUNK
    H = DN_HEADS
    na = len(a2a_srcs)

    def body(*refs):
        x_ref, bg_ref, s_ref, do_ref = refs[:4]
        dx_ref, dbg_ref = refs[4 + na:6 + na]
        dS = refs[6 + 2 * na]
        n = pl.program_id(0)
        if na:
            start, finish = _a2a_plan(refs[4:4 + na], refs[6 + na:6 + 2 * na], *refs[7 + 2 * na:])
            pl.when(n == 0)(start)

        @pl.when(n == 0)
        def _():
            dS[...] = jnp.zeros_like(dS)

        ds_all = dS[...]
        _, vjp = jax.vjp(functools.partial(_chunk_f, hs=list(range(H))), _head_cols(x_ref, 0), _head_cols(x_ref, 1),
                         _head_cols(x_ref, 2), bg_ref[...], [s_ref[0, e] for e in range(H)])
        dq, dk, dv, dbg, nxt = vjp(([do_ref[:, e * DN_HD:(e + 1) * DN_HD] for e in range(H)], [ds_all[e] for e in range(H)]))
        for part, g in enumerate((dq, dk, dv)):
            for e in range(H):
                dx_ref[:, part * DN_W + e * DN_HD:part * DN_W + (e + 1) * DN_HD] = g[e]
        for e in range(H):
            dS[e] = nxt[e]
        dbg_ref[...] = dbg
        if na:
            pl.when(n == N - 1)(finish)

    rev = lambda n: (N - 1 - n, 0)
    return pl.pallas_call(
        body, name=name, grid=(N,),
        in_specs=[pl.BlockSpec((CHUNK, 3 * DN_W), rev), pl.BlockSpec((CHUNK, 128), rev),
                  pl.BlockSpec((1, H, DN_HD, DN_HD), lambda n: (N - 1 - n, 0, 0, 0)), pl.BlockSpec((CHUNK, DN_W), rev)] + [ANY] * na,
        out_specs=[pl.BlockSpec((CHUNK, 3 * DN_W), rev), pl.BlockSpec((CHUNK, 128), rev)] + [ANY] * na,
        out_shape=[jax.ShapeDtypeStruct((T, 3 * DN_W), F32), jax.ShapeDtypeStruct((T, 128), F32)]
        + [jax.ShapeDtypeStruct(a.shape, a.dtype) for a in a2a_srcs],
        scratch_shapes=[pltpu.VMEM((H, DN_HD, DN_HD), F32)] + (_a2a_sems(na) if na else []),
        compiler_params=_params(("arbitrary",)),
    )(qkv, bg, s_saved, do, *a2a_srcs)


GRP = AT_QH // AT_KVH


def _attn_f(q, kp, kc, vp, vc, sinks, cos_c, sin_c, cos_p, sin_p, rot, has_prev):
    def rope(x, c, s):
        return x * c + _dot(x, rot) * s

    kcr, kpr = rope(kc, cos_c, sin_c), rope(kp, cos_p, sin_p)
    r = lax.broadcasted_iota(jnp.int32, (WINDOW, WINDOW), 0)
    j = lax.broadcasted_iota(jnp.int32, (WINDOW, WINDOW), 1)
    in_c, in_p = j <= r, jnp.logical_and(j > r, has_prev)
    lane = lax.broadcasted_iota(jnp.int32, (1, 128), 1)
    qrs = [rope(qe, cos_c, sin_c) for qe in q]
    scs = [jnp.where(in_c, _bdot_t(qr, kcr) * (AT_HD ** -0.5), NEG) for qr in qrs]
    sps = [jnp.where(in_p, _bdot_t(qr, kpr) * (AT_HD ** -0.5), NEG) for qr in qrs]
    snk = [jnp.sum(jnp.where(lane == 0, s, 0.0), axis=1, keepdims=True) for s in sinks]
    ms = _each(lambda sc, sp, s: jnp.maximum(jnp.maximum(jnp.max(sc, axis=1, keepdims=True), jnp.max(sp, axis=1, keepdims=True)), s),
               scs, sps, snk)
    pcs = _each(lambda sc, m: jnp.exp(sc - m), scs, ms)
    pps = _each(lambda sp, m: jnp.exp(sp - m), sps, ms)
    dens = _each(lambda pc, pp, s, m: jnp.sum(pc, axis=1, keepdims=True) + jnp.sum(pp, axis=1, keepdims=True) + jnp.exp(s - m),
                 pcs, pps, snk, ms)
    return tuple(_each(lambda pc, pp, den: (_bdot(pc, vc) + _bdot(pp, vp)) / den, pcs, pps, dens))


def _attn_specs(nb):
    qs = pl.BlockSpec((GRP, WINDOW, AT_HD), lambda g, n: (g, n, 0))
    kc = pl.BlockSpec((1, WINDOW, AT_HD), lambda g, n: (g, n, 0))
    kp = pl.BlockSpec((1, WINDOW, AT_HD), lambda g, n: (g, jnp.maximum(n - 1, 0), 0))
    tc = pl.BlockSpec((WINDOW, AT_HD), lambda g, n: (n, 0))
    tp = pl.BlockSpec((WINDOW, AT_HD), lambda g, n: (jnp.maximum(n - 1, 0), 0))
    sk = pl.BlockSpec((GRP, 1, 128), lambda g, n: (g, 0, 0))
    rt = pl.BlockSpec((AT_HD, AT_HD), lambda g, n: (0, 0))
    return qs, kc, kp, tc, tp, sk, rt


def _attn_fwd(q, k, v, sinks, cos, sin, rot, name, gather_src=None):
    T = q.shape[1]
    nb = T // WINDOW
    qs, kc, kp, tc, tp, sk, rt = _attn_specs(nb)
    carry = gather_src is not None

    def body(*refs):
        q_ref, kp_ref, kc_ref, vp_ref, vc_ref, sk_ref, cc_ref, sc_ref, cp_ref, sp_ref, rot_ref = refs[:11]
        o_ref = refs[11 + carry]
        n = pl.program_id(1)
        if carry:
            step = pl.program_id(0) * nb + n
            start, forward, finish = _gather_plan(refs[11], refs[13], *refs[14:])
            pl.when(step == 0)(start)
            pl.when(step == nb)(forward)
        heads = range(GRP)
        outs = _attn_f(tuple(q_ref[e] for e in heads), kp_ref[0], kc_ref[0], vp_ref[0], vc_ref[0], tuple(sk_ref[e] for e in heads), cc_ref[...], sc_ref[...],
                       cp_ref[...], sp_ref[...], rot_ref[...], n > 0)
        for e in range(GRP):
            o_ref[e] = outs[e]
        if carry:
            pl.when(step == AT_KVH * nb - 1)(finish)

    gshape = [jax.ShapeDtypeStruct((N_DEV,) + gather_src.shape, gather_src.dtype)] if carry else []
    return pl.pallas_call(
        body, name=name, grid=(AT_KVH, nb),
        in_specs=[qs, kp, kc, kp, kc, sk, tc, tc, tp, tp, rt] + [ANY] * carry, out_specs=[qs] + [ANY] * carry,
        out_shape=[jax.ShapeDtypeStruct(q.shape, F32)] + gshape,
        scratch_shapes=GATHER_SEMS if carry else [],
        compiler_params=_params(("arbitrary", "arbitrary") if carry else ("parallel", "parallel")),
    )(q, k, k, v, v, sinks, cos, sin, cos, sin, rot, *([gather_src] if carry else []))


def _attn_bwd(q, k, v, sinks, cos, sin, rot, do, name, a2a_srcs=()):
    T = q.shape[1]
    nb = T // WINDOW
    qs, kc, kp, tc, tp, sk, rt = _attn_specs(nb)
    na = len(a2a_srcs)

    def body(*refs):
        q_ref, kp_ref, kc_ref, vp_ref, vc_ref, sk_ref, cc_ref, sc_ref, cp_ref, sp_ref, rot_ref, do_ref = refs[:12]
        dq_ref, dkp_ref, dkc_ref, dvp_ref, dvc_ref, dsk_ref = refs[12 + na:18 + na]
        n = pl.program_id(1)
        if na:
            step = pl.program_id(0) * nb + n
            start, finish = _a2a_plan(refs[12:12 + na], refs[18 + na:18 + 2 * na], *refs[18 + 2 * na:])
            pl.when(step == 0)(start)
        f = functools.partial(_attn_f, cos_c=cc_ref[...], sin_c=sc_ref[...], cos_p=cp_ref[...], sin_p=sp_ref[...],
                              rot=rot_ref[...], has_prev=n > 0)
        heads = range(GRP)
        _, vjp = jax.vjp(f, tuple(q_ref[e] for e in heads), kp_ref[0], kc_ref[0], vp_ref[0], vc_ref[0], tuple(sk_ref[e] for e in heads))
        dq, dkp, dkc, dvp, dvc, dsk = vjp(tuple(do_ref[e] for e in heads))
        dkp_ref[0], dkc_ref[0], dvp_ref[0], dvc_ref[0] = dkp, dkc, dvp, dvc

        @pl.when(n == 0)
        def _():
            dsk_ref[...] = jnp.zeros_like(dsk_ref)

        for e in heads:
            dq_ref[e] = dq[e]
            dsk_ref[e] += dsk[e]
        if na:
            pl.when(step == AT_KVH * nb - 1)(finish)

    return pl.pallas_call(
        body, name=name, grid=(AT_KVH, nb),
        in_specs=[qs, kp, kc, kp, kc, sk, tc, tc, tp, tp, rt, qs] + [ANY] * na, out_specs=[qs, kc, kc, kc, kc, sk] + [ANY] * na,
        out_shape=[jax.ShapeDtypeStruct(q.shape, F32)] + [jax.ShapeDtypeStruct(k.shape, F32)] * 4 + [jax.ShapeDtypeStruct(sinks.shape, F32)]
        + [jax.ShapeDtypeStruct(a.shape, a.dtype) for a in a2a_srcs],
        scratch_shapes=_a2a_sems(na) if na else [],
        compiler_params=_params(("arbitrary", "arbitrary")),
    )(q, k, k, v, v, sinks, cos, sin, cos, sin, rot, do, *a2a_srcs)


def _kv_combine(dc, dp, name):
    T = dc.shape[1]
    R = 8 * WINDOW
    ns = T // R

    def body(c_ref, p_ref, q_ref, o_ref):
        n = pl.program_id(1)
        tail = jnp.where(n < ns - 1, q_ref[0], 0.0)
        o_ref[0] = c_ref[0] + jnp.concatenate([p_ref[0, WINDOW:, :], tail], axis=0)

    return pl.pallas_call(
        body, name=name, grid=(AT_KVH, ns),
        in_specs=[pl.BlockSpec((1, R, AT_HD), lambda g, n: (g, n, 0)), pl.BlockSpec((1, R, AT_HD), lambda g, n: (g, n, 0)),
                  pl.BlockSpec((1, WINDOW, AT_HD), lambda g, n: (g, jnp.minimum((n + 1) * 8, T // WINDOW - 1), 0))],
        out_specs=pl.BlockSpec((1, R, AT_HD), lambda g, n: (g, n, 0)),
        out_shape=jax.ShapeDtypeStruct((AT_KVH, T, AT_HD), F32),
        compiler_params=_params(("parallel", "parallel")),
    )(dc, dp, dp)


def _place():
    x, y, c = lax.axis_index("x"), lax.axis_index("y"), lax.axis_index("c")
    return x, y, c, 4 * x + 2 * y + c


def _gather_plan(s_ref, o_ref, send_sems, recv_sems, lsem):
    x, y, c, _ = _place()
    me, sib = (x, y, c), (x, y, 1 - c)
    chips = [(1 - x, y), (x, 1 - y), (1 - x, 1 - y)]

    def copy(k, block, to, src_ref=None):
        slab = o_ref.at[4 * block[0] + 2 * block[1] + block[2]]
        return pltpu.make_async_remote_copy(src_ref=slab if src_ref is None else src_ref, dst_ref=slab,
                                            send_sem=send_sems.at[k], recv_sem=recv_sems.at[k], device_id=to, device_id_type=MESH)

    mine = pltpu.make_async_copy(s_ref, o_ref.at[4 * x + 2 * y + c], lsem)
    first = [copy(0, me, sib, s_ref)] + [copy(1 + j, me, (*chip, c), s_ref) for j, chip in enumerate(chips)]
    passed = [copy(4 + j, (*chip, c), sib) for j, chip in enumerate(chips)]

    def start():
        mine.start()
        for cp in first:
            cp.start()

    def forward():
        for j, chip in enumerate(chips):
            copy(1 + j, (*chip, c), me).wait_recv()
            passed[j].start()

    def finish():
        copy(0, sib, me).wait_recv()
        for j, chip in enumerate(chips):
            copy(4 + j, (*chip, 1 - c), me).wait_recv()
        for cp in first + passed:
            cp.wait_send()
        mine.wait()

    return start, forward, finish


GATHER_SEMS = [pltpu.SemaphoreType.DMA((N_DEV - 1,)), pltpu.SemaphoreType.DMA((N_DEV - 1,)), pltpu.SemaphoreType.DMA]


def _gather(src, name):
    def body(s_ref, o_ref, send_sems, recv_sems, lsem):
        for phase in _gather_plan(s_ref, o_ref, send_sems, recv_sems, lsem):
            phase()

    return pl.pallas_call(
        body, name=name, in_specs=[ANY], out_specs=ANY, out_shape=jax.ShapeDtypeStruct((N_DEV,) + src.shape, src.dtype),
        scratch_shapes=GATHER_SEMS,
    )(src)


def _pair_swap(src, name):
    def body(s_ref, o_ref, send_sem, recv_sem):
        x, y, c, _ = _place()
        cp = pltpu.make_async_remote_copy(src_ref=s_ref.at[1 - c], dst_ref=o_ref, send_sem=send_sem, recv_sem=recv_sem,
                                          device_id=(x, y, 1 - c), device_id_type=MESH)
        cp.start()
        cp.wait()

    return pl.pallas_call(
        body, name=name, in_specs=[ANY], out_specs=ANY, out_shape=jax.ShapeDtypeStruct(src.shape[1:], src.dtype),
        scratch_shapes=[pltpu.SemaphoreType.DMA, pltpu.SemaphoreType.DMA],
    )(src)


def _a2a_plan(s_refs, o_refs, send_sems, recv_sems, lsems):
    x, y, c, _ = _place()
    chip = 2 * x + y
    local, remote = [], []
    for a, (s_ref, o_ref) in enumerate(zip(s_refs, o_refs)):
        local.append(pltpu.make_async_copy(s_ref.at[chip], o_ref.at[chip], lsems.at[a]))
        for k in (1, 2, 3):
            px, py = x ^ (k >> 1), y ^ (k & 1)
            remote.append(pltpu.make_async_remote_copy(
                src_ref=s_ref.at[2 * px + py], dst_ref=o_ref.at[chip], send_sem=send_sems.at[3 * a + k - 1],
                recv_sem=recv_sems.at[3 * a + k - 1], device_id=(px, py, c), device_id_type=MESH))

    def start():
        for cp in local + remote:
            cp.start()

    def finish():
        for cp in remote + local:
            cp.wait()

    return start, finish


def _a2a_sems(n):
    return [pltpu.SemaphoreType.DMA((3 * n,)), pltpu.SemaphoreType.DMA((3 * n,)), pltpu.SemaphoreType.DMA((n,))]


def _chip_a2a(srcs, name):
    n = len(srcs)

    def body(*refs):
        for phase in _a2a_plan(refs[:n], refs[n:2 * n], *refs[2 * n:]):
            phase()

    return pl.pallas_call(
        body, name=name, in_specs=[ANY] * n, out_specs=[ANY] * n, out_shape=[jax.ShapeDtypeStruct(a.shape, a.dtype) for a in srcs],
        scratch_shapes=_a2a_sems(n),
    )(*srcs)


def _add(a, b, name):
    R, C = a.shape
    tr = _pick(R, (512, 256, 128, 64, 32, 16))

    def body(a_ref, b_ref, o_ref):
        o_ref[...] = (a_ref[...].astype(F32) + b_ref[...].astype(F32)).astype(o_ref.dtype)

    s2 = pl.BlockSpec((tr, C), lambda i: (i, 0))
    return pl.pallas_call(body, name=name, grid=(R // tr,), in_specs=[s2, s2], out_specs=s2,
                          out_shape=jax.ShapeDtypeStruct((R, C), a.dtype), compiler_params=_params(("parallel",)))(a, b)


def _chip_sums(slabs, name):
    _, R, C = slabs.shape
    by_core = jnp.transpose(slabs.reshape(4, 2, R, C), (1, 0, 2, 3))
    theirs = _pair_swap(by_core, "swap_" + name)
    mine = lax.dynamic_index_in_dim(by_core, lax.axis_index("c"), axis=0, keepdims=False)
    return _add(mine.reshape(4 * R, C), theirs.reshape(4 * R, C), "add_" + name).reshape(4, R, C)


def _adamw(w, m, v, parts, name):
    R, C = w.shape
    P = parts.shape[0]
    tr = _pick(R, (256, 128, 64, 32, 16, 8))
    c1, c2 = 1.0 - B1 ** STEP, 1.0 - B2 ** STEP

    def body(w_ref, m_ref, v_ref, p_ref, g_ref, d_ref, nm_ref, nv_ref):
        g = p_ref[0].astype(F32)
        for i in range(1, P):
            g = g + p_ref[i].astype(F32)
        wv = w_ref[...]
        nm = B1 * m_ref[...] + (1.0 - B1) * g
        nv = B2 * v_ref[...] + (1.0 - B2) * (g * g)
        g_ref[...] = g
        nm_ref[...] = nm
        nv_ref[...] = nv
        d_ref[...] = -LR * ((nm / c1) / (jnp.sqrt(nv / c2) + AEPS) + WD * wv)

    s2 = pl.BlockSpec((tr, C), lambda i: (i, 0))
    return pl.pallas_call(
        body, name=name, grid=(R // tr,),
        in_specs=[s2, s2, s2, pl.BlockSpec((P, tr, C), lambda i: (0, i, 0))], out_specs=[s2] * 4,
        out_shape=[jax.ShapeDtypeStruct((R, C), F32)] * 4,
        compiler_params=_params(("parallel",)),
    )(w, m, v, parts)


def _colsum(a, name):
    def body(a_ref, o_ref):
        o_ref[...] = jnp.broadcast_to(jnp.sum(a_ref[...], axis=0, keepdims=True), o_ref.shape)

    return pl.pallas_call(body, name=name, out_shape=jax.ShapeDtypeStruct((8, 128), F32))(a)


def _rows128(a):
    f = a.reshape(-1)
    return jnp.pad(f, (0, (-f.shape[0]) % 128)).reshape(-1, 128)


def _to_aligned(w):
    return jnp.concatenate([w[..., 0:4096], w[..., 4112:5392], w[..., 4096:4112],
                            jnp.zeros(w.shape[:-1] + (IN_PAD - IN_COLS,), w.dtype)], axis=-1)


def _from_aligned(w):
    return jnp.concatenate([w[..., 0:4096], w[..., 5376:5392], w[..., 4096:5376]], axis=-1)


def kernel(x, c, ln_mix, ln_ffn, w_ada, b_ada, w_in, dn_conv_w, dn_a_log, dn_dt_bias, dn_norm_w, attn_sinks, w_out, w_gate_up, w_down, ln_final, loss_target, m_ln_mix, m_ln_ffn, m_w_ada, m_b_ada, m_w_in, m_dn_conv_w, m_dn_a_log, m_dn_dt_bias, m_dn_norm_w, m_attn_sinks, m_w_out, m_w_gate_up, m_w_down, m_ln_final, v_ln_mix, v_ln_ffn, v_w_ada, v_b_ada, v_w_in, v_dn_conv_w, v_dn_a_log, v_dn_dt_bias, v_dn_norm_w, v_attn_sinks, v_w_out, v_w_gate_up, v_w_down, v_ln_final):
    T = x.shape[1]
    L = ln_mix.shape[0]
    me = 4 * lax.axis_index("x") + 2 * lax.axis_index("y") + lax.axis_index("c")
    xs = x[0]
    tgt = loss_target[0]

    w_in_layout = lambda g: _to_aligned(jnp.transpose(g, (1, 0, 2)).reshape(D, IN_COLS))
    W_in = [w_in_layout(_gather(w_in[0].astype(BF16), "ag_w_in0"))] + [None] * (L - 1)
    W_out = None
    g_cv = _gather(dn_conv_w.reshape(L * CONV_K, -1), "ag_conv").reshape(N_DEV, L, CONV_K, -1)
    c_all = _gather(jnp.pad(c, ((0, 7), (0, 0))), "ag_c")[:, 0, :]
    W_gate, W_up, W_dn = [None] * L, [None] * L, [None] * L
    W_cv = [jnp.transpose(g_cv[:, l], (1, 0, 2)).reshape(CONV_K, 3 * DN_W) for l in range(L)]

    c_act = _ew_fwd(lambda v: (_silu(v),), [(jnp.pad(c_all, ((0, 8), (0, 0))), True, False)], [(D, F32, False)], tm=16, name="c_act")[0]
    mods = []
    for l in range(L):
        ms = _mm(c_act, w_ada[l], "nn", F32, f"mod_mm{l}")
        ga = _gather(ms, f"ag_mod{l}")
        mods.append(lax.dynamic_index_in_dim(ga, me, axis=1, keepdims=False).reshape(1, 6 * D))
    row = lambda a: a.reshape(1, -1)
    seg = lambda a, i: a[:, i * D:(i + 1) * D]

    half = AT_HD // 2
    inv_freq = 10000.0 ** (-jnp.arange(half, dtype=F32) * 2.0 / AT_HD)
    ang = jnp.arange(T, dtype=jnp.int32).astype(F32)[:, None] * inv_freq[None, :]
    cos = jnp.concatenate([jnp.cos(ang)] * 2, axis=-1)
    sin = jnp.concatenate([jnp.sin(ang)] * 2, axis=-1)
    ii = jnp.arange(AT_HD)
    rot = jnp.where(ii[:, None] == ii[None, :] + half, -1.0, 0.0) + jnp.where(ii[:, None] + half == ii[None, :], 1.0, 0.0)
    rot = rot.astype(F32)
    heads = lambda a, nh: jnp.transpose(a.reshape(T, nh, AT_HD), (1, 0, 2))
    unheads = lambda a: jnp.transpose(a, (1, 0, 2)).reshape(T, -1)
    pad16 = lambda a: jnp.pad(row(a), ((0, 0), (8, 128 - 16)))

    saved = []
    xc = xs
    for l in range(L):
        mod, bmod = mods[l], row(b_ada[l])
        s = {"x": xc}
        nm_ops = lambda xx, ln, a, b: [(xx, True, False), (row(ln), False, False), (seg(mod, a), False, False),
                                       (seg(bmod, a), False, False), (seg(mod, b), False, False), (seg(bmod, b), False, False)]
        h1 = _ew_fwd(lambda *a: _f_normmod(*a)[:1], nm_ops(xc, ln_mix[l], 1, 0), [(D, BF16, False)], tm=256, name=f"normmod1_{l}")[0]
        riders = ([w_in[l + 1].astype(BF16)] if l + 1 < L else []) + ([w_out.astype(BF16).reshape(-1, D)] if l == 0 else [])
        proj, *rode = _mm(h1, W_in[l], "nn", F32, f"mm_in{l}", gather_srcs=riders) if riders else (_mm(h1, W_in[l], "nn", F32, f"mm_in{l}"),)
        if l + 1 < L:
            W_in[l + 1] = w_in_layout(rode[0])
        if l == 0:
            g_out = rode[-1].reshape(N_DEV, L, -1, D)
            W_out = [g_out[:, i].reshape(D, D) for i in range(L)]
        aq, ak, av = proj[:, 4096:5120], proj[:, 5120:5248], proj[:, 5248:5376]
        conv = _conv_fwd(proj, W_cv[l], 3 * DN_W, f"conv{l}")
        pre_ops = [(conv, True, True)]
        qkvn = _ew_fwd(_f_dnpre, pre_ops, [(3 * DN_W, F32, True)], tm=2048, ncol=3 * DN_HEADS, name=f"dnpre{l}", with_j=True)[0]
        bg_ops = [(proj, True, False, (5376, 128)), (pad16(dn_a_log[l]), False, False), (pad16(dn_dt_bias[l]), False, False)]
        bg = _ew_fwd(_f_bg, bg_ops, [(128, F32, False)], tm=1024, name=f"bg{l}")[0]
        o, s_saved, g_gu = _chunk_fwd(qkvn, bg, f"chunk{l}", gather_src=w_gate_up[l].astype(BF16))
        W_gate[l] = jnp.transpose(g_gu[:4], (1, 0, 2)).reshape(D, FFN)
        W_up[l] = jnp.transpose(g_gu[4:], (1, 0, 2)).reshape(D, FFN)
        post_ops = [(o, True, True), (proj, True, True, (3072, DN_W)), (row(dn_norm_w[l]), False, False)]
        dn_out = _ew_fwd(_f_dnpost, post_ops, [(DN_W, BF16, True)], tm=2048, ncol=DN_HEADS, name=f"dnpost{l}")[0]
        qh, kh, vh = heads(aq, AT_QH), heads(ak, AT_KVH), heads(av, AT_KVH)
        sk = jnp.broadcast_to(attn_sinks[l][:, None, None], (AT_QH, 1, 128))
        at_o, g_dn = _attn_fwd(qh, kh, vh, sk, cos, sin, rot, f"attn{l}", gather_src=w_down[l].astype(BF16))
        W_dn[l] = g_dn.reshape(FFN, D)
        at_out = unheads(at_o).astype(BF16)
        mix = _mm(at_out, W_out[l][DN_W:], "nn", F32, f"mm_out_at{l}", acc_in=_mm(dn_out, W_out[l][:DN_W], "nn", F32, f"mm_out_dn{l}"))
        rg_ops = lambda xx, br, a: [(xx, True, False), (br, True, False), (seg(mod, a), False, False), (seg(bmod, a), False, False)]
        x1 = _ew_fwd(_f_resgate, rg_ops(xc, mix, 2), [(D, F32, False)], tm=256, name=f"resgate1_{l}")[0]
        h2 = _ew_fwd(lambda *a: _f_normmod(*a)[:1], nm_ops(x1, ln_ffn[l], 4, 3), [(D, BF16, False)], tm=256, name=f"normmod2_{l}")[0]
        gate = _mm(h2, W_gate[l], "nn", BF16, f"mm_gate{l}")
        up = _mm(h2, W_up[l], "nn", BF16, f"mm_up{l}")
        sw_ops = [(gate, True, True), (up, True, True)]
        act = _ew_fwd(_f_swiglu, sw_ops, [(FFN, BF16, True)], tm=2048, ncol=11, name=f"swiglu{l}")[0]
        down = _mm(act, W_dn[l], "nn", F32, f"mm_down{l}")
        x2 = _ew_fwd(_f_resgate, rg_ops(x1, down, 5), [(D, F32, False)], tm=256, name=f"resgate2_{l}")[0]
        s.update(h1=h1, proj=proj, qkvn=qkvn, bg=bg, s_saved=s_saved, qh=qh, kh=kh, vh=vh, sk=sk, dn_out=dn_out, at_out=at_out,
                 h2=h2, act=act, bg_ops=bg_ops, post_ops=post_ops, pre_ops=pre_ops, sw_ops=sw_ops,
                 nm1=nm_ops(xc, ln_mix[l], 1, 0), nm2=nm_ops(x1, ln_ffn[l], 4, 3), rg1=rg_ops(xc, mix, 2), rg2=rg_ops(x1, down, 5))
        saved.append(s)
        xc = x2

    fin_ops = [(xc, True, False), (tgt, True, False), (row(ln_final), False, False)]
    lrow = _ew_fwd(_f_final, fin_ops, [(128, F32, False)], tm=256, name="loss_rows")[0]
    loss = lax.psum(_colsum(lrow, "loss_sum")[0, 0], ("x", "y", "c"))
    dx, d_ln_final = _ew_bwd(_f_final, fin_ops, [(jnp.ones((T, 128), F32) / 128.0, False)], [0, 2], tm=256, name="loss_bwd")

    small = {k: [None] * L for k in ("ln_mix", "ln_ffn", "mod", "a_log", "dt", "norm_w", "sinks", "conv")}
    big = {}
    recv_ffn, recv_mix = [None] * L, [None] * L
    pending = ()

    def shards(g, cols, n=N_DEV):
        return jnp.transpose(g.reshape(g.shape[0], n, -1), (1, 0, 2)) if cols else g.reshape(n, -1, g.shape[1])

    for l in reversed(range(L)):
        s = saved[l]
        ddown, dgt_f = _ew_bwd(_f_resgate, s["rg2"], [(dx, False)], [1, 2], tm=256, name=f"resgate2_bwd{l}", gdt=[BF16, F32])
        big["w_dn"] = _mm(s["act"], ddown, "tn", BF16, f"wg_down{l}")
        dact = _mm(ddown, W_dn[l], "nt", F32, f"dg_down{l}")
        dgate, dup = _ew_bwd(_f_swiglu, s["sw_ops"], [(dact, True)], [0, 1], tm=2048, ncol=11, name=f"swiglu_bwd{l}", gdt=[BF16, BF16])
        big["w_gate"] = _mm(s["h2"], dgate, "tn", BF16, f"wg_gate{l}")
        big["w_up"] = _mm(s["h2"], dup, "tn", BF16, f"wg_up{l}")
        dh2 = _mm(dup, W_up[l], "nt", F32, f"dg_up{l}", acc_in=_mm(dgate, W_gate[l], "nt", F32, f"dg_gate{l}"))
        dx1, dln_f, dsc_f, dsh_f = _ew_bwd(_f_normmod, s["nm2"], [(dh2, False), (dx, False)], [0, 1, 2, 4], tm=256, name=f"normmod2_bwd{l}")
        dmix, dgt_m = _ew_bwd(_f_resgate, s["rg1"], [(dx1, False)], [1, 2], tm=256, name=f"resgate1_bwd{l}", gdt=[BF16, F32])
        big["w_out"] = jnp.concatenate([_mm(s["dn_out"], dmix, "tn", BF16, f"wg_out_dn{l}"),
                                        _mm(s["at_out"], dmix, "tn", BF16, f"wg_out_at{l}")], axis=0)
        d_dn = _mm(dmix, W_out[l][:DN_W], "nt", F32, f"dg_out_dn{l}")
        d_at = _mm(dmix, W_out[l][DN_W:], "nt", F32, f"dg_out_at{l}")
        ffn_sums = (_chip_sums(jnp.concatenate([shards(big["w_gate"], True, 4), shards(big["w_up"], True, 4)], axis=0), f"w_gu{l}"),
                    _chip_sums(shards(big["w_dn"], False), f"w_down{l}"))
        dqh, dkp, dkc, dvp, dvc, dsk, *recv_ffn[l] = _attn_bwd(s["qh"], s["kh"], s["vh"], s["sk"], cos, sin, rot, heads(d_at, AT_QH),
                                                               f"attn_bwd{l}", a2a_srcs=ffn_sums)
        dkh = _kv_combine(dkc, dkp, f"dk_comb{l}")
        dvh = _kv_combine(dvc, dvp, f"dv_comb{l}")
        do, dz, dnw = _ew_bwd(_f_dnpost, s["post_ops"], [(d_dn, True)], [0, 1, 2], tm=2048, ncol=DN_HEADS, name=f"dnpost_bwd{l}")
        dqkvn, dbg, *got = _chunk_bwd(s["qkvn"], s["bg"], s["s_saved"], do, f"chunk_bwd{l}", a2a_srcs=pending)
        if pending:
            recv_mix[l + 1] = got
        dconv = _ew_bwd(_f_dnpre, s["pre_ops"], [(dqkvn, True)], [0], tm=2048, ncol=3 * DN_HEADS, name=f"dnpre_bwd{l}", with_j=True)[0]
        dba, dalog, ddt = _ew_bwd(_f_bg, s["bg_ops"], [(dbg, False)], [0, 1, 2], tm=1024, name=f"bg_bwd{l}")
        dqkv, dcw = _conv_bwd(s["proj"], dconv, W_cv[l], f"conv_bwd{l}")
        dproj = jnp.concatenate([dqkv, dz, unheads(dqh), unheads(dkh), unheads(dvh), dba, jnp.zeros((T, IN_PAD - 5504), F32)],
                                axis=-1).astype(BF16)
        big["w_in"] = _mm(s["h1"], dproj, "tn", BF16, f"wg_in{l}")
        dh1 = _mm(dproj, W_in[l], "nt", F32, f"dg_in{l}")
        dx, dln_m, dsc_m, dsh_m = _ew_bwd(_f_normmod, s["nm1"], [(dh1, False), (dx1, False)], [0, 1, 2, 4], tm=256, name=f"normmod1_bwd{l}")
        small["ln_mix"][l], small["ln_ffn"][l] = dln_m, dln_f
        small["mod"][l] = jnp.concatenate([dsh_m, dsc_m, dgt_m, dsh_f, dsc_f, dgt_f], axis=-1)
        small["a_log"][l], small["dt"][l] = dalog[:, 8:16], ddt[:, 8:16]
        small["norm_w"][l], small["sinks"][l], small["conv"][l] = dnw, dsk[:, 0, 0], dcw
        pending = (_chip_sums(shards(_from_aligned(big["w_in"]), True), f"w_in{l}"), _chip_sums(shards(big["w_out"], False), f"w_out{l}"))
    recv_mix[0] = _chip_a2a(pending, "a2a_mix0")
    p_in, p_out = [jnp.concatenate([recv_mix[l][i] for l in range(L)], axis=1) for i in range(2)]
    p_gu, p_dn = [jnp.concatenate([recv_ffn[l][i] for l in range(L)], axis=1) for i in range(2)]

    cat0 = lambda xs_: jnp.concatenate([_rows128(a) for a in xs_], axis=0)
    stk = lambda k: jnp.stack(small[k])
    pack = cat0([stk("ln_mix"), stk("ln_ffn"), stk("mod"), stk("a_log"), stk("dt"), stk("norm_w"), stk("sinks"), d_ln_final, stk("conv")])
    n_small = pack.shape[0] - L * CONV_K * 3 * DN_W // 128
    pack = jnp.pad(pack, ((0, (-pack.shape[0]) % 8), (0, 0)))
    gp = _gather(pack, "ag_small")
    parts_small = gp[:, :n_small]
    dmod_all = gp[:, 2 * L * D // 128:2 * L * D // 128 + L * 6 * D // 128].reshape(N_DEV, L, 6 * D)
    conv_all = gp[:, n_small:n_small + L * CONV_K * 3 * DN_W // 128].reshape(N_DEV, L * CONV_K, 3 * DN_W)
    parts_conv = lax.dynamic_slice_in_dim(conv_all, me * (3 * DN_W // N_DEV), 3 * DN_W // N_DEV, axis=2)

    dmod_mine = lax.dynamic_slice_in_dim(dmod_all, me * (6 * D // N_DEV), 6 * D // N_DEV, axis=2)
    g_ada = jnp.stack([_mm(c_act, jnp.pad(dmod_mine[:, l], ((0, 8), (0, 0))), "tn", F32, f"wg_ada{l}") for l in range(L)])

    def upd(w, m, v, parts, name):
        shp = w.shape
        r = lambda a: a.reshape(-1, shp[-1])
        return [o_.reshape(shp) for o_ in _adamw(r(w), r(m), r(v), parts.reshape(parts.shape[0], -1, shp[-1]), name)]

    res = {}
    res["w_ada"] = upd(w_ada, m_w_ada, v_w_ada, g_ada[None], "adamw_ada")
    res["w_in"] = upd(w_in, m_w_in, v_w_in, p_in, "adamw_in")
    res["dn_conv_w"] = upd(dn_conv_w, m_dn_conv_w, v_dn_conv_w, parts_conv, "adamw_conv")
    res["w_out"] = upd(w_out, m_w_out, v_w_out, p_out, "adamw_out")
    res["w_gate_up"] = upd(w_gate_up, m_w_gate_up, v_w_gate_up, p_gu, "adamw_gu")
    res["w_down"] = upd(w_down, m_w_down, v_w_down, p_dn, "adamw_down")
    names_s = ["ln_mix", "ln_ffn", "b_ada", "dn_a_log", "dn_dt_bias", "dn_norm_w", "attn_sinks", "ln_final"]
    ws = [ln_mix, ln_ffn, b_ada, dn_a_log, dn_dt_bias, dn_norm_w, attn_sinks, ln_final]
    ms = [m_ln_mix, m_ln_ffn, m_b_ada, m_dn_a_log, m_dn_dt_bias, m_dn_norm_w, m_attn_sinks, m_ln_final]
    vs = [v_ln_mix, v_ln_ffn, v_b_ada, v_dn_a_log, v_dn_dt_bias, v_dn_norm_w, v_attn_sinks, v_ln_final]
    padr = lambda a: jnp.pad(a, ((0, (-a.shape[0]) % 8), (0, 0)))
    vpad = jnp.pad(cat0(vs), ((0, (-n_small) % 8), (0, 0)), constant_values=1.0)
    outs_s = _adamw(padr(cat0(ws)), padr(cat0(ms)), vpad, jnp.pad(parts_small, ((0, 0), (0, (-n_small) % 8), (0, 0))), "adamw_small")
    off = 0
    for nme, wv in zip(names_s, ws):
        nrow = -(-wv.size // 128)
        res[nme] = [o_[off:off + nrow].reshape(-1)[:wv.size].reshape(wv.shape) for o_ in outs_s]
        off += nrow

    order = ["ln_mix", "ln_ffn", "w_ada", "b_ada", "w_in", "dn_conv_w", "dn_a_log", "dn_dt_bias", "dn_norm_w", "attn_sinks",
             "w_out", "w_gate_up", "w_down", "ln_final"]
    return (loss, dx[None], *[res[n][0] for n in order], *[res[n][1] for n in order], *[res[n][2] for n in order],
            *[res[n][3] for n in order])
```

```python
import functools

import jax
import jax.numpy as jnp
from jax import lax
from jax.experimental import pallas as pl
from jax.experimental.pallas import tpu as pltpu

F32, BF16 = jnp.float32, jnp.bfloat16
HI = lax.Precision.HIGH
MESH = pl.DeviceIdType.MESH
ANY = pl.BlockSpec(memory_space=pl.ANY)

N_DEV = 8
D = 2048
DN_HEADS, DN_HD = 8, 128
DN_W = 1024
CONV_K = 4
CHUNK = 64
AT_HD, AT_QH, AT_KVH = 64, 16, 2
AT_W = 1024
WINDOW = 128
FFN = 5632
IN_COLS = 5392
IN_PAD = 5632
EPS = 1e-6
NEG = -1e30
LR, B1, B2, AEPS, WD, STEP = 0.001, 0.9, 0.999, 1e-08, 0.01, 10
VMEM_LIMIT = 56 * 1024 * 1024


def _pick(n, cands):
    for c in cands:
        if n % c == 0:
            return c
    return n


def _params(sem):
    return pltpu.CompilerParams(dimension_semantics=sem, vmem_limit_bytes=VMEM_LIMIT)


_DN = {"nn": (((1,), (0,)), ((), ())), "nt": (((1,), (1,)), ((), ())), "tn": (((0,), (0,)), ((), ()))}


def _mm(a, b, mode, out_dtype, name, acc_in=None, gather_srcs=()):
    if mode == "nn":
        (M, K), (_, N) = a.shape, b.shape
    elif mode == "nt":
        (M, K), (N, _) = a.shape, b.shape
    else:
        (K, M), (_, N) = a.shape, b.shape
    tm = _pick(M, (1024, 512, 256, 128, 64, 32, 16))
    tn = _pick(N, (1024, 512, 256, 128))
    tk = K if K <= 2048 else _pick(K, (2816, 2048, 1024, 512, 256, 128))
    nk = K // tk
    dn = _DN[mode]
    n_mm = 2 if acc_in is None else 3
    ng = len(gather_srcs)
    n_in = n_mm + ng
    steps = (M // tm) * (N // tn) * nk

    def body(*refs):
        a_ref, b_ref, o_ref = refs[0], refs[1], refs[n_in]
        scratch = refs[n_in + 1 + ng:]
        k = pl.program_id(2)
        step = (pl.program_id(0) * (N // tn) + pl.program_id(1)) * nk + k
        plans = [_gather_plan(refs[n_mm + g], refs[n_in + 1 + g], *scratch[(nk > 1) + 3 * g:(nk > 1) + 3 * g + 3]) for g in range(ng)]
        for start, _, _ in plans:
            pl.when(step == 0)(start)
        for _, forward, _ in plans:
            pl.when(step == steps // 2)(forward)
        part = lax.dot_general(a_ref[...].astype(BF16), b_ref[...].astype(BF16), dn, preferred_element_type=F32)
        if nk == 1:
            if acc_in is not None:
                part = part + refs[2][...]
            o_ref[...] = part.astype(o_ref.dtype)
        else:
            acc = scratch[0]

            @pl.when(k == 0)
            def _():
                acc[...] = part if acc_in is None else part + refs[2][...]

            @pl.when(k > 0)
            def _():
                acc[...] += part

            @pl.when(k == nk - 1)
            def _():
                o_ref[...] = acc[...].astype(o_ref.dtype)
        for _, _, finish in plans:
            pl.when(step == steps - 1)(finish)

    a_spec = pl.BlockSpec((tk, tm), lambda i, j, k: (k, i)) if mode == "tn" else pl.BlockSpec((tm, tk), lambda i, j, k: (i, k))
    b_spec = pl.BlockSpec((tn, tk), lambda i, j, k: (j, k)) if mode == "nt" else pl.BlockSpec((tk, tn), lambda i, j, k: (k, j))
    o_spec = pl.BlockSpec((tm, tn), lambda i, j, k: (i, j))
    res = pl.pallas_call(
        body, name=name, grid=(M // tm, N // tn, nk),
        in_specs=[a_spec, b_spec] + ([] if acc_in is None else [o_spec]) + [ANY] * ng, out_specs=[o_spec] + [ANY] * ng,
        out_shape=[jax.ShapeDtypeStruct((M, N), out_dtype)] + [jax.ShapeDtypeStruct((N_DEV,) + g.shape, g.dtype) for g in gather_srcs],
        scratch_shapes=([pltpu.VMEM((tm, tn), F32)] if nk > 1 else []) + GATHER_SEMS * ng,
        compiler_params=_params(("arbitrary",) * 3 if ng else ("parallel", "parallel", "arbitrary")),
    )(*((a, b) if acc_in is None else (a, b, acc_in)), *gather_srcs)
    return res if ng else res[0]


def _op(op):
    arr, rt, cb = op[:3]
    start, width = op[3] if len(op) > 3 else (0, arr.shape[1])
    return arr, rt, cb, start, width


def _ew_spec(op, tm, ncol):
    arr, rt, cb, start, width = _op(op)
    bw = width // ncol if cb else width
    first = start // bw
    return pl.BlockSpec((tm if rt else arr.shape[0], bw), lambda j, i: (i if rt else 0, first + (j if cb else 0)))


def _ew_fwd(f, ops, outs, *, tm, ncol=1, name, with_j=False):
    M = next(op[0].shape[0] for op in ops if op[1])
    tm = min(tm, M)
    n_in = len(ops)

    def body(*refs):
        res = f(*[r[...].astype(F32) for r in refs[:n_in]], *([pl.program_id(0)] if with_j else []))
        for o, r in zip(refs[n_in:], res):
            o[...] = r.astype(o.dtype)

    return pl.pallas_call(
        body, name=name, grid=(ncol, M // tm),
        in_specs=[_ew_spec(op, tm, ncol) for op in ops],
        out_specs=[pl.BlockSpec((tm, w // ncol if cb else w), lambda j, i, cb=cb: (i, j if cb else 0)) for w, _, cb in outs],
        out_shape=[jax.ShapeDtypeStruct((M, w), dt) for w, dt, _ in outs],
        compiler_params=_params(("parallel", "parallel")),
    )(*[op[0] for op in ops])


def _ew_bwd(f, ops, cts, diff, *, tm, ncol=1, name, gdt=None, with_j=False):
    M = next(op[0].shape[0] for op in ops if op[1])
    tm = min(tm, M)
    n_in, n_ct = len(ops), len(cts)
    gdt = gdt or [F32] * len(diff)

    def body(*refs):
        j, i = pl.program_id(0), pl.program_id(1)
        vals = [r[...].astype(F32) for r in refs[:n_in]]

        def fd(*dv):
            full = list(vals)
            for idx, v in zip(diff, dv):
                full[idx] = v
            return tuple(f(*full, *([j] if with_j else [])))

        _, vjp = jax.vjp(fd, *[vals[idx] for idx in diff])
        gs = vjp(tuple(r[...].astype(F32) for r in refs[n_in:n_in + n_ct]))
        for idx, g, gref in zip(diff, gs, refs[n_in + n_ct:]):
            _, rt, cb = ops[idx][:3]
            if rt:
                gref[...] = g.astype(gref.dtype)
            else:
                first = (i == 0) if cb else jnp.logical_and(i == 0, j == 0)

                @pl.when(first)
                def _(gref=gref):
                    gref[...] = jnp.zeros_like(gref)

                gref[...] += g

    def g_spec(op):
        arr, rt, cb, start, width = _op(op)
        return _ew_spec((jax.ShapeDtypeStruct((arr.shape[0], width), F32), rt, cb), tm, ncol) if rt else _ew_spec(op, tm, ncol)

    def g_shape(op, dt):
        arr, rt, cb, start, width = _op(op)
        return jax.ShapeDtypeStruct((M, width), dt) if rt else jax.ShapeDtypeStruct(arr.shape, F32)

    return pl.pallas_call(
        body, name=name, grid=(ncol, M // tm),
        in_specs=[_ew_spec(op, tm, ncol) for op in ops]
        + [pl.BlockSpec((tm, a.shape[1] // ncol if cb else a.shape[1]), lambda j, i, cb=cb: (i, j if cb else 0)) for a, cb in cts],
        out_specs=[g_spec(ops[idx]) for idx in diff],
        out_shape=[g_shape(ops[idx], dt) for idx, dt in zip(diff, gdt)],
        compiler_params=_params(("arbitrary", "arbitrary")),
    )(*[op[0] for op in ops], *[a for a, _ in cts])


def _silu(x):
    return x * jax.nn.sigmoid(x)


def _f_normmod(x, ln, sc, bsc, sh, bsh):
    y = x * lax.rsqrt(jnp.mean(x * x, axis=-1, keepdims=True) + EPS) * ln
    return y * (1.0 + (sc + bsc)) + (sh + bsh), x


def _f_resgate(x, br, gt, bgt):
    return (x + (gt + bgt) * br,)


def _f_swiglu(gate, up):
    return (_silu(gate) * up,)


def _f_final(x, tgt, ln):
    y = x * lax.rsqrt(jnp.mean(x * x, axis=-1, keepdims=True) + EPS) * ln
    e = y - tgt
    return (jnp.broadcast_to(0.5 * jnp.mean(e * e, axis=-1, keepdims=True), (x.shape[0], 128)),)


def _f_bg(ba, alog, dt):
    col = lax.broadcasted_iota(jnp.int32, ba.shape, 1)
    z = ba + dt
    sp = jnp.maximum(z, 0.0) + jnp.log(1.0 + jnp.exp(-jnp.abs(z)))
    return (jnp.where(col < 8, jax.nn.sigmoid(ba), jnp.where(col < 16, -jnp.exp(alog) * sp, 0.0)),)


def _l2n(x):
    return x * lax.rsqrt(jnp.sum(x * x, axis=-1, keepdims=True) + EPS)


def _f_dnpre(c, j):
    a = _silu(c)
    return (jnp.where(j < 2 * DN_HEADS, _l2n(a) * jnp.where(j < DN_HEADS, DN_HD ** -0.5, 1.0), a),)


def _f_rope(x, c, s, r):
    return (x * c + _dot(x, r) * s,)


def _f_dnpost(o, z, nw):
    return (o * lax.rsqrt(jnp.mean(o * o, axis=-1, keepdims=True) + EPS) * nw * _silu(z),)


CONV_BW = 512


def _conv_fwd(x, w, Cw, name):
    T = x.shape[0]
    tm, bw = 512, CONV_BW

    def body(x_ref, h_ref, w_ref, o_ref):
        i = pl.program_id(1)
        cur, halo, wv = x_ref[...], h_ref[...], w_ref[...]
        halo = jnp.where(i > 0, halo, 0.0)
        row = lax.broadcasted_iota(jnp.int32, (8, bw), 0)
        acc = wv[3:4, :] * cur
        for s in (1, 2, 3):
            r = pltpu.roll(cur, s, 0)
            top = jnp.where(row < s, pltpu.roll(halo, s, 0), r[:8])
            acc += wv[3 - s:4 - s, :] * jnp.concatenate([top, r[8:]], axis=0)
        o_ref[...] = acc

    return pl.pallas_call(
        body, name=name, grid=(Cw // bw, T // tm),
        in_specs=[pl.BlockSpec((tm, bw), lambda j, i: (i, j)),
                  pl.BlockSpec((8, bw), lambda j, i: (jnp.maximum(i * (tm // 8) - 1, 0), j)),
                  pl.BlockSpec((4, bw), lambda j, i: (0, j))],
        out_specs=pl.BlockSpec((tm, bw), lambda j, i: (i, j)),
        out_shape=jax.ShapeDtypeStruct((T, Cw), F32),
        compiler_params=_params(("parallel", "parallel")),
    )(x, x, w)


def _conv_bwd(x, dy, w, name):
    T, Cw = dy.shape
    tm, bw = 512, CONV_BW
    nt = T // tm

    def body(x_ref, h_ref, dy_ref, n_ref, w_ref, dx_ref, dw_ref):
        i = pl.program_id(1)
        cur, dcur, wv = x_ref[...], dy_ref[...], w_ref[...]
        halo = jnp.where(i > 0, h_ref[...], 0.0)
        nxt = jnp.where(i < nt - 1, n_ref[...], 0.0)
        row = lax.broadcasted_iota(jnp.int32, (8, bw), 0)

        @pl.when(i == 0)
        def _():
            dw_ref[...] = jnp.zeros_like(dw_ref)

        dx = wv[3:4, :] * dcur
        dw_ref[3:4, :] += jnp.sum(dcur * cur, axis=0, keepdims=True)
        for s in (1, 2, 3):
            r = pltpu.roll(cur, s, 0)
            top = jnp.where(row < s, pltpu.roll(halo, s, 0), r[:8])
            xs = jnp.concatenate([top, r[8:]], axis=0)
            dw_ref[3 - s:4 - s, :] += jnp.sum(dcur * xs, axis=0, keepdims=True)
            rf = pltpu.roll(dcur, tm - s, 0)
            bot = jnp.where(row >= 8 - s, pltpu.roll(nxt, 8 - s, 0), rf[tm - 8:])
            dx += wv[3 - s:4 - s, :] * jnp.concatenate([rf[:tm - 8], bot], axis=0)
        dx_ref[...] = dx

    return pl.pallas_call(
        body, name=name, grid=(Cw // bw, nt),
        in_specs=[pl.BlockSpec((tm, bw), lambda j, i: (i, j)),
                  pl.BlockSpec((8, bw), lambda j, i: (jnp.maximum(i * (tm // 8) - 1, 0), j)),
                  pl.BlockSpec((tm, bw), lambda j, i: (i, j)),
                  pl.BlockSpec((8, bw), lambda j, i: (jnp.minimum((i + 1) * (tm // 8), T // 8 - 1), j)),
                  pl.BlockSpec((4, bw), lambda j, i: (0, j))],
        out_specs=[pl.BlockSpec((tm, bw), lambda j, i: (i, j)), pl.BlockSpec((4, bw), lambda j, i: (0, j))],
        out_shape=[jax.ShapeDtypeStruct((T, Cw), F32), jax.ShapeDtypeStruct((4, Cw), F32)],
        compiler_params=_params(("arbitrary", "arbitrary")),
    )(x, x, dy, dy, w)


def _dot(a, b):
    return jnp.dot(a, b, precision=HI, preferred_element_type=F32)


def _dot_t(a, b):
    return lax.dot_general(a, b, (((1,), (1,)), ((), ())), precision=HI, preferred_element_type=F32)


def _bdot(a, b):
    return jnp.dot(a.astype(BF16), b.astype(BF16), preferred_element_type=F32)


def _bdot_t(a, b):
    return lax.dot_general(a.astype(BF16), b.astype(BF16), (((1,), (1,)), ((), ())), preferred_element_type=F32)


def _each(f, *lists):
    return [f(*a) for a in zip(*lists)]


@jax.custom_vjp
def _unit_lower_inverses(ps):
    C = ps[0].shape[0]
    dist = jnp.bitwise_xor(lax.broadcasted_iota(jnp.int32, (C, C), 0), lax.broadcasted_iota(jnp.int32, (C, C), 1))
    ns = [jnp.where(dist < 8, p, 0.0) for p in ps]
    tis = [jnp.where(dist == 0, 1.0, 0.0) + n for n in ns]
    for _ in range(2):
        ns = _each(lambda n: _dot(n, n), ns)
        tis = _each(lambda t, n: t + _dot(t, n), tis, ns)
    b = 8
    while b < C:
        mids = [jnp.where(jnp.logical_and(dist >= b, dist < 2 * b), p, 0.0) for p in ps]
        halves = _each(_dot, mids, tis)
        tis = _each(lambda t, h: t + _dot(t, h), tis, halves)
        b *= 2
    return tis


def _uli_fwd(ps):
    tis = _unit_lower_inverses(ps)
    return tis, tis


def _uli_bwd(tis, cts):
    tt = lambda a, b: lax.dot_general(a, b, (((0,), (0,)), ((), ())), precision=HI, preferred_element_type=F32)
    half = _each(tt, tis, cts)
    return (_each(_dot_t, half, tis),)


_unit_lower_inverses.defvjp(_uli_fwd, _uli_bwd)


@jax.custom_vjp
def _kept_inverses(ps, tis):
    return tis


_kept_inverses.defvjp(lambda ps, tis: (tis, tis), lambda tis, cts: (_uli_bwd(tis, cts)[0], [jnp.zeros_like(t) for t in tis]))


def _chunk_f(qs, ks, vs, bg, Ss, hs, kept=None):
    C = CHUNK
    lane = lax.broadcasted_iota(jnp.int32, (C, 128), 1)
    betas = [jnp.sum(jnp.where(lane == h, bg, 0.0), axis=1, keepdims=True) for h in hs]
    gs = [jnp.sum(jnp.where(lane == h + 8, bg, 0.0), axis=1, keepdims=True) for h in hs]
    ri = lax.broadcasted_iota(jnp.int32, (C, C), 0)
    ci = lax.broadcasted_iota(jnp.int32, (C, C), 1)
    causal, strict, eye = ri >= ci, ri > ci, ri == ci
    g_rows = _each(lambda g: jnp.sum(jnp.where(eye, g, 0.0), axis=0, keepdims=True), gs)
    gc_cols = _each(lambda gr: jnp.sum(jnp.where(causal, gr, 0.0), axis=1, keepdims=True), g_rows)
    gc_rows = _each(lambda g: jnp.sum(jnp.where(ri <= ci, g, 0.0), axis=0, keepdims=True), gs)
    gc_lasts = _each(lambda g: jnp.sum(g, axis=0, keepdims=True), gs)
    decays = _each(lambda c, r: jnp.exp(jnp.where(causal, c - r, NEG)), gc_cols, gc_rows)
    kbs = _each(jnp.multiply, ks, betas)
    vbs = _each(jnp.multiply, vs, betas)
    ps = _each(lambda kb, k, d: -jnp.where(strict, _bdot_t(kb, k) * d, 0.0), kbs, ks, decays)
    tis = _unit_lower_inverses(ps) if kept is None else _kept_inverses(ps, kept)
    egcs = _each(jnp.exp, gc_cols)
    ws = _each(lambda t, kb, e: _bdot(t, kb * e), tis, kbs, egcs)
    us = _each(_bdot, tis, vbs)
    intras = _each(lambda q, k, d: jnp.where(causal, _bdot_t(q, k) * d, 0.0), qs, ks, decays)
    kds = _each(lambda k, gl, gc: k * jnp.exp(gl - gc), ks, gc_lasts, gc_cols)
    vns = _each(lambda u, w, S: u - _bdot(w, S), us, ws, Ss)
    os_ = _each(lambda q, e, S, i, vn: _bdot(q * e, S) + _bdot(i, vn), qs, egcs, Ss, intras, vns)
    Sn = _each(lambda S, gl, kd, vn: S * jnp.exp(gl) + lax.dot_general(
        kd.astype(BF16), vn.astype(BF16), (((0,), (0,)), ((), ())), preferred_element_type=F32), Ss, gc_lasts, kds, vns)
    return (os_, Sn), tis


def _head_cols(ref, part):
    return [ref[:, part * DN_W + e * DN_HD:part * DN_W + (e + 1) * DN_HD] for e in range(DN_HEADS)]


def _chunk_fwd(qkv, bg, name, gather_src=None):
    T = qkv.shape[0]
    N = T // CHUNK
    H = DN_HEADS
    carry = gather_src is not None

    def body(*refs):
        x_ref, bg_ref = refs[:2]
        o_ref, s_ref, t_ref = refs[2 + carry:5 + carry]
        S = refs[5 + 2 * carry]
        n = pl.program_id(0)
        if carry:
            start, forward, finish = _gather_plan(refs[2], refs[6], *refs[8:])
            pl.when(n == 0)(start)
            pl.when(n == N // 2)(forward)

        @pl.when(n == 0)
        def _():
            S[...] = jnp.zeros_like(S)

        s_all = S[...]
        s_ref[0] = s_all
        (os_, nxt), tis = _chunk_f(_head_cols(x_ref, 0), _head_cols(x_ref, 1), _head_cols(x_ref, 2), bg_ref[...],
                                   [s_all[e] for e in range(H)], list(range(H)))
        for e in range(H):
            o_ref[:, e * DN_HD:(e + 1) * DN_HD] = os_[e]
            S[e] = nxt[e]
            t_ref[0, e] = tis[e]
        if carry:
            pl.when(n == N - 1)(finish)

    gshape = [jax.ShapeDtypeStruct((N_DEV,) + gather_src.shape, gather_src.dtype)] if carry else []
    return pl.pallas_call(
        body, name=name, grid=(N,),
        in_specs=[pl.BlockSpec((CHUNK, 3 * DN_W), lambda n: (n, 0)), pl.BlockSpec((CHUNK, 128), lambda n: (n, 0))] + [ANY] * carry,
        out_specs=[pl.BlockSpec((CHUNK, DN_W), lambda n: (n, 0)), pl.BlockSpec((1, H, DN_HD, DN_HD), lambda n: (n, 0, 0, 0)),
                   pl.BlockSpec((1, H, CHUNK, CHUNK), lambda n: (n, 0, 0, 0))] + [ANY] * carry,
        out_shape=[jax.ShapeDtypeStruct((T, DN_W), F32), jax.ShapeDtypeStruct((N, H, DN_HD, DN_HD), F32),
                   jax.ShapeDtypeStruct((N, H, CHUNK, CHUNK), F32)] + gshape,
        scratch_shapes=[pltpu.VMEM((H, DN_HD, DN_HD), F32)] + (GATHER_SEMS if carry else []),
        compiler_params=_params(("arbitrary",)),
    )(qkv, bg, *([gather_src] if carry else []))


def _chunk_bwd(qkv, bg, s_saved, t_saved, do, name, a2a_srcs=()):
    T = qkv.shape[0]
    N = T // CHUNK
    H = DN_HEADS
    na = len(a2a_srcs)

    def body(*refs):
        x_ref, bg_ref, s_ref, t_ref, do_ref = refs[:5]
        dx_ref, dbg_ref = refs[5 + na:7 + na]
        dS = refs[7 + 2 * na]
        n = pl.program_id(0)
        if na:
            start, finish = _a2a_plan(refs[5:5 + na], refs[7 + na:7 + 2 * na], *refs[8 + 2 * na:])
            pl.when(n == 0)(start)

        @pl.when(n == 0)
        def _():
            dS[...] = jnp.zeros_like(dS)

        ds_all = dS[...]
        _, vjp, _ = jax.vjp(functools.partial(_chunk_f, hs=list(range(H)), kept=[t_ref[0, e] for e in range(H)]), _head_cols(x_ref, 0),
                            _head_cols(x_ref, 1), _head_cols(x_ref, 2), bg_ref[...], [s_ref[0, e] for e in range(H)], has_aux=True)
        dq, dk, dv, dbg, nxt = vjp(([do_ref[:, e * DN_HD:(e + 1) * DN_HD] for e in range(H)], [ds_all[e] for e in range(H)]))
        for part, g in enumerate((dq, dk, dv)):
            for e in range(H):
                dx_ref[:, part * DN_W + e * DN_HD:part * DN_W + (e + 1) * DN_HD] = g[e]
        for e in range(H):
            dS[e] = nxt[e]
        dbg_ref[...] = dbg
        if na:
            pl.when(n == N - 1)(finish)

    rev = lambda n: (N - 1 - n, 0)
    return pl.pallas_call(
        body, name=name, grid=(N,),
        in_specs=[pl.BlockSpec((CHUNK, 3 * DN_W), rev), pl.BlockSpec((CHUNK, 128), rev),
                  pl.BlockSpec((1, H, DN_HD, DN_HD), lambda n: (N - 1 - n, 0, 0, 0)),
                  pl.BlockSpec((1, H, CHUNK, CHUNK), lambda n: (N - 1 - n, 0, 0, 0)), pl.BlockSpec((CHUNK, DN_W), rev)] + [ANY] * na,
        out_specs=[pl.BlockSpec((CHUNK, 3 * DN_W), rev), pl.BlockSpec((CHUNK, 128), rev)] + [ANY] * na,
        out_shape=[jax.ShapeDtypeStruct((T, 3 * DN_W), F32), jax.ShapeDtypeStruct((T, 128), F32)]
        + [jax.ShapeDtypeStruct(a.shape, a.dtype) for a in a2a_srcs],
        scratch_shapes=[pltpu.VMEM((H, DN_HD, DN_HD), F32)] + (_a2a_sems(na) if na else []),
        compiler_params=_params(("arbitrary",)),
    )(qkv, bg, s_saved, t_saved, do, *a2a_srcs)


GRP = AT_QH // AT_KVH


def _attn_f(qrs, kpr, kcr, vp, vc, sinks, has_prev):
    r = lax.broadcasted_iota(jnp.int32, (WINDOW, WINDOW), 0)
    j = lax.broadcasted_iota(jnp.int32, (WINDOW, WINDOW), 1)
    in_c, in_p = j <= r, jnp.logical_and(j > r, has_prev)
    lane = lax.broadcasted_iota(jnp.int32, (1, 128), 1)
    scs = [jnp.where(in_c, _bdot_t(qr, kcr) * (AT_HD ** -0.5), NEG) for qr in qrs]
    sps = [jnp.where(in_p, _bdot_t(qr, kpr) * (AT_HD ** -0.5), NEG) for qr in qrs]
    snk = [jnp.sum(jnp.where(lane == 0, s, 0.0), axis=1, keepdims=True) for s in sinks]
    ms = _each(lambda sc, sp, s: jnp.maximum(jnp.maximum(jnp.max(sc, axis=1, keepdims=True), jnp.max(sp, axis=1, keepdims=True)), s),
               scs, sps, snk)
    pcs = _each(lambda sc, m: jnp.exp(sc - m), scs, ms)
    pps = _each(lambda sp, m: jnp.exp(sp - m), sps, ms)
    dens = _each(lambda pc, pp, s, m: jnp.sum(pc, axis=1, keepdims=True) + jnp.sum(pp, axis=1, keepdims=True) + jnp.exp(s - m),
                 pcs, pps, snk, ms)
    return tuple(_each(lambda pc, pp, den: (_bdot(pc, vc) + _bdot(pp, vp)) / den, pcs, pps, dens))


def _attn_specs(nb):
    qs = pl.BlockSpec((GRP, WINDOW, AT_HD), lambda g, n: (g, n, 0))
    kc = pl.BlockSpec((1, WINDOW, AT_HD), lambda g, n: (g, n, 0))
    kp = pl.BlockSpec((1, WINDOW, AT_HD), lambda g, n: (g, jnp.maximum(n - 1, 0), 0))
    sk = pl.BlockSpec((GRP, 1, 128), lambda g, n: (g, 0, 0))
    return qs, kc, kp, sk


def _attn_fwd(q, k, v, sinks, name, gather_src=None):
    T = q.shape[1]
    nb = T // WINDOW
    qs, kc, kp, sk = _attn_specs(nb)
    carry = gather_src is not None

    def body(*refs):
        q_ref, kp_ref, kc_ref, vp_ref, vc_ref, sk_ref = refs[:6]
        o_ref = refs[6 + carry]
        n = pl.program_id(1)
        if carry:
            step = pl.program_id(0) * nb + n
            start, forward, finish = _gather_plan(refs[6], refs[8], *refs[9:])
            pl.when(step == 0)(start)
            pl.when(step == nb)(forward)
        heads = range(GRP)
        outs = _attn_f(tuple(q_ref[e] for e in heads), kp_ref[0], kc_ref[0], vp_ref[0], vc_ref[0], tuple(sk_ref[e] for e in heads), n > 0)
        for e in range(GRP):
            o_ref[e] = outs[e]
        if carry:
            pl.when(step == AT_KVH * nb - 1)(finish)

    gshape = [jax.ShapeDtypeStruct((N_DEV,) + gather_src.shape, gather_src.dtype)] if carry else []
    return pl.pallas_call(
        body, name=name, grid=(AT_KVH, nb),
        in_specs=[qs, kp, kc, kp, kc, sk] + [ANY] * carry, out_specs=[qs] + [ANY] * carry,
        out_shape=[jax.ShapeDtypeStruct(q.shape, F32)] + gshape,
        scratch_shapes=GATHER_SEMS if carry else [],
        compiler_params=_params(("arbitrary", "arbitrary") if carry else ("parallel", "parallel")),
    )(q, k, k, v, v, sinks, *([gather_src] if carry else []))


def _attn_bwd(q, k, v, sinks, do, name, a2a_srcs=()):
    T = q.shape[1]
    nb = T // WINDOW
    qs, kc, kp, sk = _attn_specs(nb)
    na = len(a2a_srcs)

    def body(*refs):
        q_ref, kp_ref, kc_ref, vp_ref, vc_ref, sk_ref, do_ref = refs[:7]
        dq_ref, dkp_ref, dkc_ref, dvp_ref, dvc_ref, dsk_ref = refs[7 + na:13 + na]
        n = pl.program_id(1)
        if na:
            step = pl.program_id(0) * nb + n
            start, finish = _a2a_plan(refs[7:7 + na], refs[13 + na:13 + 2 * na], *refs[13 + 2 * na:])
            pl.when(step == 0)(start)
        f = functools.partial(_attn_f, has_prev=n > 0)
        heads = range(GRP)
        _, vjp = jax.vjp(f, tuple(q_ref[e] for e in heads), kp_ref[0], kc_ref[0], vp_ref[0], vc_ref[0], tuple(sk_ref[e] for e in heads))
        dq, dkp, dkc, dvp, dvc, dsk = vjp(tuple(do_ref[e] for e in heads))
        dkp_ref[0], dkc_ref[0], dvp_ref[0], dvc_ref[0] = dkp, dkc, dvp, dvc

        @pl.when(n == 0)
        def _():
            dsk_ref[...] = jnp.zeros_like(dsk_ref)

        for e in heads:
            dq_ref[e] = dq[e]
            dsk_ref[e] += dsk[e]
        if na:
            pl.when(step == AT_KVH * nb - 1)(finish)

    return pl.pallas_call(
        body, name=name, grid=(AT_KVH, nb),
        in_specs=[qs, kp, kc, kp, kc, sk, qs] + [ANY] * na, out_specs=[qs, kc, kc, kc, kc, sk] + [ANY] * na,
        out_shape=[jax.ShapeDtypeStruct(q.shape, F32)] + [jax.ShapeDtypeStruct(k.shape, F32)] * 4 + [jax.ShapeDtypeStruct(sinks.shape, F32)]
        + [jax.ShapeDtypeStruct(a.shape, a.dtype) for a in a2a_srcs],
        scratch_shapes=_a2a_sems(na) if na else [],
        compiler_params=_params(("arbitrary", "arbitrary")),
    )(q, k, k, v, v, sinks, do, *a2a_srcs)


def _kv_combine(dc, dp, name):
    T = dc.shape[1]
    R = 8 * WINDOW
    ns = T // R

    def body(c_ref, p_ref, q_ref, o_ref):
        n = pl.program_id(1)
        tail = jnp.where(n < ns - 1, q_ref[0], 0.0)
        o_ref[0] = c_ref[0] + jnp.concatenate([p_ref[0, WINDOW:, :], tail], axis=0)

    return pl.pallas_call(
        body, name=name, grid=(AT_KVH, ns),
        in_specs=[pl.BlockSpec((1, R, AT_HD), lambda g, n: (g, n, 0)), pl.BlockSpec((1, R, AT_HD), lambda g, n: (g, n, 0)),
                  pl.BlockSpec((1, WINDOW, AT_HD), lambda g, n: (g, jnp.minimum((n + 1) * 8, T // WINDOW - 1), 0))],
        out_specs=pl.BlockSpec((1, R, AT_HD), lambda g, n: (g, n, 0)),
        out_shape=jax.ShapeDtypeStruct((AT_KVH, T, AT_HD), F32),
        compiler_params=_params(("parallel", "parallel")),
    )(dc, dp, dp)


def _place():
    x, y, c = lax.axis_index("x"), lax.axis_index("y"), lax.axis_index("c")
    return x, y, c, 4 * x + 2 * y + c


def _gather_plan(s_ref, o_ref, send_sems, recv_sems, lsem):
    x, y, c, _ = _place()
    me, sib = (x, y, c), (x, y, 1 - c)
    chips = [(1 - x, y), (x, 1 - y), (1 - x, 1 - y)]

    def copy(k, block, to, src_ref=None):
        slab = o_ref.at[4 * block[0] + 2 * block[1] + block[2]]
        return pltpu.make_async_remote_copy(src_ref=slab if src_ref is None else src_ref, dst_ref=slab,
                                            send_sem=send_sems.at[k], recv_sem=recv_sems.at[k], device_id=to, device_id_type=MESH)

    mine = pltpu.make_async_copy(s_ref, o_ref.at[4 * x + 2 * y + c], lsem)
    first = [copy(0, me, sib, s_ref)] + [copy(1 + j, me, (*chip, c), s_ref) for j, chip in enumerate(chips)]
    passed = [copy(4 + j, (*chip, c), sib) for j, chip in enumerate(chips)]

    def start():
        mine.start()
        for cp in first:
            cp.start()

    def forward():
        for j, chip in enumerate(chips):
            copy(1 + j, (*chip, c), me).wait_recv()
            passed[j].start()

    def finish():
        copy(0, sib, me).wait_recv()
        for j, chip in enumerate(chips):
            copy(4 + j, (*chip, 1 - c), me).wait_recv()
        for cp in first + passed:
            cp.wait_send()
        mine.wait()

    return start, forward, finish


GATHER_SEMS = [pltpu.SemaphoreType.DMA((N_DEV - 1,)), pltpu.SemaphoreType.DMA((N_DEV - 1,)), pltpu.SemaphoreType.DMA]


def _gather(src, name):
    def body(s_ref, o_ref, send_sems, recv_sems, lsem):
        for phase in _gather_plan(s_ref, o_ref, send_sems, recv_sems, lsem):
            phase()

    return pl.pallas_call(
        body, name=name, in_specs=[ANY], out_specs=ANY, out_shape=jax.ShapeDtypeStruct((N_DEV,) + src.shape, src.dtype),
        scratch_shapes=GATHER_SEMS,
    )(src)


def _pair_swap(src, name):
    def body(s_ref, o_ref, send_sem, recv_sem):
        x, y, c, _ = _place()
        cp = pltpu.make_async_remote_copy(src_ref=s_ref.at[1 - c], dst_ref=o_ref, send_sem=send_sem, recv_sem=recv_sem,
                                          device_id=(x, y, 1 - c), device_id_type=MESH)
        cp.start()
        cp.wait()

    return pl.pallas_call(
        body, name=name, in_specs=[ANY], out_specs=ANY, out_shape=jax.ShapeDtypeStruct(src.shape[1:], src.dtype),
        scratch_shapes=[pltpu.SemaphoreType.DMA, pltpu.SemaphoreType.DMA],
    )(src)


def _a2a_plan(s_refs, o_refs, send_sems, recv_sems, lsems):
    x, y, c, _ = _place()
    chip = 2 * x + y
    local, remote = [], []
    for a, (s_ref, o_ref) in enumerate(zip(s_refs, o_refs)):
        local.append(pltpu.make_async_copy(s_ref.at[chip], o_ref.at[chip], lsems.at[a]))
        for k in (1, 2, 3):
            px, py = x ^ (k >> 1), y ^ (k & 1)
            remote.append(pltpu.make_async_remote_copy(
                src_ref=s_ref.at[2 * px + py], dst_ref=o_ref.at[chip], send_sem=send_sems.at[3 * a + k - 1],
                recv_sem=recv_sems.at[3 * a + k - 1], device_id=(px, py, c), device_id_type=MESH))

    def start():
        for cp in local + remote:
            cp.start()

    def finish():
        for cp in remote + local:
            cp.wait()

    return start, finish


def _a2a_sems(n):
    return [pltpu.SemaphoreType.DMA((3 * n,)), pltpu.SemaphoreType.DMA((3 * n,)), pltpu.SemaphoreType.DMA((n,))]


def _chip_a2a(srcs, name):
    n = len(srcs)

    def body(*refs):
        for phase in _a2a_plan(refs[:n], refs[n:2 * n], *refs[2 * n:]):
            phase()

    return pl.pallas_call(
        body, name=name, in_specs=[ANY] * n, out_specs=[ANY] * n, out_shape=[jax.ShapeDtypeStruct(a.shape, a.dtype) for a in srcs],
        scratch_shapes=_a2a_sems(n),
    )(*srcs)


def _add(a, b, name):
    R, C = a.shape
    tr = _pick(R, (512, 256, 128, 64, 32, 16))

    def body(a_ref, b_ref, o_ref):
        o_ref[...] = (a_ref[...].astype(F32) + b_ref[...].astype(F32)).astype(o_ref.dtype)

    s2 = pl.BlockSpec((tr, C), lambda i: (i, 0))
    return pl.pallas_call(body, name=name, grid=(R // tr,), in_specs=[s2, s2], out_specs=s2,
                          out_shape=jax.ShapeDtypeStruct((R, C), a.dtype), compiler_params=_params(("parallel",)))(a, b)


def _chip_sums(slabs, name):
    _, R, C = slabs.shape
    by_core = jnp.transpose(slabs.reshape(4, 2, R, C), (1, 0, 2, 3))
    theirs = _pair_swap(by_core, "swap_" + name)
    mine = lax.dynamic_index_in_dim(by_core, lax.axis_index("c"), axis=0, keepdims=False)
    return _add(mine.reshape(4 * R, C), theirs.reshape(4 * R, C), "add_" + name).reshape(4, R, C)


def _adamw(w, m, v, parts, name):
    R, C = w.shape
    P = parts.shape[0]
    tr = _pick(R, (256, 128, 64, 32, 16, 8))
    c1, c2 = 1.0 - B1 ** STEP, 1.0 - B2 ** STEP

    def body(w_ref, m_ref, v_ref, p_ref, g_ref, d_ref, nm_ref, nv_ref):
        g = p_ref[0].astype(F32)
        for i in range(1, P):
            g = g + p_ref[i].astype(F32)
        wv = w_ref[...]
        nm = B1 * m_ref[...] + (1.0 - B1) * g
        nv = B2 * v_ref[...] + (1.0 - B2) * (g * g)
        g_ref[...] = g
        nm_ref[...] = nm
        nv_ref[...] = nv
        d_ref[...] = -LR * ((nm / c1) / (jnp.sqrt(nv / c2) + AEPS) + WD * wv)

    s2 = pl.BlockSpec((tr, C), lambda i: (i, 0))
    return pl.pallas_call(
        body, name=name, grid=(R // tr,),
        in_specs=[s2, s2, s2, pl.BlockSpec((P, tr, C), lambda i: (0, i, 0))], out_specs=[s2] * 4,
        out_shape=[jax.ShapeDtypeStruct((R, C), F32)] * 4,
        compiler_params=_params(("parallel",)),
    )(w, m, v, parts)


def _colsum(a, name):
    def body(a_ref, o_ref):
        o_ref[...] = jnp.broadcast_to(jnp.sum(a_ref[...], axis=0, keepdims=True), o_ref.shape)

    return pl.pallas_call(body, name=name, out_shape=jax.ShapeDtypeStruct((8, 128), F32))(a)


def _rows128(a):
    f = a.reshape(-1)
    return jnp.pad(f, (0, (-f.shape[0]) % 128)).reshape(-1, 128)


def _to_aligned(w):
    return jnp.concatenate([w[..., 0:4096], w[..., 4112:5392], w[..., 4096:4112],
                            jnp.zeros(w.shape[:-1] + (IN_PAD - IN_COLS,), w.dtype)], axis=-1)


def _from_aligned(w):
    return jnp.concatenate([w[..., 0:4096], w[..., 5376:5392], w[..., 4096:5376]], axis=-1)


def kernel(x, c, ln_mix, ln_ffn, w_ada, b_ada, w_in, dn_conv_w, dn_a_log, dn_dt_bias, dn_norm_w, attn_sinks, w_out, w_gate_up, w_down, ln_final, loss_target, m_ln_mix, m_ln_ffn, m_w_ada, m_b_ada, m_w_in, m_dn_conv_w, m_dn_a_log, m_dn_dt_bias, m_dn_norm_w, m_attn_sinks, m_w_out, m_w_gate_up, m_w_down, m_ln_final, v_ln_mix, v_ln_ffn, v_w_ada, v_b_ada, v_w_in, v_dn_conv_w, v_dn_a_log, v_dn_dt_bias, v_dn_norm_w, v_attn_sinks, v_w_out, v_w_gate_up, v_w_down, v_ln_final):
    T = x.shape[1]
    L = ln_mix.shape[0]
    me = 4 * lax.axis_index("x") + 2 * lax.axis_index("y") + lax.axis_index("c")
    xs = x[0]
    tgt = loss_target[0]

    w_in_layout = lambda g: _to_aligned(jnp.transpose(g, (1, 0, 2)).reshape(D, IN_COLS))
    W_in = [w_in_layout(_gather(w_in[0].astype(BF16), "ag_w_in0"))] + [None] * (L - 1)
    W_out = None
    g_cv = _gather(dn_conv_w.reshape(L * CONV_K, -1), "ag_conv").reshape(N_DEV, L, CONV_K, -1)
    c_all = _gather(jnp.pad(c, ((0, 7), (0, 0))), "ag_c")[:, 0, :]
    W_gate, W_up, W_dn = [None] * L, [None] * L, [None] * L
    W_cv = [jnp.transpose(g_cv[:, l], (1, 0, 2)).reshape(CONV_K, 3 * DN_W) for l in range(L)]

    c_act = _ew_fwd(lambda v: (_silu(v),), [(jnp.pad(c_all, ((0, 8), (0, 0))), True, False)], [(D, F32, False)], tm=16, name="c_act")[0]
    mods = []
    for l in range(L):
        ms = _mm(c_act, w_ada[l], "nn", F32, f"mod_mm{l}")
        ga = _gather(ms, f"ag_mod{l}")
        mods.append(lax.dynamic_index_in_dim(ga, me, axis=1, keepdims=False).reshape(1, 6 * D))
    row = lambda a: a.reshape(1, -1)
    seg = lambda a, i: a[:, i * D:(i + 1) * D]

    half = AT_HD // 2
    inv_freq = 10000.0 ** (-jnp.arange(half, dtype=F32) * 2.0 / AT_HD)
    ang = jnp.arange(T, dtype=jnp.int32).astype(F32)[:, None] * inv_freq[None, :]
    cos = jnp.concatenate([jnp.cos(ang)] * 4, axis=-1)
    sin = jnp.concatenate([jnp.sin(ang)] * 4, axis=-1)
    ii = jnp.arange(2 * AT_HD)
    same = (ii[:, None] // AT_HD) == (ii[None, :] // AT_HD)
    rot = (jnp.where(same & (ii[:, None] == ii[None, :] + half), -1.0, 0.0)
           + jnp.where(same & (ii[:, None] + half == ii[None, :]), 1.0, 0.0)).astype(F32)
    rope_ops = lambda arr, first, width: [(arr, True, True, (first, width)), (cos, True, False), (sin, True, False), (rot, False, False)]
    heads = lambda a, nh: jnp.transpose(a.reshape(T, nh, AT_HD), (1, 0, 2))
    unheads = lambda a: jnp.transpose(a, (1, 0, 2)).reshape(T, -1)
    pad16 = lambda a: jnp.pad(row(a), ((0, 0), (8, 128 - 16)))

    saved = []
    xc = xs
    for l in range(L):
        mod, bmod = mods[l], row(b_ada[l])
        s = {"x": xc}
        nm_ops = lambda xx, ln, a, b: [(xx, True, False), (row(ln), False, False), (seg(mod, a), False, False),
                                       (seg(bmod, a), False, False), (seg(mod, b), False, False), (seg(bmod, b), False, False)]
        h1 = _ew_fwd(lambda *a: _f_normmod(*a)[:1], nm_ops(xc, ln_mix[l], 1, 0), [(D, BF16, False)], tm=256, name=f"normmod1_{l}")[0]
        riders = ([w_in[l + 1].astype(BF16)] if l + 1 < L else []) + ([w_out.astype(BF16).reshape(-1, D)] if l == 0 else [])
        proj, *rode = _mm(h1, W_in[l], "nn", F32, f"mm_in{l}", gather_srcs=riders) if riders else (_mm(h1, W_in[l], "nn", F32, f"mm_in{l}"),)
        if l + 1 < L:
            W_in[l + 1] = w_in_layout(rode[0])
        if l == 0:
            g_out = rode[-1].reshape(N_DEV, L, -1, D)
            W_out = [g_out[:, i].reshape(D, D) for i in range(L)]
        rq_ops, rk_ops = rope_ops(proj, 4096, AT_W), rope_ops(proj, 5120, 128)
        aq = _ew_fwd(_f_rope, rq_ops, [(AT_W, F32, True)], tm=1024, ncol=AT_W // 128, name=f"rope_q{l}")[0]
        ak = _ew_fwd(_f_rope, rk_ops, [(128, F32, True)], tm=1024, ncol=1, name=f"rope_k{l}")[0]
        av = proj[:, 5248:5376]
        conv = _conv_fwd(proj, W_cv[l], 3 * DN_W, f"conv{l}")
        pre_ops = [(conv, True, True)]
        qkvn = _ew_fwd(_f_dnpre, pre_ops, [(3 * DN_W, F32, True)], tm=2048, ncol=3 * DN_HEADS, name=f"dnpre{l}", with_j=True)[0]
        bg_ops = [(proj, True, False, (5376, 128)), (pad16(dn_a_log[l]), False, False), (pad16(dn_dt_bias[l]), False, False)]
        bg = _ew_fwd(_f_bg, bg_ops, [(128, F32, False)], tm=1024, name=f"bg{l}")[0]
        o, s_saved, t_saved, g_gu = _chunk_fwd(qkvn, bg, f"chunk{l}", gather_src=w_gate_up[l].astype(BF16))
        W_gate[l] = jnp.transpose(g_gu[:4], (1, 0, 2)).reshape(D, FFN)
        W_up[l] = jnp.transpose(g_gu[4:], (1, 0, 2)).reshape(D, FFN)
        post_ops = [(o, True, True), (proj, True, True, (3072, DN_W)), (row(dn_norm_w[l]), False, False)]
        dn_out = _ew_fwd(_f_dnpost, post_ops, [(DN_W, BF16, True)], tm=2048, ncol=DN_HEADS, name=f"dnpost{l}")[0]
        qh, kh, vh = heads(aq, AT_QH), heads(ak, AT_KVH), heads(av, AT_KVH)
        sk = jnp.broadcast_to(attn_sinks[l][:, None, None], (AT_QH, 1, 128))
        at_o, g_dn = _attn_fwd(qh, kh, vh, sk, f"attn{l}", gather_src=w_down[l].astype(BF16))
        W_dn[l] = g_dn.reshape(FFN, D)
        at_out = unheads(at_o).astype(BF16)
        mix = _mm(at_out, W_out[l][DN_W:], "nn", F32, f"mm_out_at{l}", acc_in=_mm(dn_out, W_out[l][:DN_W], "nn", F32, f"mm_out_dn{l}"))
        rg_ops = lambda xx, br, a: [(xx, True, False), (br, True, False), (seg(mod, a), False, False), (seg(bmod, a), False, False)]
        x1 = _ew_fwd(_f_resgate, rg_ops(xc, mix, 2), [(D, F32, False)], tm=256, name=f"resgate1_{l}")[0]
        h2 = _ew_fwd(lambda *a: _f_normmod(*a)[:1], nm_ops(x1, ln_ffn[l], 4, 3), [(D, BF16, False)], tm=256, name=f"normmod2_{l}")[0]
        gate = _mm(h2, W_gate[l], "nn", BF16, f"mm_gate{l}")
        up = _mm(h2, W_up[l], "nn", BF16, f"mm_up{l}")
        sw_ops = [(gate, True, True), (up, True, True)]
        act = _ew_fwd(_f_swiglu, sw_ops, [(FFN, BF16, True)], tm=2048, ncol=11, name=f"swiglu{l}")[0]
        down = _mm(act, W_dn[l], "nn", F32, f"mm_down{l}")
        x2 = _ew_fwd(_f_resgate, rg_ops(x1, down, 5), [(D, F32, False)], tm=256, name=f"resgate2_{l}")[0]
        s.update(h1=h1, proj=proj, qkvn=qkvn, bg=bg, s_saved=s_saved, t_saved=t_saved, qh=qh, kh=kh, vh=vh, sk=sk, dn_out=dn_out, at_out=at_out,
                 h2=h2, act=act, bg_ops=bg_ops, post_ops=post_ops, pre_ops=pre_ops, sw_ops=sw_ops, rq_ops=rq_ops, rk_ops=rk_ops,
                 nm1=nm_ops(xc, ln_mix[l], 1, 0), nm2=nm_ops(x1, ln_ffn[l], 4, 3), rg1=rg_ops(xc, mix, 2), rg2=rg_ops(x1, down, 5))
        saved.append(s)
        xc = x2

    fin_ops = [(xc, True, False), (tgt, True, False), (row(ln_final), False, False)]
    lrow = _ew_fwd(_f_final, fin_ops, [(128, F32, False)], tm=256, name="loss_rows")[0]
    loss = lax.psum(_colsum(lrow, "loss_sum")[0, 0], ("x", "y", "c"))
    dx, d_ln_final = _ew_bwd(_f_final, fin_ops, [(jnp.ones((T, 128), F32) / 128.0, False)], [0, 2], tm=256, name="loss_bwd")

    small = {k: [None] * L for k in ("ln_mix", "ln_ffn", "mod", "a_log", "dt", "norm_w", "sinks", "conv")}
    big = {}
    recv_ffn, recv_mix = [None] * L, [None] * L
    pending = ()

    def shards(g, cols, n=N_DEV):
        return jnp.transpose(g.reshape(g.shape[0], n, -1), (1, 0, 2)) if cols else g.reshape(n, -1, g.shape[1])

    for l in reversed(range(L)):
        s = saved[l]
        ddown, dgt_f = _ew_bwd(_f_resgate, s["rg2"], [(dx, False)], [1, 2], tm=256, name=f"resgate2_bwd{l}", gdt=[BF16, F32])
        big["w_dn"] = _mm(s["act"], ddown, "tn", BF16, f"wg_down{l}")
        dact = _mm(ddown, W_dn[l], "nt", F32, f"dg_down{l}")
        dgate, dup = _ew_bwd(_f_swiglu, s["sw_ops"], [(dact, True)], [0, 1], tm=2048, ncol=11, name=f"swiglu_bwd{l}", gdt=[BF16, BF16])
        big["w_gate"] = _mm(s["h2"], dgate, "tn", BF16, f"wg_gate{l}")
        big["w_up"] = _mm(s["h2"], dup, "tn", BF16, f"wg_up{l}")
        dh2 = _mm(dup, W_up[l], "nt", F32, f"dg_up{l}", acc_in=_mm(dgate, W_gate[l], "nt", F32, f"dg_gate{l}"))
        dx1, dln_f, dsc_f, dsh_f = _ew_bwd(_f_normmod, s["nm2"], [(dh2, False), (dx, False)], [0, 1, 2, 4], tm=256, name=f"normmod2_bwd{l}")
        dmix, dgt_m = _ew_bwd(_f_resgate, s["rg1"], [(dx1, False)], [1, 2], tm=256, name=f"resgate1_bwd{l}", gdt=[BF16, F32])
        big["w_out"] = jnp.concatenate([_mm(s["dn_out"], dmix, "tn", BF16, f"wg_out_dn{l}"),
                                        _mm(s["at_out"], dmix, "tn", BF16, f"wg_out_at{l}")], axis=0)
        d_dn = _mm(dmix, W_out[l][:DN_W], "nt", F32, f"dg_out_dn{l}")
        d_at = _mm(dmix, W_out[l][DN_W:], "nt", F32, f"dg_out_at{l}")
        ffn_sums = (_chip_sums(jnp.concatenate([shards(big["w_gate"], True, 4), shards(big["w_up"], True, 4)], axis=0), f"w_gu{l}"),
                    _chip_sums(shards(big["w_dn"], False), f"w_down{l}"))
        dqh, dkp, dkc, dvp, dvc, dsk, *recv_ffn[l] = _attn_bwd(s["qh"], s["kh"], s["vh"], s["sk"], heads(d_at, AT_QH), f"attn_bwd{l}",
                                                               a2a_srcs=ffn_sums)
        dkh = _kv_combine(dkc, dkp, f"dk_comb{l}")
        dvh = _kv_combine(dvc, dvp, f"dv_comb{l}")
        do, dz, dnw = _ew_bwd(_f_dnpost, s["post_ops"], [(d_dn, True)], [0, 1, 2], tm=2048, ncol=DN_HEADS, name=f"dnpost_bwd{l}")
        dqkvn, dbg, *got = _chunk_bwd(s["qkvn"], s["bg"], s["s_saved"], s["t_saved"], do, f"chunk_bwd{l}", a2a_srcs=pending)
        if pending:
            recv_mix[l + 1] = got
        dconv = _ew_bwd(_f_dnpre, s["pre_ops"], [(dqkvn, True)], [0], tm=2048, ncol=3 * DN_HEADS, name=f"dnpre_bwd{l}", with_j=True)[0]
        dba, dalog, ddt = _ew_bwd(_f_bg, s["bg_ops"], [(dbg, False)], [0, 1, 2], tm=1024, name=f"bg_bwd{l}")
        dqkv, dcw = _conv_bwd(s["proj"], dconv, W_cv[l], f"conv_bwd{l}")
        daq = _ew_bwd(_f_rope, s["rq_ops"], [(unheads(dqh), True)], [0], tm=1024, ncol=AT_W // 128, name=f"rope_q_bwd{l}")[0]
        dak = _ew_bwd(_f_rope, s["rk_ops"], [(unheads(dkh), True)], [0], tm=1024, ncol=1, name=f"rope_k_bwd{l}")[0]
        dproj = jnp.concatenate([dqkv, dz, daq, dak, unheads(dvh), dba, jnp.zeros((T, IN_PAD - 5504), F32)],
                                axis=-1).astype(BF16)
        big["w_in"] = _mm(s["h1"], dproj, "tn", BF16, f"wg_in{l}")
        dh1 = _mm(dproj, W_in[l], "nt", F32, f"dg_in{l}")
        dx, dln_m, dsc_m, dsh_m = _ew_bwd(_f_normmod, s["nm1"], [(dh1, False), (dx1, False)], [0, 1, 2, 4], tm=256, name=f"normmod1_bwd{l}")
        small["ln_mix"][l], small["ln_ffn"][l] = dln_m, dln_f
        small["mod"][l] = jnp.concatenate([dsh_m, dsc_m, dgt_m, dsh_f, dsc_f, dgt_f], axis=-1)
        small["a_log"][l], small["dt"][l] = dalog[:, 8:16], ddt[:, 8:16]
        small["norm_w"][l], small["sinks"][l], small["conv"][l] = dnw, dsk[:, 0, 0], dcw
        pending = (_chip_sums(shards(_from_aligned(big["w_in"]), True), f"w_in{l}"), _chip_sums(shards(big["w_out"], False), f"w_out{l}"))
    recv_mix[0] = _chip_a2a(pending, "a2a_mix0")
    p_in, p_out = [jnp.concatenate([recv_mix[l][i] for l in range(L)], axis=1) for i in range(2)]
    p_gu, p_dn = [jnp.concatenate([recv_ffn[l][i] for l in range(L)], axis=1) for i in range(2)]

    cat0 = lambda xs_: jnp.concatenate([_rows128(a) for a in xs_], axis=0)
    stk = lambda k: jnp.stack(small[k])
    pack = cat0([stk("ln_mix"), stk("ln_ffn"), stk("mod"), stk("a_log"), stk("dt"), stk("norm_w"), stk("sinks"), d_ln_final, stk("conv")])
    n_small = pack.shape[0] - L * CONV_K * 3 * DN_W // 128
    pack = jnp.pad(pack, ((0, (-pack.shape[0]) % 8), (0, 0)))
    gp = _gather(pack, "ag_small")
    parts_small = gp[:, :n_small]
    dmod_all = gp[:, 2 * L * D // 128:2 * L * D // 128 + L * 6 * D // 128].reshape(N_DEV, L, 6 * D)
    conv_all = gp[:, n_small:n_small + L * CONV_K * 3 * DN_W // 128].reshape(N_DEV, L * CONV_K, 3 * DN_W)
    parts_conv = lax.dynamic_slice_in_dim(conv_all, me * (3 * DN_W // N_DEV), 3 * DN_W // N_DEV, axis=2)

    dmod_mine = lax.dynamic_slice_in_dim(dmod_all, me * (6 * D // N_DEV), 6 * D // N_DEV, axis=2)
    g_ada = jnp.stack([_mm(c_act, jnp.pad(dmod_mine[:, l], ((0, 8), (0, 0))), "tn", F32, f"wg_ada{l}") for l in range(L)])

    def upd(w, m, v, parts, name):
        shp = w.shape
        r = lambda a: a.reshape(-1, shp[-1])
        return [o_.reshape(shp) for o_ in _adamw(r(w), r(m), r(v), parts.reshape(parts.shape[0], -1, shp[-1]), name)]

    res = {}
    res["w_ada"] = upd(w_ada, m_w_ada, v_w_ada, g_ada[None], "adamw_ada")
    res["w_in"] = upd(w_in, m_w_in, v_w_in, p_in, "adamw_in")
    res["dn_conv_w"] = upd(dn_conv_w, m_dn_conv_w, v_dn_conv_w, parts_conv, "adamw_conv")
    res["w_out"] = upd(w_out, m_w_out, v_w_out, p_out, "adamw_out")
    res["w_gate_up"] = upd(w_gate_up, m_w_gate_up, v_w_gate_up, p_gu, "adamw_gu")
    res["w_down"] = upd(w_down, m_w_down, v_w_down, p_dn, "adamw_down")
    names_s = ["ln_mix", "ln_ffn", "b_ada", "dn_a_log", "dn_dt_bias", "dn_norm_w", "attn_sinks", "ln_final"]
    ws = [ln_mix, ln_ffn, b_ada, dn_a_log, dn_dt_bias, dn_norm_w, attn_sinks, ln_final]
    ms = [m_ln_mix, m_ln_ffn, m_b_ada, m_dn_a_log, m_dn_dt_bias, m_dn_norm_w, m_attn_sinks, m_ln_final]
    vs = [v_ln_mix, v_ln_ffn, v_b_ada, v_dn_a_log, v_dn_dt_bias, v_dn_norm_w, v_attn_sinks, v_ln_final]
    padr = lambda a: jnp.pad(a, ((0, (-a.shape[0]) % 8), (0, 0)))
    vpad = jnp.pad(cat0(vs), ((0, (-n_small) % 8), (0, 0)), constant_values=1.0)
    outs_s = _adamw(padr(cat0(ws)), padr(cat0(ms)), vpad, jnp.pad(parts_small, ((0, 0), (0, (-n_small) % 8), (0, 0))), "adamw_small")
    off = 0
    for nme, wv in zip(names_s, ws):
        nrow = -(-wv.size // 128)
        res[nme] = [o_[off:off + nrow].reshape(-1)[:wv.size].reshape(wv.shape) for o_ in outs_s]
        off += nrow

    order = ["ln_mix", "ln_ffn", "w_ada", "b_ada", "w_in", "dn_conv_w", "dn_a_log", "dn_dt_bias", "dn_norm_w", "attn_sinks",
             "w_out", "w_gate_up", "w_down", "ln_final"]
    return (loss, dx[None], *[res[n][0] for n in order], *[res[n][1] for n in order], *[res[n][2] for n in order],
            *[res[n][3] for n in order])
```

```python
import functools

import jax
import jax.numpy as jnp
from jax import lax
from jax.experimental import pallas as pl
from jax.experimental.pallas import tpu as pltpu

F32, BF16 = jnp.float32, jnp.bfloat16
HI = lax.Precision.HIGH
MESH = pl.DeviceIdType.MESH
ANY = pl.BlockSpec(memory_space=pl.ANY)

N_DEV = 8
D = 2048
DN_HEADS, DN_HD = 8, 128
DN_W = 1024
CONV_K = 4
CHUNK = 64
AT_HD, AT_QH, AT_KVH = 64, 16, 2
AT_W = 1024
WINDOW = 128
FFN = 5632
IN_COLS = 5392
IN_PAD = 5632
EPS = 1e-6
NEG = -1e30
LR, B1, B2, AEPS, WD, STEP = 0.001, 0.9, 0.999, 1e-08, 0.01, 10
VMEM_LIMIT = 56 * 1024 * 1024


def _pick(n, cands):
    for c in cands:
        if n % c == 0:
            return c
    return n


def _params(sem):
    return pltpu.CompilerParams(dimension_semantics=sem, vmem_limit_bytes=VMEM_LIMIT)


_DN = {"nn": (((1,), (0,)), ((), ())), "nt": (((1,), (1,)), ((), ())), "tn": (((0,), (0,)), ((), ()))}


def _mm(a, b, mode, out_dtype, name, acc_in=None, gather_srcs=()):
    if mode == "nn":
        (M, K), (_, N) = a.shape, b.shape
    elif mode == "nt":
        (M, K), (N, _) = a.shape, b.shape
    else:
        (K, M), (_, N) = a.shape, b.shape
    tm = _pick(M, (1024, 512, 256, 128, 64, 32, 16))
    tn = _pick(N, (1024, 512, 256, 128))
    tk = K if K <= 2048 else _pick(K, (2816, 2048, 1024, 512, 256, 128))
    nk = K // tk
    dn = _DN[mode]
    n_mm = 2 if acc_in is None else 3
    ng = len(gather_srcs)
    n_in = n_mm + ng
    steps = (M // tm) * (N // tn) * nk

    def body(*refs):
        a_ref, b_ref, o_ref = refs[0], refs[1], refs[n_in]
        scratch = refs[n_in + 1 + ng:]
        k = pl.program_id(2)
        step = (pl.program_id(0) * (N // tn) + pl.program_id(1)) * nk + k
        plans = [_gather_plan(refs[n_mm + g], refs[n_in + 1 + g], *scratch[(nk > 1) + 3 * g:(nk > 1) + 3 * g + 3]) for g in range(ng)]
        for start, _, _ in plans:
            pl.when(step == 0)(start)
        for _, forward, _ in plans:
            pl.when(step == steps // 2)(forward)
        part = lax.dot_general(a_ref[...].astype(BF16), b_ref[...].astype(BF16), dn, preferred_element_type=F32)
        if nk == 1:
            if acc_in is not None:
                part = part + refs[2][...]
            o_ref[...] = part.astype(o_ref.dtype)
        else:
            acc = scratch[0]

            @pl.when(k == 0)
            def _():
                acc[...] = part if acc_in is None else part + refs[2][...]

            @pl.when(k > 0)
            def _():
                acc[...] += part

            @pl.when(k == nk - 1)
            def _():
                o_ref[...] = acc[...].astype(o_ref.dtype)
        for _, _, finish in plans:
            pl.when(step == steps - 1)(finish)

    a_spec = pl.BlockSpec((tk, tm), lambda i, j, k: (k, i)) if mode == "tn" else pl.BlockSpec((tm, tk), lambda i, j, k: (i, k))
    b_spec = pl.BlockSpec((tn, tk), lambda i, j, k: (j, k)) if mode == "nt" else pl.BlockSpec((tk, tn), lambda i, j, k: (k, j))
    o_spec = pl.BlockSpec((tm, tn), lambda i, j, k: (i, j))
    res = pl.pallas_call(
        body, name=name, grid=(M // tm, N // tn, nk),
        in_specs=[a_spec, b_spec] + ([] if acc_in is None else [o_spec]) + [ANY] * ng, out_specs=[o_spec] + [ANY] * ng,
        out_shape=[jax.ShapeDtypeStruct((M, N), out_dtype)] + [jax.ShapeDtypeStruct((N_DEV,) + g.shape, g.dtype) for g in gather_srcs],
        scratch_shapes=([pltpu.VMEM((tm, tn), F32)] if nk > 1 else []) + GATHER_SEMS * ng,
        compiler_params=_params(("arbitrary",) * 3 if ng else ("parallel", "parallel", "arbitrary")),
    )(*((a, b) if acc_in is None else (a, b, acc_in)), *gather_srcs)
    return res if ng else res[0]


def _op(op):
    arr, rt, cb = op[:3]
    start, width = op[3] if len(op) > 3 else (0, arr.shape[1])
    return arr, rt, cb, start, width


def _ew_spec(op, tm, ncol):
    arr, rt, cb, start, width = _op(op)
    bw = width // ncol if cb else width
    first = start // bw
    return pl.BlockSpec((tm if rt else arr.shape[0], bw), lambda j, i: (i if rt else 0, first + (j if cb else 0)))


def _ew_fwd(f, ops, outs, *, tm, ncol=1, name, with_j=False):
    M = next(op[0].shape[0] for op in ops if op[1])
    tm = min(tm, M)
    n_in = len(ops)

    def body(*refs):
        res = f(*[r[...].astype(F32) for r in refs[:n_in]], *([pl.program_id(0)] if with_j else []))
        for o, r in zip(refs[n_in:], res):
            o[...] = r.astype(o.dtype)

    return pl.pallas_call(
        body, name=name, grid=(ncol, M // tm),
        in_specs=[_ew_spec(op, tm, ncol) for op in ops],
        out_specs=[pl.BlockSpec((tm, w // ncol if cb else w), lambda j, i, cb=cb: (i, j if cb else 0)) for w, _, cb in outs],
        out_shape=[jax.ShapeDtypeStruct((M, w), dt) for w, dt, _ in outs],
        compiler_params=_params(("parallel", "parallel")),
    )(*[op[0] for op in ops])


def _ew_bwd(f, ops, cts, diff, *, tm, ncol=1, name, gdt=None, with_j=False):
    M = next(op[0].shape[0] for op in ops if op[1])
    tm = min(tm, M)
    n_in, n_ct = len(ops), len(cts)
    gdt = gdt or [F32] * len(diff)

    def body(*refs):
        j, i = pl.program_id(0), pl.program_id(1)
        vals = [r[...].astype(F32) for r in refs[:n_in]]

        def fd(*dv):
            full = list(vals)
            for idx, v in zip(diff, dv):
                full[idx] = v
            return tuple(f(*full, *([j] if with_j else [])))

        _, vjp = jax.vjp(fd, *[vals[idx] for idx in diff])
        gs = vjp(tuple(r[...].astype(F32) for r in refs[n_in:n_in + n_ct]))
        for idx, g, gref in zip(diff, gs, refs[n_in + n_ct:]):
            _, rt, cb = ops[idx][:3]
            if rt:
                gref[...] = g.astype(gref.dtype)
            else:
                first = (i == 0) if cb else jnp.logical_and(i == 0, j == 0)

                @pl.when(first)
                def _(gref=gref):
                    gref[...] = jnp.zeros_like(gref)

                gref[...] += g

    def g_spec(op):
        arr, rt, cb, start, width = _op(op)
        return _ew_spec((jax.ShapeDtypeStruct((arr.shape[0], width), F32), rt, cb), tm, ncol) if rt else _ew_spec(op, tm, ncol)

    def g_shape(op, dt):
        arr, rt, cb, start, width = _op(op)
        return jax.ShapeDtypeStruct((M, width), dt) if rt else jax.ShapeDtypeStruct(arr.shape, F32)

    return pl.pallas_call(
        body, name=name, grid=(ncol, M // tm),
        in_specs=[_ew_spec(op, tm, ncol) for op in ops]
        + [pl.BlockSpec((tm, a.shape[1] // ncol if cb else a.shape[1]), lambda j, i, cb=cb: (i, j if cb else 0)) for a, cb in cts],
        out_specs=[g_spec(ops[idx]) for idx in diff],
        out_shape=[g_shape(ops[idx], dt) for idx, dt in zip(diff, gdt)],
        compiler_params=_params(("arbitrary", "arbitrary")),
    )(*[op[0] for op in ops], *[a for a, _ in cts])


def _silu(x):
    return x * jax.nn.sigmoid(x)


def _f_normmod(x, ln, sc, bsc, sh, bsh):
    y = x * lax.rsqrt(jnp.mean(x * x, axis=-1, keepdims=True) + EPS) * ln
    return y * (1.0 + (sc + bsc)) + (sh + bsh), x


def _f_resgate(x, br, gt, bgt):
    return (x + (gt + bgt) * br,)


def _f_swiglu(gate, up):
    return (_silu(gate) * up,)


def _f_final(x, tgt, ln):
    y = x * lax.rsqrt(jnp.mean(x * x, axis=-1, keepdims=True) + EPS) * ln
    e = y - tgt
    return (jnp.broadcast_to(0.5 * jnp.mean(e * e, axis=-1, keepdims=True), (x.shape[0], 128)),)


def _f_bg(ba, alog, dt):
    col = lax.broadcasted_iota(jnp.int32, ba.shape, 1)
    z = ba + dt
    sp = jnp.maximum(z, 0.0) + jnp.log(1.0 + jnp.exp(-jnp.abs(z)))
    return (jnp.where(col < 8, jax.nn.sigmoid(ba), jnp.where(col < 16, -jnp.exp(alog) * sp, 0.0)),)


def _l2n(x):
    return x * lax.rsqrt(jnp.sum(x * x, axis=-1, keepdims=True) + EPS)


def _f_dnpre(c, j):
    a = _silu(c)
    return (jnp.where(j < 2 * DN_HEADS, _l2n(a) * jnp.where(j < DN_HEADS, DN_HD ** -0.5, 1.0), a),)


def _f_rope(x, c, s, r):
    return (x * c + _dot(x, r) * s,)


def _f_dnpost(o, z, nw):
    return (o * lax.rsqrt(jnp.mean(o * o, axis=-1, keepdims=True) + EPS) * nw * _silu(z),)


CONV_BW = 512


def _conv_fwd(x, w, Cw, name, gather_src=None):
    T = x.shape[0]
    tm, bw = 512, CONV_BW
    carry = gather_src is not None
    steps = (Cw // bw) * (T // tm)

    def body(*refs):
        x_ref, h_ref, w_ref = refs[:3]
        o_ref = refs[3 + carry]
        i = pl.program_id(1)
        if carry:
            step = pl.program_id(0) * (T // tm) + i
            start, forward, finish = _gather_plan(refs[3], refs[5], *refs[6:])
            pl.when(step == 0)(start)
            pl.when(step == steps // 2)(forward)
        cur, halo, wv = x_ref[...], h_ref[...], w_ref[...]
        halo = jnp.where(i > 0, halo, 0.0)
        row = lax.broadcasted_iota(jnp.int32, (8, bw), 0)
        acc = wv[3:4, :] * cur
        for s in (1, 2, 3):
            r = pltpu.roll(cur, s, 0)
            top = jnp.where(row < s, pltpu.roll(halo, s, 0), r[:8])
            acc += wv[3 - s:4 - s, :] * jnp.concatenate([top, r[8:]], axis=0)
        o_ref[...] = acc
        if carry:
            pl.when(step == steps - 1)(finish)

    res = pl.pallas_call(
        body, name=name, grid=(Cw // bw, T // tm),
        in_specs=[pl.BlockSpec((tm, bw), lambda j, i: (i, j)),
                  pl.BlockSpec((8, bw), lambda j, i: (jnp.maximum(i * (tm // 8) - 1, 0), j)),
                  pl.BlockSpec((4, bw), lambda j, i: (0, j))] + [ANY] * carry,
        out_specs=[pl.BlockSpec((tm, bw), lambda j, i: (i, j))] + [ANY] * carry,
        out_shape=[jax.ShapeDtypeStruct((T, Cw), F32)] + ([jax.ShapeDtypeStruct((N_DEV,) + gather_src.shape, gather_src.dtype)] if carry else []),
        scratch_shapes=GATHER_SEMS if carry else [],
        compiler_params=_params(("arbitrary", "arbitrary") if carry else ("parallel", "parallel")),
    )(x, x, w, *([gather_src] if carry else []))
    return res if carry else res[0]


def _conv_bwd(x, dy, w, name):
    T, Cw = dy.shape
    tm, bw = 512, CONV_BW
    nt = T // tm

    def body(x_ref, h_ref, dy_ref, n_ref, w_ref, dx_ref, dw_ref):
        i = pl.program_id(1)
        cur, dcur, wv = x_ref[...], dy_ref[...], w_ref[...]
        halo = jnp.where(i > 0, h_ref[...], 0.0)
        nxt = jnp.where(i < nt - 1, n_ref[...], 0.0)
        row = lax.broadcasted_iota(jnp.int32, (8, bw), 0)

        @pl.when(i == 0)
        def _():
            dw_ref[...] = jnp.zeros_like(dw_ref)

        dx = wv[3:4, :] * dcur
        dw_ref[3:4, :] += jnp.sum(dcur * cur, axis=0, keepdims=True)
        for s in (1, 2, 3):
            r = pltpu.roll(cur, s, 0)
            top = jnp.where(row < s, pltpu.roll(halo, s, 0), r[:8])
            xs = jnp.concatenate([top, r[8:]], axis=0)
            dw_ref[3 - s:4 - s, :] += jnp.sum(dcur * xs, axis=0, keepdims=True)
            rf = pltpu.roll(dcur, tm - s, 0)
            bot = jnp.where(row >= 8 - s, pltpu.roll(nxt, 8 - s, 0), rf[tm - 8:])
            dx += wv[3 - s:4 - s, :] * jnp.concatenate([rf[:tm - 8], bot], axis=0)
        dx_ref[...] = dx

    return pl.pallas_call(
        body, name=name, grid=(Cw // bw, nt),
        in_specs=[pl.BlockSpec((tm, bw), lambda j, i: (i, j)),
                  pl.BlockSpec((8, bw), lambda j, i: (jnp.maximum(i * (tm // 8) - 1, 0), j)),
                  pl.BlockSpec((tm, bw), lambda j, i: (i, j)),
                  pl.BlockSpec((8, bw), lambda j, i: (jnp.minimum((i + 1) * (tm // 8), T // 8 - 1), j)),
                  pl.BlockSpec((4, bw), lambda j, i: (0, j))],
        out_specs=[pl.BlockSpec((tm, bw), lambda j, i: (i, j)), pl.BlockSpec((4, bw), lambda j, i: (0, j))],
        out_shape=[jax.ShapeDtypeStruct((T, Cw), F32), jax.ShapeDtypeStruct((4, Cw), F32)],
        compiler_params=_params(("arbitrary", "arbitrary")),
    )(x, x, dy, dy, w)


def _dot(a, b):
    return jnp.dot(a, b, precision=HI, preferred_element_type=F32)


def _dot_t(a, b):
    return lax.dot_general(a, b, (((1,), (1,)), ((), ())), precision=HI, preferred_element_type=F32)


def _bdot(a, b):
    return jnp.dot(a.astype(BF16), b.astype(BF16), preferred_element_type=F32)


def _bdot_t(a, b):
    return lax.dot_general(a.astype(BF16), b.astype(BF16), (((1,), (1,)), ((), ())), preferred_element_type=F32)


def _each(f, *lists):
    return [f(*a) for a in zip(*lists)]


@jax.custom_vjp
def _unit_lower_inverses(ps):
    C = ps[0].shape[0]
    dist = jnp.bitwise_xor(lax.broadcasted_iota(jnp.int32, (C, C), 0), lax.broadcasted_iota(jnp.int32, (C, C), 1))
    ns = [jnp.where(dist < 8, p, 0.0) for p in ps]
    tis = [jnp.where(dist == 0, 1.0, 0.0) + n for n in ns]
    for _ in range(2):
        ns = _each(lambda n: _bdot(n, n), ns)
        tis = _each(lambda t, n: t + _bdot(t, n), tis, ns)
    b = 8
    while b < C:
        mids = [jnp.where(jnp.logical_and(dist >= b, dist < 2 * b), p, 0.0) for p in ps]
        halves = _each(_bdot, mids, tis)
        tis = _each(lambda t, h: t + _bdot(t, h), tis, halves)
        b *= 2
    return tis


def _uli_fwd(ps):
    tis = _unit_lower_inverses(ps)
    return tis, tis


def _uli_bwd(tis, cts):
    tt = lambda a, b: lax.dot_general(a, b, (((0,), (0,)), ((), ())), precision=HI, preferred_element_type=F32)
    half = _each(tt, tis, cts)
    return (_each(_dot_t, half, tis),)


_unit_lower_inverses.defvjp(_uli_fwd, _uli_bwd)


@jax.custom_vjp
def _kept_inverses(ps, tis):
    return tis


_kept_inverses.defvjp(lambda ps, tis: (tis, tis), lambda tis, cts: (_uli_bwd(tis, cts)[0], [jnp.zeros_like(t) for t in tis]))


def _chunk_f(qs, ks, vs, bg, Ss, hs, kept=None):
    C = CHUNK
    lane = lax.broadcasted_iota(jnp.int32, (C, 128), 1)
    betas = [jnp.sum(jnp.where(lane == h, bg, 0.0), axis=1, keepdims=True) for h in hs]
    gs = [jnp.sum(jnp.where(lane == h + 8, bg, 0.0), axis=1, keepdims=True) for h in hs]
    ri = lax.broadcasted_iota(jnp.int32, (C, C), 0)
    ci = lax.broadcasted_iota(jnp.int32, (C, C), 1)
    causal, strict, eye = ri >= ci, ri > ci, ri == ci
    g_rows = _each(lambda g: jnp.sum(jnp.where(eye, g, 0.0), axis=0, keepdims=True), gs)
    gc_cols = _each(lambda gr: jnp.sum(jnp.where(causal, gr, 0.0), axis=1, keepdims=True), g_rows)
    gc_rows = _each(lambda g: jnp.sum(jnp.where(ri <= ci, g, 0.0), axis=0, keepdims=True), gs)
    gc_lasts = _each(lambda g: jnp.sum(g, axis=0, keepdims=True), gs)
    decays = _each(lambda c, r: jnp.exp(jnp.where(causal, c - r, NEG)), gc_cols, gc_rows)
    kbs = _each(jnp.multiply, ks, betas)
    vbs = _each(jnp.multiply, vs, betas)
    ps = _each(lambda kb, k, d: -jnp.where(strict, _bdot_t(kb, k) * d, 0.0), kbs, ks, decays)
    tis = _unit_lower_inverses(ps) if kept is None else _kept_inverses(ps, kept)
    egcs = _each(jnp.exp, gc_cols)
    ws = _each(lambda t, kb, e: _bdot(t, kb * e), tis, kbs, egcs)
    us = _each(_bdot, tis, vbs)
    intras = _each(lambda q, k, d: jnp.where(causal, _bdot_t(q, k) * d, 0.0), qs, ks, decays)
    kds = _each(lambda k, gl, gc: k * jnp.exp(gl - gc), ks, gc_lasts, gc_cols)
    vns = _each(lambda u, w, S: u - _bdot(w, S), us, ws, Ss)
    os_ = _each(lambda q, e, S, i, vn: _bdot(q * e, S) + _bdot(i, vn), qs, egcs, Ss, intras, vns)
    Sn = _each(lambda S, gl, kd, vn: S * jnp.exp(gl) + lax.dot_general(
        kd.astype(BF16), vn.astype(BF16), (((0,), (0,)), ((), ())), preferred_element_type=F32), Ss, gc_lasts, kds, vns)
    return (os_, Sn), tis


def _head_cols(ref, part):
    return [ref[:, part * DN_W + e * DN_HD:part * DN_W + (e + 1) * DN_HD] for e in range(DN_HEADS)]


def _chunk_fwd(qkv, bg, name, gather_src=None):
    T = qkv.shape[0]
    N = T // CHUNK
    H = DN_HEADS
    carry = gather_src is not None

    def body(*refs):
        x_ref, bg_ref = refs[:2]
        o_ref, s_ref, t_ref = refs[2 + carry:5 + carry]
        S = refs[5 + 2 * carry]
        n = pl.program_id(0)
        if carry:
            start, forward, finish = _gather_plan(refs[2], refs[6], *refs[8:])
            pl.when(n == 0)(start)
            pl.when(n == N // 2)(forward)

        @pl.when(n == 0)
        def _():
            S[...] = jnp.zeros_like(S)

        s_all = S[...]
        s_ref[0] = s_all
        (os_, nxt), tis = _chunk_f(_head_cols(x_ref, 0), _head_cols(x_ref, 1), _head_cols(x_ref, 2), bg_ref[...],
                                   [s_all[e] for e in range(H)], list(range(H)))
        for e in range(H):
            o_ref[:, e * DN_HD:(e + 1) * DN_HD] = os_[e]
            S[e] = nxt[e]
            t_ref[0, e] = tis[e]
        if carry:
            pl.when(n == N - 1)(finish)

    gshape = [jax.ShapeDtypeStruct((N_DEV,) + gather_src.shape, gather_src.dtype)] if carry else []
    return pl.pallas_call(
        body, name=name, grid=(N,),
        in_specs=[pl.BlockSpec((CHUNK, 3 * DN_W), lambda n: (n, 0)), pl.BlockSpec((CHUNK, 128), lambda n: (n, 0))] + [ANY] * carry,
        out_specs=[pl.BlockSpec((CHUNK, DN_W), lambda n: (n, 0)), pl.BlockSpec((1, H, DN_HD, DN_HD), lambda n: (n, 0, 0, 0)),
                   pl.BlockSpec((1, H, CHUNK, CHUNK), lambda n: (n, 0, 0, 0))] + [ANY] * carry,
        out_shape=[jax.ShapeDtypeStruct((T, DN_W), F32), jax.ShapeDtypeStruct((N, H, DN_HD, DN_HD), F32),
                   jax.ShapeDtypeStruct((N, H, CHUNK, CHUNK), F32)] + gshape,
        scratch_shapes=[pltpu.VMEM((H, DN_HD, DN_HD), F32)] + (GATHER_SEMS if carry else []),
        compiler_params=_params(("arbitrary",)),
    )(qkv, bg, *([gather_src] if carry else []))


def _chunk_bwd(qkv, bg, s_saved, t_saved, do, name, a2a_srcs=()):
    T = qkv.shape[0]
    N = T // CHUNK
    H = DN_HEADS
    na = len(a2a_srcs)

    def body(*refs):
        x_ref, bg_ref, s_ref, t_ref, do_ref = refs[:5]
        dx_ref, dbg_ref = refs[5 + na:7 + na]
        dS = refs[7 + 2 * na]
        n = pl.program_id(0)
        if na:
            start, finish = _a2a_plan(refs[5:5 + na], refs[7 + na:7 + 2 * na], *refs[8 + 2 * na:])
            pl.when(n == 0)(start)

        @pl.when(n == 0)
        def _():
            dS[...] = jnp.zeros_like(dS)

        ds_all = dS[...]
        _, vjp, _ = jax.vjp(functools.partial(_chunk_f, hs=list(range(H)), kept=[t_ref[0, e] for e in range(H)]), _head_cols(x_ref, 0),
                            _head_cols(x_ref, 1), _head_cols(x_ref, 2), bg_ref[...], [s_ref[0, e] for e in range(H)], has_aux=True)
        dq, dk, dv, dbg, nxt = vjp(([do_ref[:, e * DN_HD:(e + 1) * DN_HD] for e in range(H)], [ds_all[e] for e in range(H)]))
        for part, g in enumerate((dq, dk, dv)):
            for e in range(H):
                dx_ref[:, part * DN_W + e * DN_HD:part * DN_W + (e + 1) * DN_HD] = g[e]
        for e in range(H):
            dS[e] = nxt[e]
        dbg_ref[...] = dbg
        if na:
            pl.when(n == N - 1)(finish)

    rev = lambda n: (N - 1 - n, 0)
    return pl.pallas_call(
        body, name=name, grid=(N,),
        in_specs=[pl.BlockSpec((CHUNK, 3 * DN_W), rev), pl.BlockSpec((CHUNK, 128), rev),
                  pl.BlockSpec((1, H, DN_HD, DN_HD), lambda n: (N - 1 - n, 0, 0, 0)),
                  pl.BlockSpec((1, H, CHUNK, CHUNK), lambda n: (N - 1 - n, 0, 0, 0)), pl.BlockSpec((CHUNK, DN_W), rev)] + [ANY] * na,
        out_specs=[pl.BlockSpec((CHUNK, 3 * DN_W), rev), pl.BlockSpec((CHUNK, 128), rev)] + [ANY] * na,
        out_shape=[jax.ShapeDtypeStruct((T, 3 * DN_W), F32), jax.ShapeDtypeStruct((T, 128), F32)]
        + [jax.ShapeDtypeStruct(a.shape, a.dtype) for a in a2a_srcs],
        scratch_shapes=[pltpu.VMEM((H, DN_HD, DN_HD), F32)] + (_a2a_sems(na) if na else []),
        compiler_params=_params(("arbitrary",)),
    )(qkv, bg, s_saved, t_saved, do, *a2a_srcs)


GRP = AT_QH // AT_KVH


def _attn_f(qrs, kpr, kcr, vp, vc, sinks, has_prev):
    r = lax.broadcasted_iota(jnp.int32, (WINDOW, WINDOW), 0)
    j = lax.broadcasted_iota(jnp.int32, (WINDOW, WINDOW), 1)
    in_c, in_p = j <= r, jnp.logical_and(j > r, has_prev)
    lane = lax.broadcasted_iota(jnp.int32, (1, 128), 1)
    scs = [jnp.where(in_c, _bdot_t(qr, kcr) * (AT_HD ** -0.5), NEG) for qr in qrs]
    sps = [jnp.where(in_p, _bdot_t(qr, kpr) * (AT_HD ** -0.5), NEG) for qr in qrs]
    snk = [jnp.sum(jnp.where(lane == 0, s, 0.0), axis=1, keepdims=True) for s in sinks]
    ms = _each(lambda sc, sp, s: jnp.maximum(jnp.maximum(jnp.max(sc, axis=1, keepdims=True), jnp.max(sp, axis=1, keepdims=True)), s),
               scs, sps, snk)
    pcs = _each(lambda sc, m: jnp.exp(sc - m), scs, ms)
    pps = _each(lambda sp, m: jnp.exp(sp - m), sps, ms)
    dens = _each(lambda pc, pp, s, m: jnp.sum(pc, axis=1, keepdims=True) + jnp.sum(pp, axis=1, keepdims=True) + jnp.exp(s - m),
                 pcs, pps, snk, ms)
    return tuple(_each(lambda pc, pp, den: (_bdot(pc, vc) + _bdot(pp, vp)) / den, pcs, pps, dens))


def _attn_specs(nb):
    qs = pl.BlockSpec((GRP, WINDOW, AT_HD), lambda g, n: (g, n, 0))
    kc = pl.BlockSpec((1, WINDOW, AT_HD), lambda g, n: (g, n, 0))
    kp = pl.BlockSpec((1, WINDOW, AT_HD), lambda g, n: (g, jnp.maximum(n - 1, 0), 0))
    sk = pl.BlockSpec((GRP, 1, 128), lambda g, n: (g, 0, 0))
    return qs, kc, kp, sk


def _attn_fwd(q, k, v, sinks, name, gather_src=None):
    T = q.shape[1]
    nb = T // WINDOW
    qs, kc, kp, sk = _attn_specs(nb)
    carry = gather_src is not None

    def body(*refs):
        q_ref, kp_ref, kc_ref, vp_ref, vc_ref, sk_ref = refs[:6]
        o_ref = refs[6 + carry]
        n = pl.program_id(1)
        if carry:
            step = pl.program_id(0) * nb + n
            start, forward, finish = _gather_plan(refs[6], refs[8], *refs[9:])
            pl.when(step == 0)(start)
            pl.when(step == nb)(forward)
        heads = range(GRP)
        outs = _attn_f(tuple(q_ref[e] for e in heads), kp_ref[0], kc_ref[0], vp_ref[0], vc_ref[0], tuple(sk_ref[e] for e in heads), n > 0)
        for e in range(GRP):
            o_ref[e] = outs[e]
        if carry:
            pl.when(step == AT_KVH * nb - 1)(finish)

    gshape = [jax.ShapeDtypeStruct((N_DEV,) + gather_src.shape, gather_src.dtype)] if carry else []
    return pl.pallas_call(
        body, name=name, grid=(AT_KVH, nb),
        in_specs=[qs, kp, kc, kp, kc, sk] + [ANY] * carry, out_specs=[qs] + [ANY] * carry,
        out_shape=[jax.ShapeDtypeStruct(q.shape, F32)] + gshape,
        scratch_shapes=GATHER_SEMS if carry else [],
        compiler_params=_params(("arbitrary", "arbitrary") if carry else ("parallel", "parallel")),
    )(q, k, k, v, v, sinks, *([gather_src] if carry else []))


def _attn_bwd(q, k, v, sinks, do, name, a2a_srcs=()):
    T = q.shape[1]
    nb = T // WINDOW
    qs, kc, kp, sk = _attn_specs(nb)
    na = len(a2a_srcs)

    def body(*refs):
        q_ref, kp_ref, kc_ref, vp_ref, vc_ref, sk_ref, do_ref = refs[:7]
        dq_ref, dkp_ref, dkc_ref, dvp_ref, dvc_ref, dsk_ref = refs[7 + na:13 + na]
        n = pl.program_id(1)
        if na:
            step = pl.program_id(0) * nb + n
            start, finish = _a2a_plan(refs[7:7 + na], refs[13 + na:13 + 2 * na], *refs[13 + 2 * na:])
            pl.when(step == 0)(start)
        f = functools.partial(_attn_f, has_prev=n > 0)
        heads = range(GRP)
        _, vjp = jax.vjp(f, tuple(q_ref[e] for e in heads), kp_ref[0], kc_ref[0], vp_ref[0], vc_ref[0], tuple(sk_ref[e] for e in heads))
        dq, dkp, dkc, dvp, dvc, dsk = vjp(tuple(do_ref[e] for e in heads))
        dkp_ref[0], dkc_ref[0], dvp_ref[0], dvc_ref[0] = dkp, dkc, dvp, dvc

        @pl.when(n == 0)
        def _():
            dsk_ref[...] = jnp.zeros_like(dsk_ref)

        for e in heads:
            dq_ref[e] = dq[e]
            dsk_ref[e] += dsk[e]
        if na:
            pl.when(step == AT_KVH * nb - 1)(finish)

    return pl.pallas_call(
        body, name=name, grid=(AT_KVH, nb),
        in_specs=[qs, kp, kc, kp, kc, sk, qs] + [ANY] * na, out_specs=[qs, kc, kc, kc, kc, sk] + [ANY] * na,
        out_shape=[jax.ShapeDtypeStruct(q.shape, F32)] + [jax.ShapeDtypeStruct(k.shape, F32)] * 4 + [jax.ShapeDtypeStruct(sinks.shape, F32)]
        + [jax.ShapeDtypeStruct(a.shape, a.dtype) for a in a2a_srcs],
        scratch_shapes=_a2a_sems(na) if na else [],
        compiler_params=_params(("arbitrary", "arbitrary")),
    )(q, k, k, v, v, sinks, do, *a2a_srcs)


def _kv_combine(dc, dp, name):
    T = dc.shape[1]
    R = 8 * WINDOW
    ns = T // R

    def body(c_ref, p_ref, q_ref, o_ref):
        n = pl.program_id(1)
        tail = jnp.where(n < ns - 1, q_ref[0], 0.0)
        o_ref[0] = c_ref[0] + jnp.concatenate([p_ref[0, WINDOW:, :], tail], axis=0)

    return pl.pallas_call(
        body, name=name, grid=(AT_KVH, ns),
        in_specs=[pl.BlockSpec((1, R, AT_HD), lambda g, n: (g, n, 0)), pl.BlockSpec((1, R, AT_HD), lambda g, n: (g, n, 0)),
                  pl.BlockSpec((1, WINDOW, AT_HD), lambda g, n: (g, jnp.minimum((n + 1) * 8, T // WINDOW - 1), 0))],
        out_specs=pl.BlockSpec((1, R, AT_HD), lambda g, n: (g, n, 0)),
        out_shape=jax.ShapeDtypeStruct((AT_KVH, T, AT_HD), F32),
        compiler_params=_params(("parallel", "parallel")),
    )(dc, dp, dp)


def _place():
    x, y, c = lax.axis_index("x"), lax.axis_index("y"), lax.axis_index("c")
    return x, y, c, 4 * x + 2 * y + c


def _gather_plan(s_ref, o_ref, send_sems, recv_sems, lsem):
    x, y, c, _ = _place()
    me, sib = (x, y, c), (x, y, 1 - c)
    chips = [(1 - x, y), (x, 1 - y), (1 - x, 1 - y)]

    def copy(k, block, to, src_ref=None):
        slab = o_ref.at[4 * block[0] + 2 * block[1] + block[2]]
        return pltpu.make_async_remote_copy(src_ref=slab if src_ref is None else src_ref, dst_ref=slab,
                                            send_sem=send_sems.at[k], recv_sem=recv_sems.at[k], device_id=to, device_id_type=MESH)

    mine = pltpu.make_async_copy(s_ref, o_ref.at[4 * x + 2 * y + c], lsem)
    first = [copy(0, me, sib, s_ref)] + [copy(1 + j, me, (*chip, c), s_ref) for j, chip in enumerate(chips)]
    passed = [copy(4 + j, (*chip, c), sib) for j, chip in enumerate(chips)]

    def start():
        mine.start()
        for cp in first:
            cp.start()

    def forward():
        for j, chip in enumerate(chips):
            copy(1 + j, (*chip, c), me).wait_recv()
            passed[j].start()

    def finish():
        copy(0, sib, me).wait_recv()
        for j, chip in enumerate(chips):
            copy(4 + j, (*chip, 1 - c), me).wait_recv()
        for cp in first + passed:
            cp.wait_send()
        mine.wait()

    return start, forward, finish


GATHER_SEMS = [pltpu.SemaphoreType.DMA((N_DEV - 1,)), pltpu.SemaphoreType.DMA((N_DEV - 1,)), pltpu.SemaphoreType.DMA]


def _gather(src, name):
    def body(s_ref, o_ref, send_sems, recv_sems, lsem):
        for phase in _gather_plan(s_ref, o_ref, send_sems, recv_sems, lsem):
            phase()

    return pl.pallas_call(
        body, name=name, in_specs=[ANY], out_specs=ANY, out_shape=jax.ShapeDtypeStruct((N_DEV,) + src.shape, src.dtype),
        scratch_shapes=GATHER_SEMS,
    )(src)


def _pair_swap(src, name):
    def body(s_ref, o_ref, send_sem, recv_sem):
        x, y, c, _ = _place()
        cp = pltpu.make_async_remote_copy(src_ref=s_ref.at[1 - c], dst_ref=o_ref, send_sem=send_sem, recv_sem=recv_sem,
                                          device_id=(x, y, 1 - c), device_id_type=MESH)
        cp.start()
        cp.wait()

    return pl.pallas_call(
        body, name=name, in_specs=[ANY], out_specs=ANY, out_shape=jax.ShapeDtypeStruct(src.shape[1:], src.dtype),
        scratch_shapes=[pltpu.SemaphoreType.DMA, pltpu.SemaphoreType.DMA],
    )(src)


def _a2a_plan(s_refs, o_refs, send_sems, recv_sems, lsems):
    x, y, c, _ = _place()
    chip = 2 * x + y
    local, remote = [], []
    for a, (s_ref, o_ref) in enumerate(zip(s_refs, o_refs)):
        local.append(pltpu.make_async_copy(s_ref.at[chip], o_ref.at[chip], lsems.at[a]))
        for k in (1, 2, 3):
            px, py = x ^ (k >> 1), y ^ (k & 1)
            remote.append(pltpu.make_async_remote_copy(
                src_ref=s_ref.at[2 * px + py], dst_ref=o_ref.at[chip], send_sem=send_sems.at[3 * a + k - 1],
                recv_sem=recv_sems.at[3 * a + k - 1], device_id=(px, py, c), device_id_type=MESH))

    def start():
        for cp in local + remote:
            cp.start()

    def finish():
        for cp in remote + local:
            cp.wait()

    return start, finish


def _a2a_sems(n):
    return [pltpu.SemaphoreType.DMA((3 * n,)), pltpu.SemaphoreType.DMA((3 * n,)), pltpu.SemaphoreType.DMA((n,))]


def _chip_a2a(srcs, name):
    n = len(srcs)

    def body(*refs):
        for phase in _a2a_plan(refs[:n], refs[n:2 * n], *refs[2 * n:]):
            phase()

    return pl.pallas_call(
        body, name=name, in_specs=[ANY] * n, out_specs=[ANY] * n, out_shape=[jax.ShapeDtypeStruct(a.shape, a.dtype) for a in srcs],
        scratch_shapes=_a2a_sems(n),
    )(*srcs)


def _add(a, b, name):
    R, C = a.shape
    tr = _pick(R, (512, 256, 128, 64, 32, 16))

    def body(a_ref, b_ref, o_ref):
        o_ref[...] = (a_ref[...].astype(F32) + b_ref[...].astype(F32)).astype(o_ref.dtype)

    s2 = pl.BlockSpec((tr, C), lambda i: (i, 0))
    return pl.pallas_call(body, name=name, grid=(R // tr,), in_specs=[s2, s2], out_specs=s2,
                          out_shape=jax.ShapeDtypeStruct((R, C), a.dtype), compiler_params=_params(("parallel",)))(a, b)


def _chip_sums(slabs, name):
    _, R, C = slabs.shape
    by_core = jnp.transpose(slabs.reshape(4, 2, R, C), (1, 0, 2, 3))
    theirs = _pair_swap(by_core, "swap_" + name)
    mine = lax.dynamic_index_in_dim(by_core, lax.axis_index("c"), axis=0, keepdims=False)
    return _add(mine.reshape(4 * R, C), theirs.reshape(4 * R, C), "add_" + name).reshape(4, R, C)


def _adamw(w, m, v, parts, name):
    R, C = w.shape
    P = parts.shape[0]
    tr = _pick(R, (256, 128, 64, 32, 16, 8))
    c1, c2 = 1.0 - B1 ** STEP, 1.0 - B2 ** STEP

    def body(w_ref, m_ref, v_ref, p_ref, g_ref, d_ref, nm_ref, nv_ref):
        g = p_ref[0].astype(F32)
        for i in range(1, P):
            g = g + p_ref[i].astype(F32)
        wv = w_ref[...]
        nm = B1 * m_ref[...] + (1.0 - B1) * g
        nv = B2 * v_ref[...] + (1.0 - B2) * (g * g)
        g_ref[...] = g
        nm_ref[...] = nm
        nv_ref[...] = nv
        d_ref[...] = -LR * ((nm / c1) / (jnp.sqrt(nv / c2) + AEPS) + WD * wv)

    s2 = pl.BlockSpec((tr, C), lambda i: (i, 0))
    return pl.pallas_call(
        body, name=name, grid=(R // tr,),
        in_specs=[s2, s2, s2, pl.BlockSpec((P, tr, C), lambda i: (0, i, 0))], out_specs=[s2] * 4,
        out_shape=[jax.ShapeDtypeStruct((R, C), F32)] * 4,
        compiler_params=_params(("parallel",)),
    )(w, m, v, parts)


def _colsum(a, name):
    def body(a_ref, o_ref):
        o_ref[...] = jnp.broadcast_to(jnp.sum(a_ref[...], axis=0, keepdims=True), o_ref.shape)

    return pl.pallas_call(body, name=name, out_shape=jax.ShapeDtypeStruct((8, 128), F32))(a)


def _rows128(a):
    f = a.reshape(-1)
    return jnp.pad(f, (0, (-f.shape[0]) % 128)).reshape(-1, 128)


def _to_aligned(w):
    return jnp.concatenate([w[..., 0:4096], w[..., 4112:5392], w[..., 4096:4112],
                            jnp.zeros(w.shape[:-1] + (IN_PAD - IN_COLS,), w.dtype)], axis=-1)


def _from_aligned(w):
    return jnp.concatenate([w[..., 0:4096], w[..., 5376:5392], w[..., 4096:5376]], axis=-1)


def kernel(x, c, ln_mix, ln_ffn, w_ada, b_ada, w_in, dn_conv_w, dn_a_log, dn_dt_bias, dn_norm_w, attn_sinks, w_out, w_gate_up, w_down, ln_final, loss_target, m_ln_mix, m_ln_ffn, m_w_ada, m_b_ada, m_w_in, m_dn_conv_w, m_dn_a_log, m_dn_dt_bias, m_dn_norm_w, m_attn_sinks, m_w_out, m_w_gate_up, m_w_down, m_ln_final, v_ln_mix, v_ln_ffn, v_w_ada, v_b_ada, v_w_in, v_dn_conv_w, v_dn_a_log, v_dn_dt_bias, v_dn_norm_w, v_attn_sinks, v_w_out, v_w_gate_up, v_w_down, v_ln_final):
    T = x.shape[1]
    L = ln_mix.shape[0]
    me = 4 * lax.axis_index("x") + 2 * lax.axis_index("y") + lax.axis_index("c")
    xs = x[0]
    tgt = loss_target[0]

    w_in_layout = lambda g: _to_aligned(jnp.transpose(g, (1, 0, 2)).reshape(D, IN_COLS))
    W_in = [w_in_layout(_gather(w_in[0].astype(BF16), "ag_w_in0"))] + [None] * (L - 1)
    W_out = None
    g_cv = _gather(dn_conv_w.reshape(L * CONV_K, -1), "ag_conv").reshape(N_DEV, L, CONV_K, -1)
    c_all = _gather(jnp.pad(c, ((0, 7), (0, 0))), "ag_c")[:, 0, :]
    W_gate, W_up, W_dn = [None] * L, [None] * L, [None] * L
    W_cv = [jnp.transpose(g_cv[:, l], (1, 0, 2)).reshape(CONV_K, 3 * DN_W) for l in range(L)]

    c_act = _ew_fwd(lambda v: (_silu(v),), [(jnp.pad(c_all, ((0, 8), (0, 0))), True, False)], [(D, F32, False)], tm=16, name="c_act")[0]
    mods = []
    for l in range(L):
        ms = _mm(c_act, w_ada[l], "nn", F32, f"mod_mm{l}")
        ga = _gather(ms, f"ag_mod{l}")
        mods.append(lax.dynamic_index_in_dim(ga, me, axis=1, keepdims=False).reshape(1, 6 * D))
    row = lambda a: a.reshape(1, -1)
    seg = lambda a, i: a[:, i * D:(i + 1) * D]

    half = AT_HD // 2
    inv_freq = 10000.0 ** (-jnp.arange(half, dtype=F32) * 2.0 / AT_HD)
    ang = jnp.arange(T, dtype=jnp.int32).astype(F32)[:, None] * inv_freq[None, :]
    cos = jnp.concatenate([jnp.cos(ang)] * 4, axis=-1)
    sin = jnp.concatenate([jnp.sin(ang)] * 4, axis=-1)
    ii = jnp.arange(2 * AT_HD)
    same = (ii[:, None] // AT_HD) == (ii[None, :] // AT_HD)
    rot = (jnp.where(same & (ii[:, None] == ii[None, :] + half), -1.0, 0.0)
           + jnp.where(same & (ii[:, None] + half == ii[None, :]), 1.0, 0.0)).astype(F32)
    rope_ops = lambda arr, first, width: [(arr, True, True, (first, width)), (cos, True, False), (sin, True, False), (rot, False, False)]
    heads = lambda a, nh: jnp.transpose(a.reshape(T, nh, AT_HD), (1, 0, 2))
    unheads = lambda a: jnp.transpose(a, (1, 0, 2)).reshape(T, -1)
    pad16 = lambda a: jnp.pad(row(a), ((0, 0), (8, 128 - 16)))

    saved = []
    xc = xs
    for l in range(L):
        mod, bmod = mods[l], row(b_ada[l])
        s = {"x": xc}
        nm_ops = lambda xx, ln, a, b: [(xx, True, False), (row(ln), False, False), (seg(mod, a), False, False),
                                       (seg(bmod, a), False, False), (seg(mod, b), False, False), (seg(bmod, b), False, False)]
        h1 = _ew_fwd(lambda *a: _f_normmod(*a)[:1], nm_ops(xc, ln_mix[l], 1, 0), [(D, BF16, False)], tm=256, name=f"normmod1_{l}")[0]
        if l + 1 < L:
            proj, g_in = _mm(h1, W_in[l], "nn", F32, f"mm_in{l}", gather_srcs=[w_in[l + 1].astype(BF16)])
            W_in[l + 1] = w_in_layout(g_in)
        else:
            proj = _mm(h1, W_in[l], "nn", F32, f"mm_in{l}")
        rq_ops, rk_ops = rope_ops(proj, 4096, AT_W), rope_ops(proj, 5120, 128)
        aq = _ew_fwd(_f_rope, rq_ops, [(AT_W, F32, True)], tm=1024, ncol=AT_W // 128, name=f"rope_q{l}")[0]
        ak = _ew_fwd(_f_rope, rk_ops, [(128, F32, True)], tm=1024, ncol=1, name=f"rope_k{l}")[0]
        av = proj[:, 5248:5376]
        if l == 0:
            conv, g_out = _conv_fwd(proj, W_cv[l], 3 * DN_W, f"conv{l}", gather_src=w_out.astype(BF16).reshape(-1, D))
            W_out = [g_out.reshape(N_DEV, L, -1, D)[:, i].reshape(D, D) for i in range(L)]
        else:
            conv = _conv_fwd(proj, W_cv[l], 3 * DN_W, f"conv{l}")
        pre_ops = [(conv, True, True)]
        qkvn = _ew_fwd(_f_dnpre, pre_ops, [(3 * DN_W, F32, True)], tm=2048, ncol=3 * DN_HEADS, name=f"dnpre{l}", with_j=True)[0]
        bg_ops = [(proj, True, False, (5376, 128)), (pad16(dn_a_log[l]), False, False), (pad16(dn_dt_bias[l]), False, False)]
        bg = _ew_fwd(_f_bg, bg_ops, [(128, F32, False)], tm=1024, name=f"bg{l}")[0]
        o, s_saved, t_saved, g_gu = _chunk_fwd(qkvn, bg, f"chunk{l}", gather_src=w_gate_up[l].astype(BF16))
        W_gate[l] = jnp.transpose(g_gu[:4], (1, 0, 2)).reshape(D, FFN)
        W_up[l] = jnp.transpose(g_gu[4:], (1, 0, 2)).reshape(D, FFN)
        post_ops = [(o, True, True), (proj, True, True, (3072, DN_W)), (row(dn_norm_w[l]), False, False)]
        dn_out = _ew_fwd(_f_dnpost, post_ops, [(DN_W, BF16, True)], tm=2048, ncol=DN_HEADS, name=f"dnpost{l}")[0]
        qh, kh, vh = heads(aq, AT_QH), heads(ak, AT_KVH), heads(av, AT_KVH)
        sk = jnp.broadcast_to(attn_sinks[l][:, None, None], (AT_QH, 1, 128))
        at_o, g_dn = _attn_fwd(qh, kh, vh, sk, f"attn{l}", gather_src=w_down[l].astype(BF16))
        W_dn[l] = g_dn.reshape(FFN, D)
        at_out = unheads(at_o).astype(BF16)
        mix = _mm(at_out, W_out[l][DN_W:], "nn", F32, f"mm_out_at{l}", acc_in=_mm(dn_out, W_out[l][:DN_W], "nn", F32, f"mm_out_dn{l}"))
        rg_ops = lambda xx, br, a: [(xx, True, False), (br, True, False), (seg(mod, a), False, False), (seg(bmod, a), False, False)]
        x1 = _ew_fwd(_f_resgate, rg_ops(xc, mix, 2), [(D, F32, False)], tm=256, name=f"resgate1_{l}")[0]
        h2 = _ew_fwd(lambda *a: _f_normmod(*a)[:1], nm_ops(x1, ln_ffn[l], 4, 3), [(D, BF16, False)], tm=256, name=f"normmod2_{l}")[0]
        gate = _mm(h2, W_gate[l], "nn", BF16, f"mm_gate{l}")
        up = _mm(h2, W_up[l], "nn", BF16, f"mm_up{l}")
        sw_ops = [(gate, True, True), (up, True, True)]
        act = _ew_fwd(_f_swiglu, sw_ops, [(FFN, BF16, True)], tm=2048, ncol=11, name=f"swiglu{l}")[0]
        down = _mm(act, W_dn[l], "nn", F32, f"mm_down{l}")
        x2 = _ew_fwd(_f_resgate, rg_ops(x1, down, 5), [(D, F32, False)], tm=256, name=f"resgate2_{l}")[0]
        s.update(h1=h1, proj=proj, qkvn=qkvn, bg=bg, s_saved=s_saved, t_saved=t_saved, qh=qh, kh=kh, vh=vh, sk=sk, dn_out=dn_out, at_out=at_out,
                 h2=h2, act=act, bg_ops=bg_ops, post_ops=post_ops, pre_ops=pre_ops, sw_ops=sw_ops, rq_ops=rq_ops, rk_ops=rk_ops,
                 nm1=nm_ops(xc, ln_mix[l], 1, 0), nm2=nm_ops(x1, ln_ffn[l], 4, 3), rg1=rg_ops(xc, mix, 2), rg2=rg_ops(x1, down, 5))
        saved.append(s)
        xc = x2

    fin_ops = [(xc, True, False), (tgt, True, False), (row(ln_final), False, False)]
    lrow = _ew_fwd(_f_final, fin_ops, [(128, F32, False)], tm=256, name="loss_rows")[0]
    loss = lax.psum(_colsum(lrow, "loss_sum")[0, 0], ("x", "y", "c"))
    dx, d_ln_final = _ew_bwd(_f_final, fin_ops, [(jnp.ones((T, 128), F32) / 128.0, False)], [0, 2], tm=256, name="loss_bwd")

    small = {k: [None] * L for k in ("ln_mix", "ln_ffn", "mod", "a_log", "dt", "norm_w", "sinks", "conv")}
    big = {}
    recv_ffn, recv_mix = [None] * L, [None] * L
    pending = ()

    def shards(g, cols, n=N_DEV):
        return jnp.transpose(g.reshape(g.shape[0], n, -1), (1, 0, 2)) if cols else g.reshape(n, -1, g.shape[1])

    for l in reversed(range(L)):
        s = saved[l]
        ddown, dgt_f = _ew_bwd(_f_resgate, s["rg2"], [(dx, False)], [1, 2], tm=256, name=f"resgate2_bwd{l}", gdt=[BF16, F32])
        big["w_dn"] = _mm(s["act"], ddown, "tn", BF16, f"wg_down{l}")
        dact = _mm(ddown, W_dn[l], "nt", F32, f"dg_down{l}")
        dgate, dup = _ew_bwd(_f_swiglu, s["sw_ops"], [(dact, True)], [0, 1], tm=2048, ncol=11, name=f"swiglu_bwd{l}", gdt=[BF16, BF16])
        big["w_gate"] = _mm(s["h2"], dgate, "tn", BF16, f"wg_gate{l}")
        big["w_up"] = _mm(s["h2"], dup, "tn", BF16, f"wg_up{l}")
        dh2 = _mm(dup, W_up[l], "nt", F32, f"dg_up{l}", acc_in=_mm(dgate, W_gate[l], "nt", F32, f"dg_gate{l}"))
        dx1, dln_f, dsc_f, dsh_f = _ew_bwd(_f_normmod, s["nm2"], [(dh2, False), (dx, False)], [0, 1, 2, 4], tm=256, name=f"normmod2_bwd{l}")
        dmix, dgt_m = _ew_bwd(_f_resgate, s["rg1"], [(dx1, False)], [1, 2], tm=256, name=f"resgate1_bwd{l}", gdt=[BF16, F32])
        big["w_out"] = jnp.concatenate([_mm(s["dn_out"], dmix, "tn", BF16, f"wg_out_dn{l}"),
                                        _mm(s["at_out"], dmix, "tn", BF16, f"wg_out_at{l}")], axis=0)
        d_dn = _mm(dmix, W_out[l][:DN_W], "nt", F32, f"dg_out_dn{l}")
        d_at = _mm(dmix, W_out[l][DN_W:], "nt", F32, f"dg_out_at{l}")
        ffn_sums = (_chip_sums(jnp.concatenate([shards(big["w_gate"], True, 4), shards(big["w_up"], True, 4)], axis=0), f"w_gu{l}"),
                    _chip_sums(shards(big["w_dn"], False), f"w_down{l}"))
        dqh, dkp, dkc, dvp, dvc, dsk, *recv_ffn[l] = _attn_bwd(s["qh"], s["kh"], s["vh"], s["sk"], heads(d_at, AT_QH), f"attn_bwd{l}",
                                                               a2a_srcs=ffn_sums[1:])
        dkh = _kv_combine(dkc, dkp, f"dk_comb{l}")
        dvh = _kv_combine(dvc, dvp, f"dv_comb{l}")
        do, dz, dnw = _ew_bwd(_f_dnpost, s["post_ops"], [(d_dn, True)], [0, 1, 2], tm=2048, ncol=DN_HEADS, name=f"dnpost_bwd{l}")
        dqkvn, dbg, got_gu, *got = _chunk_bwd(s["qkvn"], s["bg"], s["s_saved"], s["t_saved"], do, f"chunk_bwd{l}",
                                              a2a_srcs=ffn_sums[:1] + pending)
        recv_ffn[l] = [got_gu] + recv_ffn[l]
        if pending:
            recv_mix[l + 1] = got
        dconv = _ew_bwd(_f_dnpre, s["pre_ops"], [(dqkvn, True)], [0], tm=2048, ncol=3 * DN_HEADS, name=f"dnpre_bwd{l}", with_j=True)[0]
        dba, dalog, ddt = _ew_bwd(_f_bg, s["bg_ops"], [(dbg, False)], [0, 1, 2], tm=1024, name=f"bg_bwd{l}")
        dqkv, dcw = _conv_bwd(s["proj"], dconv, W_cv[l], f"conv_bwd{l}")
        daq = _ew_bwd(_f_rope, s["rq_ops"], [(unheads(dqh), True)], [0], tm=1024, ncol=AT_W // 128, name=f"rope_q_bwd{l}")[0]
        dak = _ew_bwd(_f_rope, s["rk_ops"], [(unheads(dkh), True)], [0], tm=1024, ncol=1, name=f"rope_k_bwd{l}")[0]
        dproj = jnp.concatenate([dqkv, dz, daq, dak, unheads(dvh), dba, jnp.zeros((T, IN_PAD - 5504), F32)],
                                axis=-1).astype(BF16)
        big["w_in"] = _mm(s["h1"], dproj, "tn", BF16, f"wg_in{l}")
        dh1 = _mm(dproj, W_in[l], "nt", F32, f"dg_in{l}")
        dx, dln_m, dsc_m, dsh_m = _ew_bwd(_f_normmod, s["nm1"], [(dh1, False), (dx1, False)], [0, 1, 2, 4], tm=256, name=f"normmod1_bwd{l}")
        small["ln_mix"][l], small["ln_ffn"][l] = dln_m, dln_f
        small["mod"][l] = jnp.concatenate([dsh_m, dsc_m, dgt_m, dsh_f, dsc_f, dgt_f], axis=-1)
        small["a_log"][l], small["dt"][l] = dalog[:, 8:16], ddt[:, 8:16]
        small["norm_w"][l], small["sinks"][l], small["conv"][l] = dnw, dsk[:, 0, 0], dcw
        pending = (_chip_sums(shards(_from_aligned(big["w_in"]), True), f"w_in{l}"), _chip_sums(shards(big["w_out"], False), f"w_out{l}"))
    recv_mix[0] = _chip_a2a(pending, "a2a_mix0")
    p_in, p_out = [jnp.concatenate([recv_mix[l][i] for l in range(L)], axis=1) for i in range(2)]
    p_gu, p_dn = [jnp.concatenate([recv_ffn[l][i] for l in range(L)], axis=1) for i in range(2)]

    cat0 = lambda xs_: jnp.concatenate([_rows128(a) for a in xs_], axis=0)
    stk = lambda k: jnp.stack(small[k])
    pack = cat0([stk("ln_mix"), stk("ln_ffn"), stk("mod"), stk("a_log"), stk("dt"), stk("norm_w"), stk("sinks"), d_ln_final, stk("conv")])
    n_small = pack.shape[0] - L * CONV_K * 3 * DN_W // 128
    pack = jnp.pad(pack, ((0, (-pack.shape[0]) % 8), (0, 0)))
    gp = _gather(pack, "ag_small")
    parts_small = gp[:, :n_small]
    dmod_all = gp[:, 2 * L * D // 128:2 * L * D // 128 + L * 6 * D // 128].reshape(N_DEV, L, 6 * D)
    conv_all = gp[:, n_small:n_small + L * CONV_K * 3 * DN_W // 128].reshape(N_DEV, L * CONV_K, 3 * DN_W)
    parts_conv = lax.dynamic_slice_in_dim(conv_all, me * (3 * DN_W // N_DEV), 3 * DN_W // N_DEV, axis=2)

    dmod_mine = lax.dynamic_slice_in_dim(dmod_all, me * (6 * D // N_DEV), 6 * D // N_DEV, axis=2)
    g_ada = jnp.stack([_mm(c_act, jnp.pad(dmod_mine[:, l], ((0, 8), (0, 0))), "tn", F32, f"wg_ada{l}") for l in range(L)])

    def upd(w, m, v, parts, name):
        shp = w.shape
        r = lambda a: a.reshape(-1, shp[-1])
        return [o_.reshape(shp) for o_ in _adamw(r(w), r(m), r(v), parts.reshape(parts.shape[0], -1, shp[-1]), name)]

    res = {}
    res["w_ada"] = upd(w_ada, m_w_ada, v_w_ada, g_ada[None], "adamw_ada")
    res["w_in"] = upd(w_in, m_w_in, v_w_in, p_in, "adamw_in")
    res["dn_conv_w"] = upd(dn_conv_w, m_dn_conv_w, v_dn_conv_w, parts_conv, "adamw_conv")
    res["w_out"] = upd(w_out, m_w_out, v_w_out, p_out, "adamw_out")
    res["w_gate_up"] = upd(w_gate_up, m_w_gate_up, v_w_gate_up, p_gu, "adamw_gu")
    res["w_down"] = upd(w_down, m_w_down, v_w_down, p_dn, "adamw_down")
    names_s = ["ln_mix", "ln_ffn", "b_ada", "dn_a_log", "dn_dt_bias", "dn_norm_w", "attn_sinks", "ln_final"]
    ws = [ln_mix, ln_ffn, b_ada, dn_a_log, dn_dt_bias, dn_norm_w, attn_sinks, ln_final]
    ms = [m_ln_mix, m_ln_ffn, m_b_ada, m_dn_a_log, m_dn_dt_bias, m_dn_norm_w, m_attn_sinks, m_ln_final]
    vs = [v_ln_mix, v_ln_ffn, v_b_ada, v_dn_a_log, v_dn_dt_bias, v_dn_norm_w, v_attn_sinks, v_ln_final]
    padr = lambda a: jnp.pad(a, ((0, (-a.shape[0]) % 8), (0, 0)))
    vpad = jnp.pad(cat0(vs), ((0, (-n_small) % 8), (0, 0)), constant_values=1.0)
    outs_s = _adamw(padr(cat0(ws)), padr(cat0(ms)), vpad, jnp.pad(parts_small, ((0, 0), (0, (-n_small) % 8), (0, 0))), "adamw_small")
    off = 0
    for nme, wv in zip(names_s, ws):
        nrow = -(-wv.size // 128)
        res[nme] = [o_[off:off + nrow].reshape(-1)[:wv.size].reshape(wv.shape) for o_ in outs_s]
        off += nrow

    order = ["ln_mix", "ln_ffn", "w_ada", "b_ada", "w_in", "dn_conv_w", "dn_a_log", "dn_dt_bias", "dn_norm_w", "attn_sinks",
             "w_out", "w_gate_up", "w_down", "ln_final"]
    return (loss, dx[None], *[res[n][0] for n in order], *[res[n][1] for n in order], *[res[n][2] for n in order],
            *[res[n][3] for n in order])
```

```python
import functools

import jax
import jax.numpy as jnp
from jax import lax
from jax.experimental import pallas as pl
from jax.experimental.pallas import tpu as pltpu

F32, BF16 = jnp.float32, jnp.bfloat16
HI = lax.Precision.HIGH
MESH = pl.DeviceIdType.MESH
ANY = pl.BlockSpec(memory_space=pl.ANY)

N_DEV = 8
D = 2048
DN_HEADS, DN_HD = 8, 128
DN_W = 1024
CONV_K = 4
CHUNK = 64
AT_HD, AT_QH, AT_KVH = 64, 16, 2
AT_W = 1024
WINDOW = 128
FFN = 5632
IN_COLS = 5392
IN_PAD = 5632
EPS = 1e-6
NEG = -1e30
LR, B1, B2, AEPS, WD, STEP = 0.001, 0.9, 0.999, 1e-08, 0.01, 10
VMEM_LIMIT = 56 * 1024 * 1024
FORWARD_AT = 7


def _pick(n, cands):
    for c in cands:
        if n % c == 0:
            return c
    return n


def _params(sem):
    return pltpu.CompilerParams(dimension_semantics=sem, vmem_limit_bytes=VMEM_LIMIT)


_DN = {"nn": (((1,), (0,)), ((), ())), "nt": (((1,), (1,)), ((), ())), "tn": (((0,), (0,)), ((), ()))}


def _mm(a, b, mode, out_dtype, name, acc_in=None, gather_srcs=()):
    if mode == "nn":
        (M, K), (_, N) = a.shape, b.shape
    elif mode == "nt":
        (M, K), (N, _) = a.shape, b.shape
    else:
        (K, M), (_, N) = a.shape, b.shape
    tm = _pick(M, (1024, 512, 256, 128, 64, 32, 16))
    tn = _pick(N, (1024, 512, 256, 128))
    tk = K if K <= 2048 else _pick(K, (2816, 2048, 1024, 512, 256, 128))
    nk = K // tk
    dn = _DN[mode]
    n_mm = 2 if acc_in is None else 3
    ng = len(gather_srcs)
    n_in = n_mm + ng
    steps = (M // tm) * (N // tn) * nk

    def body(*refs):
        a_ref, b_ref, o_ref = refs[0], refs[1], refs[n_in]
        scratch = refs[n_in + 1 + ng:]
        k = pl.program_id(2)
        step = (pl.program_id(0) * (N // tn) + pl.program_id(1)) * nk + k
        plans = [_gather_plan(refs[n_mm + g], refs[n_in + 1 + g], *scratch[(nk > 1) + 3 * g:(nk > 1) + 3 * g + 3]) for g in range(ng)]
        for start, _, _ in plans:
            pl.when(step == 0)(start)
        for _, forward, _ in plans:
            pl.when(step == steps * FORWARD_AT // 8)(forward)
        part = lax.dot_general(a_ref[...].astype(BF16), b_ref[...].astype(BF16), dn, preferred_element_type=F32)
        if nk == 1:
            if acc_in is not None:
                part = part + refs[2][...]
            o_ref[...] = part.astype(o_ref.dtype)
        else:
            acc = scratch[0]

            @pl.when(k == 0)
            def _():
                acc[...] = part if acc_in is None else part + refs[2][...]

            @pl.when(k > 0)
            def _():
                acc[...] += part

            @pl.when(k == nk - 1)
            def _():
                o_ref[...] = acc[...].astype(o_ref.dtype)
        for _, _, finish in plans:
            pl.when(step == steps - 1)(finish)

    a_spec = pl.BlockSpec((tk, tm), lambda i, j, k: (k, i)) if mode == "tn" else pl.BlockSpec((tm, tk), lambda i, j, k: (i, k))
    b_spec = pl.BlockSpec((tn, tk), lambda i, j, k: (j, k)) if mode == "nt" else pl.BlockSpec((tk, tn), lambda i, j, k: (k, j))
    o_spec = pl.BlockSpec((tm, tn), lambda i, j, k: (i, j))
    res = pl.pallas_call(
        body, name=name, grid=(M // tm, N // tn, nk),
        in_specs=[a_spec, b_spec] + ([] if acc_in is None else [o_spec]) + [ANY] * ng, out_specs=[o_spec] + [ANY] * ng,
        out_shape=[jax.ShapeDtypeStruct((M, N), out_dtype)] + [jax.ShapeDtypeStruct((N_DEV,) + g.shape, g.dtype) for g in gather_srcs],
        scratch_shapes=([pltpu.VMEM((tm, tn), F32)] if nk > 1 else []) + GATHER_SEMS * ng,
        compiler_params=_params(("arbitrary",) * 3 if ng else ("parallel", "parallel", "arbitrary")),
    )(*((a, b) if acc_in is None else (a, b, acc_in)), *gather_srcs)
    return res if ng else res[0]


def _op(op):
    arr, rt, cb = op[:3]
    start, width = op[3] if len(op) > 3 else (0, arr.shape[1])
    return arr, rt, cb, start, width


def _ew_spec(op, tm, ncol):
    arr, rt, cb, start, width = _op(op)
    bw = width // ncol if cb else width
    first = start // bw
    return pl.BlockSpec((tm if rt else arr.shape[0], bw), lambda j, i: (i if rt else 0, first + (j if cb else 0)))


def _ew_fwd(f, ops, outs, *, tm, ncol=1, name, with_j=False):
    M = next(op[0].shape[0] for op in ops if op[1])
    tm = min(tm, M)
    n_in = len(ops)

    def body(*refs):
        res = f(*[r[...].astype(F32) for r in refs[:n_in]], *([pl.program_id(0)] if with_j else []))
        for o, r in zip(refs[n_in:], res):
            o[...] = r.astype(o.dtype)

    return pl.pallas_call(
        body, name=name, grid=(ncol, M // tm),
        in_specs=[_ew_spec(op, tm, ncol) for op in ops],
        out_specs=[pl.BlockSpec((tm, w // ncol if cb else w), lambda j, i, cb=cb: (i, j if cb else 0)) for w, _, cb in outs],
        out_shape=[jax.ShapeDtypeStruct((M, w), dt) for w, dt, _ in outs],
        compiler_params=_params(("parallel", "parallel")),
    )(*[op[0] for op in ops])


def _ew_bwd(f, ops, cts, diff, *, tm, ncol=1, name, gdt=None, with_j=False):
    M = next(op[0].shape[0] for op in ops if op[1])
    tm = min(tm, M)
    n_in, n_ct = len(ops), len(cts)
    gdt = gdt or [F32] * len(diff)

    def body(*refs):
        j, i = pl.program_id(0), pl.program_id(1)
        vals = [r[...].astype(F32) for r in refs[:n_in]]

        def fd(*dv):
            full = list(vals)
            for idx, v in zip(diff, dv):
                full[idx] = v
            return tuple(f(*full, *([j] if with_j else [])))

        _, vjp = jax.vjp(fd, *[vals[idx] for idx in diff])
        gs = vjp(tuple(r[...].astype(F32) for r in refs[n_in:n_in + n_ct]))
        for idx, g, gref in zip(diff, gs, refs[n_in + n_ct:]):
            _, rt, cb = ops[idx][:3]
            if rt:
                gref[...] = g.astype(gref.dtype)
            else:
                first = (i == 0) if cb else jnp.logical_and(i == 0, j == 0)

                @pl.when(first)
                def _(gref=gref):
                    gref[...] = jnp.zeros_like(gref)

                gref[...] += g

    def g_spec(op):
        arr, rt, cb, start, width = _op(op)
        return _ew_spec((jax.ShapeDtypeStruct((arr.shape[0], width), F32), rt, cb), tm, ncol) if rt else _ew_spec(op, tm, ncol)

    def g_shape(op, dt):
        arr, rt, cb, start, width = _op(op)
        return jax.ShapeDtypeStruct((M, width), dt) if rt else jax.ShapeDtypeStruct(arr.shape, F32)

    return pl.pallas_call(
        body, name=name, grid=(ncol, M // tm),
        in_specs=[_ew_spec(op, tm, ncol) for op in ops]
        + [pl.BlockSpec((tm, a.shape[1] // ncol if cb else a.shape[1]), lambda j, i, cb=cb: (i, j if cb else 0)) for a, cb in cts],
        out_specs=[g_spec(ops[idx]) for idx in diff],
        out_shape=[g_shape(ops[idx], dt) for idx, dt in zip(diff, gdt)],
        compiler_params=_params(("arbitrary", "arbitrary")),
    )(*[op[0] for op in ops], *[a for a, _ in cts])


def _silu(x):
    return x * jax.nn.sigmoid(x)


def _f_normmod(x, ln, sc, bsc, sh, bsh):
    y = x * lax.rsqrt(jnp.mean(x * x, axis=-1, keepdims=True) + EPS) * ln
    return y * (1.0 + (sc + bsc)) + (sh + bsh), x


def _f_resgate(x, br, gt, bgt):
    return (x + (gt + bgt) * br,)


def _f_swiglu(gate, up):
    return (_silu(gate) * up,)


def _f_final(x, tgt, ln):
    y = x * lax.rsqrt(jnp.mean(x * x, axis=-1, keepdims=True) + EPS) * ln
    e = y - tgt
    return (jnp.broadcast_to(0.5 * jnp.mean(e * e, axis=-1, keepdims=True), (x.shape[0], 128)),)


def _f_bg(ba, alog, dt):
    col = lax.broadcasted_iota(jnp.int32, ba.shape, 1)
    z = ba + dt
    sp = jnp.maximum(z, 0.0) + jnp.log(1.0 + jnp.exp(-jnp.abs(z)))
    return (jnp.where(col < 8, jax.nn.sigmoid(ba), jnp.where(col < 16, -jnp.exp(alog) * sp, 0.0)),)


def _l2n(x):
    return x * lax.rsqrt(jnp.sum(x * x, axis=-1, keepdims=True) + EPS)


def _f_dnpre(c, j):
    a = _silu(c)
    return (jnp.where(j < 2 * DN_HEADS, _l2n(a) * jnp.where(j < DN_HEADS, DN_HD ** -0.5, 1.0), a),)


def _f_dnpost(o, z, nw):
    return (o * lax.rsqrt(jnp.mean(o * o, axis=-1, keepdims=True) + EPS) * nw * _silu(z),)


CONV_BW = 512


def _conv_fwd(x, w, Cw, name, gather_src=None):
    T = x.shape[0]
    tm, bw = 512, CONV_BW
    carry = gather_src is not None
    steps = (Cw // bw) * (T // tm)

    def body(*refs):
        x_ref, h_ref, w_ref = refs[:3]
        o_ref = refs[3 + carry]
        i = pl.program_id(1)
        if carry:
            step = pl.program_id(0) * (T // tm) + i
            start, forward, finish = _gather_plan(refs[3], refs[5], *refs[6:])
            pl.when(step == 0)(start)
            pl.when(step == steps * FORWARD_AT // 8)(forward)
        cur, halo, wv = x_ref[...], h_ref[...], w_ref[...]
        halo = jnp.where(i > 0, halo, 0.0)
        row = lax.broadcasted_iota(jnp.int32, (8, bw), 0)
        acc = wv[3:4, :] * cur
        for s in (1, 2, 3):
            r = pltpu.roll(cur, s, 0)
            top = jnp.where(row < s, pltpu.roll(halo, s, 0), r[:8])
            acc += wv[3 - s:4 - s, :] * jnp.concatenate([top, r[8:]], axis=0)
        o_ref[...] = acc
        if carry:
            pl.when(step == steps - 1)(finish)

    res = pl.pallas_call(
        body, name=name, grid=(Cw // bw, T // tm),
        in_specs=[pl.BlockSpec((tm, bw), lambda j, i: (i, j)),
                  pl.BlockSpec((8, bw), lambda j, i: (jnp.maximum(i * (tm // 8) - 1, 0), j)),
                  pl.BlockSpec((4, bw), lambda j, i: (0, j))] + [ANY] * carry,
        out_specs=[pl.BlockSpec((tm, bw), lambda j, i: (i, j))] + [ANY] * carry,
        out_shape=[jax.ShapeDtypeStruct((T, Cw), F32)] + ([jax.ShapeDtypeStruct((N_DEV,) + gather_src.shape, gather_src.dtype)] if carry else []),
        scratch_shapes=GATHER_SEMS if carry else [],
        compiler_params=_params(("arbitrary", "arbitrary") if carry else ("parallel", "parallel")),
    )(x, x, w, *([gather_src] if carry else []))
    return res if carry else res[0]


def _conv_bwd(x, dy, w, name):
    T, Cw = dy.shape
    tm, bw = 512, CONV_BW
    nt = T // tm

    def body(x_ref, h_ref, dy_ref, n_ref, w_ref, dx_ref, dw_ref):
        i = pl.program_id(1)
        cur, dcur, wv = x_ref[...], dy_ref[...], w_ref[...]
        halo = jnp.where(i > 0, h_ref[...], 0.0)
        nxt = jnp.where(i < nt - 1, n_ref[...], 0.0)
        row = lax.broadcasted_iota(jnp.int32, (8, bw), 0)

        @pl.when(i == 0)
        def _():
            dw_ref[...] = jnp.zeros_like(dw_ref)

        dx = wv[3:4, :] * dcur
        dw_ref[3:4, :] += jnp.sum(dcur * cur, axis=0, keepdims=True)
        for s in (1, 2, 3):
            r = pltpu.roll(cur, s, 0)
            top = jnp.where(row < s, pltpu.roll(halo, s, 0), r[:8])
            xs = jnp.concatenate([top, r[8:]], axis=0)
            dw_ref[3 - s:4 - s, :] += jnp.sum(dcur * xs, axis=0, keepdims=True)
            rf = pltpu.roll(dcur, tm - s, 0)
            bot = jnp.where(row >= 8 - s, pltpu.roll(nxt, 8 - s, 0), rf[tm - 8:])
            dx += wv[3 - s:4 - s, :] * jnp.concatenate([rf[:tm - 8], bot], axis=0)
        dx_ref[...] = dx

    return pl.pallas_call(
        body, name=name, grid=(Cw // bw, nt),
        in_specs=[pl.BlockSpec((tm, bw), lambda j, i: (i, j)),
                  pl.BlockSpec((8, bw), lambda j, i: (jnp.maximum(i * (tm // 8) - 1, 0), j)),
                  pl.BlockSpec((tm, bw), lambda j, i: (i, j)),
                  pl.BlockSpec((8, bw), lambda j, i: (jnp.minimum((i + 1) * (tm // 8), T // 8 - 1), j)),
                  pl.BlockSpec((4, bw), lambda j, i: (0, j))],
        out_specs=[pl.BlockSpec((tm, bw), lambda j, i: (i, j)), pl.BlockSpec((4, bw), lambda j, i: (0, j))],
        out_shape=[jax.ShapeDtypeStruct((T, Cw), F32), jax.ShapeDtypeStruct((4, Cw), F32)],
        compiler_params=_params(("arbitrary", "arbitrary")),
    )(x, x, dy, dy, w)


def _dot(a, b):
    return jnp.dot(a, b, precision=HI, preferred_element_type=F32)


def _dot_t(a, b):
    return lax.dot_general(a, b, (((1,), (1,)), ((), ())), precision=HI, preferred_element_type=F32)


def _bdot(a, b):
    return jnp.dot(a.astype(BF16), b.astype(BF16), preferred_element_type=F32)


def _bdot_t(a, b):
    return lax.dot_general(a.astype(BF16), b.astype(BF16), (((1,), (1,)), ((), ())), preferred_element_type=F32)


def _each(f, *lists):
    return [f(*a) for a in zip(*lists)]


@jax.custom_vjp
def _unit_lower_inverses(ps):
    C = ps[0].shape[0]
    dist = jnp.bitwise_xor(lax.broadcasted_iota(jnp.int32, (C, C), 0), lax.broadcasted_iota(jnp.int32, (C, C), 1))
    ns = [jnp.where(dist < 8, p, 0.0) for p in ps]
    tis = [jnp.where(dist == 0, 1.0, 0.0) + n for n in ns]
    for _ in range(2):
        ns = _each(lambda n: _bdot(n, n), ns)
        tis = _each(lambda t, n: t + _bdot(t, n), tis, ns)
    b = 8
    while b < C:
        mids = [jnp.where(jnp.logical_and(dist >= b, dist < 2 * b), p, 0.0) for p in ps]
        halves = _each(_bdot, mids, tis)
        tis = _each(lambda t, h: t + _bdot(t, h), tis, halves)
        b *= 2
    return tis


def _uli_fwd(ps):
    tis = _unit_lower_inverses(ps)
    return tis, tis


def _uli_bwd(tis, cts):
    tt = lambda a, b: lax.dot_general(a, b, (((0,), (0,)), ((), ())), precision=HI, preferred_element_type=F32)
    half = _each(tt, tis, cts)
    return (_each(_dot_t, half, tis),)


_unit_lower_inverses.defvjp(_uli_fwd, _uli_bwd)


@jax.custom_vjp
def _kept_inverses(ps, tis):
    return tis


_kept_inverses.defvjp(lambda ps, tis: (tis, tis), lambda tis, cts: (_uli_bwd(tis, cts)[0], [jnp.zeros_like(t) for t in tis]))


def _chunk_f(qs, ks, vs, bg, Ss, hs, kept=None):
    C = CHUNK
    lane = lax.broadcasted_iota(jnp.int32, (C, 128), 1)
    betas = [jnp.sum(jnp.where(lane == h, bg, 0.0), axis=1, keepdims=True) for h in hs]
    gs = [jnp.sum(jnp.where(lane == h + 8, bg, 0.0), axis=1, keepdims=True) for h in hs]
    ri = lax.broadcasted_iota(jnp.int32, (C, C), 0)
    ci = lax.broadcasted_iota(jnp.int32, (C, C), 1)
    causal, strict, eye = ri >= ci, ri > ci, ri == ci
    g_rows = _each(lambda g: jnp.sum(jnp.where(eye, g, 0.0), axis=0, keepdims=True), gs)
    gc_cols = _each(lambda gr: jnp.sum(jnp.where(causal, gr, 0.0), axis=1, keepdims=True), g_rows)
    gc_rows = _each(lambda g: jnp.sum(jnp.where(ri <= ci, g, 0.0), axis=0, keepdims=True), gs)
    gc_lasts = _each(lambda g: jnp.sum(g, axis=0, keepdims=True), gs)
    decays = _each(lambda c, r: jnp.exp(jnp.where(causal, c - r, NEG)), gc_cols, gc_rows)
    kbs = _each(jnp.multiply, ks, betas)
    vbs = _each(jnp.multiply, vs, betas)
    ps = _each(lambda kb, k, d: -jnp.where(strict, _bdot_t(kb, k) * d, 0.0), kbs, ks, decays)
    tis = _unit_lower_inverses(ps) if kept is None else _kept_inverses(ps, kept)
    egcs = _each(jnp.exp, gc_cols)
    ws = _each(lambda t, kb, e: _bdot(t, kb * e), tis, kbs, egcs)
    us = _each(_bdot, tis, vbs)
    intras = _each(lambda q, k, d: jnp.where(causal, _bdot_t(q, k) * d, 0.0), qs, ks, decays)
    kds = _each(lambda k, gl, gc: k * jnp.exp(gl - gc), ks, gc_lasts, gc_cols)
    vns = _each(lambda u, w, S: u - _bdot(w, S), us, ws, Ss)
    os_ = _each(lambda q, e, S, i, vn: _bdot(q * e, S) + _bdot(i, vn), qs, egcs, Ss, intras, vns)
    Sn = _each(lambda S, gl, kd, vn: S * jnp.exp(gl) + lax.dot_general(
        kd.astype(BF16), vn.astype(BF16), (((0,), (0,)), ((), ())), preferred_element_type=F32), Ss, gc_lasts, kds, vns)
    return (os_, Sn), tis


def _head_cols(ref, part):
    return [ref[:, part * DN_W + e * DN_HD:part * DN_W + (e + 1) * DN_HD] for e in range(DN_HEADS)]


def _chunk_fwd(qkv, bg, name, gather_src=None):
    T = qkv.shape[0]
    N = T // CHUNK
    H = DN_HEADS
    carry = gather_src is not None

    def body(*refs):
        x_ref, bg_ref = refs[:2]
        o_ref, s_ref, t_ref = refs[2 + carry:5 + carry]
        S = refs[5 + 2 * carry]
        n = pl.program_id(0)
        if carry:
            start, forward, finish = _gather_plan(refs[2], refs[6], *refs[8:])
            pl.when(n == 0)(start)
            pl.when(n == N * FORWARD_AT // 8)(forward)

        @pl.when(n == 0)
        def _():
            S[...] = jnp.zeros_like(S)

        s_all = S[...]
        s_ref[0] = s_all
        (os_, nxt), tis = _chunk_f(_head_cols(x_ref, 0), _head_cols(x_ref, 1), _head_cols(x_ref, 2), bg_ref[...],
                                   [s_all[e] for e in range(H)], list(range(H)))
        for e in range(H):
            o_ref[:, e * DN_HD:(e + 1) * DN_HD] = os_[e]
            S[e] = nxt[e]
            t_ref[0, e] = tis[e]
        if carry:
            pl.when(n == N - 1)(finish)

    gshape = [jax.ShapeDtypeStruct((N_DEV,) + gather_src.shape, gather_src.dtype)] if carry else []
    return pl.pallas_call(
        body, name=name, grid=(N,),
        in_specs=[pl.BlockSpec((CHUNK, 3 * DN_W), lambda n: (n, 0)), pl.BlockSpec((CHUNK, 128), lambda n: (n, 0))] + [ANY] * carry,
        out_specs=[pl.BlockSpec((CHUNK, DN_W), lambda n: (n, 0)), pl.BlockSpec((1, H, DN_HD, DN_HD), lambda n: (n, 0, 0, 0)),
                   pl.BlockSpec((1, H, CHUNK, CHUNK), lambda n: (n, 0, 0, 0))] + [ANY] * carry,
        out_shape=[jax.ShapeDtypeStruct((T, DN_W), F32), jax.ShapeDtypeStruct((N, H, DN_HD, DN_HD), F32),
                   jax.ShapeDtypeStruct((N, H, CHUNK, CHUNK), F32)] + gshape,
        scratch_shapes=[pltpu.VMEM((H, DN_HD, DN_HD), F32)] + (GATHER_SEMS if carry else []),
        compiler_params=_params(("arbitrary",)),
    )(qkv, bg, *([gather_src] if carry else []))


def _chunk_bwd(qkv, bg, s_saved, t_saved, do, name, a2a_srcs=()):
    T = qkv.shape[0]
    N = T // CHUNK
    H = DN_HEADS
    na = len(a2a_srcs)

    def body(*refs):
        x_ref, bg_ref, s_ref, t_ref, do_ref = refs[:5]
        dx_ref, dbg_ref = refs[5 + na:7 + na]
        dS = refs[7 + 2 * na]
        n = pl.program_id(0)
        if na:
            start, finish = _a2a_plan(refs[5:5 + na], refs[7 + na:7 + 2 * na], *refs[8 + 2 * na:])
            pl.when(n == 0)(start)

        @pl.when(n == 0)
        def _():
            dS[...] = jnp.zeros_like(dS)

        ds_all = dS[...]
        _, vjp, _ = jax.vjp(functools.partial(_chunk_f, hs=list(range(H)), kept=[t_ref[0, e] for e in range(H)]), _head_cols(x_ref, 0),
                            _head_cols(x_ref, 1), _head_cols(x_ref, 2), bg_ref[...], [s_ref[0, e] for e in range(H)], has_aux=True)
        dq, dk, dv, dbg, nxt = vjp(([do_ref[:, e * DN_HD:(e + 1) * DN_HD] for e in range(H)], [ds_all[e] for e in range(H)]))
        for part, g in enumerate((dq, dk, dv)):
            for e in range(H):
                dx_ref[:, part * DN_W + e * DN_HD:part * DN_W + (e + 1) * DN_HD] = g[e]
        for e in range(H):
            dS[e] = nxt[e]
        dbg_ref[...] = dbg
        if na:
            pl.when(n == N - 1)(finish)

    rev = lambda n: (N - 1 - n, 0)
    return pl.pallas_call(
        body, name=name, grid=(N,),
        in_specs=[pl.BlockSpec((CHUNK, 3 * DN_W), rev), pl.BlockSpec((CHUNK, 128), rev),
                  pl.BlockSpec((1, H, DN_HD, DN_HD), lambda n: (N - 1 - n, 0, 0, 0)),
                  pl.BlockSpec((1, H, CHUNK, CHUNK), lambda n: (N - 1 - n, 0, 0, 0)), pl.BlockSpec((CHUNK, DN_W), rev)] + [ANY] * na,
        out_specs=[pl.BlockSpec((CHUNK, 3 * DN_W), rev), pl.BlockSpec((CHUNK, 128), rev)] + [ANY] * na,
        out_shape=[jax.ShapeDtypeStruct((T, 3 * DN_W), F32), jax.ShapeDtypeStruct((T, 128), F32)]
        + [jax.ShapeDtypeStruct(a.shape, a.dtype) for a in a2a_srcs],
        scratch_shapes=[pltpu.VMEM((H, DN_HD, DN_HD), F32)] + (_a2a_sems(na) if na else []),
        compiler_params=_params(("arbitrary",)),
    )(qkv, bg, s_saved, t_saved, do, *a2a_srcs)


GRP = AT_QH // AT_KVH


def _attn_f(q, kp, kc, vp, vc, sinks, cos_c, sin_c, cos_p, sin_p, rot, has_prev):
    def rope(x, c, s):
        return x * c + _dot(x, rot) * s

    kcr, kpr = rope(kc, cos_c, sin_c), rope(kp, cos_p, sin_p)
    qrs = [rope(qe, cos_c, sin_c) for qe in q]
    r = lax.broadcasted_iota(jnp.int32, (WINDOW, WINDOW), 0)
    j = lax.broadcasted_iota(jnp.int32, (WINDOW, WINDOW), 1)
    in_c, in_p = j <= r, jnp.logical_and(j > r, has_prev)
    lane = lax.broadcasted_iota(jnp.int32, (1, 128), 1)
    scs = [jnp.where(in_c, _bdot_t(qr, kcr) * (AT_HD ** -0.5), NEG) for qr in qrs]
    sps = [jnp.where(in_p, _bdot_t(qr, kpr) * (AT_HD ** -0.5), NEG) for qr in qrs]
    snk = [jnp.sum(jnp.where(lane == 0, s, 0.0), axis=1, keepdims=True) for s in sinks]
    ms = _each(lambda sc, sp, s: jnp.maximum(jnp.maximum(jnp.max(sc, axis=1, keepdims=True), jnp.max(sp, axis=1, keepdims=True)), s),
               scs, sps, snk)
    pcs = _each(lambda sc, m: jnp.exp(sc - m), scs, ms)
    pps = _each(lambda sp, m: jnp.exp(sp - m), sps, ms)
    dens = _each(lambda pc, pp, s, m: jnp.sum(pc, axis=1, keepdims=True) + jnp.sum(pp, axis=1, keepdims=True) + jnp.exp(s - m),
                 pcs, pps, snk, ms)
    return tuple(_each(lambda pc, pp, den: (_bdot(pc, vc) + _bdot(pp, vp)) / den, pcs, pps, dens))


def _attn_specs(nb):
    qs = pl.BlockSpec((GRP, WINDOW, AT_HD), lambda g, n: (g, n, 0))
    kc = pl.BlockSpec((1, WINDOW, AT_HD), lambda g, n: (g, n, 0))
    kp = pl.BlockSpec((1, WINDOW, AT_HD), lambda g, n: (g, jnp.maximum(n - 1, 0), 0))
    tc = pl.BlockSpec((WINDOW, AT_HD), lambda g, n: (n, 0))
    tp = pl.BlockSpec((WINDOW, AT_HD), lambda g, n: (jnp.maximum(n - 1, 0), 0))
    sk = pl.BlockSpec((GRP, 1, 128), lambda g, n: (g, 0, 0))
    rt = pl.BlockSpec((AT_HD, AT_HD), lambda g, n: (0, 0))
    return qs, kc, kp, tc, tp, sk, rt


def _attn_fwd(q, k, v, sinks, cos, sin, rot, name, gather_src=None):
    T = q.shape[1]
    nb = T // WINDOW
    qs, kc, kp, tc, tp, sk, rt = _attn_specs(nb)
    carry = gather_src is not None

    def body(*refs):
        q_ref, kp_ref, kc_ref, vp_ref, vc_ref, sk_ref, cc_ref, sc_ref, cp_ref, sp_ref, rot_ref = refs[:11]
        o_ref = refs[11 + carry]
        n = pl.program_id(1)
        if carry:
            step = pl.program_id(0) * nb + n
            start, forward, finish = _gather_plan(refs[11], refs[13], *refs[14:])
            pl.when(step == 0)(start)
            pl.when(step == AT_KVH * nb * FORWARD_AT // 8)(forward)
        heads = range(GRP)
        outs = _attn_f(tuple(q_ref[e] for e in heads), kp_ref[0], kc_ref[0], vp_ref[0], vc_ref[0], tuple(sk_ref[e] for e in heads), cc_ref[...], sc_ref[...],
                       cp_ref[...], sp_ref[...], rot_ref[...], n > 0)
        for e in range(GRP):
            o_ref[e] = outs[e]
        if carry:
            pl.when(step == AT_KVH * nb - 1)(finish)

    gshape = [jax.ShapeDtypeStruct((N_DEV,) + gather_src.shape, gather_src.dtype)] if carry else []
    return pl.pallas_call(
        body, name=name, grid=(AT_KVH, nb),
        in_specs=[qs, kp, kc, kp, kc, sk, tc, tc, tp, tp, rt] + [ANY] * carry, out_specs=[qs] + [ANY] * carry,
        out_shape=[jax.ShapeDtypeStruct(q.shape, F32)] + gshape,
        scratch_shapes=GATHER_SEMS if carry else [],
        compiler_params=_params(("arbitrary", "arbitrary") if carry else ("parallel", "parallel")),
    )(q, k, k, v, v, sinks, cos, sin, cos, sin, rot, *([gather_src] if carry else []))


def _attn_bwd(q, k, v, sinks, cos, sin, rot, do, name, a2a_srcs=()):
    T = q.shape[1]
    nb = T // WINDOW
    qs, kc, kp, tc, tp, sk, rt = _attn_specs(nb)
    na = len(a2a_srcs)

    def body(*refs):
        q_ref, kp_ref, kc_ref, vp_ref, vc_ref, sk_ref, cc_ref, sc_ref, cp_ref, sp_ref, rot_ref, do_ref = refs[:12]
        dq_ref, dkp_ref, dkc_ref, dvp_ref, dvc_ref, dsk_ref = refs[12 + na:18 + na]
        n = pl.program_id(1)
        if na:
            step = pl.program_id(0) * nb + n
            start, finish = _a2a_plan(refs[12:12 + na], refs[18 + na:18 + 2 * na], *refs[18 + 2 * na:])
            pl.when(step == 0)(start)
        f = functools.partial(_attn_f, cos_c=cc_ref[...], sin_c=sc_ref[...], cos_p=cp_ref[...], sin_p=sp_ref[...],
                              rot=rot_ref[...], has_prev=n > 0)
        heads = range(GRP)
        _, vjp = jax.vjp(f, tuple(q_ref[e] for e in heads), kp_ref[0], kc_ref[0], vp_ref[0], vc_ref[0], tuple(sk_ref[e] for e in heads))
        dq, dkp, dkc, dvp, dvc, dsk = vjp(tuple(do_ref[e] for e in heads))
        dkp_ref[0], dkc_ref[0], dvp_ref[0], dvc_ref[0] = dkp, dkc, dvp, dvc

        @pl.when(n == 0)
        def _():
            dsk_ref[...] = jnp.zeros_like(dsk_ref)

        for e in heads:
            dq_ref[e] = dq[e]
            dsk_ref[e] += dsk[e]
        if na:
            pl.when(step == AT_KVH * nb - 1)(finish)

    return pl.pallas_call(
        body, name=name, grid=(AT_KVH, nb),
        in_specs=[qs, kp, kc, kp, kc, sk, tc, tc, tp, tp, rt, qs] + [ANY] * na, out_specs=[qs, kc, kc, kc, kc, sk] + [ANY] * na,
        out_shape=[jax.ShapeDtypeStruct(q.shape, F32)] + [jax.ShapeDtypeStruct(k.shape, F32)] * 4 + [jax.ShapeDtypeStruct(sinks.shape, F32)]
        + [jax.ShapeDtypeStruct(a.shape, a.dtype) for a in a2a_srcs],
        scratch_shapes=_a2a_sems(na) if na else [],
        compiler_params=_params(("arbitrary", "arbitrary")),
    )(q, k, k, v, v, sinks, cos, sin, cos, sin, rot, do, *a2a_srcs)


def _kv_combine(dc, dp, name):
    T = dc.shape[1]
    R = 8 * WINDOW
    ns = T // R

    def body(c_ref, p_ref, q_ref, o_ref):
        n = pl.program_id(1)
        tail = jnp.where(n < ns - 1, q_ref[0], 0.0)
        o_ref[0] = c_ref[0] + jnp.concatenate([p_ref[0, WINDOW:, :], tail], axis=0)

    return pl.pallas_call(
        body, name=name, grid=(AT_KVH, ns),
        in_specs=[pl.BlockSpec((1, R, AT_HD), lambda g, n: (g, n, 0)), pl.BlockSpec((1, R, AT_HD), lambda g, n: (g, n, 0)),
                  pl.BlockSpec((1, WINDOW, AT_HD), lambda g, n: (g, jnp.minimum((n + 1) * 8, T // WINDOW - 1), 0))],
        out_specs=pl.BlockSpec((1, R, AT_HD), lambda g, n: (g, n, 0)),
        out_shape=jax.ShapeDtypeStruct((AT_KVH, T, AT_HD), F32),
        compiler_params=_params(("parallel", "parallel")),
    )(dc, dp, dp)


def _place():
    x, y, c = lax.axis_index("x"), lax.axis_index("y"), lax.axis_index("c")
    return x, y, c, 4 * x + 2 * y + c


def _gather_plan(s_ref, o_ref, send_sems, recv_sems, lsem):
    x, y, c, _ = _place()
    me, sib = (x, y, c), (x, y, 1 - c)
    chips = [(1 - x, y), (x, 1 - y), (1 - x, 1 - y)]

    def copy(k, block, to, src_ref=None):
        slab = o_ref.at[4 * block[0] + 2 * block[1] + block[2]]
        return pltpu.make_async_remote_copy(src_ref=slab if src_ref is None else src_ref, dst_ref=slab,
                                            send_sem=send_sems.at[k], recv_sem=recv_sems.at[k], device_id=to, device_id_type=MESH)

    mine = pltpu.make_async_copy(s_ref, o_ref.at[4 * x + 2 * y + c], lsem)
    first = [copy(0, me, sib, s_ref)] + [copy(1 + j, me, (*chip, c), s_ref) for j, chip in enumerate(chips)]
    passed = [copy(4 + j, (*chip, c), sib) for j, chip in enumerate(chips)]

    def start():
        mine.start()
        for cp in first:
            cp.start()

    def forward():
        for j, chip in enumerate(chips):
            copy(1 + j, (*chip, c), me).wait_recv()
            passed[j].start()

    def finish():
        copy(0, sib, me).wait_recv()
        for j, chip in enumerate(chips):
            copy(4 + j, (*chip, 1 - c), me).wait_recv()
        for cp in first + passed:
            cp.wait_send()
        mine.wait()

    return start, forward, finish


GATHER_SEMS = [pltpu.SemaphoreType.DMA((N_DEV - 1,)), pltpu.SemaphoreType.DMA((N_DEV - 1,)), pltpu.SemaphoreType.DMA]


def _gather(src, name):
    def body(s_ref, o_ref, send_sems, recv_sems, lsem):
        for phase in _gather_plan(s_ref, o_ref, send_sems, recv_sems, lsem):
            phase()

    return pl.pallas_call(
        body, name=name, in_specs=[ANY], out_specs=ANY, out_shape=jax.ShapeDtypeStruct((N_DEV,) + src.shape, src.dtype),
        scratch_shapes=GATHER_SEMS,
    )(src)


def _pair_swap(src, name):
    def body(s_ref, o_ref, send_sem, recv_sem):
        x, y, c, _ = _place()
        cp = pltpu.make_async_remote_copy(src_ref=s_ref.at[1 - c], dst_ref=o_ref, send_sem=send_sem, recv_sem=recv_sem,
                                          device_id=(x, y, 1 - c), device_id_type=MESH)
        cp.start()
        cp.wait()

    return pl.pallas_call(
        body, name=name, in_specs=[ANY], out_specs=ANY, out_shape=jax.ShapeDtypeStruct(src.shape[1:], src.dtype),
        scratch_shapes=[pltpu.SemaphoreType.DMA, pltpu.SemaphoreType.DMA],
    )(src)


def _a2a_plan(s_refs, o_refs, send_sems, recv_sems, lsems):
    x, y, c, _ = _place()
    chip = 2 * x + y
    local, remote = [], []
    for a, (s_ref, o_ref) in enumerate(zip(s_refs, o_refs)):
        local.append(pltpu.make_async_copy(s_ref.at[chip], o_ref.at[chip], lsems.at[a]))
        for k in (1, 2, 3):
            px, py = x ^ (k >> 1), y ^ (k & 1)
            remote.append(pltpu.make_async_remote_copy(
                src_ref=s_ref.at[2 * px + py], dst_ref=o_ref.at[chip], send_sem=send_sems.at[3 * a + k - 1],
                recv_sem=recv_sems.at[3 * a + k - 1], device_id=(px, py, c), device_id_type=MESH))

    def start():
        for cp in local + remote:
            cp.start()

    def finish():
        for cp in remote + local:
            cp.wait()

    return start, finish


def _a2a_sems(n):
    return [pltpu.SemaphoreType.DMA((3 * n,)), pltpu.SemaphoreType.DMA((3 * n,)), pltpu.SemaphoreType.DMA((n,))]


def _chip_a2a(srcs, name):
    n = len(srcs)

    def body(*refs):
        for phase in _a2a_plan(refs[:n], refs[n:2 * n], *refs[2 * n:]):
            phase()

    return pl.pallas_call(
        body, name=name, in_specs=[ANY] * n, out_specs=[ANY] * n, out_shape=[jax.ShapeDtypeStruct(a.shape, a.dtype) for a in srcs],
        scratch_shapes=_a2a_sems(n),
    )(*srcs)


def _add(a, b, name):
    R, C = a.shape
    tr = _pick(R, (512, 256, 128, 64, 32, 16))

    def body(a_ref, b_ref, o_ref):
        o_ref[...] = (a_ref[...].astype(F32) + b_ref[...].astype(F32)).astype(o_ref.dtype)

    s2 = pl.BlockSpec((tr, C), lambda i: (i, 0))
    return pl.pallas_call(body, name=name, grid=(R // tr,), in_specs=[s2, s2], out_specs=s2,
                          out_shape=jax.ShapeDtypeStruct((R, C), a.dtype), compiler_params=_params(("parallel",)))(a, b)


def _chip_sums(slabs, name):
    _, R, C = slabs.shape
    by_core = jnp.transpose(slabs.reshape(4, 2, R, C), (1, 0, 2, 3))
    theirs = _pair_swap(by_core, "swap_" + name)
    mine = lax.dynamic_index_in_dim(by_core, lax.axis_index("c"), axis=0, keepdims=False)
    return _add(mine.reshape(4 * R, C), theirs.reshape(4 * R, C), "add_" + name).reshape(4, R, C)


def _adamw(w, m, v, parts, name):
    R, C = w.shape
    P = parts.shape[0]
    tr = _pick(R, (256, 128, 64, 32, 16, 8))
    c1, c2 = 1.0 - B1 ** STEP, 1.0 - B2 ** STEP

    def body(w_ref, m_ref, v_ref, p_ref, g_ref, d_ref, nm_ref, nv_ref):
        g = p_ref[0].astype(F32)
        for i in range(1, P):
            g = g + p_ref[i].astype(F32)
        wv = w_ref[...]
        nm = B1 * m_ref[...] + (1.0 - B1) * g
        nv = B2 * v_ref[...] + (1.0 - B2) * (g * g)
        g_ref[...] = g
        nm_ref[...] = nm
        nv_ref[...] = nv
        d_ref[...] = -LR * ((nm / c1) / (jnp.sqrt(nv / c2) + AEPS) + WD * wv)

    s2 = pl.BlockSpec((tr, C), lambda i: (i, 0))
    return pl.pallas_call(
        body, name=name, grid=(R // tr,),
        in_specs=[s2, s2, s2, pl.BlockSpec((P, tr, C), lambda i: (0, i, 0))], out_specs=[s2] * 4,
        out_shape=[jax.ShapeDtypeStruct((R, C), F32)] * 4,
        compiler_params=_params(("parallel",)),
    )(w, m, v, parts)


def _colsum(a, name):
    def body(a_ref, o_ref):
        o_ref[...] = jnp.broadcast_to(jnp.sum(a_ref[...], axis=0, keepdims=True), o_ref.shape)

    return pl.pallas_call(body, name=name, out_shape=jax.ShapeDtypeStruct((8, 128), F32))(a)


def _rows128(a):
    f = a.reshape(-1)
    return jnp.pad(f, (0, (-f.shape[0]) % 128)).reshape(-1, 128)


def _to_aligned(w):
    return jnp.concatenate([w[..., 0:4096], w[..., 4112:5392], w[..., 4096:4112],
                            jnp.zeros(w.shape[:-1] + (IN_PAD - IN_COLS,), w.dtype)], axis=-1)


def _from_aligned(w):
    return jnp.concatenate([w[..., 0:4096], w[..., 5376:5392], w[..., 4096:5376]], axis=-1)


def kernel(x, c, ln_mix, ln_ffn, w_ada, b_ada, w_in, dn_conv_w, dn_a_log, dn_dt_bias, dn_norm_w, attn_sinks, w_out, w_gate_up, w_down, ln_final, loss_target, m_ln_mix, m_ln_ffn, m_w_ada, m_b_ada, m_w_in, m_dn_conv_w, m_dn_a_log, m_dn_dt_bias, m_dn_norm_w, m_attn_sinks, m_w_out, m_w_gate_up, m_w_down, m_ln_final, v_ln_mix, v_ln_ffn, v_w_ada, v_b_ada, v_w_in, v_dn_conv_w, v_dn_a_log, v_dn_dt_bias, v_dn_norm_w, v_attn_sinks, v_w_out, v_w_gate_up, v_w_down, v_ln_final):
    T = x.shape[1]
    L = ln_mix.shape[0]
    me = 4 * lax.axis_index("x") + 2 * lax.axis_index("y") + lax.axis_index("c")
    xs = x[0]
    tgt = loss_target[0]

    w_in_layout = lambda g: _to_aligned(jnp.transpose(g, (1, 0, 2)).reshape(D, IN_COLS))
    W_in = [w_in_layout(_gather(w_in[0].astype(BF16), "ag_w_in0"))] + [None] * (L - 1)
    W_out = None
    g_cv = _gather(dn_conv_w.reshape(L * CONV_K, -1), "ag_conv").reshape(N_DEV, L, CONV_K, -1)
    c_all = _gather(jnp.pad(c, ((0, 7), (0, 0))), "ag_c")[:, 0, :]
    W_gate, W_up, W_dn = [None] * L, [None] * L, [None] * L
    W_cv = [jnp.transpose(g_cv[:, l], (1, 0, 2)).reshape(CONV_K, 3 * DN_W) for l in range(L)]

    c_act = _ew_fwd(lambda v: (_silu(v),), [(jnp.pad(c_all, ((0, 8), (0, 0))), True, False)], [(D, F32, False)], tm=16, name="c_act")[0]
    mods = []
    for l in range(L):
        ms = _mm(c_act, w_ada[l], "nn", F32, f"mod_mm{l}")
        ga = _gather(ms, f"ag_mod{l}")
        mods.append(lax.dynamic_index_in_dim(ga, me, axis=1, keepdims=False).reshape(1, 6 * D))
    row = lambda a: a.reshape(1, -1)
    seg = lambda a, i: a[:, i * D:(i + 1) * D]

    half = AT_HD // 2
    inv_freq = 10000.0 ** (-jnp.arange(half, dtype=F32) * 2.0 / AT_HD)
    ang = jnp.arange(T, dtype=jnp.int32).astype(F32)[:, None] * inv_freq[None, :]
    cos = jnp.concatenate([jnp.cos(ang)] * 2, axis=-1)
    sin = jnp.concatenate([jnp.sin(ang)] * 2, axis=-1)
    ii = jnp.arange(AT_HD)
    rot = jnp.where(ii[:, None] == ii[None, :] + half, -1.0, 0.0) + jnp.where(ii[:, None] + half == ii[None, :], 1.0, 0.0)
    rot = rot.astype(F32)
    heads = lambda a, nh: jnp.transpose(a.reshape(T, nh, AT_HD), (1, 0, 2))
    unheads = lambda a: jnp.transpose(a, (1, 0, 2)).reshape(T, -1)
    pad16 = lambda a: jnp.pad(row(a), ((0, 0), (8, 128 - 16)))

    saved = []
    xc = xs
    for l in range(L):
        mod, bmod = mods[l], row(b_ada[l])
        s = {"x": xc}
        nm_ops = lambda xx, ln, a, b: [(xx, True, False), (row(ln), False, False), (seg(mod, a), False, False),
                                       (seg(bmod, a), False, False), (seg(mod, b), False, False), (seg(bmod, b), False, False)]
        h1 = _ew_fwd(lambda *a: _f_normmod(*a)[:1], nm_ops(xc, ln_mix[l], 1, 0), [(D, BF16, False)], tm=256, name=f"normmod1_{l}")[0]
        if l + 1 < L:
            proj, g_in = _mm(h1, W_in[l], "nn", F32, f"mm_in{l}", gather_srcs=[w_in[l + 1].astype(BF16)])
            W_in[l + 1] = w_in_layout(g_in)
        else:
            proj = _mm(h1, W_in[l], "nn", F32, f"mm_in{l}")
        aq, ak, av = proj[:, 4096:5120], proj[:, 5120:5248], proj[:, 5248:5376]
        if l == 0:
            conv, g_out = _conv_fwd(proj, W_cv[l], 3 * DN_W, f"conv{l}", gather_src=w_out.astype(BF16).reshape(-1, D))
            W_out = [g_out.reshape(N_DEV, L, -1, D)[:, i].reshape(D, D) for i in range(L)]
        else:
            conv = _conv_fwd(proj, W_cv[l], 3 * DN_W, f"conv{l}")
        pre_ops = [(conv, True, True)]
        qkvn = _ew_fwd(_f_dnpre, pre_ops, [(3 * DN_W, F32, True)], tm=2048, ncol=3 * DN_HEADS, name=f"dnpre{l}", with_j=True)[0]
        bg_ops = [(proj, True, False, (5376, 128)), (pad16(dn_a_log[l]), False, False), (pad16(dn_dt_bias[l]), False, False)]
        bg = _ew_fwd(_f_bg, bg_ops, [(128, F32, False)], tm=1024, name=f"bg{l}")[0]
        o, s_saved, t_saved, g_gu = _chunk_fwd(qkvn, bg, f"chunk{l}", gather_src=w_gate_up[l].astype(BF16))
        W_gate[l] = jnp.transpose(g_gu[:4], (1, 0, 2)).reshape(D, FFN)
        W_up[l] = jnp.transpose(g_gu[4:], (1, 0, 2)).reshape(D, FFN)
        post_ops = [(o, True, True), (proj, True, True, (3072, DN_W)), (row(dn_norm_w[l]), False, False)]
        dn_out = _ew_fwd(_f_dnpost, post_ops, [(DN_W, BF16, True)], tm=2048, ncol=DN_HEADS, name=f"dnpost{l}")[0]
        qh, kh, vh = heads(aq, AT_QH), heads(ak, AT_KVH), heads(av, AT_KVH)
        sk = jnp.broadcast_to(attn_sinks[l][:, None, None], (AT_QH, 1, 128))
        at_o, g_dn = _attn_fwd(qh, kh, vh, sk, cos, sin, rot, f"attn{l}", gather_src=w_down[l].astype(BF16))
        W_dn[l] = g_dn.reshape(FFN, D)
        at_out = unheads(at_o).astype(BF16)
        mix = _mm(at_out, W_out[l][DN_W:], "nn", F32, f"mm_out_at{l}", acc_in=_mm(dn_out, W_out[l][:DN_W], "nn", F32, f"mm_out_dn{l}"))
        rg_ops = lambda xx, br, a: [(xx, True, False), (br, True, False), (seg(mod, a), False, False), (seg(bmod, a), False, False)]
        x1 = _ew_fwd(_f_resgate, rg_ops(xc, mix, 2), [(D, F32, False)], tm=256, name=f"resgate1_{l}")[0]
        h2 = _ew_fwd(lambda *a: _f_normmod(*a)[:1], nm_ops(x1, ln_ffn[l], 4, 3), [(D, BF16, False)], tm=256, name=f"normmod2_{l}")[0]
        gate = _mm(h2, W_gate[l], "nn", BF16, f"mm_gate{l}")
        up = _mm(h2, W_up[l], "nn", BF16, f"mm_up{l}")
        sw_ops = [(gate, True, True), (up, True, True)]
        act = _ew_fwd(_f_swiglu, sw_ops, [(FFN, BF16, True)], tm=2048, ncol=11, name=f"swiglu{l}")[0]
        down = _mm(act, W_dn[l], "nn", F32, f"mm_down{l}")
        x2 = _ew_fwd(_f_resgate, rg_ops(x1, down, 5), [(D, F32, False)], tm=256, name=f"resgate2_{l}")[0]
        s.update(h1=h1, proj=proj, qkvn=qkvn, bg=bg, s_saved=s_saved, t_saved=t_saved, qh=qh, kh=kh, vh=vh, sk=sk, dn_out=dn_out, at_out=at_out,
                 h2=h2, act=act, bg_ops=bg_ops, post_ops=post_ops, pre_ops=pre_ops, sw_ops=sw_ops,
                 nm1=nm_ops(xc, ln_mix[l], 1, 0), nm2=nm_ops(x1, ln_ffn[l], 4, 3), rg1=rg_ops(xc, mix, 2), rg2=rg_ops(x1, down, 5))
        saved.append(s)
        xc = x2

    fin_ops = [(xc, True, False), (tgt, True, False), (row(ln_final), False, False)]
    lrow = _ew_fwd(_f_final, fin_ops, [(128, F32, False)], tm=256, name="loss_rows")[0]
    loss = lax.psum(_colsum(lrow, "loss_sum")[0, 0], ("x", "y", "c"))
    dx, d_ln_final = _ew_bwd(_f_final, fin_ops, [(jnp.ones((T, 128), F32) / 128.0, False)], [0, 2], tm=256, name="loss_bwd")

    small = {k: [None] * L for k in ("ln_mix", "ln_ffn", "mod", "a_log", "dt", "norm_w", "sinks", "conv")}
    big = {}
    recv_ffn, recv_mix = [None] * L, [None] * L
    pending = ()

    def shards(g, cols, n=N_DEV):
        return jnp.transpose(g.reshape(g.shape[0], n, -1), (1, 0, 2)) if cols else g.reshape(n, -1, g.shape[1])

    for l in reversed(range(L)):
        s = saved[l]
        ddown, dgt_f = _ew_bwd(_f_resgate, s["rg2"], [(dx, False)], [1, 2], tm=256, name=f"resgate2_bwd{l}", gdt=[BF16, F32])
        big["w_dn"] = _mm(s["act"], ddown, "tn", BF16, f"wg_down{l}")
        dact = _mm(ddown, W_dn[l], "nt", F32, f"dg_down{l}")
        dgate, dup = _ew_bwd(_f_swiglu, s["sw_ops"], [(dact, True)], [0, 1], tm=2048, ncol=11, name=f"swiglu_bwd{l}", gdt=[BF16, BF16])
        big["w_gate"] = _mm(s["h2"], dgate, "tn", BF16, f"wg_gate{l}")
        big["w_up"] = _mm(s["h2"], dup, "tn", BF16, f"wg_up{l}")
        dh2 = _mm(dup, W_up[l], "nt", F32, f"dg_up{l}", acc_in=_mm(dgate, W_gate[l], "nt", F32, f"dg_gate{l}"))
        dx1, dln_f, dsc_f, dsh_f = _ew_bwd(_f_normmod, s["nm2"], [(dh2, False), (dx, False)], [0, 1, 2, 4], tm=256, name=f"normmod2_bwd{l}")
        dmix, dgt_m = _ew_bwd(_f_resgate, s["rg1"], [(dx1, False)], [1, 2], tm=256, name=f"resgate1_bwd{l}", gdt=[BF16, F32])
        big["w_out"] = jnp.concatenate([_mm(s["dn_out"], dmix, "tn", BF16, f"wg_out_dn{l}"),
                                        _mm(s["at_out"], dmix, "tn", BF16, f"wg_out_at{l}")], axis=0)
        d_dn = _mm(dmix, W_out[l][:DN_W], "nt", F32, f"dg_out_dn{l}")
        d_at = _mm(dmix, W_out[l][DN_W:], "nt", F32, f"dg_out_at{l}")
        ffn_sums = (_chip_sums(jnp.concatenate([shards(big["w_gate"], True, 4), shards(big["w_up"], True, 4)], axis=0), f"w_gu{l}"),
                    _chip_sums(shards(big["w_dn"], False), f"w_down{l}"))
        dqh, dkp, dkc, dvp, dvc, dsk, *recv_ffn[l] = _attn_bwd(s["qh"], s["kh"], s["vh"], s["sk"], cos, sin, rot, heads(d_at, AT_QH),
                                                               f"attn_bwd{l}", a2a_srcs=ffn_sums[1:])
        dkh = _kv_combine(dkc, dkp, f"dk_comb{l}")
        dvh = _kv_combine(dvc, dvp, f"dv_comb{l}")
        do, dz, dnw = _ew_bwd(_f_dnpost, s["post_ops"], [(d_dn, True)], [0, 1, 2], tm=2048, ncol=DN_HEADS, name=f"dnpost_bwd{l}")
        dqkvn, dbg, got_gu, *got = _chunk_bwd(s["qkvn"], s["bg"], s["s_saved"], s["t_saved"], do, f"chunk_bwd{l}",
                                              a2a_srcs=ffn_sums[:1] + pending)
        recv_ffn[l] = [got_gu] + recv_ffn[l]
        if pending:
            recv_mix[l + 1] = got
        dconv = _ew_bwd(_f_dnpre, s["pre_ops"], [(dqkvn, True)], [0], tm=2048, ncol=3 * DN_HEADS, name=f"dnpre_bwd{l}", with_j=True)[0]
        dba, dalog, ddt = _ew_bwd(_f_bg, s["bg_ops"], [(dbg, False)], [0, 1, 2], tm=1024, name=f"bg_bwd{l}")
        dqkv, dcw = _conv_bwd(s["proj"], dconv, W_cv[l], f"conv_bwd{l}")
        dproj = jnp.concatenate([dqkv, dz, unheads(dqh), unheads(dkh), unheads(dvh), dba, jnp.zeros((T, IN_PAD - 5504), F32)],
                                axis=-1).astype(BF16)
        big["w_in"] = _mm(s["h1"], dproj, "tn", BF16, f"wg_in{l}")
        dh1 = _mm(dproj, W_in[l], "nt", F32, f"dg_in{l}")
        dx, dln_m, dsc_m, dsh_m = _ew_bwd(_f_normmod, s["nm1"], [(dh1, False), (dx1, False)], [0, 1, 2, 4], tm=256, name=f"normmod1_bwd{l}")
        small["ln_mix"][l], small["ln_ffn"][l] = dln_m, dln_f
        small["mod"][l] = jnp.concatenate([dsh_m, dsc_m, dgt_m, dsh_f, dsc_f, dgt_f], axis=-1)
        small["a_log"][l], small["dt"][l] = dalog[:, 8:16], ddt[:, 8:16]
        small["norm_w"][l], small["sinks"][l], small["conv"][l] = dnw, dsk[:, 0, 0], dcw
        pending = (_chip_sums(shards(_from_aligned(big["w_in"]), True), f"w_in{l}"), _chip_sums(shards(big["w_out"], False), f"w_out{l}"))
    recv_mix[0] = _chip_a2a(pending, "a2a_mix0")
    p_in, p_out = [jnp.concatenate([recv_mix[l][i] for l in range(L)], axis=1) for i in range(2)]
    p_gu, p_dn = [jnp.concatenate([recv_ffn[l][i] for l in range(L)], axis=1) for i in range(2)]

    cat0 = lambda xs_: jnp.concatenate([_rows128(a) for a in xs_], axis=0)
    stk = lambda k: jnp.stack(small[k])
    pack = cat0([stk("ln_mix"), stk("ln_ffn"), stk("mod"), stk("a_log"), stk("dt"), stk("norm_w"), stk("sinks"), d_ln_final, stk("conv")])
    n_small = pack.shape[0] - L * CONV_K * 3 * DN_W // 128
    pack = jnp.pad(pack, ((0, (-pack.shape[0]) % 8), (0, 0)))
    gp = _gather(pack, "ag_small")
    parts_small = gp[:, :n_small]
    dmod_all = gp[:, 2 * L * D // 128:2 * L * D // 128 + L * 6 * D // 128].reshape(N_DEV, L, 6 * D)
    conv_all = gp[:, n_small:n_small + L * CONV_K * 3 * DN_W // 128].reshape(N_DEV, L * CONV_K, 3 * DN_W)
    parts_conv = lax.dynamic_slice_in_dim(conv_all, me * (3 * DN_W // N_DEV), 3 * DN_W // N_DEV, axis=2)

    dmod_mine = lax.dynamic_slice_in_dim(dmod_all, me * (6 * D // N_DEV), 6 * D // N_DEV, axis=2)
    g_ada = jnp.stack([_mm(c_act, jnp.pad(dmod_mine[:, l], ((0, 8), (0, 0))), "tn", F32, f"wg_ada{l}") for l in range(L)])

    def upd(w, m, v, parts, name):
        shp = w.shape
        r = lambda a: a.reshape(-1, shp[-1])
        return [o_.reshape(shp) for o_ in _adamw(r(w), r(m), r(v), parts.reshape(parts.shape[0], -1, shp[-1]), name)]

    res = {}
    res["w_ada"] = upd(w_ada, m_w_ada, v_w_ada, g_ada[None], "adamw_ada")
    res["w_in"] = upd(w_in, m_w_in, v_w_in, p_in, "adamw_in")
    res["dn_conv_w"] = upd(dn_conv_w, m_dn_conv_w, v_dn_conv_w, parts_conv, "adamw_conv")
    res["w_out"] = upd(w_out, m_w_out, v_w_out, p_out, "adamw_out")
    res["w_gate_up"] = upd(w_gate_up, m_w_gate_up, v_w_gate_up, p_gu, "adamw_gu")
    res["w_down"] = upd(w_down, m_w_down, v_w_down, p_dn, "adamw_down")
    names_s = ["ln_mix", "ln_ffn", "b_ada", "dn_a_log", "dn_dt_bias", "dn_norm_w", "attn_sinks", "ln_final"]
    ws = [ln_mix, ln_ffn, b_ada, dn_a_log, dn_dt_bias, dn_norm_w, attn_sinks, ln_final]
    ms = [m_ln_mix, m_ln_ffn, m_b_ada, m_dn_a_log, m_dn_dt_bias, m_dn_norm_w, m_attn_sinks, m_ln_final]
    vs = [v_ln_mix, v_ln_ffn, v_b_ada, v_dn_a_log, v_dn_dt_bias, v_dn_norm_w, v_attn_sinks, v_ln_final]
    padr = lambda a: jnp.pad(a, ((0, (-a.shape[0]) % 8), (0, 0)))
    vpad = jnp.pad(cat0(vs), ((0, (-n_small) % 8), (0, 0)), constant_values=1.0)
    outs_s = _adamw(padr(cat0(ws)), padr(cat0(ms)), vpad, jnp.pad(parts_small, ((0, 0), (0, (-n_small) % 8), (0, 0))), "adamw_small")
    off = 0
    for nme, wv in zip(names_s, ws):
        nrow = -(-wv.size // 128)
        res[nme] = [o_[off:off + nrow].reshape(-1)[:wv.size].reshape(wv.shape) for o_ in outs_s]
        off += nrow

    order = ["ln_mix", "ln_ffn", "w_ada", "b_ada", "w_in", "dn_conv_w", "dn_a_log", "dn_dt_bias", "dn_norm_w", "attn_sinks",
             "w_out", "w_gate_up", "w_down", "ln_final"]
    return (loss, dx[None], *[res[n][0] for n in order], *[res[n][1] for n in order], *[res[n][2] for n in order],
            *[res[n][3] for n in order])
```

```python
import functools

import jax
import jax.numpy as jnp
from jax import lax
from jax.experimental import pallas as pl
from jax.experimental.pallas import tpu as pltpu

F32, BF16 = jnp.float32, jnp.bfloat16
HI = lax.Precision.HIGH
MESH = pl.DeviceIdType.MESH
ANY = pl.BlockSpec(memory_space=pl.ANY)

N_DEV = 8
D = 2048
DN_HEADS, DN_HD = 8, 128
DN_W = 1024
CONV_K = 4
CHUNK = 64
AT_HD, AT_QH, AT_KVH = 64, 16, 2
AT_W = 1024
WINDOW = 128
FFN = 5632
IN_COLS = 5392
IN_PAD = 5632
EPS = 1e-6
NEG = -1e30
LR, B1, B2, AEPS, WD, STEP = 0.001, 0.9, 0.999, 1e-08, 0.01, 10
VMEM_LIMIT = 56 * 1024 * 1024
FORWARD_AT = 7


def _pick(n, cands):
    for c in cands:
        if n % c == 0:
            return c
    return n


def _params(sem):
    return pltpu.CompilerParams(dimension_semantics=sem, vmem_limit_bytes=VMEM_LIMIT)


_DN = {"nn": (((1,), (0,)), ((), ())), "nt": (((1,), (1,)), ((), ())), "tn": (((0,), (0,)), ((), ()))}


def _mm(a, b, mode, out_dtype, name, acc_in=None, gather_srcs=(), a2a_srcs=(), gate=None):
    if mode == "nn":
        (M, K), (_, N) = a.shape, b.shape
    elif mode == "nt":
        (M, K), (N, _) = a.shape, b.shape
    else:
        (K, M), (_, N) = a.shape, b.shape
    tm = _pick(M, (1024, 512, 256, 128, 64, 32, 16))
    tn = _pick(N, (1024, 512, 256, 128))
    tk = K if K <= 2048 else _pick(K, (2816, 2048, 1024, 512, 256, 128))
    nk = K // tk
    dn = _DN[mode]
    ins = [a, b] + ([] if acc_in is None else [acc_in]) + (list(gate) if gate else [])
    n_mm, ng, na = len(ins), len(gather_srcs), len(a2a_srcs)
    n_in = n_mm + ng + na
    n_out = 1 + (gate is not None)
    steps = (M // tm) * (N // tn) * nk

    def body(*refs):
        a_ref, b_ref, o_ref = refs[0], refs[1], refs[n_in]
        outs = refs[n_in + n_out:n_in + n_out + ng + na]
        scratch = refs[n_in + n_out + ng + na:]
        sems = scratch[(nk > 1):]
        k = pl.program_id(2)
        step = (pl.program_id(0) * (N // tn) + pl.program_id(1)) * nk + k
        plans = [_gather_plan(refs[n_mm + g], outs[g], *sems[3 * g:3 * g + 3]) for g in range(ng)]
        if na:
            plans.append(_a2a_plan(refs[n_mm + ng:n_in], outs[ng:], *sems[3 * ng:]))
        for plan in plans:
            pl.when(step == 0)(plan[0])
        for plan in plans[:ng]:
            pl.when(step == steps * FORWARD_AT // 8)(plan[1])

        def emit(val):
            if acc_in is not None:
                val = val + refs[2][...]
            if gate:
                x_ref, g_ref, bg_ref = refs[n_mm - 3:n_mm]
                o_ref[...] = x_ref[...] + (g_ref[...] + bg_ref[...]) * val
                refs[n_in + 1][...] = val.astype(BF16)
            else:
                o_ref[...] = val.astype(o_ref.dtype)

        part = lax.dot_general(a_ref[...].astype(BF16), b_ref[...].astype(BF16), dn, preferred_element_type=F32)
        if nk == 1:
            emit(part)
        else:
            acc = scratch[0]

            @pl.when(k == 0)
            def _():
                acc[...] = part

            @pl.when(k > 0)
            def _():
                acc[...] += part

            @pl.when(k == nk - 1)
            def _():
                emit(acc[...])
        for plan in plans:
            pl.when(step == steps - 1)(plan[-1])

    a_spec = pl.BlockSpec((tk, tm), lambda i, j, k: (k, i)) if mode == "tn" else pl.BlockSpec((tm, tk), lambda i, j, k: (i, k))
    b_spec = pl.BlockSpec((tn, tk), lambda i, j, k: (j, k)) if mode == "nt" else pl.BlockSpec((tk, tn), lambda i, j, k: (k, j))
    o_spec = pl.BlockSpec((tm, tn), lambda i, j, k: (i, j))
    row_spec = pl.BlockSpec((1, tn), lambda i, j, k: (0, j))
    carried = list(gather_srcs) + list(a2a_srcs)
    res = pl.pallas_call(
        body, name=name, grid=(M // tm, N // tn, nk),
        in_specs=[a_spec, b_spec] + ([] if acc_in is None else [o_spec]) + ([o_spec, row_spec, row_spec] if gate else []) + [ANY] * len(carried),
        out_specs=[o_spec] * n_out + [ANY] * len(carried),
        out_shape=[jax.ShapeDtypeStruct((M, N), F32 if gate else out_dtype)] + ([jax.ShapeDtypeStruct((M, N), BF16)] if gate else [])
        + [jax.ShapeDtypeStruct((N_DEV,) + g.shape, g.dtype) for g in gather_srcs] + [jax.ShapeDtypeStruct(g.shape, g.dtype) for g in a2a_srcs],
        scratch_shapes=([pltpu.VMEM((tm, tn), F32)] if nk > 1 else []) + GATHER_SEMS * ng + (_a2a_sems(na) if na else []),
        compiler_params=_params(("arbitrary",) * 3 if carried else ("parallel", "parallel", "arbitrary")),
    )(*ins, *carried)
    return res if len(res) > 1 else res[0]


def _op(op):
    arr, rt, cb = op[:3]
    start, width = op[3] if len(op) > 3 else (0, arr.shape[1])
    return arr, rt, cb, start, width


def _ew_spec(op, tm, ncol):
    arr, rt, cb, start, width = _op(op)
    bw = width // ncol if cb else width
    first = start // bw
    return pl.BlockSpec((tm if rt else arr.shape[0], bw), lambda j, i: (i if rt else 0, first + (j if cb else 0)))


def _ew_fwd(f, ops, outs, *, tm, ncol=1, name, with_j=False):
    M = next(op[0].shape[0] for op in ops if op[1])
    tm = min(tm, M)
    n_in = len(ops)

    def body(*refs):
        res = f(*[r[...].astype(F32) for r in refs[:n_in]], *([pl.program_id(0)] if with_j else []))
        for o, r in zip(refs[n_in:], res):
            o[...] = r.astype(o.dtype)

    return pl.pallas_call(
        body, name=name, grid=(ncol, M // tm),
        in_specs=[_ew_spec(op, tm, ncol) for op in ops],
        out_specs=[pl.BlockSpec((tm, w // ncol if cb else w), lambda j, i, cb=cb: (i, j if cb else 0)) for w, _, cb in outs],
        out_shape=[jax.ShapeDtypeStruct((M, w), dt) for w, dt, _ in outs],
        compiler_params=_params(("parallel", "parallel")),
    )(*[op[0] for op in ops])


def _ew_bwd(f, ops, cts, diff, *, tm, ncol=1, name, gdt=None, with_j=False):
    M = next(op[0].shape[0] for op in ops if op[1])
    tm = min(tm, M)
    n_in, n_ct = len(ops), len(cts)
    gdt = gdt or [F32] * len(diff)

    def body(*refs):
        j, i = pl.program_id(0), pl.program_id(1)
        vals = [r[...].astype(F32) for r in refs[:n_in]]

        def fd(*dv):
            full = list(vals)
            for idx, v in zip(diff, dv):
                full[idx] = v
            return tuple(f(*full, *([j] if with_j else [])))

        _, vjp = jax.vjp(fd, *[vals[idx] for idx in diff])
        gs = vjp(tuple(r[...].astype(F32) for r in refs[n_in:n_in + n_ct]))
        for idx, g, gref in zip(diff, gs, refs[n_in + n_ct:]):
            _, rt, cb = ops[idx][:3]
            if rt:
                gref[...] = g.astype(gref.dtype)
            else:
                first = (i == 0) if cb else jnp.logical_and(i == 0, j == 0)

                @pl.when(first)
                def _(gref=gref):
                    gref[...] = jnp.zeros_like(gref)

                gref[...] += g

    def g_spec(op):
        arr, rt, cb, start, width = _op(op)
        return _ew_spec((jax.ShapeDtypeStruct((arr.shape[0], width), F32), rt, cb), tm, ncol) if rt else _ew_spec(op, tm, ncol)

    def g_shape(op, dt):
        arr, rt, cb, start, width = _op(op)
        return jax.ShapeDtypeStruct((M, width), dt) if rt else jax.ShapeDtypeStruct(arr.shape, F32)

    return pl.pallas_call(
        body, name=name, grid=(ncol, M // tm),
        in_specs=[_ew_spec(op, tm, ncol) for op in ops]
        + [pl.BlockSpec((tm, a.shape[1] // ncol if cb else a.shape[1]), lambda j, i, cb=cb: (i, j if cb else 0)) for a, cb in cts],
        out_specs=[g_spec(ops[idx]) for idx in diff],
        out_shape=[g_shape(ops[idx], dt) for idx, dt in zip(diff, gdt)],
        compiler_params=_params(("arbitrary", "arbitrary")),
    )(*[op[0] for op in ops], *[a for a, _ in cts])


def _silu(x):
    return x * jax.nn.sigmoid(x)


def _f_normmod(x, ln, sc, bsc, sh, bsh):
    y = x * lax.rsqrt(jnp.mean(x * x, axis=-1, keepdims=True) + EPS) * ln
    return y * (1.0 + (sc + bsc)) + (sh + bsh), x


def _f_resgate(x, br, gt, bgt):
    return (x + (gt + bgt) * br,)


def _f_swiglu(gate, up):
    return (_silu(gate) * up,)


def _f_final(x, tgt, ln):
    y = x * lax.rsqrt(jnp.mean(x * x, axis=-1, keepdims=True) + EPS) * ln
    e = y - tgt
    return (jnp.broadcast_to(0.5 * jnp.mean(e * e, axis=-1, keepdims=True), (x.shape[0], 128)),)


def _f_bg(ba, alog, dt):
    col = lax.broadcasted_iota(jnp.int32, ba.shape, 1)
    z = ba + dt
    sp = jnp.maximum(z, 0.0) + jnp.log(1.0 + jnp.exp(-jnp.abs(z)))
    return (jnp.where(col < 8, jax.nn.sigmoid(ba), jnp.where(col < 16, -jnp.exp(alog) * sp, 0.0)),)


def _l2n(x):
    return x * lax.rsqrt(jnp.sum(x * x, axis=-1, keepdims=True) + EPS)


def _f_dnpre(c, j):
    a = _silu(c)
    return (jnp.where(j < 2 * DN_HEADS, _l2n(a) * jnp.where(j < DN_HEADS, DN_HD ** -0.5, 1.0), a),)


def _f_dnpost(o, z, nw):
    return (o * lax.rsqrt(jnp.mean(o * o, axis=-1, keepdims=True) + EPS) * nw * _silu(z),)


CONV_BW = 512


def _conv_fwd(x, w, Cw, name, gather_src=None):
    T = x.shape[0]
    tm, bw = 512, CONV_BW
    carry = gather_src is not None
    steps = (Cw // bw) * (T // tm)

    def body(*refs):
        x_ref, h_ref, w_ref = refs[:3]
        o_ref = refs[3 + carry]
        i = pl.program_id(1)
        if carry:
            step = pl.program_id(0) * (T // tm) + i
            start, forward, finish = _gather_plan(refs[3], refs[5], *refs[6:])
            pl.when(step == 0)(start)
            pl.when(step == steps * FORWARD_AT // 8)(forward)
        cur, halo, wv = x_ref[...], h_ref[...], w_ref[...]
        halo = jnp.where(i > 0, halo, 0.0)
        row = lax.broadcasted_iota(jnp.int32, (8, bw), 0)
        acc = wv[3:4, :] * cur
        for s in (1, 2, 3):
            r = pltpu.roll(cur, s, 0)
            top = jnp.where(row < s, pltpu.roll(halo, s, 0), r[:8])
            acc += wv[3 - s:4 - s, :] * jnp.concatenate([top, r[8:]], axis=0)
        o_ref[...] = acc
        if carry:
            pl.when(step == steps - 1)(finish)

    res = pl.pallas_call(
        body, name=name, grid=(Cw // bw, T // tm),
        in_specs=[pl.BlockSpec((tm, bw), lambda j, i: (i, j)),
                  pl.BlockSpec((8, bw), lambda j, i: (jnp.maximum(i * (tm // 8) - 1, 0), j)),
                  pl.BlockSpec((4, bw), lambda j, i: (0, j))] + [ANY] * carry,
        out_specs=[pl.BlockSpec((tm, bw), lambda j, i: (i, j))] + [ANY] * carry,
        out_shape=[jax.ShapeDtypeStruct((T, Cw), F32)] + ([jax.ShapeDtypeStruct((N_DEV,) + gather_src.shape, gather_src.dtype)] if carry else []),
        scratch_shapes=GATHER_SEMS if carry else [],
        compiler_params=_params(("arbitrary", "arbitrary") if carry else ("parallel", "parallel")),
    )(x, x, w, *([gather_src] if carry else []))
    return res if carry else res[0]


def _conv_bwd(x, dy, w, name):
    T, Cw = dy.shape
    tm, bw = 512, CONV_BW
    nt = T // tm

    def body(x_ref, h_ref, dy_ref, n_ref, w_ref, dx_ref, dw_ref):
        i = pl.program_id(1)
        cur, dcur, wv = x_ref[...], dy_ref[...], w_ref[...]
        halo = jnp.where(i > 0, h_ref[...], 0.0)
        nxt = jnp.where(i < nt - 1, n_ref[...], 0.0)
        row = lax.broadcasted_iota(jnp.int32, (8, bw), 0)

        @pl.when(i == 0)
        def _():
            dw_ref[...] = jnp.zeros_like(dw_ref)

        dx = wv[3:4, :] * dcur
        dw_ref[3:4, :] += jnp.sum(dcur * cur, axis=0, keepdims=True)
        for s in (1, 2, 3):
            r = pltpu.roll(cur, s, 0)
            top = jnp.where(row < s, pltpu.roll(halo, s, 0), r[:8])
            xs = jnp.concatenate([top, r[8:]], axis=0)
            dw_ref[3 - s:4 - s, :] += jnp.sum(dcur * xs, axis=0, keepdims=True)
            rf = pltpu.roll(dcur, tm - s, 0)
            bot = jnp.where(row >= 8 - s, pltpu.roll(nxt, 8 - s, 0), rf[tm - 8:])
            dx += wv[3 - s:4 - s, :] * jnp.concatenate([rf[:tm - 8], bot], axis=0)
        dx_ref[...] = dx

    return pl.pallas_call(
        body, name=name, grid=(Cw // bw, nt),
        in_specs=[pl.BlockSpec((tm, bw), lambda j, i: (i, j)),
                  pl.BlockSpec((8, bw), lambda j, i: (jnp.maximum(i * (tm // 8) - 1, 0), j)),
                  pl.BlockSpec((tm, bw), lambda j, i: (i, j)),
                  pl.BlockSpec((8, bw), lambda j, i: (jnp.minimum((i + 1) * (tm // 8), T // 8 - 1), j)),
                  pl.BlockSpec((4, bw), lambda j, i: (0, j))],
        out_specs=[pl.BlockSpec((tm, bw), lambda j, i: (i, j)), pl.BlockSpec((4, bw), lambda j, i: (0, j))],
        out_shape=[jax.ShapeDtypeStruct((T, Cw), F32), jax.ShapeDtypeStruct((4, Cw), F32)],
        compiler_params=_params(("arbitrary", "arbitrary")),
    )(x, x, dy, dy, w)


def _dot(a, b):
    return jnp.dot(a, b, precision=HI, preferred_element_type=F32)


def _dot_t(a, b):
    return lax.dot_general(a, b, (((1,), (1,)), ((), ())), precision=HI, preferred_element_type=F32)


def _bdot(a, b):
    return jnp.dot(a.astype(BF16), b.astype(BF16), preferred_element_type=F32)


def _bdot_t(a, b):
    return lax.dot_general(a.astype(BF16), b.astype(BF16), (((1,), (1,)), ((), ())), preferred_element_type=F32)


def _each(f, *lists):
    return [f(*a) for a in zip(*lists)]


@jax.custom_vjp
def _unit_lower_inverses(ps):
    C = ps[0].shape[0]
    dist = jnp.bitwise_xor(lax.broadcasted_iota(jnp.int32, (C, C), 0), lax.broadcasted_iota(jnp.int32, (C, C), 1))
    ns = [jnp.where(dist < 8, p, 0.0) for p in ps]
    tis = [jnp.where(dist == 0, 1.0, 0.0) + n for n in ns]
    for _ in range(2):
        ns = _each(lambda n: _bdot(n, n), ns)
        tis = _each(lambda t, n: t + _bdot(t, n), tis, ns)
    b = 8
    while b < C:
        mids = [jnp.where(jnp.logical_and(dist >= b, dist < 2 * b), p, 0.0) for p in ps]
        halves = _each(_bdot, mids, tis)
        tis = _each(lambda t, h: t + _bdot(t, h), tis, halves)
        b *= 2
    return tis


def _uli_fwd(ps):
    tis = _unit_lower_inverses(ps)
    return tis, tis


def _uli_bwd(tis, cts):
    tt = lambda a, b: lax.dot_general(a, b, (((0,), (0,)), ((), ())), precision=HI, preferred_element_type=F32)
    half = _each(tt, tis, cts)
    return (_each(_dot_t, half, tis),)


_unit_lower_inverses.defvjp(_uli_fwd, _uli_bwd)


@jax.custom_vjp
def _kept_inverses(ps, tis):
    return tis


_kept_inverses.defvjp(lambda ps, tis: (tis, tis), lambda tis, cts: (_uli_bwd(tis, cts)[0], [jnp.zeros_like(t) for t in tis]))


def _chunk_f(qs, ks, vs, bg, Ss, hs, kept=None):
    C = CHUNK
    lane = lax.broadcasted_iota(jnp.int32, (C, 128), 1)
    betas = [jnp.sum(jnp.where(lane == h, bg, 0.0), axis=1, keepdims=True) for h in hs]
    gs = [jnp.sum(jnp.where(lane == h + 8, bg, 0.0), axis=1, keepdims=True) for h in hs]
    ri = lax.broadcasted_iota(jnp.int32, (C, C), 0)
    ci = lax.broadcasted_iota(jnp.int32, (C, C), 1)
    causal, strict, eye = ri >= ci, ri > ci, ri == ci
    g_rows = _each(lambda g: jnp.sum(jnp.where(eye, g, 0.0), axis=0, keepdims=True), gs)
    gc_cols = _each(lambda gr: jnp.sum(jnp.where(causal, gr, 0.0), axis=1, keepdims=True), g_rows)
    gc_rows = _each(lambda g: jnp.sum(jnp.where(ri <= ci, g, 0.0), axis=0, keepdims=True), gs)
    gc_lasts = _each(lambda g: jnp.sum(g, axis=0, keepdims=True), gs)
    decays = _each(lambda c, r: jnp.exp(jnp.where(causal, c - r, NEG)), gc_cols, gc_rows)
    kbs = _each(jnp.multiply, ks, betas)
    vbs = _each(jnp.multiply, vs, betas)
    ps = _each(lambda kb, k, d: -jnp.where(strict, _bdot_t(kb, k) * d, 0.0), kbs, ks, decays)
    tis = _unit_lower_inverses(ps) if kept is None else _kept_inverses(ps, kept)
    egcs = _each(jnp.exp, gc_cols)
    ws = _each(lambda t, kb, e: _bdot(t, kb * e), tis, kbs, egcs)
    us = _each(_bdot, tis, vbs)
    intras = _each(lambda q, k, d: jnp.where(causal, _bdot_t(q, k) * d, 0.0), qs, ks, decays)
    kds = _each(lambda k, gl, gc: k * jnp.exp(gl - gc), ks, gc_lasts, gc_cols)
    vns = _each(lambda u, w, S: u - _bdot(w, S), us, ws, Ss)
    os_ = _each(lambda q, e, S, i, vn: _bdot(q * e, S) + _bdot(i, vn), qs, egcs, Ss, intras, vns)
    Sn = _each(lambda S, gl, kd, vn: S * jnp.exp(gl) + lax.dot_general(
        kd.astype(BF16), vn.astype(BF16), (((0,), (0,)), ((), ())), preferred_element_type=F32), Ss, gc_lasts, kds, vns)
    return (os_, Sn), tis


def _head_cols(ref, part):
    return [ref[:, part * DN_W + e * DN_HD:part * DN_W + (e + 1) * DN_HD] for e in range(DN_HEADS)]


def _chunk_fwd(qkv, bg, name, gather_src=None):
    T = qkv.shape[0]
    N = T // CHUNK
    H = DN_HEADS
    carry = gather_src is not None

    def body(*refs):
        x_ref, bg_ref = refs[:2]
        o_ref, s_ref, t_ref = refs[2 + carry:5 + carry]
        S = refs[5 + 2 * carry]
        n = pl.program_id(0)
        if carry:
            start, forward, finish = _gather_plan(refs[2], refs[6], *refs[8:])
            pl.when(n == 0)(start)
            pl.when(n == N * FORWARD_AT // 8)(forward)

        @pl.when(n == 0)
        def _():
            S[...] = jnp.zeros_like(S)

        s_all = S[...]
        s_ref[0] = s_all
        (os_, nxt), tis = _chunk_f(_head_cols(x_ref, 0), _head_cols(x_ref, 1), _head_cols(x_ref, 2), bg_ref[...],
                                   [s_all[e] for e in range(H)], list(range(H)))
        for e in range(H):
            o_ref[:, e * DN_HD:(e + 1) * DN_HD] = os_[e]
            S[e] = nxt[e]
            t_ref[0, e] = tis[e]
        if carry:
            pl.when(n == N - 1)(finish)

    gshape = [jax.ShapeDtypeStruct((N_DEV,) + gather_src.shape, gather_src.dtype)] if carry else []
    return pl.pallas_call(
        body, name=name, grid=(N,),
        in_specs=[pl.BlockSpec((CHUNK, 3 * DN_W), lambda n: (n, 0)), pl.BlockSpec((CHUNK, 128), lambda n: (n, 0))] + [ANY] * carry,
        out_specs=[pl.BlockSpec((CHUNK, DN_W), lambda n: (n, 0)), pl.BlockSpec((1, H, DN_HD, DN_HD), lambda n: (n, 0, 0, 0)),
                   pl.BlockSpec((1, H, CHUNK, CHUNK), lambda n: (n, 0, 0, 0))] + [ANY] * carry,
        out_shape=[jax.ShapeDtypeStruct((T, DN_W), F32), jax.ShapeDtypeStruct((N, H, DN_HD, DN_HD), F32),
                   jax.ShapeDtypeStruct((N, H, CHUNK, CHUNK), F32)] + gshape,
        scratch_shapes=[pltpu.VMEM((H, DN_HD, DN_HD), F32)] + (GATHER_SEMS if carry else []),
        compiler_params=_params(("arbitrary",)),
    )(qkv, bg, *([gather_src] if carry else []))


def _chunk_bwd(qkv, bg, s_saved, t_saved, do, name, a2a_srcs=()):
    T = qkv.shape[0]
    N = T // CHUNK
    H = DN_HEADS
    na = len(a2a_srcs)

    def body(*refs):
        x_ref, bg_ref, s_ref, t_ref, do_ref = refs[:5]
        dx_ref, dbg_ref = refs[5 + na:7 + na]
        dS = refs[7 + 2 * na]
        n = pl.program_id(0)
        if na:
            start, finish = _a2a_plan(refs[5:5 + na], refs[7 + na:7 + 2 * na], *refs[8 + 2 * na:])
            pl.when(n == 0)(start)

        @pl.when(n == 0)
        def _():
            dS[...] = jnp.zeros_like(dS)

        ds_all = dS[...]
        _, vjp, _ = jax.vjp(functools.partial(_chunk_f, hs=list(range(H)), kept=[t_ref[0, e] for e in range(H)]), _head_cols(x_ref, 0),
                            _head_cols(x_ref, 1), _head_cols(x_ref, 2), bg_ref[...], [s_ref[0, e] for e in range(H)], has_aux=True)
        dq, dk, dv, dbg, nxt = vjp(([do_ref[:, e * DN_HD:(e + 1) * DN_HD] for e in range(H)], [ds_all[e] for e in range(H)]))
        for part, g in enumerate((dq, dk, dv)):
            for e in range(H):
                dx_ref[:, part * DN_W + e * DN_HD:part * DN_W + (e + 1) * DN_HD] = g[e]
        for e in range(H):
            dS[e] = nxt[e]
        dbg_ref[...] = dbg
        if na:
            pl.when(n == N - 1)(finish)

    rev = lambda n: (N - 1 - n, 0)
    return pl.pallas_call(
        body, name=name, grid=(N,),
        in_specs=[pl.BlockSpec((CHUNK, 3 * DN_W), rev), pl.BlockSpec((CHUNK, 128), rev),
                  pl.BlockSpec((1, H, DN_HD, DN_HD), lambda n: (N - 1 - n, 0, 0, 0)),
                  pl.BlockSpec((1, H, CHUNK, CHUNK), lambda n: (N - 1 - n, 0, 0, 0)), pl.BlockSpec((CHUNK, DN_W), rev)] + [ANY] * na,
        out_specs=[pl.BlockSpec((CHUNK, 3 * DN_W), rev), pl.BlockSpec((CHUNK, 128), rev)] + [ANY] * na,
        out_shape=[jax.ShapeDtypeStruct((T, 3 * DN_W), F32), jax.ShapeDtypeStruct((T, 128), F32)]
        + [jax.ShapeDtypeStruct(a.shape, a.dtype) for a in a2a_srcs],
        scratch_shapes=[pltpu.VMEM((H, DN_HD, DN_HD), F32)] + (_a2a_sems(na) if na else []),
        compiler_params=_params(("arbitrary",)),
    )(qkv, bg, s_saved, t_saved, do, *a2a_srcs)


GRP = AT_QH // AT_KVH


def _attn_f(q, kp, kc, vp, vc, sinks, cos_c, sin_c, cos_p, sin_p, rot, has_prev):
    def rope(x, c, s):
        return x * c + _dot(x, rot) * s

    kcr, kpr = rope(kc, cos_c, sin_c), rope(kp, cos_p, sin_p)
    qrs = [rope(qe, cos_c, sin_c) for qe in q]
    r = lax.broadcasted_iota(jnp.int32, (WINDOW, WINDOW), 0)
    j = lax.broadcasted_iota(jnp.int32, (WINDOW, WINDOW), 1)
    in_c, in_p = j <= r, jnp.logical_and(j > r, has_prev)
    lane = lax.broadcasted_iota(jnp.int32, (1, 128), 1)
    scs = [jnp.where(in_c, _bdot_t(qr, kcr) * (AT_HD ** -0.5), NEG) for qr in qrs]
    sps = [jnp.where(in_p, _bdot_t(qr, kpr) * (AT_HD ** -0.5), NEG) for qr in qrs]
    snk = [jnp.sum(jnp.where(lane == 0, s, 0.0), axis=1, keepdims=True) for s in sinks]
    ms = _each(lambda sc, sp, s: jnp.maximum(jnp.maximum(jnp.max(sc, axis=1, keepdims=True), jnp.max(sp, axis=1, keepdims=True)), s),
               scs, sps, snk)
    pcs = _each(lambda sc, m: jnp.exp(sc - m), scs, ms)
    pps = _each(lambda sp, m: jnp.exp(sp - m), sps, ms)
    dens = _each(lambda pc, pp, s, m: jnp.sum(pc, axis=1, keepdims=True) + jnp.sum(pp, axis=1, keepdims=True) + jnp.exp(s - m),
                 pcs, pps, snk, ms)
    return tuple(_each(lambda pc, pp, den: (_bdot(pc, vc) + _bdot(pp, vp)) / den, pcs, pps, dens))


def _attn_specs(nb):
    qs = pl.BlockSpec((GRP, WINDOW, AT_HD), lambda g, n: (g, n, 0))
    kc = pl.BlockSpec((1, WINDOW, AT_HD), lambda g, n: (g, n, 0))
    kp = pl.BlockSpec((1, WINDOW, AT_HD), lambda g, n: (g, jnp.maximum(n - 1, 0), 0))
    tc = pl.BlockSpec((WINDOW, AT_HD), lambda g, n: (n, 0))
    tp = pl.BlockSpec((WINDOW, AT_HD), lambda g, n: (jnp.maximum(n - 1, 0), 0))
    sk = pl.BlockSpec((GRP, 1, 128), lambda g, n: (g, 0, 0))
    rt = pl.BlockSpec((AT_HD, AT_HD), lambda g, n: (0, 0))
    return qs, kc, kp, tc, tp, sk, rt


def _attn_fwd(q, k, v, sinks, cos, sin, rot, name, gather_src=None):
    T = q.shape[1]
    nb = T // WINDOW
    qs, kc, kp, tc, tp, sk, rt = _attn_specs(nb)
    carry = gather_src is not None

    def body(*refs):
        q_ref, kp_ref, kc_ref, vp_ref, vc_ref, sk_ref, cc_ref, sc_ref, cp_ref, sp_ref, rot_ref = refs[:11]
        o_ref = refs[11 + carry]
        n = pl.program_id(1)
        if carry:
            step = pl.program_id(0) * nb + n
            start, forward, finish = _gather_plan(refs[11], refs[13], *refs[14:])
            pl.when(step == 0)(start)
            pl.when(step == AT_KVH * nb * FORWARD_AT // 8)(forward)
        heads = range(GRP)
        outs = _attn_f(tuple(q_ref[e] for e in heads), kp_ref[0], kc_ref[0], vp_ref[0], vc_ref[0], tuple(sk_ref[e] for e in heads), cc_ref[...], sc_ref[...],
                       cp_ref[...], sp_ref[...], rot_ref[...], n > 0)
        for e in range(GRP):
            o_ref[e] = outs[e]
        if carry:
            pl.when(step == AT_KVH * nb - 1)(finish)

    gshape = [jax.ShapeDtypeStruct((N_DEV,) + gather_src.shape, gather_src.dtype)] if carry else []
    return pl.pallas_call(
        body, name=name, grid=(AT_KVH, nb),
        in_specs=[qs, kp, kc, kp, kc, sk, tc, tc, tp, tp, rt] + [ANY] * carry, out_specs=[qs] + [ANY] * carry,
        out_shape=[jax.ShapeDtypeStruct(q.shape, F32)] + gshape,
        scratch_shapes=GATHER_SEMS if carry else [],
        compiler_params=_params(("arbitrary", "arbitrary") if carry else ("parallel", "parallel")),
    )(q, k, k, v, v, sinks, cos, sin, cos, sin, rot, *([gather_src] if carry else []))


def _attn_bwd(q, k, v, sinks, cos, sin, rot, do, name, a2a_srcs=()):
    T = q.shape[1]
    nb = T // WINDOW
    qs, kc, kp, tc, tp, sk, rt = _attn_specs(nb)
    na = len(a2a_srcs)

    def body(*refs):
        q_ref, kp_ref, kc_ref, vp_ref, vc_ref, sk_ref, cc_ref, sc_ref, cp_ref, sp_ref, rot_ref, do_ref = refs[:12]
        dq_ref, dkp_ref, dkc_ref, dvp_ref, dvc_ref, dsk_ref = refs[12 + na:18 + na]
        n = pl.program_id(1)
        if na:
            step = pl.program_id(0) * nb + n
            start, finish = _a2a_plan(refs[12:12 + na], refs[18 + na:18 + 2 * na], *refs[18 + 2 * na:])
            pl.when(step == 0)(start)
        f = functools.partial(_attn_f, cos_c=cc_ref[...], sin_c=sc_ref[...], cos_p=cp_ref[...], sin_p=sp_ref[...],
                              rot=rot_ref[...], has_prev=n > 0)
        heads = range(GRP)
        _, vjp = jax.vjp(f, tuple(q_ref[e] for e in heads), kp_ref[0], kc_ref[0], vp_ref[0], vc_ref[0], tuple(sk_ref[e] for e in heads))
        dq, dkp, dkc, dvp, dvc, dsk = vjp(tuple(do_ref[e] for e in heads))
        dkp_ref[0], dkc_ref[0], dvp_ref[0], dvc_ref[0] = dkp, dkc, dvp, dvc

        @pl.when(n == 0)
        def _():
            dsk_ref[...] = jnp.zeros_like(dsk_ref)

        for e in heads:
            dq_ref[e] = dq[e]
            dsk_ref[e] += dsk[e]
        if na:
            pl.when(step == AT_KVH * nb - 1)(finish)

    return pl.pallas_call(
        body, name=name, grid=(AT_KVH, nb),
        in_specs=[qs, kp, kc, kp, kc, sk, tc, tc, tp, tp, rt, qs] + [ANY] * na, out_specs=[qs, kc, kc, kc, kc, sk] + [ANY] * na,
        out_shape=[jax.ShapeDtypeStruct(q.shape, F32)] + [jax.ShapeDtypeStruct(k.shape, F32)] * 4 + [jax.ShapeDtypeStruct(sinks.shape, F32)]
        + [jax.ShapeDtypeStruct(a.shape, a.dtype) for a in a2a_srcs],
        scratch_shapes=_a2a_sems(na) if na else [],
        compiler_params=_params(("arbitrary", "arbitrary")),
    )(q, k, k, v, v, sinks, cos, sin, cos, sin, rot, do, *a2a_srcs)


def _kv_combine(dc, dp, name):
    T = dc.shape[1]
    R = 8 * WINDOW
    ns = T // R

    def body(c_ref, p_ref, q_ref, o_ref):
        n = pl.program_id(1)
        tail = jnp.where(n < ns - 1, q_ref[0], 0.0)
        o_ref[0] = c_ref[0] + jnp.concatenate([p_ref[0, WINDOW:, :], tail], axis=0)

    return pl.pallas_call(
        body, name=name, grid=(AT_KVH, ns),
        in_specs=[pl.BlockSpec((1, R, AT_HD), lambda g, n: (g, n, 0)), pl.BlockSpec((1, R, AT_HD), lambda g, n: (g, n, 0)),
                  pl.BlockSpec((1, WINDOW, AT_HD), lambda g, n: (g, jnp.minimum((n + 1) * 8, T // WINDOW - 1), 0))],
        out_specs=pl.BlockSpec((1, R, AT_HD), lambda g, n: (g, n, 0)),
        out_shape=jax.ShapeDtypeStruct((AT_KVH, T, AT_HD), F32),
        compiler_params=_params(("parallel", "parallel")),
    )(dc, dp, dp)


def _place():
    x, y, c = lax.axis_index("x"), lax.axis_index("y"), lax.axis_index("c")
    return x, y, c, 4 * x + 2 * y + c


def _gather_plan(s_ref, o_ref, send_sems, recv_sems, lsem):
    x, y, c, _ = _place()
    me, sib = (x, y, c), (x, y, 1 - c)
    chips = [(1 - x, y), (x, 1 - y), (1 - x, 1 - y)]

    def copy(k, block, to, src_ref=None):
        slab = o_ref.at[4 * block[0] + 2 * block[1] + block[2]]
        return pltpu.make_async_remote_copy(src_ref=slab if src_ref is None else src_ref, dst_ref=slab,
                                            send_sem=send_sems.at[k], recv_sem=recv_sems.at[k], device_id=to, device_id_type=MESH)

    mine = pltpu.make_async_copy(s_ref, o_ref.at[4 * x + 2 * y + c], lsem)
    first = [copy(0, me, sib, s_ref)] + [copy(1 + j, me, (*chip, c), s_ref) for j, chip in enumerate(chips)]
    passed = [copy(4 + j, (*chip, c), sib) for j, chip in enumerate(chips)]

    def start():
        mine.start()
        for cp in first:
            cp.start()

    def forward():
        for j, chip in enumerate(chips):
            copy(1 + j, (*chip, c), me).wait_recv()
            passed[j].start()

    def finish():
        copy(0, sib, me).wait_recv()
        for j, chip in enumerate(chips):
            copy(4 + j, (*chip, 1 - c), me).wait_recv()
        for cp in first + passed:
            cp.wait_send()
        mine.wait()

    return start, forward, finish


GATHER_SEMS = [pltpu.SemaphoreType.DMA((N_DEV - 1,)), pltpu.SemaphoreType.DMA((N_DEV - 1,)), pltpu.SemaphoreType.DMA]


def _gather(src, name):
    def body(s_ref, o_ref, send_sems, recv_sems, lsem):
        for phase in _gather_plan(s_ref, o_ref, send_sems, recv_sems, lsem):
            phase()

    return pl.pallas_call(
        body, name=name, in_specs=[ANY], out_specs=ANY, out_shape=jax.ShapeDtypeStruct((N_DEV,) + src.shape, src.dtype),
        scratch_shapes=GATHER_SEMS,
    )(src)


def _pair_swap(src, name):
    def body(s_ref, o_ref, send_sem, recv_sem):
        x, y, c, _ = _place()
        cp = pltpu.make_async_remote_copy(src_ref=s_ref.at[1 - c], dst_ref=o_ref, send_sem=send_sem, recv_sem=recv_sem,
                                          device_id=(x, y, 1 - c), device_id_type=MESH)
        cp.start()
        cp.wait()

    return pl.pallas_call(
        body, name=name, in_specs=[ANY], out_specs=ANY, out_shape=jax.ShapeDtypeStruct(src.shape[1:], src.dtype),
        scratch_shapes=[pltpu.SemaphoreType.DMA, pltpu.SemaphoreType.DMA],
    )(src)


def _a2a_plan(s_refs, o_refs, send_sems, recv_sems, lsems):
    x, y, c, _ = _place()
    chip = 2 * x + y
    local, remote = [], []
    for a, (s_ref, o_ref) in enumerate(zip(s_refs, o_refs)):
        local.append(pltpu.make_async_copy(s_ref.at[chip], o_ref.at[chip], lsems.at[a]))
        for k in (1, 2, 3):
            px, py = x ^ (k >> 1), y ^ (k & 1)
            remote.append(pltpu.make_async_remote_copy(
                src_ref=s_ref.at[2 * px + py], dst_ref=o_ref.at[chip], send_sem=send_sems.at[3 * a + k - 1],
                recv_sem=recv_sems.at[3 * a + k - 1], device_id=(px, py, c), device_id_type=MESH))

    def start():
        for cp in local + remote:
            cp.start()

    def finish():
        for cp in remote + local:
            cp.wait()

    return start, finish


def _a2a_sems(n):
    return [pltpu.SemaphoreType.DMA((3 * n,)), pltpu.SemaphoreType.DMA((3 * n,)), pltpu.SemaphoreType.DMA((n,))]


def _add(a, b, name):
    R, C = a.shape
    tr = _pick(R, (512, 256, 128, 64, 32, 16))

    def body(a_ref, b_ref, o_ref):
        o_ref[...] = (a_ref[...].astype(F32) + b_ref[...].astype(F32)).astype(o_ref.dtype)

    s2 = pl.BlockSpec((tr, C), lambda i: (i, 0))
    return pl.pallas_call(body, name=name, grid=(R // tr,), in_specs=[s2, s2], out_specs=s2,
                          out_shape=jax.ShapeDtypeStruct((R, C), a.dtype), compiler_params=_params(("parallel",)))(a, b)


def _chip_sums(slabs, name):
    _, R, C = slabs.shape
    by_core = jnp.transpose(slabs.reshape(4, 2, R, C), (1, 0, 2, 3))
    theirs = _pair_swap(by_core, "swap_" + name)
    mine = lax.dynamic_index_in_dim(by_core, lax.axis_index("c"), axis=0, keepdims=False)
    return _add(mine.reshape(4 * R, C), theirs.reshape(4 * R, C), "add_" + name).reshape(4, R, C)


def _adamw(w, m, v, parts, name):
    R, C = w.shape
    P = parts.shape[0]
    tr = _pick(R, (256, 128, 64, 32, 16, 8))
    c1, c2 = 1.0 - B1 ** STEP, 1.0 - B2 ** STEP

    def body(w_ref, m_ref, v_ref, p_ref, g_ref, d_ref, nm_ref, nv_ref):
        g = p_ref[0].astype(F32)
        for i in range(1, P):
            g = g + p_ref[i].astype(F32)
        wv = w_ref[...]
        nm = B1 * m_ref[...] + (1.0 - B1) * g
        nv = B2 * v_ref[...] + (1.0 - B2) * (g * g)
        g_ref[...] = g
        nm_ref[...] = nm
        nv_ref[...] = nv
        d_ref[...] = -LR * ((nm / c1) / (jnp.sqrt(nv / c2) + AEPS) + WD * wv)

    s2 = pl.BlockSpec((tr, C), lambda i: (i, 0))
    return pl.pallas_call(
        body, name=name, grid=(R // tr,),
        in_specs=[s2, s2, s2, pl.BlockSpec((P, tr, C), lambda i: (0, i, 0))], out_specs=[s2] * 4,
        out_shape=[jax.ShapeDtypeStruct((R, C), F32)] * 4,
        compiler_params=_params(("parallel",)),
    )(w, m, v, parts)


def _colsum(a, name):
    def body(a_ref, o_ref):
        o_ref[...] = jnp.broadcast_to(jnp.sum(a_ref[...], axis=0, keepdims=True), o_ref.shape)

    return pl.pallas_call(body, name=name, out_shape=jax.ShapeDtypeStruct((8, 128), F32))(a)


def _rows128(a):
    f = a.reshape(-1)
    return jnp.pad(f, (0, (-f.shape[0]) % 128)).reshape(-1, 128)


def _to_aligned(w):
    return jnp.concatenate([w[..., 0:4096], w[..., 4112:5392], w[..., 4096:4112],
                            jnp.zeros(w.shape[:-1] + (IN_PAD - IN_COLS,), w.dtype)], axis=-1)


def _from_aligned(w):
    return jnp.concatenate([w[..., 0:4096], w[..., 5376:5392], w[..., 4096:5376]], axis=-1)


def kernel(x, c, ln_mix, ln_ffn, w_ada, b_ada, w_in, dn_conv_w, dn_a_log, dn_dt_bias, dn_norm_w, attn_sinks, w_out, w_gate_up, w_down, ln_final, loss_target, m_ln_mix, m_ln_ffn, m_w_ada, m_b_ada, m_w_in, m_dn_conv_w, m_dn_a_log, m_dn_dt_bias, m_dn_norm_w, m_attn_sinks, m_w_out, m_w_gate_up, m_w_down, m_ln_final, v_ln_mix, v_ln_ffn, v_w_ada, v_b_ada, v_w_in, v_dn_conv_w, v_dn_a_log, v_dn_dt_bias, v_dn_norm_w, v_attn_sinks, v_w_out, v_w_gate_up, v_w_down, v_ln_final):
    T = x.shape[1]
    L = ln_mix.shape[0]
    me = 4 * lax.axis_index("x") + 2 * lax.axis_index("y") + lax.axis_index("c")
    xs = x[0]
    tgt = loss_target[0]

    w_in_layout = lambda g: _to_aligned(jnp.transpose(g, (1, 0, 2)).reshape(D, IN_COLS))
    W_in = [w_in_layout(_gather(w_in[0].astype(BF16), "ag_w_in0"))] + [None] * (L - 1)
    W_out = None
    g_cv = _gather(dn_conv_w.reshape(L * CONV_K, -1), "ag_conv").reshape(N_DEV, L, CONV_K, -1)
    c_all = _gather(jnp.pad(c, ((0, 7), (0, 0))), "ag_c")[:, 0, :]
    W_gate, W_up, W_dn = [None] * L, [None] * L, [None] * L
    W_cv = [jnp.transpose(g_cv[:, l], (1, 0, 2)).reshape(CONV_K, 3 * DN_W) for l in range(L)]

    c_act = _ew_fwd(lambda v: (_silu(v),), [(jnp.pad(c_all, ((0, 8), (0, 0))), True, False)], [(D, F32, False)], tm=16, name="c_act")[0]
    mods = []
    for l in range(L):
        ms = _mm(c_act, w_ada[l], "nn", F32, f"mod_mm{l}")
        ga = _gather(ms, f"ag_mod{l}")
        mods.append(lax.dynamic_index_in_dim(ga, me, axis=1, keepdims=False).reshape(1, 6 * D))
    row = lambda a: a.reshape(1, -1)
    seg = lambda a, i: a[:, i * D:(i + 1) * D]

    half = AT_HD // 2
    inv_freq = 10000.0 ** (-jnp.arange(half, dtype=F32) * 2.0 / AT_HD)
    ang = jnp.arange(T, dtype=jnp.int32).astype(F32)[:, None] * inv_freq[None, :]
    cos = jnp.concatenate([jnp.cos(ang)] * 2, axis=-1)
    sin = jnp.concatenate([jnp.sin(ang)] * 2, axis=-1)
    ii = jnp.arange(AT_HD)
    rot = jnp.where(ii[:, None] == ii[None, :] + half, -1.0, 0.0) + jnp.where(ii[:, None] + half == ii[None, :], 1.0, 0.0)
    rot = rot.astype(F32)
    heads = lambda a, nh: jnp.transpose(a.reshape(T, nh, AT_HD), (1, 0, 2))
    unheads = lambda a: jnp.transpose(a, (1, 0, 2)).reshape(T, -1)
    pad16 = lambda a: jnp.pad(row(a), ((0, 0), (8, 128 - 16)))

    saved = []
    xc = xs
    for l in range(L):
        mod, bmod = mods[l], row(b_ada[l])
        s = {"x": xc}
        nm_ops = lambda xx, ln, a, b: [(xx, True, False), (row(ln), False, False), (seg(mod, a), False, False),
                                       (seg(bmod, a), False, False), (seg(mod, b), False, False), (seg(bmod, b), False, False)]
        h1 = _ew_fwd(lambda *a: _f_normmod(*a)[:1], nm_ops(xc, ln_mix[l], 1, 0), [(D, BF16, False)], tm=256, name=f"normmod1_{l}")[0]
        if l + 1 < L:
            proj, g_in = _mm(h1, W_in[l], "nn", F32, f"mm_in{l}", gather_srcs=[w_in[l + 1].astype(BF16)])
            W_in[l + 1] = w_in_layout(g_in)
        else:
            proj = _mm(h1, W_in[l], "nn", F32, f"mm_in{l}")
        aq, ak, av = proj[:, 4096:5120], proj[:, 5120:5248], proj[:, 5248:5376]
        if l == 0:
            conv, g_out = _conv_fwd(proj, W_cv[l], 3 * DN_W, f"conv{l}", gather_src=w_out.astype(BF16).reshape(-1, D))
            W_out = [g_out.reshape(N_DEV, L, -1, D)[:, i].reshape(D, D) for i in range(L)]
        else:
            conv = _conv_fwd(proj, W_cv[l], 3 * DN_W, f"conv{l}")
        pre_ops = [(conv, True, True)]
        qkvn = _ew_fwd(_f_dnpre, pre_ops, [(3 * DN_W, F32, True)], tm=2048, ncol=3 * DN_HEADS, name=f"dnpre{l}", with_j=True)[0]
        bg_ops = [(proj, True, False, (5376, 128)), (pad16(dn_a_log[l]), False, False), (pad16(dn_dt_bias[l]), False, False)]
        bg = _ew_fwd(_f_bg, bg_ops, [(128, F32, False)], tm=1024, name=f"bg{l}")[0]
        o, s_saved, t_saved, g_gu = _chunk_fwd(qkvn, bg, f"chunk{l}", gather_src=w_gate_up[l].astype(BF16))
        W_gate[l] = jnp.transpose(g_gu[:4], (1, 0, 2)).reshape(D, FFN)
        W_up[l] = jnp.transpose(g_gu[4:], (1, 0, 2)).reshape(D, FFN)
        post_ops = [(o, True, True), (proj, True, True, (3072, DN_W)), (row(dn_norm_w[l]), False, False)]
        dn_out = _ew_fwd(_f_dnpost, post_ops, [(DN_W, BF16, True)], tm=2048, ncol=DN_HEADS, name=f"dnpost{l}")[0]
        qh, kh, vh = heads(aq, AT_QH), heads(ak, AT_KVH), heads(av, AT_KVH)
        sk = jnp.broadcast_to(attn_sinks[l][:, None, None], (AT_QH, 1, 128))
        at_o, g_dn = _attn_fwd(qh, kh, vh, sk, cos, sin, rot, f"attn{l}", gather_src=w_down[l].astype(BF16))
        W_dn[l] = g_dn.reshape(FFN, D)
        at_out = unheads(at_o).astype(BF16)
        x1, mix = _mm(at_out, W_out[l][DN_W:], "nn", F32, f"mm_out_at{l}", acc_in=_mm(dn_out, W_out[l][:DN_W], "nn", F32, f"mm_out_dn{l}"),
                      gate=(xc, seg(mod, 2), seg(bmod, 2)))
        rg_ops = lambda xx, br, a: [(xx, True, False), (br, True, False), (seg(mod, a), False, False), (seg(bmod, a), False, False)]
        h2 = _ew_fwd(lambda *a: _f_normmod(*a)[:1], nm_ops(x1, ln_ffn[l], 4, 3), [(D, BF16, False)], tm=256, name=f"normmod2_{l}")[0]
        gate = _mm(h2, W_gate[l], "nn", BF16, f"mm_gate{l}")
        up = _mm(h2, W_up[l], "nn", BF16, f"mm_up{l}")
        sw_ops = [(gate, True, True), (up, True, True)]
        act = _ew_fwd(_f_swiglu, sw_ops, [(FFN, BF16, True)], tm=2048, ncol=11, name=f"swiglu{l}")[0]
        x2, down = _mm(act, W_dn[l], "nn", F32, f"mm_down{l}", gate=(x1, seg(mod, 5), seg(bmod, 5)))
        s.update(h1=h1, proj=proj, qkvn=qkvn, bg=bg, s_saved=s_saved, t_saved=t_saved, qh=qh, kh=kh, vh=vh, sk=sk, dn_out=dn_out, at_out=at_out,
                 h2=h2, act=act, bg_ops=bg_ops, post_ops=post_ops, pre_ops=pre_ops, sw_ops=sw_ops,
                 nm1=nm_ops(xc, ln_mix[l], 1, 0), nm2=nm_ops(x1, ln_ffn[l], 4, 3), rg1=rg_ops(xc, mix, 2), rg2=rg_ops(x1, down, 5))
        saved.append(s)
        xc = x2

    fin_ops = [(xc, True, False), (tgt, True, False), (row(ln_final), False, False)]
    lrow = _ew_fwd(_f_final, fin_ops, [(128, F32, False)], tm=256, name="loss_rows")[0]
    loss = lax.psum(_colsum(lrow, "loss_sum")[0, 0], ("x", "y", "c"))
    dx, d_ln_final = _ew_bwd(_f_final, fin_ops, [(jnp.ones((T, 128), F32) / 128.0, False)], [0, 2], tm=256, name="loss_bwd")

    small = {k: [None] * L for k in ("ln_mix", "ln_ffn", "mod", "a_log", "dt", "norm_w", "sinks", "conv")}
    big = {}
    recv_ffn, recv_mix = [None] * L, [None] * L
    pending = ()

    def shards(g, cols, n=N_DEV):
        return jnp.transpose(g.reshape(g.shape[0], n, -1), (1, 0, 2)) if cols else g.reshape(n, -1, g.shape[1])

    for l in reversed(range(L)):
        s = saved[l]
        ddown, dgt_f = _ew_bwd(_f_resgate, s["rg2"], [(dx, False)], [1, 2], tm=256, name=f"resgate2_bwd{l}", gdt=[BF16, F32])
        big["w_dn"] = _mm(s["act"], ddown, "tn", BF16, f"wg_down{l}")
        dact = _mm(ddown, W_dn[l], "nt", F32, f"dg_down{l}")
        dgate, dup = _ew_bwd(_f_swiglu, s["sw_ops"], [(dact, True)], [0, 1], tm=2048, ncol=11, name=f"swiglu_bwd{l}", gdt=[BF16, BF16])
        big["w_gate"] = _mm(s["h2"], dgate, "tn", BF16, f"wg_gate{l}")
        big["w_up"] = _mm(s["h2"], dup, "tn", BF16, f"wg_up{l}")
        dh2 = _mm(dup, W_up[l], "nt", F32, f"dg_up{l}", acc_in=_mm(dgate, W_gate[l], "nt", F32, f"dg_gate{l}"))
        dx1, dln_f, dsc_f, dsh_f = _ew_bwd(_f_normmod, s["nm2"], [(dh2, False), (dx, False)], [0, 1, 2, 4], tm=256, name=f"normmod2_bwd{l}")
        dmix, dgt_m = _ew_bwd(_f_resgate, s["rg1"], [(dx1, False)], [1, 2], tm=256, name=f"resgate1_bwd{l}", gdt=[BF16, F32])
        big["w_out"] = jnp.concatenate([_mm(s["dn_out"], dmix, "tn", BF16, f"wg_out_dn{l}"),
                                        _mm(s["at_out"], dmix, "tn", BF16, f"wg_out_at{l}")], axis=0)
        d_dn = _mm(dmix, W_out[l][:DN_W], "nt", F32, f"dg_out_dn{l}")
        d_at = _mm(dmix, W_out[l][DN_W:], "nt", F32, f"dg_out_at{l}")
        ffn_sums = (_chip_sums(jnp.concatenate([shards(big["w_gate"], True, 4), shards(big["w_up"], True, 4)], axis=0), f"w_gu{l}"),
                    _chip_sums(shards(big["w_dn"], False), f"w_down{l}"))
        dqh, dkp, dkc, dvp, dvc, dsk, *recv_ffn[l] = _attn_bwd(s["qh"], s["kh"], s["vh"], s["sk"], cos, sin, rot, heads(d_at, AT_QH),
                                                               f"attn_bwd{l}", a2a_srcs=ffn_sums[1:])
        dkh = _kv_combine(dkc, dkp, f"dk_comb{l}")
        dvh = _kv_combine(dvc, dvp, f"dv_comb{l}")
        do, dz, dnw = _ew_bwd(_f_dnpost, s["post_ops"], [(d_dn, True)], [0, 1, 2], tm=2048, ncol=DN_HEADS, name=f"dnpost_bwd{l}")
        dqkvn, dbg, got_gu, *got = _chunk_bwd(s["qkvn"], s["bg"], s["s_saved"], s["t_saved"], do, f"chunk_bwd{l}",
                                              a2a_srcs=ffn_sums[:1] + pending)
        recv_ffn[l] = [got_gu] + recv_ffn[l]
        if pending:
            recv_mix[l + 1] = got
        dconv = _ew_bwd(_f_dnpre, s["pre_ops"], [(dqkvn, True)], [0], tm=2048, ncol=3 * DN_HEADS, name=f"dnpre_bwd{l}", with_j=True)[0]
        dba, dalog, ddt = _ew_bwd(_f_bg, s["bg_ops"], [(dbg, False)], [0, 1, 2], tm=1024, name=f"bg_bwd{l}")
        dqkv, dcw = _conv_bwd(s["proj"], dconv, W_cv[l], f"conv_bwd{l}")
        dproj = jnp.concatenate([dqkv, dz, unheads(dqh), unheads(dkh), unheads(dvh), dba, jnp.zeros((T, IN_PAD - 5504), F32)],
                                axis=-1).astype(BF16)
        big["w_in"] = _mm(s["h1"], dproj, "tn", BF16, f"wg_in{l}")
        pending = (_chip_sums(shards(_from_aligned(big["w_in"]), True), f"w_in{l}"), _chip_sums(shards(big["w_out"], False), f"w_out{l}"))
        if l == 0:
            dh1, *recv_mix[0] = _mm(dproj, W_in[l], "nt", F32, f"dg_in{l}", a2a_srcs=pending)
        else:
            dh1 = _mm(dproj, W_in[l], "nt", F32, f"dg_in{l}")
        dx, dln_m, dsc_m, dsh_m = _ew_bwd(_f_normmod, s["nm1"], [(dh1, False), (dx1, False)], [0, 1, 2, 4], tm=256, name=f"normmod1_bwd{l}")
        small["ln_mix"][l], small["ln_ffn"][l] = dln_m, dln_f
        small["mod"][l] = jnp.concatenate([dsh_m, dsc_m, dgt_m, dsh_f, dsc_f, dgt_f], axis=-1)
        small["a_log"][l], small["dt"][l] = dalog[:, 8:16], ddt[:, 8:16]
        small["norm_w"][l], small["sinks"][l], small["conv"][l] = dnw, dsk[:, 0, 0], dcw
    p_in, p_out = [jnp.concatenate([recv_mix[l][i] for l in range(L)], axis=1) for i in range(2)]
    p_gu, p_dn = [jnp.concatenate([recv_ffn[l][i] for l in range(L)], axis=1) for i in range(2)]

    cat0 = lambda xs_: jnp.concatenate([_rows128(a) for a in xs_], axis=0)
    stk = lambda k: jnp.stack(small[k])
    pack = cat0([stk("ln_mix"), stk("ln_ffn"), stk("mod"), stk("a_log"), stk("dt"), stk("norm_w"), stk("sinks"), d_ln_final, stk("conv")])
    n_small = pack.shape[0] - L * CONV_K * 3 * DN_W // 128
    pack = jnp.pad(pack, ((0, (-pack.shape[0]) % 8), (0, 0)))
    gp = _gather(pack, "ag_small")
    parts_small = gp[:, :n_small]
    dmod_all = gp[:, 2 * L * D // 128:2 * L * D // 128 + L * 6 * D // 128].reshape(N_DEV, L, 6 * D)
    conv_all = gp[:, n_small:n_small + L * CONV_K * 3 * DN_W // 128].reshape(N_DEV, L * CONV_K, 3 * DN_W)
    parts_conv = lax.dynamic_slice_in_dim(conv_all, me * (3 * DN_W // N_DEV), 3 * DN_W // N_DEV, axis=2)

    dmod_mine = lax.dynamic_slice_in_dim(dmod_all, me * (6 * D // N_DEV), 6 * D // N_DEV, axis=2)
    g_ada = jnp.stack([_mm(c_act, jnp.pad(dmod_mine[:, l], ((0, 8), (0, 0))), "tn", F32, f"wg_ada{l}") for l in range(L)])

    def upd(w, m, v, parts, name):
        shp = w.shape
        r = lambda a: a.reshape(-1, shp[-1])
        return [o_.reshape(shp) for o_ in _adamw(r(w), r(m), r(v), parts.reshape(parts.shape[0], -1, shp[-1]), name)]

    res = {}
    res["w_ada"] = upd(w_ada, m_w_ada, v_w_ada, g_ada[None], "adamw_ada")
    res["w_in"] = upd(w_in, m_w_in, v_w_in, p_in, "adamw_in")
    res["dn_conv_w"] = upd(dn_conv_w, m_dn_conv_w, v_dn_conv_w, parts_conv, "adamw_conv")
    res["w_out"] = upd(w_out, m_w_out, v_w_out, p_out, "adamw_out")
    res["w_gate_up"] = upd(w_gate_up, m_w_gate_up, v_w_gate_up, p_gu, "adamw_gu")
    res["w_down"] = upd(w_down, m_w_down, v_w_down, p_dn, "adamw_down")
    names_s = ["ln_mix", "ln_ffn", "b_ada", "dn_a_log", "dn_dt_bias", "dn_norm_w", "attn_sinks", "ln_final"]
    ws = [ln_mix, ln_ffn, b_ada, dn_a_log, dn_dt_bias, dn_norm_w, attn_sinks, ln_final]
    ms = [m_ln_mix, m_ln_ffn, m_b_ada, m_dn_a_log, m_dn_dt_bias, m_dn_norm_w, m_attn_sinks, m_ln_final]
    vs = [v_ln_mix, v_ln_ffn, v_b_ada, v_dn_a_log, v_dn_dt_bias, v_dn_norm_w, v_attn_sinks, v_ln_final]
    padr = lambda a: jnp.pad(a, ((0, (-a.shape[0]) % 8), (0, 0)))
    vpad = jnp.pad(cat0(vs), ((0, (-n_small) % 8), (0, 0)), constant_values=1.0)
    outs_s = _adamw(padr(cat0(ws)), padr(cat0(ms)), vpad, jnp.pad(parts_small, ((0, 0), (0, (-n_small) % 8), (0, 0))), "adamw_small")
    off = 0
    for nme, wv in zip(names_s, ws):
        nrow = -(-wv.size // 128)
        res[nme] = [o_[off:off + nrow].reshape(-1)[:wv.size].reshape(wv.shape) for o_ in outs_s]
        off += nrow

    order = ["ln_mix", "ln_ffn", "w_ada", "b_ada", "w_in", "dn_conv_w", "dn_a_log", "dn_dt_bias", "dn_norm_w", "attn_sinks",
             "w_out", "w_gate_up", "w_down", "ln_final"]
    return (loss, dx[None], *[res[n][0] for n in order], *[res[n][1] for n in order], *[res[n][2] for n in order],
            *[res[n][3] for n in order])
```

```python
import functools

import jax
import jax.numpy as jnp
from jax import lax
from jax.experimental import pallas as pl
from jax.experimental.pallas import tpu as pltpu

F32, BF16 = jnp.float32, jnp.bfloat16
HI = lax.Precision.HIGH
MESH = pl.DeviceIdType.MESH
ANY = pl.BlockSpec(memory_space=pl.ANY)

N_DEV = 8
D = 2048
DN_HEADS, DN_HD = 8, 128
DN_W = 1024
CONV_K = 4
CHUNK = 64
AT_HD, AT_QH, AT_KVH = 64, 16, 2
AT_W = 1024
WINDOW = 128
FFN = 5632
IN_COLS = 5392
IN_PAD = 5632
EPS = 1e-6
NEG = -1e30
LR, B1, B2, AEPS, WD, STEP = 0.001, 0.9, 0.999, 1e-08, 0.01, 10
VMEM_LIMIT = 56 * 1024 * 1024
FORWARD_AT = 7


def _pick(n, cands):
    for c in cands:
        if n % c == 0:
            return c
    return n


def _params(sem):
    return pltpu.CompilerParams(dimension_semantics=sem, vmem_limit_bytes=VMEM_LIMIT)


_DN = {"nn": (((1,), (0,)), ((), ())), "nt": (((1,), (1,)), ((), ())), "tn": (((0,), (0,)), ((), ()))}


def _mm(a, b, mode, out_dtype, name, acc_in=None, gather_srcs=(), a2a_srcs=(), gate=None):
    if mode == "nn":
        (M, K), (_, N) = a.shape, b.shape
    elif mode == "nt":
        (M, K), (N, _) = a.shape, b.shape
    else:
        (K, M), (_, N) = a.shape, b.shape
    tm = _pick(M, (1024, 512, 256, 128, 64, 32, 16))
    tn = _pick(N, (1024, 512, 256, 128))
    tk = K if K <= 2048 else _pick(K, (2816, 2048, 1024, 512, 256, 128))
    nk = K // tk
    dn = _DN[mode]
    ins = [a, b] + ([] if acc_in is None else [acc_in]) + (list(gate) if gate else [])
    n_mm, ng, na = len(ins), len(gather_srcs), len(a2a_srcs)
    n_in = n_mm + ng + na
    n_out = 1 + (gate is not None)
    steps = (M // tm) * (N // tn) * nk

    def body(*refs):
        a_ref, b_ref, o_ref = refs[0], refs[1], refs[n_in]
        outs = refs[n_in + n_out:n_in + n_out + ng + na]
        scratch = refs[n_in + n_out + ng + na:]
        sems = scratch[(nk > 1):]
        k = pl.program_id(2)
        step = (pl.program_id(0) * (N // tn) + pl.program_id(1)) * nk + k
        plans = [_gather_plan(refs[n_mm + g], outs[g], *sems[3 * g:3 * g + 3]) for g in range(ng)]
        if na:
            plans.append(_a2a_plan(refs[n_mm + ng:n_in], outs[ng:], *sems[3 * ng:]))
        for plan in plans:
            pl.when(step == 0)(plan[0])
        for plan in plans[:ng]:
            pl.when(step == steps * FORWARD_AT // 8)(plan[1])

        def emit(val):
            if acc_in is not None:
                val = val + refs[2][...]
            if gate:
                x_ref, g_ref, bg_ref = refs[n_mm - 3:n_mm]
                o_ref[...] = x_ref[...] + (g_ref[...] + bg_ref[...]) * val
                refs[n_in + 1][...] = val.astype(BF16)
            else:
                o_ref[...] = val.astype(o_ref.dtype)

        part = lax.dot_general(a_ref[...].astype(BF16), b_ref[...].astype(BF16), dn, preferred_element_type=F32)
        if nk == 1:
            emit(part)
        else:
            acc = scratch[0]

            @pl.when(k == 0)
            def _():
                acc[...] = part

            @pl.when(k > 0)
            def _():
                acc[...] += part

            @pl.when(k == nk - 1)
            def _():
                emit(acc[...])
        for plan in plans:
            pl.when(step == steps - 1)(plan[-1])

    a_spec = pl.BlockSpec((tk, tm), lambda i, j, k: (k, i)) if mode == "tn" else pl.BlockSpec((tm, tk), lambda i, j, k: (i, k))
    b_spec = pl.BlockSpec((tn, tk), lambda i, j, k: (j, k)) if mode == "nt" else pl.BlockSpec((tk, tn), lambda i, j, k: (k, j))
    o_spec = pl.BlockSpec((tm, tn), lambda i, j, k: (i, j))
    row_spec = pl.BlockSpec((1, tn), lambda i, j, k: (0, j))
    carried = list(gather_srcs) + list(a2a_srcs)
    res = pl.pallas_call(
        body, name=name, grid=(M // tm, N // tn, nk),
        in_specs=[a_spec, b_spec] + ([] if acc_in is None else [o_spec]) + ([o_spec, row_spec, row_spec] if gate else []) + [ANY] * len(carried),
        out_specs=[o_spec] * n_out + [ANY] * len(carried),
        out_shape=[jax.ShapeDtypeStruct((M, N), F32 if gate else out_dtype)] + ([jax.ShapeDtypeStruct((M, N), BF16)] if gate else [])
        + [jax.ShapeDtypeStruct((N_DEV,) + g.shape, g.dtype) for g in gather_srcs] + [jax.ShapeDtypeStruct(g.shape, g.dtype) for g in a2a_srcs],
        scratch_shapes=([pltpu.VMEM((tm, tn), F32)] if nk > 1 else []) + GATHER_SEMS * ng + (_a2a_sems(na) if na else []),
        compiler_params=_params(("arbitrary",) * 3 if carried else ("parallel", "parallel", "arbitrary")),
    )(*ins, *carried)
    return res if len(res) > 1 else res[0]


def _op(op):
    arr, rt, cb = op[:3]
    start, width = op[3] if len(op) > 3 else (0, arr.shape[1])
    return arr, rt, cb, start, width


def _ew_spec(op, tm, ncol):
    arr, rt, cb, start, width = _op(op)
    bw = width // ncol if cb else width
    first = start // bw
    return pl.BlockSpec((tm if rt else arr.shape[0], bw), lambda j, i: (i if rt else 0, first + (j if cb else 0)))


def _ew_fwd(f, ops, outs, *, tm, ncol=1, name, with_j=False):
    M = next(op[0].shape[0] for op in ops if op[1])
    tm = min(tm, M)
    n_in = len(ops)

    def body(*refs):
        res = f(*[r[...].astype(F32) for r in refs[:n_in]], *([pl.program_id(0)] if with_j else []))
        for o, r in zip(refs[n_in:], res):
            o[...] = r.astype(o.dtype)

    return pl.pallas_call(
        body, name=name, grid=(ncol, M // tm),
        in_specs=[_ew_spec(op, tm, ncol) for op in ops],
        out_specs=[pl.BlockSpec((tm, w // ncol if cb else w), lambda j, i, cb=cb: (i, j if cb else 0)) for w, _, cb in outs],
        out_shape=[jax.ShapeDtypeStruct((M, w), dt) for w, dt, _ in outs],
        compiler_params=_params(("parallel", "parallel")),
    )(*[op[0] for op in ops])


def _ew_bwd(f, ops, cts, diff, *, tm, ncol=1, name, gdt=None, with_j=False):
    M = next(op[0].shape[0] for op in ops if op[1])
    tm = min(tm, M)
    n_in, n_ct = len(ops), len(cts)
    gdt = gdt or [F32] * len(diff)

    def body(*refs):
        j, i = pl.program_id(0), pl.program_id(1)
        vals = [r[...].astype(F32) for r in refs[:n_in]]

        def fd(*dv):
            full = list(vals)
            for idx, v in zip(diff, dv):
                full[idx] = v
            return tuple(f(*full, *([j] if with_j else [])))

        _, vjp = jax.vjp(fd, *[vals[idx] for idx in diff])
        gs = vjp(tuple(r[...].astype(F32) for r in refs[n_in:n_in + n_ct]))
        for idx, g, gref in zip(diff, gs, refs[n_in + n_ct:]):
            _, rt, cb = ops[idx][:3]
            if rt:
                gref[...] = g.astype(gref.dtype)
            else:
                first = (i == 0) if cb else jnp.logical_and(i == 0, j == 0)

                @pl.when(first)
                def _(gref=gref):
                    gref[...] = jnp.zeros_like(gref)

                gref[...] += g

    def g_spec(op):
        arr, rt, cb, start, width = _op(op)
        return _ew_spec((jax.ShapeDtypeStruct((arr.shape[0], width), F32), rt, cb), tm, ncol) if rt else _ew_spec(op, tm, ncol)

    def g_shape(op, dt):
        arr, rt, cb, start, width = _op(op)
        return jax.ShapeDtypeStruct((M, width), dt) if rt else jax.ShapeDtypeStruct(arr.shape, F32)

    return pl.pallas_call(
        body, name=name, grid=(ncol, M // tm),
        in_specs=[_ew_spec(op, tm, ncol) for op in ops]
        + [pl.BlockSpec((tm, a.shape[1] // ncol if cb else a.shape[1]), lambda j, i, cb=cb: (i, j if cb else 0)) for a, cb in cts],
        out_specs=[g_spec(ops[idx]) for idx in diff],
        out_shape=[g_shape(ops[idx], dt) for idx, dt in zip(diff, gdt)],
        compiler_params=_params(("arbitrary", "arbitrary")),
    )(*[op[0] for op in ops], *[a for a, _ in cts])


def _silu(x):
    return x * jax.nn.sigmoid(x)


def _f_normmod(x, ln, sc, bsc, sh, bsh):
    y = x * lax.rsqrt(jnp.mean(x * x, axis=-1, keepdims=True) + EPS) * ln
    return y * (1.0 + (sc + bsc)) + (sh + bsh), x


def _f_resgate(x, br, gt, bgt):
    return (x + (gt + bgt) * br,)


def _f_swiglu(gate, up):
    return (_silu(gate) * up,)


def _f_final(x, tgt, ln):
    y = x * lax.rsqrt(jnp.mean(x * x, axis=-1, keepdims=True) + EPS) * ln
    e = y - tgt
    return (jnp.broadcast_to(0.5 * jnp.mean(e * e, axis=-1, keepdims=True), (x.shape[0], 128)),)


def _f_bg(ba, alog, dt):
    col = lax.broadcasted_iota(jnp.int32, ba.shape, 1)
    z = ba + dt
    sp = jnp.maximum(z, 0.0) + jnp.log(1.0 + jnp.exp(-jnp.abs(z)))
    return (jnp.where(col < 8, jax.nn.sigmoid(ba), jnp.where(col < 16, -jnp.exp(alog) * sp, 0.0)),)


def _l2n(x):
    return x * lax.rsqrt(jnp.sum(x * x, axis=-1, keepdims=True) + EPS)


def _f_dnpre(c, j):
    a = _silu(c)
    return (jnp.where(j < 2 * DN_HEADS, _l2n(a) * jnp.where(j < DN_HEADS, DN_HD ** -0.5, 1.0), a),)


def _f_dnpost(o, z, nw):
    return (o * lax.rsqrt(jnp.mean(o * o, axis=-1, keepdims=True) + EPS) * nw * _silu(z),)


CONV_BW = 512


def _conv_fwd(x, w, Cw, name, gather_src=None):
    T = x.shape[0]
    tm, bw = 512, CONV_BW
    carry = gather_src is not None
    steps = (Cw // bw) * (T // tm)

    def body(*refs):
        x_ref, h_ref, w_ref = refs[:3]
        o_ref = refs[3 + carry]
        i = pl.program_id(1)
        if carry:
            step = pl.program_id(0) * (T // tm) + i
            start, forward, finish = _gather_plan(refs[3], refs[5], *refs[6:])
            pl.when(step == 0)(start)
            pl.when(step == steps * FORWARD_AT // 8)(forward)
        cur, halo, wv = x_ref[...], h_ref[...], w_ref[...]
        halo = jnp.where(i > 0, halo, 0.0)
        row = lax.broadcasted_iota(jnp.int32, (8, bw), 0)
        acc = wv[3:4, :] * cur
        for s in (1, 2, 3):
            r = pltpu.roll(cur, s, 0)
            top = jnp.where(row < s, pltpu.roll(halo, s, 0), r[:8])
            acc += wv[3 - s:4 - s, :] * jnp.concatenate([top, r[8:]], axis=0)
        o_ref[...] = acc
        if carry:
            pl.when(step == steps - 1)(finish)

    res = pl.pallas_call(
        body, name=name, grid=(Cw // bw, T // tm),
        in_specs=[pl.BlockSpec((tm, bw), lambda j, i: (i, j)),
                  pl.BlockSpec((8, bw), lambda j, i: (jnp.maximum(i * (tm // 8) - 1, 0), j)),
                  pl.BlockSpec((4, bw), lambda j, i: (0, j))] + [ANY] * carry,
        out_specs=[pl.BlockSpec((tm, bw), lambda j, i: (i, j))] + [ANY] * carry,
        out_shape=[jax.ShapeDtypeStruct((T, Cw), F32)] + ([jax.ShapeDtypeStruct((N_DEV,) + gather_src.shape, gather_src.dtype)] if carry else []),
        scratch_shapes=GATHER_SEMS if carry else [],
        compiler_params=_params(("arbitrary", "arbitrary") if carry else ("parallel", "parallel")),
    )(x, x, w, *([gather_src] if carry else []))
    return res if carry else res[0]


def _conv_bwd(x, dy, w, name):
    T, Cw = dy.shape
    tm, bw = 512, CONV_BW
    nt = T // tm

    def body(x_ref, h_ref, dy_ref, n_ref, w_ref, dx_ref, dw_ref):
        i = pl.program_id(1)
        cur, dcur, wv = x_ref[...], dy_ref[...], w_ref[...]
        halo = jnp.where(i > 0, h_ref[...], 0.0)
        nxt = jnp.where(i < nt - 1, n_ref[...], 0.0)
        row = lax.broadcasted_iota(jnp.int32, (8, bw), 0)

        @pl.when(i == 0)
        def _():
            dw_ref[...] = jnp.zeros_like(dw_ref)

        dx = wv[3:4, :] * dcur
        dw_ref[3:4, :] += jnp.sum(dcur * cur, axis=0, keepdims=True)
        for s in (1, 2, 3):
            r = pltpu.roll(cur, s, 0)
            top = jnp.where(row < s, pltpu.roll(halo, s, 0), r[:8])
            xs = jnp.concatenate([top, r[8:]], axis=0)
            dw_ref[3 - s:4 - s, :] += jnp.sum(dcur * xs, axis=0, keepdims=True)
            rf = pltpu.roll(dcur, tm - s, 0)
            bot = jnp.where(row >= 8 - s, pltpu.roll(nxt, 8 - s, 0), rf[tm - 8:])
            dx += wv[3 - s:4 - s, :] * jnp.concatenate([rf[:tm - 8], bot], axis=0)
        dx_ref[...] = dx

    return pl.pallas_call(
        body, name=name, grid=(Cw // bw, nt),
        in_specs=[pl.BlockSpec((tm, bw), lambda j, i: (i, j)),
                  pl.BlockSpec((8, bw), lambda j, i: (jnp.maximum(i * (tm // 8) - 1, 0), j)),
                  pl.BlockSpec((tm, bw), lambda j, i: (i, j)),
                  pl.BlockSpec((8, bw), lambda j, i: (jnp.minimum((i + 1) * (tm // 8), T // 8 - 1), j)),
                  pl.BlockSpec((4, bw), lambda j, i: (0, j))],
        out_specs=[pl.BlockSpec((tm, bw), lambda j, i: (i, j)), pl.BlockSpec((4, bw), lambda j, i: (0, j))],
        out_shape=[jax.ShapeDtypeStruct((T, Cw), F32), jax.ShapeDtypeStruct((4, Cw), F32)],
        compiler_params=_params(("arbitrary", "arbitrary")),
    )(x, x, dy, dy, w)


def _dot(a, b):
    return jnp.dot(a, b, precision=HI, preferred_element_type=F32)


def _dot_t(a, b):
    return lax.dot_general(a, b, (((1,), (1,)), ((), ())), precision=HI, preferred_element_type=F32)


def _bdot(a, b):
    return jnp.dot(a.astype(BF16), b.astype(BF16), preferred_element_type=F32)


def _bdot_t(a, b):
    return lax.dot_general(a.astype(BF16), b.astype(BF16), (((1,), (1,)), ((), ())), preferred_element_type=F32)


def _each(f, *lists):
    return [f(*a) for a in zip(*lists)]


@jax.custom_vjp
def _unit_lower_inverses(ps):
    C = ps[0].shape[0]
    dist = jnp.bitwise_xor(lax.broadcasted_iota(jnp.int32, (C, C), 0), lax.broadcasted_iota(jnp.int32, (C, C), 1))
    ns = [jnp.where(dist < 8, p, 0.0) for p in ps]
    tis = [jnp.where(dist == 0, 1.0, 0.0) + n for n in ns]
    for _ in range(2):
        ns = _each(lambda n: _bdot(n, n), ns)
        tis = _each(lambda t, n: t + _bdot(t, n), tis, ns)
    b = 8
    while b < C:
        mids = [jnp.where(jnp.logical_and(dist >= b, dist < 2 * b), p, 0.0) for p in ps]
        halves = _each(_bdot, mids, tis)
        tis = _each(lambda t, h: t + _bdot(t, h), tis, halves)
        b *= 2
    return tis


def _uli_fwd(ps):
    tis = _unit_lower_inverses(ps)
    return tis, tis


def _uli_bwd(tis, cts):
    tt = lambda a, b: lax.dot_general(a, b, (((0,), (0,)), ((), ())), precision=HI, preferred_element_type=F32)
    half = _each(tt, tis, cts)
    return (_each(_dot_t, half, tis),)


_unit_lower_inverses.defvjp(_uli_fwd, _uli_bwd)


@jax.custom_vjp
def _kept_inverses(ps, tis):
    return tis


_kept_inverses.defvjp(lambda ps, tis: (tis, tis), lambda tis, cts: (_uli_bwd(tis, cts)[0], [jnp.zeros_like(t) for t in tis]))


def _chunk_f(qs, ks, vs, bg, Ss, hs, kept=None):
    C = CHUNK
    lane = lax.broadcasted_iota(jnp.int32, (C, 128), 1)
    betas = [jnp.sum(jnp.where(lane == h, bg, 0.0), axis=1, keepdims=True) for h in hs]
    gs = [jnp.sum(jnp.where(lane == h + 8, bg, 0.0), axis=1, keepdims=True) for h in hs]
    ri = lax.broadcasted_iota(jnp.int32, (C, C), 0)
    ci = lax.broadcasted_iota(jnp.int32, (C, C), 1)
    causal, strict, eye = ri >= ci, ri > ci, ri == ci
    g_rows = _each(lambda g: jnp.sum(jnp.where(eye, g, 0.0), axis=0, keepdims=True), gs)
    gc_cols = _each(lambda gr: jnp.sum(jnp.where(causal, gr, 0.0), axis=1, keepdims=True), g_rows)
    gc_rows = _each(lambda g: jnp.sum(jnp.where(ri <= ci, g, 0.0), axis=0, keepdims=True), gs)
    gc_lasts = _each(lambda g: jnp.sum(g, axis=0, keepdims=True), gs)
    decays = _each(lambda c, r: jnp.exp(jnp.where(causal, c - r, NEG)), gc_cols, gc_rows)
    kbs = _each(jnp.multiply, ks, betas)
    vbs = _each(jnp.multiply, vs, betas)
    ps = _each(lambda kb, k, d: -jnp.where(strict, _bdot_t(kb, k) * d, 0.0), kbs, ks, decays)
    tis = _unit_lower_inverses(ps) if kept is None else _kept_inverses(ps, kept)
    egcs = _each(jnp.exp, gc_cols)
    ws = _each(lambda t, kb, e: _bdot(t, kb * e), tis, kbs, egcs)
    us = _each(_bdot, tis, vbs)
    intras = _each(lambda q, k, d: jnp.where(causal, _bdot_t(q, k) * d, 0.0), qs, ks, decays)
    kds = _each(lambda k, gl, gc: k * jnp.exp(gl - gc), ks, gc_lasts, gc_cols)
    vns = _each(lambda u, w, S: u - _bdot(w, S), us, ws, Ss)
    os_ = _each(lambda q, e, S, i, vn: _bdot(q * e, S) + _bdot(i, vn), qs, egcs, Ss, intras, vns)
    Sn = _each(lambda S, gl, kd, vn: S * jnp.exp(gl) + lax.dot_general(
        kd.astype(BF16), vn.astype(BF16), (((0,), (0,)), ((), ())), preferred_element_type=F32), Ss, gc_lasts, kds, vns)
    return (os_, Sn), tis


def _head_cols(ref, part):
    return [ref[:, part * DN_W + e * DN_HD:part * DN_W + (e + 1) * DN_HD] for e in range(DN_HEADS)]


def _chunk_fwd(qkv, bg, name, gather_src=None):
    T = qkv.shape[0]
    N = T // CHUNK
    H = DN_HEADS
    carry = gather_src is not None

    def body(*refs):
        x_ref, bg_ref = refs[:2]
        o_ref, s_ref, t_ref = refs[2 + carry:5 + carry]
        S = refs[5 + 2 * carry]
        n = pl.program_id(0)
        if carry:
            start, forward, finish = _gather_plan(refs[2], refs[6], *refs[8:])
            pl.when(n == 0)(start)
            pl.when(n == N * FORWARD_AT // 8)(forward)

        @pl.when(n == 0)
        def _():
            S[...] = jnp.zeros_like(S)

        s_all = S[...]
        s_ref[0] = s_all
        (os_, nxt), tis = _chunk_f(_head_cols(x_ref, 0), _head_cols(x_ref, 1), _head_cols(x_ref, 2), bg_ref[...],
                                   [s_all[e] for e in range(H)], list(range(H)))
        for e in range(H):
            o_ref[:, e * DN_HD:(e + 1) * DN_HD] = os_[e]
            S[e] = nxt[e]
            t_ref[0, e] = tis[e]
        if carry:
            pl.when(n == N - 1)(finish)

    gshape = [jax.ShapeDtypeStruct((N_DEV,) + gather_src.shape, gather_src.dtype)] if carry else []
    return pl.pallas_call(
        body, name=name, grid=(N,),
        in_specs=[pl.BlockSpec((CHUNK, 3 * DN_W), lambda n: (n, 0)), pl.BlockSpec((CHUNK, 128), lambda n: (n, 0))] + [ANY] * carry,
        out_specs=[pl.BlockSpec((CHUNK, DN_W), lambda n: (n, 0)), pl.BlockSpec((1, H, DN_HD, DN_HD), lambda n: (n, 0, 0, 0)),
                   pl.BlockSpec((1, H, CHUNK, CHUNK), lambda n: (n, 0, 0, 0))] + [ANY] * carry,
        out_shape=[jax.ShapeDtypeStruct((T, DN_W), F32), jax.ShapeDtypeStruct((N, H, DN_HD, DN_HD), F32),
                   jax.ShapeDtypeStruct((N, H, CHUNK, CHUNK), F32)] + gshape,
        scratch_shapes=[pltpu.VMEM((H, DN_HD, DN_HD), F32)] + (GATHER_SEMS if carry else []),
        compiler_params=_params(("arbitrary",)),
    )(qkv, bg, *([gather_src] if carry else []))


def _chunk_bwd(qkv, bg, s_saved, t_saved, do, name, a2a_srcs=()):
    T = qkv.shape[0]
    N = T // CHUNK
    H = DN_HEADS
    na = len(a2a_srcs)

    def body(*refs):
        x_ref, bg_ref, s_ref, t_ref, do_ref = refs[:5]
        dx_ref, dbg_ref = refs[5 + na:7 + na]
        dS = refs[7 + 2 * na]
        n = pl.program_id(0)
        if na:
            start, finish = _a2a_plan(refs[5:5 + na], refs[7 + na:7 + 2 * na], *refs[8 + 2 * na:])
            pl.when(n == 0)(start)

        @pl.when(n == 0)
        def _():
            dS[...] = jnp.zeros_like(dS)

        ds_all = dS[...]
        _, vjp, _ = jax.vjp(functools.partial(_chunk_f, hs=list(range(H)), kept=[t_ref[0, e] for e in range(H)]), _head_cols(x_ref, 0),
                            _head_cols(x_ref, 1), _head_cols(x_ref, 2), bg_ref[...], [s_ref[0, e] for e in range(H)], has_aux=True)
        dq, dk, dv, dbg, nxt = vjp(([do_ref[:, e * DN_HD:(e + 1) * DN_HD] for e in range(H)], [ds_all[e] for e in range(H)]))
        for part, g in enumerate((dq, dk, dv)):
            for e in range(H):
                dx_ref[:, part * DN_W + e * DN_HD:part * DN_W + (e + 1) * DN_HD] = g[e]
        for e in range(H):
            dS[e] = nxt[e]
        dbg_ref[...] = dbg
        if na:
            pl.when(n == N - 1)(finish)

    rev = lambda n: (N - 1 - n, 0)
    return pl.pallas_call(
        body, name=name, grid=(N,),
        in_specs=[pl.BlockSpec((CHUNK, 3 * DN_W), rev), pl.BlockSpec((CHUNK, 128), rev),
                  pl.BlockSpec((1, H, DN_HD, DN_HD), lambda n: (N - 1 - n, 0, 0, 0)),
                  pl.BlockSpec((1, H, CHUNK, CHUNK), lambda n: (N - 1 - n, 0, 0, 0)), pl.BlockSpec((CHUNK, DN_W), rev)] + [ANY] * na,
        out_specs=[pl.BlockSpec((CHUNK, 3 * DN_W), rev), pl.BlockSpec((CHUNK, 128), rev)] + [ANY] * na,
        out_shape=[jax.ShapeDtypeStruct((T, 3 * DN_W), F32), jax.ShapeDtypeStruct((T, 128), F32)]
        + [jax.ShapeDtypeStruct(a.shape, a.dtype) for a in a2a_srcs],
        scratch_shapes=[pltpu.VMEM((H, DN_HD, DN_HD), F32)] + (_a2a_sems(na) if na else []),
        compiler_params=_params(("arbitrary",)),
    )(qkv, bg, s_saved, t_saved, do, *a2a_srcs)


GRP = AT_QH // AT_KVH


def _attn_f(q, kp, kc, vp, vc, sinks, cos_c, sin_c, cos_p, sin_p, rot, has_prev):
    def rope(x, c, s):
        return x * c + _dot(x, rot) * s

    kcr, kpr = rope(kc, cos_c, sin_c), rope(kp, cos_p, sin_p)
    qrs = [rope(qe, cos_c, sin_c) for qe in q]
    r = lax.broadcasted_iota(jnp.int32, (WINDOW, WINDOW), 0)
    j = lax.broadcasted_iota(jnp.int32, (WINDOW, WINDOW), 1)
    in_c, in_p = j <= r, jnp.logical_and(j > r, has_prev)
    lane = lax.broadcasted_iota(jnp.int32, (1, 128), 1)
    scs = [jnp.where(in_c, _bdot_t(qr, kcr) * (AT_HD ** -0.5), NEG) for qr in qrs]
    sps = [jnp.where(in_p, _bdot_t(qr, kpr) * (AT_HD ** -0.5), NEG) for qr in qrs]
    snk = [jnp.sum(jnp.where(lane == 0, s, 0.0), axis=1, keepdims=True) for s in sinks]
    ms = _each(lambda sc, sp, s: lax.stop_gradient(jnp.maximum(jnp.maximum(jnp.max(sc, axis=1, keepdims=True),
                                                                           jnp.max(sp, axis=1, keepdims=True)), s)), scs, sps, snk)
    pcs = _each(lambda sc, m: jnp.exp(sc - m), scs, ms)
    pps = _each(lambda sp, m: jnp.exp(sp - m), sps, ms)
    dens = _each(lambda pc, pp, s, m: jnp.sum(pc, axis=1, keepdims=True) + jnp.sum(pp, axis=1, keepdims=True) + jnp.exp(s - m),
                 pcs, pps, snk, ms)
    return tuple(_each(lambda pc, pp, den: (_bdot(pc, vc) + _bdot(pp, vp)) / den, pcs, pps, dens))


def _attn_specs(nb):
    qs = pl.BlockSpec((GRP, WINDOW, AT_HD), lambda g, n: (g, n, 0))
    kc = pl.BlockSpec((1, WINDOW, AT_HD), lambda g, n: (g, n, 0))
    kp = pl.BlockSpec((1, WINDOW, AT_HD), lambda g, n: (g, jnp.maximum(n - 1, 0), 0))
    tc = pl.BlockSpec((WINDOW, AT_HD), lambda g, n: (n, 0))
    tp = pl.BlockSpec((WINDOW, AT_HD), lambda g, n: (jnp.maximum(n - 1, 0), 0))
    sk = pl.BlockSpec((GRP, 1, 128), lambda g, n: (g, 0, 0))
    rt = pl.BlockSpec((AT_HD, AT_HD), lambda g, n: (0, 0))
    return qs, kc, kp, tc, tp, sk, rt


def _attn_fwd(q, k, v, sinks, cos, sin, rot, name, gather_src=None):
    T = q.shape[1]
    nb = T // WINDOW
    qs, kc, kp, tc, tp, sk, rt = _attn_specs(nb)
    carry = gather_src is not None

    def body(*refs):
        q_ref, kp_ref, kc_ref, vp_ref, vc_ref, sk_ref, cc_ref, sc_ref, cp_ref, sp_ref, rot_ref = refs[:11]
        o_ref = refs[11 + carry]
        n = pl.program_id(1)
        if carry:
            step = pl.program_id(0) * nb + n
            start, forward, finish = _gather_plan(refs[11], refs[13], *refs[14:])
            pl.when(step == 0)(start)
            pl.when(step == AT_KVH * nb * FORWARD_AT // 8)(forward)
        heads = range(GRP)
        outs = _attn_f(tuple(q_ref[e] for e in heads), kp_ref[0], kc_ref[0], vp_ref[0], vc_ref[0], tuple(sk_ref[e] for e in heads), cc_ref[...], sc_ref[...],
                       cp_ref[...], sp_ref[...], rot_ref[...], n > 0)
        for e in range(GRP):
            o_ref[e] = outs[e]
        if carry:
            pl.when(step == AT_KVH * nb - 1)(finish)

    gshape = [jax.ShapeDtypeStruct((N_DEV,) + gather_src.shape, gather_src.dtype)] if carry else []
    return pl.pallas_call(
        body, name=name, grid=(AT_KVH, nb),
        in_specs=[qs, kp, kc, kp, kc, sk, tc, tc, tp, tp, rt] + [ANY] * carry, out_specs=[qs] + [ANY] * carry,
        out_shape=[jax.ShapeDtypeStruct(q.shape, F32)] + gshape,
        scratch_shapes=GATHER_SEMS if carry else [],
        compiler_params=_params(("arbitrary", "arbitrary") if carry else ("parallel", "parallel")),
    )(q, k, k, v, v, sinks, cos, sin, cos, sin, rot, *([gather_src] if carry else []))


def _attn_bwd(q, k, v, sinks, cos, sin, rot, do, name, a2a_srcs=()):
    T = q.shape[1]
    nb = T // WINDOW
    qs, kc, kp, tc, tp, sk, rt = _attn_specs(nb)
    na = len(a2a_srcs)

    def body(*refs):
        q_ref, kp_ref, kc_ref, vp_ref, vc_ref, sk_ref, cc_ref, sc_ref, cp_ref, sp_ref, rot_ref, do_ref = refs[:12]
        dq_ref, dkp_ref, dkc_ref, dvp_ref, dvc_ref, dsk_ref = refs[12 + na:18 + na]
        n = pl.program_id(1)
        if na:
            step = pl.program_id(0) * nb + n
            start, finish = _a2a_plan(refs[12:12 + na], refs[18 + na:18 + 2 * na], *refs[18 + 2 * na:])
            pl.when(step == 0)(start)
        f = functools.partial(_attn_f, cos_c=cc_ref[...], sin_c=sc_ref[...], cos_p=cp_ref[...], sin_p=sp_ref[...],
                              rot=rot_ref[...], has_prev=n > 0)
        heads = range(GRP)
        _, vjp = jax.vjp(f, tuple(q_ref[e] for e in heads), kp_ref[0], kc_ref[0], vp_ref[0], vc_ref[0], tuple(sk_ref[e] for e in heads))
        dq, dkp, dkc, dvp, dvc, dsk = vjp(tuple(do_ref[e] for e in heads))
        dkp_ref[0], dkc_ref[0], dvp_ref[0], dvc_ref[0] = dkp, dkc, dvp, dvc

        @pl.when(n == 0)
        def _():
            dsk_ref[...] = jnp.zeros_like(dsk_ref)

        for e in heads:
            dq_ref[e] = dq[e]
            dsk_ref[e] += dsk[e]
        if na:
            pl.when(step == AT_KVH * nb - 1)(finish)

    return pl.pallas_call(
        body, name=name, grid=(AT_KVH, nb),
        in_specs=[qs, kp, kc, kp, kc, sk, tc, tc, tp, tp, rt, qs] + [ANY] * na, out_specs=[qs, kc, kc, kc, kc, sk] + [ANY] * na,
        out_shape=[jax.ShapeDtypeStruct(q.shape, F32)] + [jax.ShapeDtypeStruct(k.shape, F32)] * 4 + [jax.ShapeDtypeStruct(sinks.shape, F32)]
        + [jax.ShapeDtypeStruct(a.shape, a.dtype) for a in a2a_srcs],
        scratch_shapes=_a2a_sems(na) if na else [],
        compiler_params=_params(("arbitrary", "arbitrary")),
    )(q, k, k, v, v, sinks, cos, sin, cos, sin, rot, do, *a2a_srcs)


def _kv_combine(dc, dp, name):
    T = dc.shape[1]
    R = 8 * WINDOW
    ns = T // R

    def body(c_ref, p_ref, q_ref, o_ref):
        n = pl.program_id(1)
        tail = jnp.where(n < ns - 1, q_ref[0], 0.0)
        o_ref[0] = c_ref[0] + jnp.concatenate([p_ref[0, WINDOW:, :], tail], axis=0)

    return pl.pallas_call(
        body, name=name, grid=(AT_KVH, ns),
        in_specs=[pl.BlockSpec((1, R, AT_HD), lambda g, n: (g, n, 0)), pl.BlockSpec((1, R, AT_HD), lambda g, n: (g, n, 0)),
                  pl.BlockSpec((1, WINDOW, AT_HD), lambda g, n: (g, jnp.minimum((n + 1) * 8, T // WINDOW - 1), 0))],
        out_specs=pl.BlockSpec((1, R, AT_HD), lambda g, n: (g, n, 0)),
        out_shape=jax.ShapeDtypeStruct((AT_KVH, T, AT_HD), F32),
        compiler_params=_params(("parallel", "parallel")),
    )(dc, dp, dp)


def _place():
    x, y, c = lax.axis_index("x"), lax.axis_index("y"), lax.axis_index("c")
    return x, y, c, 4 * x + 2 * y + c


def _gather_plan(s_ref, o_ref, send_sems, recv_sems, lsem):
    x, y, c, _ = _place()
    me, sib = (x, y, c), (x, y, 1 - c)
    chips = [(1 - x, y), (x, 1 - y), (1 - x, 1 - y)]

    def copy(k, block, to, src_ref=None):
        slab = o_ref.at[4 * block[0] + 2 * block[1] + block[2]]
        return pltpu.make_async_remote_copy(src_ref=slab if src_ref is None else src_ref, dst_ref=slab,
                                            send_sem=send_sems.at[k], recv_sem=recv_sems.at[k], device_id=to, device_id_type=MESH)

    mine = pltpu.make_async_copy(s_ref, o_ref.at[4 * x + 2 * y + c], lsem)
    first = [copy(0, me, sib, s_ref)] + [copy(1 + j, me, (*chip, c), s_ref) for j, chip in enumerate(chips)]
    passed = [copy(4 + j, (*chip, c), sib) for j, chip in enumerate(chips)]

    def start():
        mine.start()
        for cp in first:
            cp.start()

    def forward():
        for j, chip in enumerate(chips):
            copy(1 + j, (*chip, c), me).wait_recv()
            passed[j].start()

    def finish():
        copy(0, sib, me).wait_recv()
        for j, chip in enumerate(chips):
            copy(4 + j, (*chip, 1 - c), me).wait_recv()
        for cp in first + passed:
            cp.wait_send()
        mine.wait()

    return start, forward, finish


GATHER_SEMS = [pltpu.SemaphoreType.DMA((N_DEV - 1,)), pltpu.SemaphoreType.DMA((N_DEV - 1,)), pltpu.SemaphoreType.DMA]


def _gather(src, name):
    def body(s_ref, o_ref, send_sems, recv_sems, lsem):
        for phase in _gather_plan(s_ref, o_ref, send_sems, recv_sems, lsem):
            phase()

    return pl.pallas_call(
        body, name=name, in_specs=[ANY], out_specs=ANY, out_shape=jax.ShapeDtypeStruct((N_DEV,) + src.shape, src.dtype),
        scratch_shapes=GATHER_SEMS,
    )(src)


def _pair_swap(src, name):
    def body(s_ref, o_ref, send_sem, recv_sem):
        x, y, c, _ = _place()
        cp = pltpu.make_async_remote_copy(src_ref=s_ref.at[1 - c], dst_ref=o_ref, send_sem=send_sem, recv_sem=recv_sem,
                                          device_id=(x, y, 1 - c), device_id_type=MESH)
        cp.start()
        cp.wait()

    return pl.pallas_call(
        body, name=name, in_specs=[ANY], out_specs=ANY, out_shape=jax.ShapeDtypeStruct(src.shape[1:], src.dtype),
        scratch_shapes=[pltpu.SemaphoreType.DMA, pltpu.SemaphoreType.DMA],
    )(src)


def _a2a_plan(s_refs, o_refs, send_sems, recv_sems, lsems):
    x, y, c, _ = _place()
    chip = 2 * x + y
    local, remote = [], []
    for a, (s_ref, o_ref) in enumerate(zip(s_refs, o_refs)):
        local.append(pltpu.make_async_copy(s_ref.at[chip], o_ref.at[chip], lsems.at[a]))
        for k in (1, 2, 3):
            px, py = x ^ (k >> 1), y ^ (k & 1)
            remote.append(pltpu.make_async_remote_copy(
                src_ref=s_ref.at[2 * px + py], dst_ref=o_ref.at[chip], send_sem=send_sems.at[3 * a + k - 1],
                recv_sem=recv_sems.at[3 * a + k - 1], device_id=(px, py, c), device_id_type=MESH))

    def start():
        for cp in local + remote:
            cp.start()

    def finish():
        for cp in remote + local:
            cp.wait()

    return start, finish


def _a2a_sems(n):
    return [pltpu.SemaphoreType.DMA((3 * n,)), pltpu.SemaphoreType.DMA((3 * n,)), pltpu.SemaphoreType.DMA((n,))]


def _add(a, b, name):
    R, C = a.shape
    tr = _pick(R, (512, 256, 128, 64, 32, 16))

    def body(a_ref, b_ref, o_ref):
        o_ref[...] = (a_ref[...].astype(F32) + b_ref[...].astype(F32)).astype(o_ref.dtype)

    s2 = pl.BlockSpec((tr, C), lambda i: (i, 0))
    return pl.pallas_call(body, name=name, grid=(R // tr,), in_specs=[s2, s2], out_specs=s2,
                          out_shape=jax.ShapeDtypeStruct((R, C), a.dtype), compiler_params=_params(("parallel",)))(a, b)


def _chip_sums(slabs, name):
    _, R, C = slabs.shape
    by_core = jnp.transpose(slabs.reshape(4, 2, R, C), (1, 0, 2, 3))
    theirs = _pair_swap(by_core, "swap_" + name)
    mine = lax.dynamic_index_in_dim(by_core, lax.axis_index("c"), axis=0, keepdims=False)
    return _add(mine.reshape(4 * R, C), theirs.reshape(4 * R, C), "add_" + name).reshape(4, R, C)


def _adamw(w, m, v, parts, name):
    R, C = w.shape
    P = parts.shape[0]
    tr = _pick(R, (256, 128, 64, 32, 16, 8))
    c1, c2 = 1.0 - B1 ** STEP, 1.0 - B2 ** STEP

    def body(w_ref, m_ref, v_ref, p_ref, g_ref, d_ref, nm_ref, nv_ref):
        g = p_ref[0].astype(F32)
        for i in range(1, P):
            g = g + p_ref[i].astype(F32)
        wv = w_ref[...]
        nm = B1 * m_ref[...] + (1.0 - B1) * g
        nv = B2 * v_ref[...] + (1.0 - B2) * (g * g)
        g_ref[...] = g
        nm_ref[...] = nm
        nv_ref[...] = nv
        d_ref[...] = -LR * ((nm / c1) / (jnp.sqrt(nv / c2) + AEPS) + WD * wv)

    s2 = pl.BlockSpec((tr, C), lambda i: (i, 0))
    return pl.pallas_call(
        body, name=name, grid=(R // tr,),
        in_specs=[s2, s2, s2, pl.BlockSpec((P, tr, C), lambda i: (0, i, 0))], out_specs=[s2] * 4,
        out_shape=[jax.ShapeDtypeStruct((R, C), F32)] * 4,
        compiler_params=_params(("parallel",)),
    )(w, m, v, parts)


def _colsum(a, name):
    def body(a_ref, o_ref):
        o_ref[...] = jnp.broadcast_to(jnp.sum(a_ref[...], axis=0, keepdims=True), o_ref.shape)

    return pl.pallas_call(body, name=name, out_shape=jax.ShapeDtypeStruct((8, 128), F32))(a)


def _rows128(a):
    f = a.reshape(-1)
    return jnp.pad(f, (0, (-f.shape[0]) % 128)).reshape(-1, 128)


def _to_aligned(w):
    return jnp.concatenate([w[..., 0:4096], w[..., 4112:5392], w[..., 4096:4112],
                            jnp.zeros(w.shape[:-1] + (IN_PAD - IN_COLS,), w.dtype)], axis=-1)


def _from_aligned(w):
    return jnp.concatenate([w[..., 0:4096], w[..., 5376:5392], w[..., 4096:5376]], axis=-1)


def kernel(x, c, ln_mix, ln_ffn, w_ada, b_ada, w_in, dn_conv_w, dn_a_log, dn_dt_bias, dn_norm_w, attn_sinks, w_out, w_gate_up, w_down, ln_final, loss_target, m_ln_mix, m_ln_ffn, m_w_ada, m_b_ada, m_w_in, m_dn_conv_w, m_dn_a_log, m_dn_dt_bias, m_dn_norm_w, m_attn_sinks, m_w_out, m_w_gate_up, m_w_down, m_ln_final, v_ln_mix, v_ln_ffn, v_w_ada, v_b_ada, v_w_in, v_dn_conv_w, v_dn_a_log, v_dn_dt_bias, v_dn_norm_w, v_attn_sinks, v_w_out, v_w_gate_up, v_w_down, v_ln_final):
    T = x.shape[1]
    L = ln_mix.shape[0]
    me = 4 * lax.axis_index("x") + 2 * lax.axis_index("y") + lax.axis_index("c")
    xs = x[0]
    tgt = loss_target[0]

    w_in_layout = lambda g: _to_aligned(jnp.transpose(g, (1, 0, 2)).reshape(D, IN_COLS))
    W_in = [w_in_layout(_gather(w_in[0].astype(BF16), "ag_w_in0"))] + [None] * (L - 1)
    W_out = None
    g_cv = _gather(dn_conv_w.reshape(L * CONV_K, -1), "ag_conv").reshape(N_DEV, L, CONV_K, -1)
    c_all = _gather(jnp.pad(c, ((0, 7), (0, 0))), "ag_c")[:, 0, :]
    W_gate, W_up, W_dn = [None] * L, [None] * L, [None] * L
    W_cv = [jnp.transpose(g_cv[:, l], (1, 0, 2)).reshape(CONV_K, 3 * DN_W) for l in range(L)]

    c_act = _ew_fwd(lambda v: (_silu(v),), [(jnp.pad(c_all, ((0, 8), (0, 0))), True, False)], [(D, F32, False)], tm=16, name="c_act")[0]
    mods = []
    for l in range(L):
        ms = _mm(c_act, w_ada[l], "nn", F32, f"mod_mm{l}")
        ga = _gather(ms, f"ag_mod{l}")
        mods.append(lax.dynamic_index_in_dim(ga, me, axis=1, keepdims=False).reshape(1, 6 * D))
    row = lambda a: a.reshape(1, -1)
    seg = lambda a, i: a[:, i * D:(i + 1) * D]

    half = AT_HD // 2
    inv_freq = 10000.0 ** (-jnp.arange(half, dtype=F32) * 2.0 / AT_HD)
    ang = jnp.arange(T, dtype=jnp.int32).astype(F32)[:, None] * inv_freq[None, :]
    cos = jnp.concatenate([jnp.cos(ang)] * 2, axis=-1)
    sin = jnp.concatenate([jnp.sin(ang)] * 2, axis=-1)
    ii = jnp.arange(AT_HD)
    rot = jnp.where(ii[:, None] == ii[None, :] + half, -1.0, 0.0) + jnp.where(ii[:, None] + half == ii[None, :], 1.0, 0.0)
    rot = rot.astype(F32)
    heads = lambda a, nh: jnp.transpose(a.reshape(T, nh, AT_HD), (1, 0, 2))
    unheads = lambda a: jnp.transpose(a, (1, 0, 2)).reshape(T, -1)
    pad16 = lambda a: jnp.pad(row(a), ((0, 0), (8, 128 - 16)))

    saved = []
    xc = xs
    for l in range(L):
        mod, bmod = mods[l], row(b_ada[l])
        s = {"x": xc}
        nm_ops = lambda xx, ln, a, b: [(xx, True, False), (row(ln), False, False), (seg(mod, a), False, False),
                                       (seg(bmod, a), False, False), (seg(mod, b), False, False), (seg(bmod, b), False, False)]
        h1 = _ew_fwd(lambda *a: _f_normmod(*a)[:1], nm_ops(xc, ln_mix[l], 1, 0), [(D, BF16, False)], tm=256, name=f"normmod1_{l}")[0]
        if l + 1 < L:
            proj, g_in = _mm(h1, W_in[l], "nn", F32, f"mm_in{l}", gather_srcs=[w_in[l + 1].astype(BF16)])
            W_in[l + 1] = w_in_layout(g_in)
        else:
            proj = _mm(h1, W_in[l], "nn", F32, f"mm_in{l}")
        aq, ak, av = proj[:, 4096:5120], proj[:, 5120:5248], proj[:, 5248:5376]
        if l == 0:
            conv, g_out = _conv_fwd(proj, W_cv[l], 3 * DN_W, f"conv{l}", gather_src=w_out.astype(BF16).reshape(-1, D))
            W_out = [g_out.reshape(N_DEV, L, -1, D)[:, i].reshape(D, D) for i in range(L)]
        else:
            conv = _conv_fwd(proj, W_cv[l], 3 * DN_W, f"conv{l}")
        pre_ops = [(conv, True, True)]
        qkvn = _ew_fwd(_f_dnpre, pre_ops, [(3 * DN_W, F32, True)], tm=2048, ncol=3 * DN_HEADS, name=f"dnpre{l}", with_j=True)[0]
        bg_ops = [(proj, True, False, (5376, 128)), (pad16(dn_a_log[l]), False, False), (pad16(dn_dt_bias[l]), False, False)]
        bg = _ew_fwd(_f_bg, bg_ops, [(128, F32, False)], tm=1024, name=f"bg{l}")[0]
        o, s_saved, t_saved, g_gu = _chunk_fwd(qkvn, bg, f"chunk{l}", gather_src=w_gate_up[l].astype(BF16))
        W_gate[l] = jnp.transpose(g_gu[:4], (1, 0, 2)).reshape(D, FFN)
        W_up[l] = jnp.transpose(g_gu[4:], (1, 0, 2)).reshape(D, FFN)
        post_ops = [(o, True, True), (proj, True, True, (3072, DN_W)), (row(dn_norm_w[l]), False, False)]
        dn_out = _ew_fwd(_f_dnpost, post_ops, [(DN_W, BF16, True)], tm=2048, ncol=DN_HEADS, name=f"dnpost{l}")[0]
        qh, kh, vh = heads(aq, AT_QH), heads(ak, AT_KVH), heads(av, AT_KVH)
        sk = jnp.broadcast_to(attn_sinks[l][:, None, None], (AT_QH, 1, 128))
        at_o, g_dn = _attn_fwd(qh, kh, vh, sk, cos, sin, rot, f"attn{l}", gather_src=w_down[l].astype(BF16))
        W_dn[l] = g_dn.reshape(FFN, D)
        at_out = unheads(at_o).astype(BF16)
        x1, mix = _mm(at_out, W_out[l][DN_W:], "nn", F32, f"mm_out_at{l}", acc_in=_mm(dn_out, W_out[l][:DN_W], "nn", F32, f"mm_out_dn{l}"),
                      gate=(xc, seg(mod, 2), seg(bmod, 2)))
        rg_ops = lambda xx, br, a: [(xx, True, False), (br, True, False), (seg(mod, a), False, False), (seg(bmod, a), False, False)]
        h2 = _ew_fwd(lambda *a: _f_normmod(*a)[:1], nm_ops(x1, ln_ffn[l], 4, 3), [(D, BF16, False)], tm=256, name=f"normmod2_{l}")[0]
        gate = _mm(h2, W_gate[l], "nn", BF16, f"mm_gate{l}")
        up = _mm(h2, W_up[l], "nn", BF16, f"mm_up{l}")
        sw_ops = [(gate, True, True), (up, True, True)]
        act = _ew_fwd(_f_swiglu, sw_ops, [(FFN, BF16, True)], tm=2048, ncol=11, name=f"swiglu{l}")[0]
        x2, down = _mm(act, W_dn[l], "nn", F32, f"mm_down{l}", gate=(x1, seg(mod, 5), seg(bmod, 5)))
        s.update(h1=h1, proj=proj, qkvn=qkvn, bg=bg, s_saved=s_saved, t_saved=t_saved, qh=qh, kh=kh, vh=vh, sk=sk, dn_out=dn_out, at_out=at_out,
                 h2=h2, act=act, bg_ops=bg_ops, post_ops=post_ops, pre_ops=pre_ops, sw_ops=sw_ops,
                 nm1=nm_ops(xc, ln_mix[l], 1, 0), nm2=nm_ops(x1, ln_ffn[l], 4, 3), rg1=rg_ops(xc, mix, 2), rg2=rg_ops(x1, down, 5))
        saved.append(s)
        xc = x2

    fin_ops = [(xc, True, False), (tgt, True, False), (row(ln_final), False, False)]
    lrow = _ew_fwd(_f_final, fin_ops, [(128, F32, False)], tm=256, name="loss_rows")[0]
    loss = lax.psum(_colsum(lrow, "loss_sum")[0, 0], ("x", "y", "c"))
    dx, d_ln_final = _ew_bwd(_f_final, fin_ops, [(jnp.ones((T, 128), F32) / 128.0, False)], [0, 2], tm=256, name="loss_bwd")

    small = {k: [None] * L for k in ("ln_mix", "ln_ffn", "mod", "a_log", "dt", "norm_w", "sinks", "conv")}
    big = {}
    recv_ffn, recv_mix = [None] * L, [None] * L
    pending = ()

    def shards(g, cols, n=N_DEV):
        return jnp.transpose(g.reshape(g.shape[0], n, -1), (1, 0, 2)) if cols else g.reshape(n, -1, g.shape[1])

    for l in reversed(range(L)):
        s = saved[l]
        ddown, dgt_f = _ew_bwd(_f_resgate, s["rg2"], [(dx, False)], [1, 2], tm=256, name=f"resgate2_bwd{l}", gdt=[BF16, F32])
        big["w_dn"] = _mm(s["act"], ddown, "tn", BF16, f"wg_down{l}")
        dact = _mm(ddown, W_dn[l], "nt", F32, f"dg_down{l}")
        dgate, dup = _ew_bwd(_f_swiglu, s["sw_ops"], [(dact, True)], [0, 1], tm=2048, ncol=11, name=f"swiglu_bwd{l}", gdt=[BF16, BF16])
        big["w_gate"] = _mm(s["h2"], dgate, "tn", BF16, f"wg_gate{l}")
        big["w_up"] = _mm(s["h2"], dup, "tn", BF16, f"wg_up{l}")
        dh2 = _mm(dup, W_up[l], "nt", F32, f"dg_up{l}", acc_in=_mm(dgate, W_gate[l], "nt", F32, f"dg_gate{l}"))
        dx1, dln_f, dsc_f, dsh_f = _ew_bwd(_f_normmod, s["nm2"], [(dh2, False), (dx, False)], [0, 1, 2, 4], tm=256, name=f"normmod2_bwd{l}")
        dmix, dgt_m = _ew_bwd(_f_resgate, s["rg1"], [(dx1, False)], [1, 2], tm=256, name=f"resgate1_bwd{l}", gdt=[BF16, F32])
        big["w_out"] = jnp.concatenate([_mm(s["dn_out"], dmix, "tn", BF16, f"wg_out_dn{l}"),
                                        _mm(s["at_out"], dmix, "tn", BF16, f"wg_out_at{l}")], axis=0)
        d_dn = _mm(dmix, W_out[l][:DN_W], "nt", F32, f"dg_out_dn{l}")
        d_at = _mm(dmix, W_out[l][DN_W:], "nt", F32, f"dg_out_at{l}")
        ffn_sums = (_chip_sums(jnp.concatenate([shards(big["w_gate"], True, 4), shards(big["w_up"], True, 4)], axis=0), f"w_gu{l}"),
                    _chip_sums(shards(big["w_dn"], False), f"w_down{l}"))
        dqh, dkp, dkc, dvp, dvc, dsk, *recv_ffn[l] = _attn_bwd(s["qh"], s["kh"], s["vh"], s["sk"], cos, sin, rot, heads(d_at, AT_QH),
                                                               f"attn_bwd{l}", a2a_srcs=ffn_sums[1:])
        dkh = _kv_combine(dkc, dkp, f"dk_comb{l}")
        dvh = _kv_combine(dvc, dvp, f"dv_comb{l}")
        do, dz, dnw = _ew_bwd(_f_dnpost, s["post_ops"], [(d_dn, True)], [0, 1, 2], tm=2048, ncol=DN_HEADS, name=f"dnpost_bwd{l}")
        dqkvn, dbg, got_gu, *got = _chunk_bwd(s["qkvn"], s["bg"], s["s_saved"], s["t_saved"], do, f"chunk_bwd{l}",
                                              a2a_srcs=ffn_sums[:1] + pending)
        recv_ffn[l] = [got_gu] + recv_ffn[l]
        if pending:
            recv_mix[l + 1] = got
        dconv = _ew_bwd(_f_dnpre, s["pre_ops"], [(dqkvn, True)], [0], tm=2048, ncol=3 * DN_HEADS, name=f"dnpre_bwd{l}", with_j=True)[0]
        dba, dalog, ddt = _ew_bwd(_f_bg, s["bg_ops"], [(dbg, False)], [0, 1, 2], tm=1024, name=f"bg_bwd{l}")
        dqkv, dcw = _conv_bwd(s["proj"], dconv, W_cv[l], f"conv_bwd{l}")
        dproj = jnp.concatenate([dqkv, dz, unheads(dqh), unheads(dkh), unheads(dvh), dba, jnp.zeros((T, IN_PAD - 5504), F32)],
                                axis=-1).astype(BF16)
        big["w_in"] = _mm(s["h1"], dproj, "tn", BF16, f"wg_in{l}")
        pending = (_chip_sums(shards(_from_aligned(big["w_in"]), True), f"w_in{l}"), _chip_sums(shards(big["w_out"], False), f"w_out{l}"))
        if l == 0:
            dh1, *recv_mix[0] = _mm(dproj, W_in[l], "nt", F32, f"dg_in{l}", a2a_srcs=pending)
        else:
            dh1 = _mm(dproj, W_in[l], "nt", F32, f"dg_in{l}")
        dx, dln_m, dsc_m, dsh_m = _ew_bwd(_f_normmod, s["nm1"], [(dh1, False), (dx1, False)], [0, 1, 2, 4], tm=256, name=f"normmod1_bwd{l}")
        small["ln_mix"][l], small["ln_ffn"][l] = dln_m, dln_f
        small["mod"][l] = jnp.concatenate([dsh_m, dsc_m, dgt_m, dsh_f, dsc_f, dgt_f], axis=-1)
        small["a_log"][l], small["dt"][l] = dalog[:, 8:16], ddt[:, 8:16]
        small["norm_w"][l], small["sinks"][l], small["conv"][l] = dnw, dsk[:, 0, 0], dcw
    p_in, p_out = [jnp.concatenate([recv_mix[l][i] for l in range(L)], axis=1) for i in range(2)]
    p_gu, p_dn = [jnp.concatenate([recv_ffn[l][i] for l in range(L)], axis=1) for i in range(2)]

    cat0 = lambda xs_: jnp.concatenate([_rows128(a) for a in xs_], axis=0)
    stk = lambda k: jnp.stack(small[k])
    pack = cat0([stk("ln_mix"), stk("ln_ffn"), stk("mod"), stk("a_log"), stk("dt"), stk("norm_w"), stk("sinks"), d_ln_final, stk("conv")])
    n_small = pack.shape[0] - L * CONV_K * 3 * DN_W // 128
    pack = jnp.pad(pack, ((0, (-pack.shape[0]) % 8), (0, 0)))
    gp = _gather(pack, "ag_small")
    parts_small = gp[:, :n_small]
    dmod_all = gp[:, 2 * L * D // 128:2 * L * D // 128 + L * 6 * D // 128].reshape(N_DEV, L, 6 * D)
    conv_all = gp[:, n_small:n_small + L * CONV_K * 3 * DN_W // 128].reshape(N_DEV, L * CONV_K, 3 * DN_W)
    parts_conv = lax.dynamic_slice_in_dim(conv_all, me * (3 * DN_W // N_DEV), 3 * DN_W // N_DEV, axis=2)

    dmod_mine = lax.dynamic_slice_in_dim(dmod_all, me * (6 * D // N_DEV), 6 * D // N_DEV, axis=2)
    g_ada = jnp.stack([_mm(c_act, jnp.pad(dmod_mine[:, l], ((0, 8), (0, 0))), "tn", F32, f"wg_ada{l}") for l in range(L)])

    def upd(w, m, v, parts, name):
        shp = w.shape
        r = lambda a: a.reshape(-1, shp[-1])
        return [o_.reshape(shp) for o_ in _adamw(r(w), r(m), r(v), parts.reshape(parts.shape[0], -1, shp[-1]), name)]

    res = {}
    res["w_ada"] = upd(w_ada, m_w_ada, v_w_ada, g_ada[None], "adamw_ada")
    res["w_in"] = upd(w_in, m_w_in, v_w_in, p_in, "adamw_in")
    res["dn_conv_w"] = upd(dn_conv_w, m_dn_conv_w, v_dn_conv_w, parts_conv, "adamw_conv")
    res["w_out"] = upd(w_out, m_w_out, v_w_out, p_out, "adamw_out")
    res["w_gate_up"] = upd(w_gate_up, m_w_gate_up, v_w_gate_up, p_gu, "adamw_gu")
    res["w_down"] = upd(w_down, m_w_down, v_w_down, p_dn, "adamw_down")
    names_s = ["ln_mix", "ln_ffn", "b_ada", "dn_a_log", "dn_dt_bias", "dn_norm_w", "attn_sinks", "ln_final"]
    ws = [ln_mix, ln_ffn, b_ada, dn_a_log, dn_dt_bias, dn_norm_w, attn_sinks, ln_final]
    ms = [m_ln_mix, m_ln_ffn, m_b_ada, m_dn_a_log, m_dn_dt_bias, m_dn_norm_w, m_attn_sinks, m_ln_final]
    vs = [v_ln_mix, v_ln_ffn, v_b_ada, v_dn_a_log, v_dn_dt_bias, v_dn_norm_w, v_attn_sinks, v_ln_final]
    padr = lambda a: jnp.pad(a, ((0, (-a.shape[0]) % 8), (0, 0)))
    vpad = jnp.pad(cat0(vs), ((0, (-n_small) % 8), (0, 0)), constant_values=1.0)
    outs_s = _adamw(padr(cat0(ws)), padr(cat0(ms)), vpad, jnp.pad(parts_small, ((0, 0), (0, (-n_small) % 8), (0, 0))), "adamw_small")
    off = 0
    for nme, wv in zip(names_s, ws):
        nrow = -(-wv.size // 128)
        res[nme] = [o_[off:off + nrow].reshape(-1)[:wv.size].reshape(wv.shape) for o_ in outs_s]
        off += nrow

    order = ["ln_mix", "ln_ffn", "w_ada", "b_ada", "w_in", "dn_conv_w", "dn_a_log", "dn_dt_bias", "dn_norm_w", "attn_sinks",
             "w_out", "w_gate_up", "w_down", "ln_final"]
    return (loss, dx[None], *[res[n][0] for n in order], *[res[n][1] for n in order], *[res[n][2] for n in order],
            *[res[n][3] for n in order])
```

```python
import functools

import jax
import jax.numpy as jnp
from jax import lax
from jax.experimental import pallas as pl
from jax.experimental.pallas import tpu as pltpu

F32, BF16 = jnp.float32, jnp.bfloat16
HI = lax.Precision.HIGH
MESH = pl.DeviceIdType.MESH
ANY = pl.BlockSpec(memory_space=pl.ANY)

N_DEV = 8
D = 2048
DN_HEADS, DN_HD = 8, 128
DN_W = 1024
CONV_K = 4
CHUNK = 64
AT_HD, AT_QH, AT_KVH = 64, 16, 2
AT_W = 1024
WINDOW = 128
FFN = 5632
IN_COLS = 5392
IN_PAD = 5632
EPS = 1e-6
NEG = -1e30
LR, B1, B2, AEPS, WD, STEP = 0.001, 0.9, 0.999, 1e-08, 0.01, 10
VMEM_LIMIT = 56 * 1024 * 1024
FORWARD_AT = 7


def _pick(n, cands):
    for c in cands:
        if n % c == 0:
            return c
    return n


def _params(sem):
    return pltpu.CompilerParams(dimension_semantics=sem, vmem_limit_bytes=VMEM_LIMIT)


_DN = {"nn": (((1,), (0,)), ((), ())), "nt": (((1,), (1,)), ((), ())), "tn": (((0,), (0,)), ((), ()))}


def _mm(a, b, mode, out_dtype, name, acc_in=None, gather_srcs=(), a2a_srcs=(), gate=None):
    if mode == "nn":
        (M, K), (_, N) = a.shape, b.shape
    elif mode == "nt":
        (M, K), (N, _) = a.shape, b.shape
    else:
        (K, M), (_, N) = a.shape, b.shape
    tm = _pick(M, (1024, 512, 256, 128, 64, 32, 16))
    tn = _pick(N, (1024, 512, 256, 128))
    tk = K if K <= 2048 else _pick(K, (2816, 2048, 1024, 512, 256, 128))
    nk = K // tk
    dn = _DN[mode]
    ins = [a, b] + ([] if acc_in is None else [acc_in]) + (list(gate) if gate else [])
    n_mm, ng, na = len(ins), len(gather_srcs), len(a2a_srcs)
    n_in = n_mm + ng + na
    n_out = 1 + (gate is not None)
    steps = (M // tm) * (N // tn) * nk

    def body(*refs):
        a_ref, b_ref, o_ref = refs[0], refs[1], refs[n_in]
        outs = refs[n_in + n_out:n_in + n_out + ng + na]
        scratch = refs[n_in + n_out + ng + na:]
        sems = scratch[(nk > 1):]
        k = pl.program_id(2)
        step = (pl.program_id(0) * (N // tn) + pl.program_id(1)) * nk + k
        plans = [_gather_plan(refs[n_mm + g], outs[g], *sems[3 * g:3 * g + 3]) for g in range(ng)]
        if na:
            plans.append(_a2a_plan(refs[n_mm + ng:n_in], outs[ng:], *sems[3 * ng:]))
        for plan in plans:
            pl.when(step == 0)(plan[0])
        for plan in plans[:ng]:
            pl.when(step == steps * FORWARD_AT // 8)(plan[1])

        def emit(val):
            if acc_in is not None:
                val = val + refs[2][...]
            if gate:
                x_ref, g_ref, bg_ref = refs[n_mm - 3:n_mm]
                o_ref[...] = x_ref[...] + (g_ref[...] + bg_ref[...]) * val
                refs[n_in + 1][...] = val.astype(BF16)
            else:
                o_ref[...] = val.astype(o_ref.dtype)

        part = lax.dot_general(a_ref[...].astype(BF16), b_ref[...].astype(BF16), dn, preferred_element_type=F32)
        if nk == 1:
            emit(part)
        else:
            acc = scratch[0]

            @pl.when(k == 0)
            def _():
                acc[...] = part

            @pl.when(k > 0)
            def _():
                acc[...] += part

            @pl.when(k == nk - 1)
            def _():
                emit(acc[...])
        for plan in plans:
            pl.when(step == steps - 1)(plan[-1])

    a_spec = pl.BlockSpec((tk, tm), lambda i, j, k: (k, i)) if mode == "tn" else pl.BlockSpec((tm, tk), lambda i, j, k: (i, k))
    b_spec = pl.BlockSpec((tn, tk), lambda i, j, k: (j, k)) if mode == "nt" else pl.BlockSpec((tk, tn), lambda i, j, k: (k, j))
    o_spec = pl.BlockSpec((tm, tn), lambda i, j, k: (i, j))
    row_spec = pl.BlockSpec((1, tn), lambda i, j, k: (0, j))
    carried = list(gather_srcs) + list(a2a_srcs)
    res = pl.pallas_call(
        body, name=name, grid=(M // tm, N // tn, nk),
        in_specs=[a_spec, b_spec] + ([] if acc_in is None else [o_spec]) + ([o_spec, row_spec, row_spec] if gate else []) + [ANY] * len(carried),
        out_specs=[o_spec] * n_out + [ANY] * len(carried),
        out_shape=[jax.ShapeDtypeStruct((M, N), F32 if gate else out_dtype)] + ([jax.ShapeDtypeStruct((M, N), BF16)] if gate else [])
        + [jax.ShapeDtypeStruct((N_DEV,) + g.shape, g.dtype) for g in gather_srcs] + [jax.ShapeDtypeStruct(g.shape, g.dtype) for g in a2a_srcs],
        scratch_shapes=([pltpu.VMEM((tm, tn), F32)] if nk > 1 else []) + GATHER_SEMS * ng + (_a2a_sems(na) if na else []),
        compiler_params=_params(("arbitrary",) * 3 if carried else ("parallel", "parallel", "arbitrary")),
    )(*ins, *carried)
    return res if len(res) > 1 else res[0]


def _op(op):
    arr, rt, cb = op[:3]
    start, width = op[3] if len(op) > 3 else (0, arr.shape[1])
    return arr, rt, cb, start, width


def _ew_spec(op, tm, ncol):
    arr, rt, cb, start, width = _op(op)
    bw = width // ncol if cb else width
    first = start // bw
    return pl.BlockSpec((tm if rt else arr.shape[0], bw), lambda j, i: (i if rt else 0, first + (j if cb else 0)))


def _ew_fwd(f, ops, outs, *, tm, ncol=1, name, with_j=False):
    M = next(op[0].shape[0] for op in ops if op[1])
    tm = min(tm, M)
    n_in = len(ops)

    def body(*refs):
        res = f(*[r[...].astype(F32) for r in refs[:n_in]], *([pl.program_id(0)] if with_j else []))
        for o, r in zip(refs[n_in:], res):
            o[...] = r.astype(o.dtype)

    return pl.pallas_call(
        body, name=name, grid=(ncol, M // tm),
        in_specs=[_ew_spec(op, tm, ncol) for op in ops],
        out_specs=[pl.BlockSpec((tm, w // ncol if cb else w), lambda j, i, cb=cb: (i, j if cb else 0)) for w, _, cb in outs],
        out_shape=[jax.ShapeDtypeStruct((M, w), dt) for w, dt, _ in outs],
        compiler_params=_params(("parallel", "parallel")),
    )(*[op[0] for op in ops])


def _ew_bwd(f, ops, cts, diff, *, tm, ncol=1, name, gdt=None, with_j=False):
    M = next(op[0].shape[0] for op in ops if op[1])
    tm = min(tm, M)
    n_in, n_ct = len(ops), len(cts)
    gdt = gdt or [F32] * len(diff)

    def body(*refs):
        j, i = pl.program_id(0), pl.program_id(1)
        vals = [r[...].astype(F32) for r in refs[:n_in]]

        def fd(*dv):
            full = list(vals)
            for idx, v in zip(diff, dv):
                full[idx] = v
            return tuple(f(*full, *([j] if with_j else [])))

        _, vjp = jax.vjp(fd, *[vals[idx] for idx in diff])
        gs = vjp(tuple(r[...].astype(F32) for r in refs[n_in:n_in + n_ct]))
        for idx, g, gref in zip(diff, gs, refs[n_in + n_ct:]):
            _, rt, cb = ops[idx][:3]
            if rt:
                gref[...] = g.astype(gref.dtype)
            else:
                first = (i == 0) if cb else jnp.logical_and(i == 0, j == 0)

                @pl.when(first)
                def _(gref=gref):
                    gref[...] = jnp.zeros_like(gref)

                gref[...] += g

    def g_spec(op):
        arr, rt, cb, start, width = _op(op)
        return _ew_spec((jax.ShapeDtypeStruct((arr.shape[0], width), F32), rt, cb), tm, ncol) if rt else _ew_spec(op, tm, ncol)

    def g_shape(op, dt):
        arr, rt, cb, start, width = _op(op)
        return jax.ShapeDtypeStruct((M, width), dt) if rt else jax.ShapeDtypeStruct(arr.shape, F32)

    return pl.pallas_call(
        body, name=name, grid=(ncol, M // tm),
        in_specs=[_ew_spec(op, tm, ncol) for op in ops]
        + [pl.BlockSpec((tm, a.shape[1] // ncol if cb else a.shape[1]), lambda j, i, cb=cb: (i, j if cb else 0)) for a, cb in cts],
        out_specs=[g_spec(ops[idx]) for idx in diff],
        out_shape=[g_shape(ops[idx], dt) for idx, dt in zip(diff, gdt)],
        compiler_params=_params(("arbitrary", "arbitrary")),
    )(*[op[0] for op in ops], *[a for a, _ in cts])


def _silu(x):
    return x * jax.nn.sigmoid(x)


def _f_normmod(x, ln, sc, bsc, sh, bsh):
    y = x * lax.rsqrt(jnp.mean(x * x, axis=-1, keepdims=True) + EPS) * ln
    return y * (1.0 + (sc + bsc)) + (sh + bsh), x


def _f_resgate(x, br, gt, bgt):
    return (x + (gt + bgt) * br,)


def _f_swiglu(gate, up):
    return (_silu(gate) * up,)


def _f_final(x, tgt, ln):
    y = x * lax.rsqrt(jnp.mean(x * x, axis=-1, keepdims=True) + EPS) * ln
    e = y - tgt
    return (jnp.broadcast_to(0.5 * jnp.mean(e * e, axis=-1, keepdims=True), (x.shape[0], 128)),)


def _f_bg(ba, alog, dt):
    col = lax.broadcasted_iota(jnp.int32, ba.shape, 1)
    z = ba + dt
    sp = jnp.maximum(z, 0.0) + jnp.log(1.0 + jnp.exp(-jnp.abs(z)))
    return (jnp.where(col < 8, jax.nn.sigmoid(ba), jnp.where(col < 16, -jnp.exp(alog) * sp, 0.0)),)


def _l2n(x):
    return x * lax.rsqrt(jnp.sum(x * x, axis=-1, keepdims=True) + EPS)


def _f_dnpre(c, j):
    a = _silu(c)
    return (jnp.where(j < 2 * DN_HEADS, _l2n(a) * jnp.where(j < DN_HEADS, DN_HD ** -0.5, 1.0), a),)


def _f_dnpost(o, z, nw):
    return (o * lax.rsqrt(jnp.mean(o * o, axis=-1, keepdims=True) + EPS) * nw * _silu(z),)


CONV_BW = 512


def _conv_fwd(x, w, Cw, name, gather_src=None):
    T = x.shape[0]
    tm, bw = min(1024, T), CONV_BW
    carry = gather_src is not None
    steps = (Cw // bw) * (T // tm)

    def body(*refs):
        x_ref, h_ref, w_ref = refs[:3]
        o_ref = refs[3 + carry]
        i = pl.program_id(1)
        if carry:
            step = pl.program_id(0) * (T // tm) + i
            start, forward, finish = _gather_plan(refs[3], refs[5], *refs[6:])
            pl.when(step == 0)(start)
            pl.when(step == steps * FORWARD_AT // 8)(forward)
        cur, halo, wv = x_ref[...], h_ref[...], w_ref[...]
        halo = jnp.where(i > 0, halo, 0.0)
        row = lax.broadcasted_iota(jnp.int32, (8, bw), 0)
        acc = wv[3:4, :] * cur
        for s in (1, 2, 3):
            r = pltpu.roll(cur, s, 0)
            top = jnp.where(row < s, pltpu.roll(halo, s, 0), r[:8])
            acc += wv[3 - s:4 - s, :] * jnp.concatenate([top, r[8:]], axis=0)
        o_ref[...] = acc
        if carry:
            pl.when(step == steps - 1)(finish)

    res = pl.pallas_call(
        body, name=name, grid=(Cw // bw, T // tm),
        in_specs=[pl.BlockSpec((tm, bw), lambda j, i: (i, j)),
                  pl.BlockSpec((8, bw), lambda j, i: (jnp.maximum(i * (tm // 8) - 1, 0), j)),
                  pl.BlockSpec((4, bw), lambda j, i: (0, j))] + [ANY] * carry,
        out_specs=[pl.BlockSpec((tm, bw), lambda j, i: (i, j))] + [ANY] * carry,
        out_shape=[jax.ShapeDtypeStruct((T, Cw), F32)] + ([jax.ShapeDtypeStruct((N_DEV,) + gather_src.shape, gather_src.dtype)] if carry else []),
        scratch_shapes=GATHER_SEMS if carry else [],
        compiler_params=_params(("arbitrary", "arbitrary") if carry else ("parallel", "parallel")),
    )(x, x, w, *([gather_src] if carry else []))
    return res if carry else res[0]


def _conv_bwd(x, dy, w, name):
    T, Cw = dy.shape
    tm, bw = min(1024, T), CONV_BW
    nt = T // tm

    def body(x_ref, h_ref, dy_ref, n_ref, w_ref, dx_ref, dw_ref):
        i = pl.program_id(1)
        cur, dcur, wv = x_ref[...], dy_ref[...], w_ref[...]
        halo = jnp.where(i > 0, h_ref[...], 0.0)
        nxt = jnp.where(i < nt - 1, n_ref[...], 0.0)
        row = lax.broadcasted_iota(jnp.int32, (8, bw), 0)

        @pl.when(i == 0)
        def _():
            dw_ref[...] = jnp.zeros_like(dw_ref)

        dx = wv[3:4, :] * dcur
        dw_ref[3:4, :] += jnp.sum(dcur * cur, axis=0, keepdims=True)
        for s in (1, 2, 3):
            r = pltpu.roll(cur, s, 0)
            top = jnp.where(row < s, pltpu.roll(halo, s, 0), r[:8])
            xs = jnp.concatenate([top, r[8:]], axis=0)
            dw_ref[3 - s:4 - s, :] += jnp.sum(dcur * xs, axis=0, keepdims=True)
            rf = pltpu.roll(dcur, tm - s, 0)
            bot = jnp.where(row >= 8 - s, pltpu.roll(nxt, 8 - s, 0), rf[tm - 8:])
            dx += wv[3 - s:4 - s, :] * jnp.concatenate([rf[:tm - 8], bot], axis=0)
        dx_ref[...] = dx

    return pl.pallas_call(
        body, name=name, grid=(Cw // bw, nt),
        in_specs=[pl.BlockSpec((tm, bw), lambda j, i: (i, j)),
                  pl.BlockSpec((8, bw), lambda j, i: (jnp.maximum(i * (tm // 8) - 1, 0), j)),
                  pl.BlockSpec((tm, bw), lambda j, i: (i, j)),
                  pl.BlockSpec((8, bw), lambda j, i: (jnp.minimum((i + 1) * (tm // 8), T // 8 - 1), j)),
                  pl.BlockSpec((4, bw), lambda j, i: (0, j))],
        out_specs=[pl.BlockSpec((tm, bw), lambda j, i: (i, j)), pl.BlockSpec((4, bw), lambda j, i: (0, j))],
        out_shape=[jax.ShapeDtypeStruct((T, Cw), F32), jax.ShapeDtypeStruct((4, Cw), F32)],
        compiler_params=_params(("arbitrary", "arbitrary")),
    )(x, x, dy, dy, w)


def _dot(a, b):
    return jnp.dot(a, b, precision=HI, preferred_element_type=F32)


def _dot_t(a, b):
    return lax.dot_general(a, b, (((1,), (1,)), ((), ())), precision=HI, preferred_element_type=F32)


def _bdot(a, b):
    return jnp.dot(a.astype(BF16), b.astype(BF16), preferred_element_type=F32)


def _bdot_t(a, b):
    return lax.dot_general(a.astype(BF16), b.astype(BF16), (((1,), (1,)), ((), ())), preferred_element_type=F32)


def _each(f, *lists):
    return [f(*a) for a in zip(*lists)]


@jax.custom_vjp
def _unit_lower_inverses(ps):
    C = ps[0].shape[0]
    dist = jnp.bitwise_xor(lax.broadcasted_iota(jnp.int32, (C, C), 0), lax.broadcasted_iota(jnp.int32, (C, C), 1))
    ns = [jnp.where(dist < 8, p, 0.0) for p in ps]
    tis = [jnp.where(dist == 0, 1.0, 0.0) + n for n in ns]
    for _ in range(2):
        ns = _each(lambda n: _bdot(n, n), ns)
        tis = _each(lambda t, n: t + _bdot(t, n), tis, ns)
    b = 8
    while b < C:
        mids = [jnp.where(jnp.logical_and(dist >= b, dist < 2 * b), p, 0.0) for p in ps]
        halves = _each(_bdot, mids, tis)
        tis = _each(lambda t, h: t + _bdot(t, h), tis, halves)
        b *= 2
    return tis


def _uli_fwd(ps):
    tis = _unit_lower_inverses(ps)
    return tis, tis


def _uli_bwd(tis, cts):
    tt = lambda a, b: lax.dot_general(a, b, (((0,), (0,)), ((), ())), precision=HI, preferred_element_type=F32)
    half = _each(tt, tis, cts)
    return (_each(_dot_t, half, tis),)


_unit_lower_inverses.defvjp(_uli_fwd, _uli_bwd)


@jax.custom_vjp
def _kept_inverses(ps, tis):
    return tis


_kept_inverses.defvjp(lambda ps, tis: (tis, tis), lambda tis, cts: (_uli_bwd(tis, cts)[0], [jnp.zeros_like(t) for t in tis]))


def _chunk_f(qs, ks, vs, bg, Ss, hs, kept=None):
    C = CHUNK
    lane = lax.broadcasted_iota(jnp.int32, (C, 128), 1)
    betas = [jnp.sum(jnp.where(lane == h, bg, 0.0), axis=1, keepdims=True) for h in hs]
    gs = [jnp.sum(jnp.where(lane == h + 8, bg, 0.0), axis=1, keepdims=True) for h in hs]
    ri = lax.broadcasted_iota(jnp.int32, (C, C), 0)
    ci = lax.broadcasted_iota(jnp.int32, (C, C), 1)
    causal, strict, eye = ri >= ci, ri > ci, ri == ci
    g_rows = _each(lambda g: jnp.sum(jnp.where(eye, g, 0.0), axis=0, keepdims=True), gs)
    gc_cols = _each(lambda gr: jnp.sum(jnp.where(causal, gr, 0.0), axis=1, keepdims=True), g_rows)
    gc_rows = _each(lambda g: jnp.sum(jnp.where(ri <= ci, g, 0.0), axis=0, keepdims=True), gs)
    gc_lasts = _each(lambda g: jnp.sum(g, axis=0, keepdims=True), gs)
    decays = _each(lambda c, r: jnp.exp(jnp.where(causal, c - r, NEG)), gc_cols, gc_rows)
    kbs = _each(jnp.multiply, ks, betas)
    vbs = _each(jnp.multiply, vs, betas)
    ps = _each(lambda kb, k, d: -jnp.where(strict, _bdot_t(kb, k) * d, 0.0), kbs, ks, decays)
    tis = _unit_lower_inverses(ps) if kept is None else _kept_inverses(ps, kept)
    egcs = _each(jnp.exp, gc_cols)
    ws = _each(lambda t, kb, e: _bdot(t, kb * e), tis, kbs, egcs)
    us = _each(_bdot, tis, vbs)
    intras = _each(lambda q, k, d: jnp.where(causal, _bdot_t(q, k) * d, 0.0), qs, ks, decays)
    kds = _each(lambda k, gl, gc: k * jnp.exp(gl - gc), ks, gc_lasts, gc_cols)
    vns = _each(lambda u, w, S: u - _bdot(w, S), us, ws, Ss)
    os_ = _each(lambda q, e, S, i, vn: _bdot(q * e, S) + _bdot(i, vn), qs, egcs, Ss, intras, vns)
    Sn = _each(lambda S, gl, kd, vn: S * jnp.exp(gl) + lax.dot_general(
        kd.astype(BF16), vn.astype(BF16), (((0,), (0,)), ((), ())), preferred_element_type=F32), Ss, gc_lasts, kds, vns)
    return (os_, Sn), tis


def _head_cols(ref, part):
    return [ref[:, part * DN_W + e * DN_HD:part * DN_W + (e + 1) * DN_HD] for e in range(DN_HEADS)]


def _chunk_fwd(qkv, bg, name, gather_src=None):
    T = qkv.shape[0]
    N = T // CHUNK
    H = DN_HEADS
    carry = gather_src is not None

    def body(*refs):
        x_ref, bg_ref = refs[:2]
        o_ref, s_ref, t_ref = refs[2 + carry:5 + carry]
        S = refs[5 + 2 * carry]
        n = pl.program_id(0)
        if carry:
            start, forward, finish = _gather_plan(refs[2], refs[6], *refs[8:])
            pl.when(n == 0)(start)
            pl.when(n == N * FORWARD_AT // 8)(forward)

        @pl.when(n == 0)
        def _():
            S[...] = jnp.zeros_like(S)

        s_all = S[...]
        s_ref[0] = s_all
        (os_, nxt), tis = _chunk_f(_head_cols(x_ref, 0), _head_cols(x_ref, 1), _head_cols(x_ref, 2), bg_ref[...],
                                   [s_all[e] for e in range(H)], list(range(H)))
        for e in range(H):
            o_ref[:, e * DN_HD:(e + 1) * DN_HD] = os_[e]
            S[e] = nxt[e]
            t_ref[0, e] = tis[e]
        if carry:
            pl.when(n == N - 1)(finish)

    gshape = [jax.ShapeDtypeStruct((N_DEV,) + gather_src.shape, gather_src.dtype)] if carry else []
    return pl.pallas_call(
        body, name=name, grid=(N,),
        in_specs=[pl.BlockSpec((CHUNK, 3 * DN_W), lambda n: (n, 0)), pl.BlockSpec((CHUNK, 128), lambda n: (n, 0))] + [ANY] * carry,
        out_specs=[pl.BlockSpec((CHUNK, DN_W), lambda n: (n, 0)), pl.BlockSpec((1, H, DN_HD, DN_HD), lambda n: (n, 0, 0, 0)),
                   pl.BlockSpec((1, H, CHUNK, CHUNK), lambda n: (n, 0, 0, 0))] + [ANY] * carry,
        out_shape=[jax.ShapeDtypeStruct((T, DN_W), F32), jax.ShapeDtypeStruct((N, H, DN_HD, DN_HD), F32),
                   jax.ShapeDtypeStruct((N, H, CHUNK, CHUNK), F32)] + gshape,
        scratch_shapes=[pltpu.VMEM((H, DN_HD, DN_HD), F32)] + (GATHER_SEMS if carry else []),
        compiler_params=_params(("arbitrary",)),
    )(qkv, bg, *([gather_src] if carry else []))


def _chunk_bwd(qkv, bg, s_saved, t_saved, do, name, a2a_srcs=()):
    T = qkv.shape[0]
    N = T // CHUNK
    H = DN_HEADS
    na = len(a2a_srcs)

    def body(*refs):
        x_ref, bg_ref, s_ref, t_ref, do_ref = refs[:5]
        dx_ref, dbg_ref = refs[5 + na:7 + na]
        dS = refs[7 + 2 * na]
        n = pl.program_id(0)
        if na:
            start, finish = _a2a_plan(refs[5:5 + na], refs[7 + na:7 + 2 * na], *refs[8 + 2 * na:])
            pl.when(n == 0)(start)

        @pl.when(n == 0)
        def _():
            dS[...] = jnp.zeros_like(dS)

        ds_all = dS[...]
        _, vjp, _ = jax.vjp(functools.partial(_chunk_f, hs=list(range(H)), kept=[t_ref[0, e] for e in range(H)]), _head_cols(x_ref, 0),
                            _head_cols(x_ref, 1), _head_cols(x_ref, 2), bg_ref[...], [s_ref[0, e] for e in range(H)], has_aux=True)
        dq, dk, dv, dbg, nxt = vjp(([do_ref[:, e * DN_HD:(e + 1) * DN_HD] for e in range(H)], [ds_all[e] for e in range(H)]))
        for part, g in enumerate((dq, dk, dv)):
            for e in range(H):
                dx_ref[:, part * DN_W + e * DN_HD:part * DN_W + (e + 1) * DN_HD] = g[e]
        for e in range(H):
            dS[e] = nxt[e]
        dbg_ref[...] = dbg
        if na:
            pl.when(n == N - 1)(finish)

    rev = lambda n: (N - 1 - n, 0)
    return pl.pallas_call(
        body, name=name, grid=(N,),
        in_specs=[pl.BlockSpec((CHUNK, 3 * DN_W), rev), pl.BlockSpec((CHUNK, 128), rev),
                  pl.BlockSpec((1, H, DN_HD, DN_HD), lambda n: (N - 1 - n, 0, 0, 0)),
                  pl.BlockSpec((1, H, CHUNK, CHUNK), lambda n: (N - 1 - n, 0, 0, 0)), pl.BlockSpec((CHUNK, DN_W), rev)] + [ANY] * na,
        out_specs=[pl.BlockSpec((CHUNK, 3 * DN_W), rev), pl.BlockSpec((CHUNK, 128), rev)] + [ANY] * na,
        out_shape=[jax.ShapeDtypeStruct((T, 3 * DN_W), F32), jax.ShapeDtypeStruct((T, 128), F32)]
        + [jax.ShapeDtypeStruct(a.shape, a.dtype) for a in a2a_srcs],
        scratch_shapes=[pltpu.VMEM((H, DN_HD, DN_HD), F32)] + (_a2a_sems(na) if na else []),
        compiler_params=_params(("arbitrary",)),
    )(qkv, bg, s_saved, t_saved, do, *a2a_srcs)


GRP = AT_QH // AT_KVH


def _attn_f(q, kp, kc, vp, vc, sinks, cos_c, sin_c, cos_p, sin_p, rot, has_prev):
    def rope(x, c, s):
        return x * c + _dot(x, rot) * s

    kcr, kpr = rope(kc, cos_c, sin_c), rope(kp, cos_p, sin_p)
    qrs = [rope(qe, cos_c, sin_c) for qe in q]
    r = lax.broadcasted_iota(jnp.int32, (WINDOW, WINDOW), 0)
    j = lax.broadcasted_iota(jnp.int32, (WINDOW, WINDOW), 1)
    in_c, in_p = j <= r, jnp.logical_and(j > r, has_prev)
    lane = lax.broadcasted_iota(jnp.int32, (1, 128), 1)
    scs = [jnp.where(in_c, _bdot_t(qr, kcr) * (AT_HD ** -0.5), NEG) for qr in qrs]
    sps = [jnp.where(in_p, _bdot_t(qr, kpr) * (AT_HD ** -0.5), NEG) for qr in qrs]
    snk = [jnp.sum(jnp.where(lane == 0, s, 0.0), axis=1, keepdims=True) for s in sinks]
    ms = _each(lambda sc, sp, s: lax.stop_gradient(jnp.maximum(jnp.maximum(jnp.max(sc, axis=1, keepdims=True),
                                                                           jnp.max(sp, axis=1, keepdims=True)), s)), scs, sps, snk)
    pcs = _each(lambda sc, m: jnp.exp(sc - m), scs, ms)
    pps = _each(lambda sp, m: jnp.exp(sp - m), sps, ms)
    dens = _each(lambda pc, pp, s, m: jnp.sum(pc, axis=1, keepdims=True) + jnp.sum(pp, axis=1, keepdims=True) + jnp.exp(s - m),
                 pcs, pps, snk, ms)
    return tuple(_each(lambda pc, pp, den: (_bdot(pc, vc) + _bdot(pp, vp)) / den, pcs, pps, dens))


def _attn_specs(nb):
    qs = pl.BlockSpec((GRP, WINDOW, AT_HD), lambda g, n: (g, n, 0))
    kc = pl.BlockSpec((1, WINDOW, AT_HD), lambda g, n: (g, n, 0))
    kp = pl.BlockSpec((1, WINDOW, AT_HD), lambda g, n: (g, jnp.maximum(n - 1, 0), 0))
    tc = pl.BlockSpec((WINDOW, AT_HD), lambda g, n: (n, 0))
    tp = pl.BlockSpec((WINDOW, AT_HD), lambda g, n: (jnp.maximum(n - 1, 0), 0))
    sk = pl.BlockSpec((GRP, 1, 128), lambda g, n: (g, 0, 0))
    rt = pl.BlockSpec((AT_HD, AT_HD), lambda g, n: (0, 0))
    return qs, kc, kp, tc, tp, sk, rt


def _attn_fwd(q, k, v, sinks, cos, sin, rot, name, gather_src=None):
    T = q.shape[1]
    nb = T // WINDOW
    qs, kc, kp, tc, tp, sk, rt = _attn_specs(nb)
    carry = gather_src is not None

    def body(*refs):
        q_ref, kp_ref, kc_ref, vp_ref, vc_ref, sk_ref, cc_ref, sc_ref, cp_ref, sp_ref, rot_ref = refs[:11]
        o_ref = refs[11 + carry]
        n = pl.program_id(1)
        if carry:
            step = pl.program_id(0) * nb + n
            start, forward, finish = _gather_plan(refs[11], refs[13], *refs[14:])
            pl.when(step == 0)(start)
            pl.when(step == AT_KVH * nb * FORWARD_AT // 8)(forward)
        heads = range(GRP)
        outs = _attn_f(tuple(q_ref[e] for e in heads), kp_ref[0], kc_ref[0], vp_ref[0], vc_ref[0], tuple(sk_ref[e] for e in heads), cc_ref[...], sc_ref[...],
                       cp_ref[...], sp_ref[...], rot_ref[...], n > 0)
        for e in range(GRP):
            o_ref[e] = outs[e]
        if carry:
            pl.when(step == AT_KVH * nb - 1)(finish)

    gshape = [jax.ShapeDtypeStruct((N_DEV,) + gather_src.shape, gather_src.dtype)] if carry else []
    return pl.pallas_call(
        body, name=name, grid=(AT_KVH, nb),
        in_specs=[qs, kp, kc, kp, kc, sk, tc, tc, tp, tp, rt] + [ANY] * carry, out_specs=[qs] + [ANY] * carry,
        out_shape=[jax.ShapeDtypeStruct(q.shape, F32)] + gshape,
        scratch_shapes=GATHER_SEMS if carry else [],
        compiler_params=_params(("arbitrary", "arbitrary") if carry else ("parallel", "parallel")),
    )(q, k, k, v, v, sinks, cos, sin, cos, sin, rot, *([gather_src] if carry else []))


def _attn_bwd(q, k, v, sinks, cos, sin, rot, do, name, a2a_srcs=()):
    T = q.shape[1]
    nb = T // WINDOW
    qs, kc, kp, tc, tp, sk, rt = _attn_specs(nb)
    na = len(a2a_srcs)

    def body(*refs):
        q_ref, kp_ref, kc_ref, vp_ref, vc_ref, sk_ref, cc_ref, sc_ref, cp_ref, sp_ref, rot_ref, do_ref = refs[:12]
        dq_ref, dkp_ref, dkc_ref, dvp_ref, dvc_ref, dsk_ref = refs[12 + na:18 + na]
        n = pl.program_id(1)
        if na:
            step = pl.program_id(0) * nb + n
            start, finish = _a2a_plan(refs[12:12 + na], refs[18 + na:18 + 2 * na], *refs[18 + 2 * na:])
            pl.when(step == 0)(start)
        f = functools.partial(_attn_f, cos_c=cc_ref[...], sin_c=sc_ref[...], cos_p=cp_ref[...], sin_p=sp_ref[...],
                              rot=rot_ref[...], has_prev=n > 0)
        heads = range(GRP)
        _, vjp = jax.vjp(f, tuple(q_ref[e] for e in heads), kp_ref[0], kc_ref[0], vp_ref[0], vc_ref[0], tuple(sk_ref[e] for e in heads))
        dq, dkp, dkc, dvp, dvc, dsk = vjp(tuple(do_ref[e] for e in heads))
        dkp_ref[0], dkc_ref[0], dvp_ref[0], dvc_ref[0] = dkp, dkc, dvp, dvc

        @pl.when(n == 0)
        def _():
            dsk_ref[...] = jnp.zeros_like(dsk_ref)

        for e in heads:
            dq_ref[e] = dq[e]
            dsk_ref[e] += dsk[e]
        if na:
            pl.when(step == AT_KVH * nb - 1)(finish)

    return pl.pallas_call(
        body, name=name, grid=(AT_KVH, nb),
        in_specs=[qs, kp, kc, kp, kc, sk, tc, tc, tp, tp, rt, qs] + [ANY] * na, out_specs=[qs, kc, kc, kc, kc, sk] + [ANY] * na,
        out_shape=[jax.ShapeDtypeStruct(q.shape, F32)] + [jax.ShapeDtypeStruct(k.shape, F32)] * 4 + [jax.ShapeDtypeStruct(sinks.shape, F32)]
        + [jax.ShapeDtypeStruct(a.shape, a.dtype) for a in a2a_srcs],
        scratch_shapes=_a2a_sems(na) if na else [],
        compiler_params=_params(("arbitrary", "arbitrary")),
    )(q, k, k, v, v, sinks, cos, sin, cos, sin, rot, do, *a2a_srcs)


def _kv_combine(dc, dp, name):
    T = dc.shape[1]
    R = 8 * WINDOW
    ns = T // R

    def body(c_ref, p_ref, q_ref, o_ref):
        n = pl.program_id(1)
        tail = jnp.where(n < ns - 1, q_ref[0], 0.0)
        o_ref[0] = c_ref[0] + jnp.concatenate([p_ref[0, WINDOW:, :], tail], axis=0)

    return pl.pallas_call(
        body, name=name, grid=(AT_KVH, ns),
        in_specs=[pl.BlockSpec((1, R, AT_HD), lambda g, n: (g, n, 0)), pl.BlockSpec((1, R, AT_HD), lambda g, n: (g, n, 0)),
                  pl.BlockSpec((1, WINDOW, AT_HD), lambda g, n: (g, jnp.minimum((n + 1) * 8, T // WINDOW - 1), 0))],
        out_specs=pl.BlockSpec((1, R, AT_HD), lambda g, n: (g, n, 0)),
        out_shape=jax.ShapeDtypeStruct((AT_KVH, T, AT_HD), F32),
        compiler_params=_params(("parallel", "parallel")),
    )(dc, dp, dp)


def _place():
    x, y, c = lax.axis_index("x"), lax.axis_index("y"), lax.axis_index("c")
    return x, y, c, 4 * x + 2 * y + c


def _gather_plan(s_ref, o_ref, send_sems, recv_sems, lsem):
    x, y, c, _ = _place()
    me, sib = (x, y, c), (x, y, 1 - c)
    chips = [(1 - x, y), (x, 1 - y), (1 - x, 1 - y)]

    def copy(k, block, to, src_ref=None):
        slab = o_ref.at[4 * block[0] + 2 * block[1] + block[2]]
        return pltpu.make_async_remote_copy(src_ref=slab if src_ref is None else src_ref, dst_ref=slab,
                                            send_sem=send_sems.at[k], recv_sem=recv_sems.at[k], device_id=to, device_id_type=MESH)

    mine = pltpu.make_async_copy(s_ref, o_ref.at[4 * x + 2 * y + c], lsem)
    first = [copy(0, me, sib, s_ref)] + [copy(1 + j, me, (*chip, c), s_ref) for j, chip in enumerate(chips)]
    passed = [copy(4 + j, (*chip, c), sib) for j, chip in enumerate(chips)]

    def start():
        mine.start()
        for cp in first:
            cp.start()

    def forward():
        for j, chip in enumerate(chips):
            copy(1 + j, (*chip, c), me).wait_recv()
            passed[j].start()

    def finish():
        copy(0, sib, me).wait_recv()
        for j, chip in enumerate(chips):
            copy(4 + j, (*chip, 1 - c), me).wait_recv()
        for cp in first + passed:
            cp.wait_send()
        mine.wait()

    return start, forward, finish


GATHER_SEMS = [pltpu.SemaphoreType.DMA((N_DEV - 1,)), pltpu.SemaphoreType.DMA((N_DEV - 1,)), pltpu.SemaphoreType.DMA]


def _gather(src, name):
    def body(s_ref, o_ref, send_sems, recv_sems, lsem):
        for phase in _gather_plan(s_ref, o_ref, send_sems, recv_sems, lsem):
            phase()

    return pl.pallas_call(
        body, name=name, in_specs=[ANY], out_specs=ANY, out_shape=jax.ShapeDtypeStruct((N_DEV,) + src.shape, src.dtype),
        scratch_shapes=GATHER_SEMS,
    )(src)


def _pair_swap(src, name):
    def body(s_ref, o_ref, send_sem, recv_sem):
        x, y, c, _ = _place()
        cp = pltpu.make_async_remote_copy(src_ref=s_ref.at[1 - c], dst_ref=o_ref, send_sem=send_sem, recv_sem=recv_sem,
                                          device_id=(x, y, 1 - c), device_id_type=MESH)
        cp.start()
        cp.wait()

    return pl.pallas_call(
        body, name=name, in_specs=[ANY], out_specs=ANY, out_shape=jax.ShapeDtypeStruct(src.shape[1:], src.dtype),
        scratch_shapes=[pltpu.SemaphoreType.DMA, pltpu.SemaphoreType.DMA],
    )(src)


def _a2a_plan(s_refs, o_refs, send_sems, recv_sems, lsems):
    x, y, c, _ = _place()
    chip = 2 * x + y
    local, remote = [], []
    for a, (s_ref, o_ref) in enumerate(zip(s_refs, o_refs)):
        local.append(pltpu.make_async_copy(s_ref.at[chip], o_ref.at[chip], lsems.at[a]))
        for k in (1, 2, 3):
            px, py = x ^ (k >> 1), y ^ (k & 1)
            remote.append(pltpu.make_async_remote_copy(
                src_ref=s_ref.at[2 * px + py], dst_ref=o_ref.at[chip], send_sem=send_sems.at[3 * a + k - 1],
                recv_sem=recv_sems.at[3 * a + k - 1], device_id=(px, py, c), device_id_type=MESH))

    def start():
        for cp in local + remote:
            cp.start()

    def finish():
        for cp in remote + local:
            cp.wait()

    return start, finish


def _a2a_sems(n):
    return [pltpu.SemaphoreType.DMA((3 * n,)), pltpu.SemaphoreType.DMA((3 * n,)), pltpu.SemaphoreType.DMA((n,))]


def _add(a, b, name):
    R, C = a.shape
    tr = _pick(R, (512, 256, 128, 64, 32, 16))

    def body(a_ref, b_ref, o_ref):
        o_ref[...] = (a_ref[...].astype(F32) + b_ref[...].astype(F32)).astype(o_ref.dtype)

    s2 = pl.BlockSpec((tr, C), lambda i: (i, 0))
    return pl.pallas_call(body, name=name, grid=(R // tr,), in_specs=[s2, s2], out_specs=s2,
                          out_shape=jax.ShapeDtypeStruct((R, C), a.dtype), compiler_params=_params(("parallel",)))(a, b)


def _chip_sums(slabs, name):
    _, R, C = slabs.shape
    by_core = jnp.transpose(slabs.reshape(4, 2, R, C), (1, 0, 2, 3))
    theirs = _pair_swap(by_core, "swap_" + name)
    mine = lax.dynamic_index_in_dim(by_core, lax.axis_index("c"), axis=0, keepdims=False)
    return _add(mine.reshape(4 * R, C), theirs.reshape(4 * R, C), "add_" + name).reshape(4, R, C)


def _adamw(w, m, v, parts, name):
    R, C = w.shape
    P = parts.shape[0]
    tr = _pick(R, (256, 128, 64, 32, 16, 8))
    c1, c2 = 1.0 - B1 ** STEP, 1.0 - B2 ** STEP

    def body(w_ref, m_ref, v_ref, p_ref, g_ref, d_ref, nm_ref, nv_ref):
        g = p_ref[0].astype(F32)
        for i in range(1, P):
            g = g + p_ref[i].astype(F32)
        wv = w_ref[...]
        nm = B1 * m_ref[...] + (1.0 - B1) * g
        nv = B2 * v_ref[...] + (1.0 - B2) * (g * g)
        g_ref[...] = g
        nm_ref[...] = nm
        nv_ref[...] = nv
        d_ref[...] = -LR * ((nm / c1) / (jnp.sqrt(nv / c2) + AEPS) + WD * wv)

    s2 = pl.BlockSpec((tr, C), lambda i: (i, 0))
    return pl.pallas_call(
        body, name=name, grid=(R // tr,),
        in_specs=[s2, s2, s2, pl.BlockSpec((P, tr, C), lambda i: (0, i, 0))], out_specs=[s2] * 4,
        out_shape=[jax.ShapeDtypeStruct((R, C), F32)] * 4,
        compiler_params=_params(("parallel",)),
    )(w, m, v, parts)


def _colsum(a, name):
    def body(a_ref, o_ref):
        o_ref[...] = jnp.broadcast_to(jnp.sum(a_ref[...], axis=0, keepdims=True), o_ref.shape)

    return pl.pallas_call(body, name=name, out_shape=jax.ShapeDtypeStruct((8, 128), F32))(a)


def _rows128(a):
    f = a.reshape(-1)
    return jnp.pad(f, (0, (-f.shape[0]) % 128)).reshape(-1, 128)


def _to_aligned(w):
    return jnp.concatenate([w[..., 0:4096], w[..., 4112:5392], w[..., 4096:4112],
                            jnp.zeros(w.shape[:-1] + (IN_PAD - IN_COLS,), w.dtype)], axis=-1)


def _from_aligned(w):
    return jnp.concatenate([w[..., 0:4096], w[..., 5376:5392], w[..., 4096:5376]], axis=-1)


def kernel(x, c, ln_mix, ln_ffn, w_ada, b_ada, w_in, dn_conv_w, dn_a_log, dn_dt_bias, dn_norm_w, attn_sinks, w_out, w_gate_up, w_down, ln_final, loss_target, m_ln_mix, m_ln_ffn, m_w_ada, m_b_ada, m_w_in, m_dn_conv_w, m_dn_a_log, m_dn_dt_bias, m_dn_norm_w, m_attn_sinks, m_w_out, m_w_gate_up, m_w_down, m_ln_final, v_ln_mix, v_ln_ffn, v_w_ada, v_b_ada, v_w_in, v_dn_conv_w, v_dn_a_log, v_dn_dt_bias, v_dn_norm_w, v_attn_sinks, v_w_out, v_w_gate_up, v_w_down, v_ln_final):
    T = x.shape[1]
    L = ln_mix.shape[0]
    me = 4 * lax.axis_index("x") + 2 * lax.axis_index("y") + lax.axis_index("c")
    xs = x[0]
    tgt = loss_target[0]

    w_in_layout = lambda g: _to_aligned(jnp.transpose(g, (1, 0, 2)).reshape(D, IN_COLS))
    W_in = [w_in_layout(_gather(w_in[0].astype(BF16), "ag_w_in0"))] + [None] * (L - 1)
    W_out = None
    g_cv = _gather(dn_conv_w.reshape(L * CONV_K, -1), "ag_conv").reshape(N_DEV, L, CONV_K, -1)
    c_all = _gather(jnp.pad(c, ((0, 7), (0, 0))), "ag_c")[:, 0, :]
    W_gate, W_up, W_dn = [None] * L, [None] * L, [None] * L
    W_cv = [jnp.transpose(g_cv[:, l], (1, 0, 2)).reshape(CONV_K, 3 * DN_W) for l in range(L)]

    c_act = _ew_fwd(lambda v: (_silu(v),), [(jnp.pad(c_all, ((0, 8), (0, 0))), True, False)], [(D, F32, False)], tm=16, name="c_act")[0]
    mods = []
    for l in range(L):
        ms = _mm(c_act, w_ada[l], "nn", F32, f"mod_mm{l}")
        ga = _gather(ms, f"ag_mod{l}")
        mods.append(lax.dynamic_index_in_dim(ga, me, axis=1, keepdims=False).reshape(1, 6 * D))
    row = lambda a: a.reshape(1, -1)
    seg = lambda a, i: a[:, i * D:(i + 1) * D]

    half = AT_HD // 2
    inv_freq = 10000.0 ** (-jnp.arange(half, dtype=F32) * 2.0 / AT_HD)
    ang = jnp.arange(T, dtype=jnp.int32).astype(F32)[:, None] * inv_freq[None, :]
    cos = jnp.concatenate([jnp.cos(ang)] * 2, axis=-1)
    sin = jnp.concatenate([jnp.sin(ang)] * 2, axis=-1)
    ii = jnp.arange(AT_HD)
    rot = jnp.where(ii[:, None] == ii[None, :] + half, -1.0, 0.0) + jnp.where(ii[:, None] + half == ii[None, :], 1.0, 0.0)
    rot = rot.astype(F32)
    heads = lambda a, nh: jnp.transpose(a.reshape(T, nh, AT_HD), (1, 0, 2))
    unheads = lambda a: jnp.transpose(a, (1, 0, 2)).reshape(T, -1)
    pad16 = lambda a: jnp.pad(row(a), ((0, 0), (8, 128 - 16)))

    saved = []
    xc = xs
    for l in range(L):
        mod, bmod = mods[l], row(b_ada[l])
        s = {"x": xc}
        nm_ops = lambda xx, ln, a, b: [(xx, True, False), (row(ln), False, False), (seg(mod, a), False, False),
                                       (seg(bmod, a), False, False), (seg(mod, b), False, False), (seg(bmod, b), False, False)]
        h1 = _ew_fwd(lambda *a: _f_normmod(*a)[:1], nm_ops(xc, ln_mix[l], 1, 0), [(D, BF16, False)], tm=256, name=f"normmod1_{l}")[0]
        if l + 1 < L:
            proj, g_in = _mm(h1, W_in[l], "nn", F32, f"mm_in{l}", gather_srcs=[w_in[l + 1].astype(BF16)])
            W_in[l + 1] = w_in_layout(g_in)
        else:
            proj = _mm(h1, W_in[l], "nn", F32, f"mm_in{l}")
        aq, ak, av = proj[:, 4096:5120], proj[:, 5120:5248], proj[:, 5248:5376]
        if l == 0:
            conv, g_out = _conv_fwd(proj, W_cv[l], 3 * DN_W, f"conv{l}", gather_src=w_out.astype(BF16).reshape(-1, D))
            W_out = [g_out.reshape(N_DEV, L, -1, D)[:, i].reshape(D, D) for i in range(L)]
        else:
            conv = _conv_fwd(proj, W_cv[l], 3 * DN_W, f"conv{l}")
        pre_ops = [(conv, True, True)]
        qkvn = _ew_fwd(_f_dnpre, pre_ops, [(3 * DN_W, F32, True)], tm=4096, ncol=3 * DN_HEADS, name=f"dnpre{l}", with_j=True)[0]
        bg_ops = [(proj, True, False, (5376, 128)), (pad16(dn_a_log[l]), False, False), (pad16(dn_dt_bias[l]), False, False)]
        bg = _ew_fwd(_f_bg, bg_ops, [(128, F32, False)], tm=1024, name=f"bg{l}")[0]
        o, s_saved, t_saved, g_gu = _chunk_fwd(qkvn, bg, f"chunk{l}", gather_src=w_gate_up[l].astype(BF16))
        W_gate[l] = jnp.transpose(g_gu[:4], (1, 0, 2)).reshape(D, FFN)
        W_up[l] = jnp.transpose(g_gu[4:], (1, 0, 2)).reshape(D, FFN)
        post_ops = [(o, True, True), (proj, True, True, (3072, DN_W)), (row(dn_norm_w[l]), False, False)]
        dn_out = _ew_fwd(_f_dnpost, post_ops, [(DN_W, BF16, True)], tm=2048, ncol=DN_HEADS, name=f"dnpost{l}")[0]
        qh, kh, vh = heads(aq, AT_QH), heads(ak, AT_KVH), heads(av, AT_KVH)
        sk = jnp.broadcast_to(attn_sinks[l][:, None, None], (AT_QH, 1, 128))
        at_o, g_dn = _attn_fwd(qh, kh, vh, sk, cos, sin, rot, f"attn{l}", gather_src=w_down[l].astype(BF16))
        W_dn[l] = g_dn.reshape(FFN, D)
        at_out = unheads(at_o).astype(BF16)
        x1, mix = _mm(at_out, W_out[l][DN_W:], "nn", F32, f"mm_out_at{l}", acc_in=_mm(dn_out, W_out[l][:DN_W], "nn", F32, f"mm_out_dn{l}"),
                      gate=(xc, seg(mod, 2), seg(bmod, 2)))
        rg_ops = lambda xx, br, a: [(xx, True, False), (br, True, False), (seg(mod, a), False, False), (seg(bmod, a), False, False)]
        h2 = _ew_fwd(lambda *a: _f_normmod(*a)[:1], nm_ops(x1, ln_ffn[l], 4, 3), [(D, BF16, False)], tm=256, name=f"normmod2_{l}")[0]
        gate = _mm(h2, W_gate[l], "nn", BF16, f"mm_gate{l}")
        up = _mm(h2, W_up[l], "nn", BF16, f"mm_up{l}")
        sw_ops = [(gate, True, True), (up, True, True)]
        act = _ew_fwd(_f_swiglu, sw_ops, [(FFN, BF16, True)], tm=2048, ncol=11, name=f"swiglu{l}")[0]
        x2, down = _mm(act, W_dn[l], "nn", F32, f"mm_down{l}", gate=(x1, seg(mod, 5), seg(bmod, 5)))
        s.update(h1=h1, proj=proj, qkvn=qkvn, bg=bg, s_saved=s_saved, t_saved=t_saved, qh=qh, kh=kh, vh=vh, sk=sk, dn_out=dn_out, at_out=at_out,
                 h2=h2, act=act, bg_ops=bg_ops, post_ops=post_ops, pre_ops=pre_ops, sw_ops=sw_ops,
                 nm1=nm_ops(xc, ln_mix[l], 1, 0), nm2=nm_ops(x1, ln_ffn[l], 4, 3), rg1=rg_ops(xc, mix, 2), rg2=rg_ops(x1, down, 5))
        saved.append(s)
        xc = x2

    fin_ops = [(xc, True, False), (tgt, True, False), (row(ln_final), False, False)]
    lrow = _ew_fwd(_f_final, fin_ops, [(128, F32, False)], tm=256, name="loss_rows")[0]
    loss = lax.psum(_colsum(lrow, "loss_sum")[0, 0], ("x", "y", "c"))
    dx, d_ln_final = _ew_bwd(_f_final, fin_ops, [(jnp.ones((T, 128), F32) / 128.0, False)], [0, 2], tm=256, name="loss_bwd")

    small = {k: [None] * L for k in ("ln_mix", "ln_ffn", "mod", "a_log", "dt", "norm_w", "sinks", "conv")}
    big = {}
    recv_ffn, recv_mix = [None] * L, [None] * L
    pending = ()

    def shards(g, cols, n=N_DEV):
        return jnp.transpose(g.reshape(g.shape[0], n, -1), (1, 0, 2)) if cols else g.reshape(n, -1, g.shape[1])

    for l in reversed(range(L)):
        s = saved[l]
        ddown, dgt_f = _ew_bwd(_f_resgate, s["rg2"], [(dx, False)], [1, 2], tm=256, name=f"resgate2_bwd{l}", gdt=[BF16, F32])
        big["w_dn"] = _mm(s["act"], ddown, "tn", BF16, f"wg_down{l}")
        dact = _mm(ddown, W_dn[l], "nt", F32, f"dg_down{l}")
        dgate, dup = _ew_bwd(_f_swiglu, s["sw_ops"], [(dact, True)], [0, 1], tm=2048, ncol=11, name=f"swiglu_bwd{l}", gdt=[BF16, BF16])
        big["w_gate"] = _mm(s["h2"], dgate, "tn", BF16, f"wg_gate{l}")
        big["w_up"] = _mm(s["h2"], dup, "tn", BF16, f"wg_up{l}")
        dh2 = _mm(dup, W_up[l], "nt", F32, f"dg_up{l}", acc_in=_mm(dgate, W_gate[l], "nt", F32, f"dg_gate{l}"))
        dx1, dln_f, dsc_f, dsh_f = _ew_bwd(_f_normmod, s["nm2"], [(dh2, False), (dx, False)], [0, 1, 2, 4], tm=256, name=f"normmod2_bwd{l}")
        dmix, dgt_m = _ew_bwd(_f_resgate, s["rg1"], [(dx1, False)], [1, 2], tm=256, name=f"resgate1_bwd{l}", gdt=[BF16, F32])
        big["w_out"] = jnp.concatenate([_mm(s["dn_out"], dmix, "tn", BF16, f"wg_out_dn{l}"),
                                        _mm(s["at_out"], dmix, "tn", BF16, f"wg_out_at{l}")], axis=0)
        d_dn = _mm(dmix, W_out[l][:DN_W], "nt", F32, f"dg_out_dn{l}")
        d_at = _mm(dmix, W_out[l][DN_W:], "nt", F32, f"dg_out_at{l}")
        ffn_sums = (_chip_sums(jnp.concatenate([shards(big["w_gate"], True, 4), shards(big["w_up"], True, 4)], axis=0), f"w_gu{l}"),
                    _chip_sums(shards(big["w_dn"], False), f"w_down{l}"))
        dqh, dkp, dkc, dvp, dvc, dsk, *recv_ffn[l] = _attn_bwd(s["qh"], s["kh"], s["vh"], s["sk"], cos, sin, rot, heads(d_at, AT_QH),
                                                               f"attn_bwd{l}", a2a_srcs=ffn_sums[1:])
        dkh = _kv_combine(dkc, dkp, f"dk_comb{l}")
        dvh = _kv_combine(dvc, dvp, f"dv_comb{l}")
        do, dz, dnw = _ew_bwd(_f_dnpost, s["post_ops"], [(d_dn, True)], [0, 1, 2], tm=2048, ncol=DN_HEADS, name=f"dnpost_bwd{l}")
        dqkvn, dbg, got_gu, *got = _chunk_bwd(s["qkvn"], s["bg"], s["s_saved"], s["t_saved"], do, f"chunk_bwd{l}",
                                              a2a_srcs=ffn_sums[:1] + pending)
        recv_ffn[l] = [got_gu] + recv_ffn[l]
        if pending:
            recv_mix[l + 1] = got
        dconv = _ew_bwd(_f_dnpre, s["pre_ops"], [(dqkvn, True)], [0], tm=4096, ncol=3 * DN_HEADS, name=f"dnpre_bwd{l}", with_j=True)[0]
        dba, dalog, ddt = _ew_bwd(_f_bg, s["bg_ops"], [(dbg, False)], [0, 1, 2], tm=1024, name=f"bg_bwd{l}")
        dqkv, dcw = _conv_bwd(s["proj"], dconv, W_cv[l], f"conv_bwd{l}")
        dproj = jnp.concatenate([dqkv, dz, unheads(dqh), unheads(dkh), unheads(dvh), dba, jnp.zeros((T, IN_PAD - 5504), F32)],
                                axis=-1).astype(BF16)
        big["w_in"] = _mm(s["h1"], dproj, "tn", BF16, f"wg_in{l}")
        pending = (_chip_sums(shards(_from_aligned(big["w_in"]), True), f"w_in{l}"), _chip_sums(shards(big["w_out"], False), f"w_out{l}"))
        if l == 0:
            dh1, *recv_mix[0] = _mm(dproj, W_in[l], "nt", F32, f"dg_in{l}", a2a_srcs=pending)
        else:
            dh1 = _mm(dproj, W_in[l], "nt", F32, f"dg_in{l}")
        dx, dln_m, dsc_m, dsh_m = _ew_bwd(_f_normmod, s["nm1"], [(dh1, False), (dx1, False)], [0, 1, 2, 4], tm=256, name=f"normmod1_bwd{l}")
        small["ln_mix"][l], small["ln_ffn"][l] = dln_m, dln_f
        small["mod"][l] = jnp.concatenate([dsh_m, dsc_m, dgt_m, dsh_f, dsc_f, dgt_f], axis=-1)
        small["a_log"][l], small["dt"][l] = dalog[:, 8:16], ddt[:, 8:16]
        small["norm_w"][l], small["sinks"][l], small["conv"][l] = dnw, dsk[:, 0, 0], dcw
    p_in, p_out = [jnp.concatenate([recv_mix[l][i] for l in range(L)], axis=1) for i in range(2)]
    p_gu, p_dn = [jnp.concatenate([recv_ffn[l][i] for l in range(L)], axis=1) for i in range(2)]

    cat0 = lambda xs_: jnp.concatenate([_rows128(a) for a in xs_], axis=0)
    stk = lambda k: jnp.stack(small[k])
    pack = cat0([stk("ln_mix"), stk("ln_ffn"), stk("mod"), stk("a_log"), stk("dt"), stk("norm_w"), stk("sinks"), d_ln_final, stk("conv")])
    n_small = pack.shape[0] - L * CONV_K * 3 * DN_W // 128
    pack = jnp.pad(pack, ((0, (-pack.shape[0]) % 8), (0, 0)))
    gp = _gather(pack, "ag_small")
    parts_small = gp[:, :n_small]
    dmod_all = gp[:, 2 * L * D // 128:2 * L * D // 128 + L * 6 * D // 128].reshape(N_DEV, L, 6 * D)
    conv_all = gp[:, n_small:n_small + L * CONV_K * 3 * DN_W // 128].reshape(N_DEV, L * CONV_K, 3 * DN_W)
    parts_conv = lax.dynamic_slice_in_dim(conv_all, me * (3 * DN_W // N_DEV), 3 * DN_W // N_DEV, axis=2)

    dmod_mine = lax.dynamic_slice_in_dim(dmod_all, me * (6 * D // N_DEV), 6 * D // N_DEV, axis=2)
    g_ada = jnp.stack([_mm(c_act, jnp.pad(dmod_mine[:, l], ((0, 8), (0, 0))), "tn", F32, f"wg_ada{l}") for l in range(L)])

    def upd(w, m, v, parts, name):
        shp = w.shape
        r = lambda a: a.reshape(-1, shp[-1])
        return [o_.reshape(shp) for o_ in _adamw(r(w), r(m), r(v), parts.reshape(parts.shape[0], -1, shp[-1]), name)]

    res = {}
    res["w_ada"] = upd(w_ada, m_w_ada, v_w_ada, g_ada[None], "adamw_ada")
    res["w_in"] = upd(w_in, m_w_in, v_w_in, p_in, "adamw_in")
    res["dn_conv_w"] = upd(dn_conv_w, m_dn_conv_w, v_dn_conv_w, parts_conv, "adamw_conv")
    res["w_out"] = upd(w_out, m_w_out, v_w_out, p_out, "adamw_out")
    res["w_gate_up"] = upd(w_gate_up, m_w_gate_up, v_w_gate_up, p_gu, "adamw_gu")
    res["w_down"] = upd(w_down, m_w_down, v_w_down, p_dn, "adamw_down")
    names_s = ["ln_mix", "ln_ffn", "b_ada", "dn_a_log", "dn_dt_bias", "dn_norm_w", "attn_sinks", "ln_final"]
    ws = [ln_mix, ln_ffn, b_ada, dn_a_log, dn_dt_bias, dn_norm_w, attn_sinks, ln_final]
    ms = [m_ln_mix, m_ln_ffn, m_b_ada, m_dn_a_log, m_dn_dt_bias, m_dn_norm_w, m_attn_sinks, m_ln_final]
    vs = [v_ln_mix, v_ln_ffn, v_b_ada, v_dn_a_log, v_dn_dt_bias, v_dn_norm_w, v_attn_sinks, v_ln_final]
    padr = lambda a: jnp.pad(a, ((0, (-a.shape[0]) % 8), (0, 0)))
    vpad = jnp.pad(cat0(vs), ((0, (-n_small) % 8), (0, 0)), constant_values=1.0)
    outs_s = _adamw(padr(cat0(ws)), padr(cat0(ms)), vpad, jnp.pad(parts_small, ((0, 0), (0, (-n_small) % 8), (0, 0))), "adamw_small")
    off = 0
    for nme, wv in zip(names_s, ws):
        nrow = -(-wv.size // 128)
        res[nme] = [o_[off:off + nrow].reshape(-1)[:wv.size].reshape(wv.shape) for o_ in outs_s]
        off += nrow

    order = ["ln_mix", "ln_ffn", "w_ada", "b_ada", "w_in", "dn_conv_w", "dn_a_log", "dn_dt_bias", "dn_norm_w", "attn_sinks",
             "w_out", "w_gate_up", "w_down", "ln_final"]
    return (loss, dx[None], *[res[n][0] for n in order], *[res[n][1] for n in order], *[res[n][2] for n in order],
            *[res[n][3] for n in order])
```
